```python
import jax, jax.numpy as jnp
from jax import lax
import numpy as np

D_MODEL = 2048
BATCH = 8
SEQ = 2048
DEPTH = 4

CONF_WIDTH = D_MODEL // 2
CONF_KERNEL = 31
SCONV_WIDTH = D_MODEL // 2
SCONV_KERNEL = 3
HEAD_DIM = 64
N_Q_HEADS = (D_MODEL // 2) // HEAD_DIM
N_KV_HEADS = 4
Q_WIDTH = N_Q_HEADS * HEAD_DIM
KV_WIDTH = N_KV_HEADS * HEAD_DIM
WINDOW = 128
BLOCK = 128
N_BRANCH = 3
D_FF = 4 * D_MODEL
RMS_EPS = 1e-6
LN_EPS = 1e-5
IN_SPLITS = (CONF_WIDTH, CONF_WIDTH,
             SCONV_WIDTH, SCONV_WIDTH, SCONV_WIDTH,
             Q_WIDTH, KV_WIDTH, KV_WIDTH,
             N_BRANCH * D_MODEL)
IN_WIDTH = sum(IN_SPLITS)

kernel_name = "hybrid_conformer_shortconv_swa_sink_block"


def rmsnorm(x, g):
    xf = x.astype(jnp.float32)
    y = xf * lax.rsqrt(jnp.mean(xf * xf, axis=-1, keepdims=True) + RMS_EPS)
    return y.astype(x.dtype) * g


def layernorm(x, g, b):
    xf = x.astype(jnp.float32)
    mu = jnp.mean(xf, axis=-1, keepdims=True)
    var = jnp.mean(jnp.square(xf - mu), axis=-1, keepdims=True)
    y = (xf - mu) * lax.rsqrt(var + LN_EPS)
    return y.astype(x.dtype) * g + b


def causal_depthwise_conv(x, w):
    k, c = w.shape
    return lax.conv_general_dilated(
        x, w[:, None, :].astype(x.dtype), window_strides=(1,),
        padding=((k - 1, 0),), dimension_numbers=("NWC", "WIO", "NWC"),
        feature_group_count=c)


def split_columns(u):
    outs, start = [], 0
    for width in IN_SPLITS:
        outs.append(u[..., start:start + width])
        start += width
    return outs


def sliding_window_sink_attention(q, k, v, sinks):
    b, s, _ = q.shape
    nb = s // BLOCK
    grp = N_Q_HEADS // N_KV_HEADS
    qb = q.reshape(b, nb, BLOCK, N_KV_HEADS, grp, HEAD_DIM)

    def band(t):
        t = t.reshape(b, s, N_KV_HEADS, HEAD_DIM)
        prev = jnp.pad(t, ((0, 0), (BLOCK, 0), (0, 0), (0, 0)))[:, :s]
        prev = prev.reshape(b, nb, BLOCK, N_KV_HEADS, HEAD_DIM)
        cur = t.reshape(b, nb, BLOCK, N_KV_HEADS, HEAD_DIM)
        return jnp.concatenate([prev, cur], axis=2)

    kb, vb = band(k), band(v)
    scores = jnp.einsum("bnqhgd,bnkhd->bnhgqk", qb, kb,
                        preferred_element_type=jnp.float32) * (HEAD_DIM ** -0.5)
    blk = jnp.arange(nb)
    qpos = blk[:, None, None] * BLOCK + jnp.arange(BLOCK)[None, :, None]
    kpos = (blk[:, None, None] - 1) * BLOCK + jnp.arange(2 * BLOCK)[None, None, :]
    diff = qpos - kpos
    valid = (diff >= 0) & (diff < WINDOW) & (kpos >= 0)
    scores = jnp.where(valid[None, :, None, None], scores, -jnp.inf)
    sink = sinks.astype(jnp.float32).reshape(1, 1, N_KV_HEADS, grp, 1, 1)
    m = jnp.maximum(jnp.max(scores, axis=-1, keepdims=True), sink)
    p = jnp.exp(scores - m)
    denom = jnp.sum(p, axis=-1, keepdims=True) + jnp.exp(sink - m)
    probs = (p / denom).astype(v.dtype)
    out = jnp.einsum("bnhgqk,bnkhd->bnqhgd", probs, vb)
    return out.reshape(b, s, Q_WIDTH)


def _fwd_setup_inputs(seed: int = 0) -> dict:
    key = jax.random.key(seed)
    ks = jax.random.split(key, 20)
    f32 = jnp.float32

    def nrm(k, shape, scale):
        return jax.random.normal(k, shape, f32) * scale

    return {
        "x": nrm(ks[0], (BATCH, SEQ, D_MODEL), 1.0),
        "norm_mix_g": 1.0 + nrm(ks[1], (DEPTH, D_MODEL), 0.02),
        "w_in": nrm(ks[2], (DEPTH, D_MODEL, IN_WIDTH), D_MODEL ** -0.5),
        "gate_b": nrm(ks[3], (DEPTH, N_BRANCH, D_MODEL), 0.02),
        "conf_dw": nrm(ks[4], (DEPTH, CONF_KERNEL, CONF_WIDTH), CONF_KERNEL ** -0.5),
        "conf_ln_g": 1.0 + nrm(ks[5], (DEPTH, CONF_WIDTH), 0.02),
        "conf_ln_b": nrm(ks[6], (DEPTH, CONF_WIDTH), 0.02),
        "w_conf_out": nrm(ks[7], (DEPTH, CONF_WIDTH, D_MODEL), CONF_WIDTH ** -0.5),
        "sconv_w": nrm(ks[8], (DEPTH, SCONV_KERNEL, SCONV_WIDTH), SCONV_KERNEL ** -0.5),
        "w_sconv_out": nrm(ks[9], (DEPTH, SCONV_WIDTH, D_MODEL), SCONV_WIDTH ** -0.5),
        "sinks": nrm(ks[10], (DEPTH, N_Q_HEADS), 0.5),
        "w_attn_out": nrm(ks[11], (DEPTH, Q_WIDTH, D_MODEL), Q_WIDTH ** -0.5),
        "w_mix_out": nrm(ks[12], (DEPTH, D_MODEL, D_MODEL), D_MODEL ** -0.5),
        "norm_ffn_g": 1.0 + nrm(ks[13], (DEPTH, D_MODEL), 0.02),
        "w_up": nrm(ks[14], (DEPTH, D_MODEL, D_FF), D_MODEL ** -0.5),
        "w_down": nrm(ks[15], (DEPTH, D_FF, D_MODEL), D_FF ** -0.5),
        "final_g": 1.0 + nrm(ks[16], (D_MODEL,), 0.02),
    }


def _fwd_reference(x, norm_mix_g, w_in, gate_b, conf_dw, conf_ln_g, conf_ln_b, w_conf_out,
              sconv_w, w_sconv_out, sinks, w_attn_out, w_mix_out, norm_ffn_g,
              w_up, w_down, final_g):
    b, s, d = x.shape
    for l in range(DEPTH):
        h = rmsnorm(x, norm_mix_g[l])
        u = h @ w_in[l]
        a_val, a_gate, b_gate, c_gate, b_h, q, k, v, g = split_columns(u)

        a = a_val * jax.nn.sigmoid(a_gate)
        a = causal_depthwise_conv(a, conf_dw[l])
        a = jax.nn.silu(layernorm(a, conf_ln_g[l], conf_ln_b[l]))
        y_a = a @ w_conf_out[l]

        sc = causal_depthwise_conv(c_gate * b_h, sconv_w[l])
        y_b = (b_gate * sc) @ w_sconv_out[l]

        y_c = sliding_window_sink_attention(q, k, v, sinks[l]) @ w_attn_out[l]

        gates = jax.nn.sigmoid(g.reshape(b, s, N_BRANCH, d) + gate_b[l])
        merged = gates[:, :, 0] * y_a + gates[:, :, 1] * y_b + gates[:, :, 2] * y_c
        x = x + merged @ w_mix_out[l]

        h = rmsnorm(x, norm_ffn_g[l])
        x = x + jnp.square(jax.nn.relu(h @ w_up[l])) @ w_down[l]
    return rmsnorm(x, final_g)


import jax as _jax
import jax.numpy as _jnp

TWIN_FORMAT = 'train_step'
FWD_PARAMS = ['x', 'norm_mix_g', 'w_in', 'gate_b', 'conf_dw', 'conf_ln_g', 'conf_ln_b', 'w_conf_out', 'sconv_w', 'w_sconv_out', 'sinks', 'w_attn_out', 'w_mix_out', 'norm_ffn_g', 'w_up', 'w_down', 'final_g']
TWIN_WEIGHTS = ['norm_mix_g', 'w_in', 'gate_b', 'conf_dw', 'conf_ln_g', 'conf_ln_b', 'w_conf_out', 'sconv_w', 'w_sconv_out', 'sinks', 'w_attn_out', 'w_mix_out', 'norm_ffn_g', 'w_up', 'w_down', 'final_g']
TWIN_DIFF_INPUT = 'x'
TWIN_INPUTS = ['x', 'norm_mix_g', 'w_in', 'gate_b', 'conf_dw', 'conf_ln_g', 'conf_ln_b', 'w_conf_out', 'sconv_w', 'w_sconv_out', 'sinks', 'w_attn_out', 'w_mix_out', 'norm_ffn_g', 'w_up', 'w_down', 'final_g', 'loss_target', 'm_norm_mix_g', 'm_w_in', 'm_gate_b', 'm_conf_dw', 'm_conf_ln_g', 'm_conf_ln_b', 'm_w_conf_out', 'm_sconv_w', 'm_w_sconv_out', 'm_sinks', 'm_w_attn_out', 'm_w_mix_out', 'm_norm_ffn_g', 'm_w_up', 'm_w_down', 'm_final_g', 'v_norm_mix_g', 'v_w_in', 'v_gate_b', 'v_conf_dw', 'v_conf_ln_g', 'v_conf_ln_b', 'v_w_conf_out', 'v_sconv_w', 'v_w_sconv_out', 'v_sinks', 'v_w_attn_out', 'v_w_mix_out', 'v_norm_ffn_g', 'v_w_up', 'v_w_down', 'v_final_g']
TWIN_OUTPUTS = ['loss', 'grad_x', 'grad_norm_mix_g', 'grad_w_in', 'grad_gate_b', 'grad_conf_dw', 'grad_conf_ln_g', 'grad_conf_ln_b', 'grad_w_conf_out', 'grad_sconv_w', 'grad_w_sconv_out', 'grad_sinks', 'grad_w_attn_out', 'grad_w_mix_out', 'grad_norm_ffn_g', 'grad_w_up', 'grad_w_down', 'grad_final_g', 'delta_norm_mix_g', 'delta_w_in', 'delta_gate_b', 'delta_conf_dw', 'delta_conf_ln_g', 'delta_conf_ln_b', 'delta_w_conf_out', 'delta_sconv_w', 'delta_w_sconv_out', 'delta_sinks', 'delta_w_attn_out', 'delta_w_mix_out', 'delta_norm_ffn_g', 'delta_w_up', 'delta_w_down', 'delta_final_g', 'new_m_norm_mix_g', 'new_m_w_in', 'new_m_gate_b', 'new_m_conf_dw', 'new_m_conf_ln_g', 'new_m_conf_ln_b', 'new_m_w_conf_out', 'new_m_sconv_w', 'new_m_w_sconv_out', 'new_m_sinks', 'new_m_w_attn_out', 'new_m_w_mix_out', 'new_m_norm_ffn_g', 'new_m_w_up', 'new_m_w_down', 'new_m_final_g', 'new_v_norm_mix_g', 'new_v_w_in', 'new_v_gate_b', 'new_v_conf_dw', 'new_v_conf_ln_g', 'new_v_conf_ln_b', 'new_v_w_conf_out', 'new_v_sconv_w', 'new_v_w_sconv_out', 'new_v_sinks', 'new_v_w_attn_out', 'new_v_w_mix_out', 'new_v_norm_ffn_g', 'new_v_w_up', 'new_v_w_down', 'new_v_final_g']
TWIN_LEAF_KINDS = {'loss': 'loss', 'grad_x': 'grad_x', 'grad_norm_mix_g': 'grad_w', 'grad_w_in': 'grad_w', 'grad_gate_b': 'grad_w', 'grad_conf_dw': 'grad_w', 'grad_conf_ln_g': 'grad_w', 'grad_conf_ln_b': 'grad_w', 'grad_w_conf_out': 'grad_w', 'grad_sconv_w': 'grad_w', 'grad_w_sconv_out': 'grad_w', 'grad_sinks': 'grad_w', 'grad_w_attn_out': 'grad_w', 'grad_w_mix_out': 'grad_w', 'grad_norm_ffn_g': 'grad_w', 'grad_w_up': 'grad_w', 'grad_w_down': 'grad_w', 'grad_final_g': 'grad_w', 'delta_norm_mix_g': 'delta_w', 'delta_w_in': 'delta_w', 'delta_gate_b': 'delta_w', 'delta_conf_dw': 'delta_w', 'delta_conf_ln_g': 'delta_w', 'delta_conf_ln_b': 'delta_w', 'delta_w_conf_out': 'delta_w', 'delta_sconv_w': 'delta_w', 'delta_w_sconv_out': 'delta_w', 'delta_sinks': 'delta_w', 'delta_w_attn_out': 'delta_w', 'delta_w_mix_out': 'delta_w', 'delta_norm_ffn_g': 'delta_w', 'delta_w_up': 'delta_w', 'delta_w_down': 'delta_w', 'delta_final_g': 'delta_w', 'new_m_norm_mix_g': 'new_m', 'new_m_w_in': 'new_m', 'new_m_gate_b': 'new_m', 'new_m_conf_dw': 'new_m', 'new_m_conf_ln_g': 'new_m', 'new_m_conf_ln_b': 'new_m', 'new_m_w_conf_out': 'new_m', 'new_m_sconv_w': 'new_m', 'new_m_w_sconv_out': 'new_m', 'new_m_sinks': 'new_m', 'new_m_w_attn_out': 'new_m', 'new_m_w_mix_out': 'new_m', 'new_m_norm_ffn_g': 'new_m', 'new_m_w_up': 'new_m', 'new_m_w_down': 'new_m', 'new_m_final_g': 'new_m', 'new_v_norm_mix_g': 'new_v', 'new_v_w_in': 'new_v', 'new_v_gate_b': 'new_v', 'new_v_conf_dw': 'new_v', 'new_v_conf_ln_g': 'new_v', 'new_v_conf_ln_b': 'new_v', 'new_v_w_conf_out': 'new_v', 'new_v_sconv_w': 'new_v', 'new_v_w_sconv_out': 'new_v', 'new_v_sinks': 'new_v', 'new_v_w_attn_out': 'new_v', 'new_v_w_mix_out': 'new_v', 'new_v_norm_ffn_g': 'new_v', 'new_v_w_up': 'new_v', 'new_v_w_down': 'new_v', 'new_v_final_g': 'new_v'}


def _forward(args):
    return _fwd_reference(*[args[k] for k in FWD_PARAMS])


def _output_shape():
    out = _jax.eval_shape(lambda: _forward(_fwd_setup_inputs(0)))
    return out.shape, out.dtype

N_MICROBATCH = 1
ADAM_LR = 0.001
ADAM_B1 = 0.9
ADAM_B2 = 0.999
ADAM_EPS = 1e-08
ADAM_WD = 0.01
ADAM_STEP = 10
PER_EXAMPLE_BATCH_AXIS = {'x': 0, 'loss_target': 0}
SHARED_INPUTS = []
_WEIGHT_DTYPES = {'norm_mix_g': _jnp.float32, 'w_in': _jnp.float32, 'gate_b': _jnp.float32, 'conf_dw': _jnp.float32, 'conf_ln_g': _jnp.float32, 'conf_ln_b': _jnp.float32, 'w_conf_out': _jnp.float32, 'sconv_w': _jnp.float32, 'w_sconv_out': _jnp.float32, 'sinks': _jnp.float32, 'w_attn_out': _jnp.float32, 'w_mix_out': _jnp.float32, 'norm_ffn_g': _jnp.float32, 'w_up': _jnp.float32, 'w_down': _jnp.float32, 'final_g': _jnp.float32}
MOMENT_SCALE = {'norm_mix_g': 5.075165e-02, 'w_in': 2.015036e-02, 'gate_b': 7.085190e-03, 'conf_dw': 2.326229e-02, 'conf_ln_g': 2.913309e-02, 'conf_ln_b': 2.859926e-02, 'w_conf_out': 1.661388e-02, 'sconv_w': 3.710082e-02, 'w_sconv_out': 2.571634e-02, 'sinks': 6.813540e-03, 'w_attn_out': 8.241609e-03, 'w_mix_out': 3.165132e-02, 'norm_ffn_g': 5.075177e-02, 'w_up': 2.469276e-02, 'w_down': 4.746389e-02, 'final_g': 8.210007e+00}


def _to_microbatches(a, axis):
    t = _jnp.moveaxis(a, axis, 0)
    t = t.reshape((N_MICROBATCH, t.shape[0] // N_MICROBATCH) + t.shape[1:])
    return _jnp.moveaxis(t, 1, axis + 1)


def setup_inputs(seed: int = 0) -> dict:
    inp = _fwd_setup_inputs(seed)
    key = _jax.random.fold_in(_jax.random.key(seed), 7919)
    shape, _ = _output_shape()
    out = dict(inp)
    out["loss_target"] = _jax.random.normal(_jax.random.fold_in(key, 0), shape, _jnp.float32)
    for i, name in enumerate(TWIN_WEIGHTS):
        w = inp[name].astype(_jnp.float32)
        if MOMENT_SCALE is None:
            s = _jnp.sqrt(_jnp.mean(_jnp.square(w)) + 1e-30)
        else:
            s = MOMENT_SCALE[name]
        km, kv = _jax.random.split(_jax.random.fold_in(key, i + 1))
        out[name] = w
        out["m_" + name] = s * _jax.random.normal(km, w.shape, _jnp.float32)
        out["v_" + name] = (s * s) * _jax.random.uniform(kv, w.shape, _jnp.float32, 0.5, 1.5)
    if N_MICROBATCH > 1:
        for name, axis in PER_EXAMPLE_BATCH_AXIS.items():
            out[name] = _to_microbatches(out[name], axis)
    return {'x': out['x'], 'norm_mix_g': out['norm_mix_g'], 'w_in': out['w_in'], 'gate_b': out['gate_b'], 'conf_dw': out['conf_dw'], 'conf_ln_g': out['conf_ln_g'], 'conf_ln_b': out['conf_ln_b'], 'w_conf_out': out['w_conf_out'], 'sconv_w': out['sconv_w'], 'w_sconv_out': out['w_sconv_out'], 'sinks': out['sinks'], 'w_attn_out': out['w_attn_out'], 'w_mix_out': out['w_mix_out'], 'norm_ffn_g': out['norm_ffn_g'], 'w_up': out['w_up'], 'w_down': out['w_down'], 'final_g': out['final_g'], 'loss_target': out['loss_target'], 'm_norm_mix_g': out['m_norm_mix_g'], 'm_w_in': out['m_w_in'], 'm_gate_b': out['m_gate_b'], 'm_conf_dw': out['m_conf_dw'], 'm_conf_ln_g': out['m_conf_ln_g'], 'm_conf_ln_b': out['m_conf_ln_b'], 'm_w_conf_out': out['m_w_conf_out'], 'm_sconv_w': out['m_sconv_w'], 'm_w_sconv_out': out['m_w_sconv_out'], 'm_sinks': out['m_sinks'], 'm_w_attn_out': out['m_w_attn_out'], 'm_w_mix_out': out['m_w_mix_out'], 'm_norm_ffn_g': out['m_norm_ffn_g'], 'm_w_up': out['m_w_up'], 'm_w_down': out['m_w_down'], 'm_final_g': out['m_final_g'], 'v_norm_mix_g': out['v_norm_mix_g'], 'v_w_in': out['v_w_in'], 'v_gate_b': out['v_gate_b'], 'v_conf_dw': out['v_conf_dw'], 'v_conf_ln_g': out['v_conf_ln_g'], 'v_conf_ln_b': out['v_conf_ln_b'], 'v_w_conf_out': out['v_w_conf_out'], 'v_sconv_w': out['v_sconv_w'], 'v_w_sconv_out': out['v_w_sconv_out'], 'v_sinks': out['v_sinks'], 'v_w_attn_out': out['v_w_attn_out'], 'v_w_mix_out': out['v_w_mix_out'], 'v_norm_ffn_g': out['v_norm_ffn_g'], 'v_w_up': out['v_w_up'], 'v_w_down': out['v_w_down'], 'v_final_g': out['v_final_g']}


def _loss(weights, diff, rest, loss_target):
    with _jax.named_scope("forward"):
        args = {**rest, TWIN_DIFF_INPUT: diff, **{k: w.astype(_WEIGHT_DTYPES[k]) for k, w in weights.items()}}
        y = _forward(args)
    with _jax.named_scope("loss_head"):
        err = _jnp.square(y.astype(_jnp.float32) - loss_target)
        return 0.5 * _jnp.sum(_jnp.mean(err, axis=-1)) if err.ndim else 0.5 * err


def _adamw(w, g, m, v):
    m = ADAM_B1 * m + (1.0 - ADAM_B1) * g
    v = ADAM_B2 * v + (1.0 - ADAM_B2) * _jnp.square(g)
    m_hat = m / (1.0 - ADAM_B1 ** ADAM_STEP)
    v_hat = v / (1.0 - ADAM_B2 ** ADAM_STEP)
    delta = -ADAM_LR * (m_hat / (_jnp.sqrt(v_hat) + ADAM_EPS) + ADAM_WD * w)
    return delta, m, v


def reference(x, norm_mix_g, w_in, gate_b, conf_dw, conf_ln_g, conf_ln_b, w_conf_out, sconv_w, w_sconv_out, sinks, w_attn_out, w_mix_out, norm_ffn_g, w_up, w_down, final_g, loss_target, m_norm_mix_g, m_w_in, m_gate_b, m_conf_dw, m_conf_ln_g, m_conf_ln_b, m_w_conf_out, m_sconv_w, m_w_sconv_out, m_sinks, m_w_attn_out, m_w_mix_out, m_norm_ffn_g, m_w_up, m_w_down, m_final_g, v_norm_mix_g, v_w_in, v_gate_b, v_conf_dw, v_conf_ln_g, v_conf_ln_b, v_w_conf_out, v_sconv_w, v_w_sconv_out, v_sinks, v_w_attn_out, v_w_mix_out, v_norm_ffn_g, v_w_up, v_w_down, v_final_g):
    given = dict(x=x, norm_mix_g=norm_mix_g, w_in=w_in, gate_b=gate_b, conf_dw=conf_dw, conf_ln_g=conf_ln_g, conf_ln_b=conf_ln_b, w_conf_out=w_conf_out, sconv_w=sconv_w, w_sconv_out=w_sconv_out, sinks=sinks, w_attn_out=w_attn_out, w_mix_out=w_mix_out, norm_ffn_g=norm_ffn_g, w_up=w_up, w_down=w_down, final_g=final_g, loss_target=loss_target, m_norm_mix_g=m_norm_mix_g, m_w_in=m_w_in, m_gate_b=m_gate_b, m_conf_dw=m_conf_dw, m_conf_ln_g=m_conf_ln_g, m_conf_ln_b=m_conf_ln_b, m_w_conf_out=m_w_conf_out, m_sconv_w=m_sconv_w, m_w_sconv_out=m_w_sconv_out, m_sinks=m_sinks, m_w_attn_out=m_w_attn_out, m_w_mix_out=m_w_mix_out, m_norm_ffn_g=m_norm_ffn_g, m_w_up=m_w_up, m_w_down=m_w_down, m_final_g=m_final_g, v_norm_mix_g=v_norm_mix_g, v_w_in=v_w_in, v_gate_b=v_gate_b, v_conf_dw=v_conf_dw, v_conf_ln_g=v_conf_ln_g, v_conf_ln_b=v_conf_ln_b, v_w_conf_out=v_w_conf_out, v_sconv_w=v_sconv_w, v_w_sconv_out=v_w_sconv_out, v_sinks=v_sinks, v_w_attn_out=v_w_attn_out, v_w_mix_out=v_w_mix_out, v_norm_ffn_g=v_norm_ffn_g, v_w_up=v_w_up, v_w_down=v_w_down, v_final_g=v_final_g)
    weights = {n: given[n] for n in TWIN_WEIGHTS}
    shared = {n: given[n] for n in SHARED_INPUTS}
    per_example = {n: given[n] for n in ['x']}
    grad_fn = _jax.value_and_grad(_loss, argnums=(0, 1))

    def one_microbatch(ex, loss_target):
        ex = dict(ex)
        diff = ex.pop(TWIN_DIFF_INPUT)
        return grad_fn(weights, diff, {**shared, **ex}, loss_target)

    if N_MICROBATCH == 1:
        loss, (grad_w, grad_x) = one_microbatch(per_example, given["loss_target"])
    else:
        def body(carry, xs):
            loss_sum, grad_sum = carry
            l_k, (gw_k, gx_k) = one_microbatch(xs[0], xs[1])
            with _jax.named_scope("update"):
                return (loss_sum + l_k, _jax.tree.map(_jnp.add, grad_sum, gw_k)), gx_k

        init = (_jnp.zeros((), _jnp.float32), _jax.tree.map(_jnp.zeros_like, weights))
        (loss, grad_w), grad_x = _jax.lax.scan(body, init, (per_example, given["loss_target"]))
    with _jax.named_scope("update"):
        delta_w, new_m, new_v = {}, {}, {}
        for n in TWIN_WEIGHTS:
            delta_w[n], new_m[n], new_v[n] = _adamw(weights[n], grad_w[n], given["m_" + n], given["v_" + n])
    return (loss, grad_x, *[grad_w[n] for n in TWIN_WEIGHTS], *[delta_w[n] for n in TWIN_WEIGHTS],
            *[new_m[n] for n in TWIN_WEIGHTS], *[new_v[n] for n in TWIN_WEIGHTS])
```

```python
import functools

import jax
import jax.numpy as jnp
from jax import lax
from jax.experimental import pallas as pl
from jax.experimental.pallas import tpu as pltpu

F32 = jnp.float32
BF16 = jnp.bfloat16

N_DEV = 8
HEAD_DIM = 64
N_KV_HEADS = 4
ATTN_BLOCK = 128
CONF_KERNEL = 31
SCONV_KERNEL = 3
N_BRANCH = 3
RMS_EPS = 1e-6
LN_EPS = 1e-5
ADAM_LR = 0.001
ADAM_B1 = 0.9
ADAM_B2 = 0.999
ADAM_EPS = 1e-08
ADAM_WD = 0.01
ADAM_STEP = 10
LANES = 128
NEG_BIG = -1e30
VMEM_LIMIT_BYTES = 56 * 1024 * 1024
MESH = pl.DeviceIdType.MESH

NN = (((1,), (0,)), ((), ()))
NT = (((1,), (1,)), ((), ()))
TN = (((0,), (0,)), ((), ()))


def _pick(n, cap, mult=LANES):
    best = None
    for d in range(mult, min(n, cap) + 1, mult):
        if n % d == 0:
            best = d
    assert best is not None, (n, cap, mult)
    return best


def _sigmoid(x):
    return 1.0 / (1.0 + jnp.exp(-x))


def _params(sem):
    return pltpu.CompilerParams(dimension_semantics=sem, vmem_limit_bytes=VMEM_LIMIT_BYTES)


def _epi_cast(p, ex, outs):
    outs[0][...] = p.astype(outs[0].dtype)


def _epi_resid(p, ex, outs):
    outs[0][...] = ex[0][...] + p


def _epi_relu2(p, ex, outs):
    outs[0][...] = p.astype(outs[0].dtype)
    r = jnp.maximum(p, 0.0)
    outs[1][...] = (r * r).astype(outs[1].dtype)


def _epi_drelu2(p, ex, outs):
    up = ex[0][...].astype(F32)
    outs[0][...] = (p * (2.0 * jnp.maximum(up, 0.0))).astype(outs[0].dtype)


def _matmul(name, a, b, dnums, grid, a_spec, b_spec, out_shape, out_specs, epi, acc_shape, extra=(), extra_specs=()):
    nk = grid[2]
    n_extra, n_out = len(extra), len(out_shape)

    def body(*refs):
        a_ref, b_ref = refs[0], refs[1]
        ex = refs[2:2 + n_extra]
        outs = refs[2 + n_extra:2 + n_extra + n_out]
        p = lax.dot_general(a_ref[...], b_ref[...], dnums, preferred_element_type=F32)
        if nk == 1:
            epi(p, ex, outs)
        else:
            acc = refs[-1]
            k = pl.program_id(2)

            @pl.when(k == 0)
            def _():
                acc[...] = p

            @pl.when(k > 0)
            def _():
                acc[...] += p

            @pl.when(k == nk - 1)
            def _():
                epi(acc[...], ex, outs)

    scratch = [pltpu.VMEM(acc_shape, F32)] if nk > 1 else []
    return pl.pallas_call(
        body, name=name, grid=grid, in_specs=[a_spec, b_spec, *extra_specs], out_specs=list(out_specs),
        out_shape=list(out_shape), scratch_shapes=scratch,
        compiler_params=_params(("parallel", "parallel", "arbitrary")))(a, b, *extra)


def _mm_plain(name, a, b, form, out_dtypes, epi=_epi_cast, extra=(), tm=1024, tn=1024, tk=2048):
    if form == "NN":
        (M, K), N = a.shape, b.shape[1]
    elif form == "NT":
        (M, K), N = a.shape, b.shape[0]
    else:
        (K, M), N = a.shape, b.shape[1]
    tm, tn, tk = _pick(M, tm, 8), _pick(N, tn), _pick(K, tk)
    grid = (N // tn, M // tm, K // tk)
    if form == "TN":
        a_spec = pl.BlockSpec((tk, tm), lambda j, i, k: (k, i))
    else:
        a_spec = pl.BlockSpec((tm, tk), lambda j, i, k: (i, k))
    if form == "NT":
        b_spec = pl.BlockSpec((tn, tk), lambda j, i, k: (j, k))
    else:
        b_spec = pl.BlockSpec((tk, tn), lambda j, i, k: (k, j))
    o_spec = pl.BlockSpec((tm, tn), lambda j, i, k: (i, j))
    dn = {"NN": NN, "NT": NT, "TN": TN}[form]
    return _matmul(name, a, b, dn, grid, a_spec, b_spec,
                   [jax.ShapeDtypeStruct((M, N), dt) for dt in out_dtypes], [o_spec] * len(out_dtypes), epi,
                   (tm, tn), extra, [o_spec] * len(extra))


def _mm_nn_colblocked(name, a, bb, out_dtypes, epi=_epi_cast, tm=1024, tn=1024, tk=2048):
    M, K = a.shape
    ns = bb.shape[2]
    N = N_DEV * ns
    tm, tn, tk = _pick(M, tm, 8), _pick(ns, tn), _pick(K, tk)
    q = ns // tn
    grid = (N // tn, M // tm, K // tk)
    a_spec = pl.BlockSpec((tm, tk), lambda j, i, k: (i, k))
    b_spec = pl.BlockSpec((None, tk, tn), lambda j, i, k: (j // q, k, j % q))
    o_spec = pl.BlockSpec((tm, tn), lambda j, i, k: (i, j))
    return _matmul(name, a, bb, NN, grid, a_spec, b_spec,
                   [jax.ShapeDtypeStruct((M, N), dt) for dt in out_dtypes], [o_spec] * len(out_dtypes), epi, (tm, tn))


def _mm_nt_colblocked(name, a, bb, out_dtype, tm=1024, tn=1024, tk=1024):
    M, N = a.shape
    K, ns = bb.shape[1], bb.shape[2]
    tm, tn, tk = _pick(M, tm, 8), _pick(K, tn), _pick(ns, tk)
    q = ns // tk
    grid = (K // tn, M // tm, N // tk)
    a_spec = pl.BlockSpec((tm, tk), lambda j, i, k: (i, k))
    b_spec = pl.BlockSpec((None, tn, tk), lambda j, i, k: (k // q, j, k % q))
    o_spec = pl.BlockSpec((tm, tn), lambda j, i, k: (i, j))
    return _matmul(name, a, bb, NT, grid, a_spec, b_spec, [jax.ShapeDtypeStruct((M, K), out_dtype)], [o_spec],
                   _epi_cast, (tm, tn))[0]


def _mm_tn_colblocked_out(name, a, b, out_dtype, tm=1024, tn=1024, tk=2048):
    T, M = a.shape
    N = b.shape[1]
    ns = N // N_DEV
    tm, tn, tk = _pick(M, tm, 8), _pick(ns, tn), _pick(T, tk)
    q = ns // tn
    grid = (N // tn, M // tm, T // tk)
    a_spec = pl.BlockSpec((tk, tm), lambda j, i, k: (k, i))
    b_spec = pl.BlockSpec((tk, tn), lambda j, i, k: (k, j))
    o_spec = pl.BlockSpec((None, tm, tn), lambda j, i, k: (j // q, i, j % q))
    return _matmul(name, a, b, TN, grid, a_spec, b_spec, [jax.ShapeDtypeStruct((N_DEV, M, ns), out_dtype)], [o_spec],
                   _epi_cast, (tm, tn))[0]


ROW_TILE = 256


def _rms_fwd(x, g):
    T, D = x.shape
    tr = _pick(T, ROW_TILE, 8)

    def body(x_ref, g_ref, h_ref):
        xv = x_ref[...]
        r = lax.rsqrt(jnp.mean(xv * xv, axis=-1, keepdims=True) + RMS_EPS)
        h_ref[...] = (xv * r * g_ref[...]).astype(BF16)

    return pl.pallas_call(
        body, name="rms_fwd", grid=(T // tr,),
        in_specs=[pl.BlockSpec((tr, D), lambda i: (i, 0)), pl.BlockSpec((1, D), lambda i: (0, 0))],
        out_specs=pl.BlockSpec((tr, D), lambda i: (i, 0)),
        out_shape=jax.ShapeDtypeStruct((T, D), BF16), compiler_params=_params(("parallel",)))(x, g)


def _rms_bwd_math(dh, xv, g):
    r = lax.rsqrt(jnp.mean(xv * xv, axis=-1, keepdims=True) + RMS_EPS)
    gdh = dh * g
    dot = jnp.mean(gdh * xv, axis=-1, keepdims=True)
    dx = r * gdh - xv * (r * r * r * dot)
    return dx, dh * xv * r


def _rms_bwd(dh, x, g, dres):
    T, D = x.shape
    tr = _pick(T, ROW_TILE, 8)

    def body(dh_ref, x_ref, g_ref, dres_ref, dx_ref, dxb_ref, dg_ref):
        dx, dgrow = _rms_bwd_math(dh_ref[...], x_ref[...], g_ref[...])
        dx = dx + dres_ref[...]
        dx_ref[...] = dx
        dxb_ref[...] = dx.astype(BF16)
        part = jnp.sum(dgrow, axis=0, keepdims=True)

        @pl.when(pl.program_id(0) == 0)
        def _():
            dg_ref[...] = part

        @pl.when(pl.program_id(0) > 0)
        def _():
            dg_ref[...] += part

    row = pl.BlockSpec((tr, D), lambda i: (i, 0))
    vec = pl.BlockSpec((1, D), lambda i: (0, 0))
    return pl.pallas_call(
        body, name="rms_bwd", grid=(T // tr,), in_specs=[row, row, vec, row], out_specs=[row, row, vec],
        out_shape=[jax.ShapeDtypeStruct((T, D), F32), jax.ShapeDtypeStruct((T, D), BF16),
                   jax.ShapeDtypeStruct((1, D), F32)],
        compiler_params=_params(("arbitrary",)))(dh, x, g, dres)


def _loss_head(x, g, target):
    T, D = x.shape
    tr = _pick(T, ROW_TILE, 8)

    def body(x_ref, g_ref, t_ref, dx_ref, dxb_ref, dg_ref, loss_ref):
        xv, gv = x_ref[...], g_ref[...]
        r = lax.rsqrt(jnp.mean(xv * xv, axis=-1, keepdims=True) + RMS_EPS)
        err = xv * r * gv - t_ref[...]
        part_loss = 0.5 * jnp.sum(jnp.mean(err * err, axis=-1, keepdims=True), axis=0, keepdims=True)
        dx, dgrow = _rms_bwd_math(err * (1.0 / D), xv, gv)
        dx_ref[...] = dx
        dxb_ref[...] = dx.astype(BF16)
        part = jnp.sum(dgrow, axis=0, keepdims=True)
        lpart = jnp.broadcast_to(part_loss, (8, LANES))

        @pl.when(pl.program_id(0) == 0)
        def _():
            dg_ref[...] = part
            loss_ref[...] = lpart

        @pl.when(pl.program_id(0) > 0)
        def _():
            dg_ref[...] += part
            loss_ref[...] += lpart

    row = pl.BlockSpec((tr, D), lambda i: (i, 0))
    vec = pl.BlockSpec((1, D), lambda i: (0, 0))
    lsp = pl.BlockSpec((8, LANES), lambda i: (0, 0))
    return pl.pallas_call(
        body, name="loss_head", grid=(T // tr,), in_specs=[row, vec, row], out_specs=[row, row, vec, lsp],
        out_shape=[jax.ShapeDtypeStruct((T, D), F32), jax.ShapeDtypeStruct((T, D), BF16),
                   jax.ShapeDtypeStruct((1, D), F32), jax.ShapeDtypeStruct((8, LANES), F32)],
        compiler_params=_params(("arbitrary",)))(x, g, target)


def _ln_math(a1, g, b):
    mu = jnp.mean(a1, axis=-1, keepdims=True)
    xc = a1 - mu
    rstd = lax.rsqrt(jnp.mean(xc * xc, axis=-1, keepdims=True) + LN_EPS)
    xhat = xc * rstd
    return xhat, rstd, xhat * g + b


def _ln_silu(a1, g, b):
    T, C = a1.shape
    tr = _pick(T, ROW_TILE, 8)

    def body(a_ref, g_ref, b_ref, o_ref):
        _, _, y = _ln_math(a_ref[...], g_ref[...], b_ref[...])
        o_ref[...] = (y * _sigmoid(y)).astype(BF16)

    row = pl.BlockSpec((tr, C), lambda i: (i, 0))
    vec = pl.BlockSpec((1, C), lambda i: (0, 0))
    return pl.pallas_call(body, name="ln_silu", grid=(T // tr,), in_specs=[row, vec, vec], out_specs=row,
                          out_shape=jax.ShapeDtypeStruct((T, C), BF16), compiler_params=_params(("parallel",)))(a1, g, b)


def _ln_silu_bwd(a1, g, b, d_a3):
    T, C = a1.shape
    tr = _pick(T, ROW_TILE, 8)

    def body(a_ref, g_ref, b_ref, d_ref, da_ref, dg_ref, db_ref):
        gv = g_ref[...]
        xhat, rstd, y = _ln_math(a_ref[...], gv, b_ref[...])
        s = _sigmoid(y)
        dy = d_ref[...].astype(F32) * (s * (1.0 + y * (1.0 - s)))
        dxh = dy * gv
        m1 = jnp.mean(dxh, axis=-1, keepdims=True)
        m2 = jnp.mean(dxh * xhat, axis=-1, keepdims=True)
        da_ref[...] = rstd * (dxh - m1 - xhat * m2)
        pg = jnp.sum(dy * xhat, axis=0, keepdims=True)
        pb = jnp.sum(dy, axis=0, keepdims=True)

        @pl.when(pl.program_id(0) == 0)
        def _():
            dg_ref[...] = pg
            db_ref[...] = pb

        @pl.when(pl.program_id(0) > 0)
        def _():
            dg_ref[...] += pg
            db_ref[...] += pb

    row = pl.BlockSpec((tr, C), lambda i: (i, 0))
    vec = pl.BlockSpec((1, C), lambda i: (0, 0))
    return pl.pallas_call(
        body, name="ln_silu_bwd", grid=(T // tr,), in_specs=[row, vec, vec, row], out_specs=[row, vec, vec],
        out_shape=[jax.ShapeDtypeStruct((T, C), F32), jax.ShapeDtypeStruct((1, C), F32),
                   jax.ShapeDtypeStruct((1, C), F32)],
        compiler_params=_params(("arbitrary",)))(a1, g, b, d_a3)


CONV_ROWS = 128
PAD_A = 32
PAD_B = 8


def _u_block(T, first):
    return pl.BlockSpec((T, LANES), lambda i: (0, first + i))


def _causal_conv(xpad_ref, w_ref, ksize, pad, T, emit):
    for r0 in range(0, T, CONV_ROWS):
        acc = None
        for j in range(ksize):
            off = pad - (ksize - 1) + j + r0
            term = w_ref[j:j + 1, :] * xpad_ref[off:off + CONV_ROWS, :]
            acc = term if acc is None else acc + term
        emit(r0, acc)


def _anticausal_conv(gpad_ref, w_ref, ksize, T, emit):
    for r0 in range(0, T, CONV_ROWS):
        acc = None
        for j in range(ksize):
            off = (ksize - 1) - j + r0
            term = w_ref[j:j + 1, :] * gpad_ref[off:off + CONV_ROWS, :]
            acc = term if acc is None else acc + term
        emit(r0, acc)


def _conv_wgrad(xpad_ref, g_ref, dw_ref, ksize, pad, T):
    for j in range(ksize):
        acc = None
        for r0 in range(0, T, CONV_ROWS):
            off = pad - (ksize - 1) + j + r0
            term = g_ref[r0:r0 + CONV_ROWS, :] * xpad_ref[off:off + CONV_ROWS, :]
            term = jnp.sum(term.reshape(CONV_ROWS // 8, 8, LANES), axis=0)
            acc = term if acc is None else acc + term
        dw_ref[j:j + 1, :] = jnp.sum(acc, axis=0, keepdims=True)


def _convs_fwd(u, conf_dw, sconv_w, C):
    T = u.shape[0]
    nb = C // LANES

    def body(av_ref, ag_ref, bg_ref, cg_ref, bh_ref, dw_ref, sw_ref, a1_ref, s2_ref, xa, xs, s1):
        xa[0:PAD_A, :] = jnp.zeros((PAD_A, LANES), F32)
        xa[PAD_A:PAD_A + T, :] = av_ref[...].astype(F32) * _sigmoid(ag_ref[...].astype(F32))

        def emit_a(r0, acc):
            a1_ref[r0:r0 + CONV_ROWS, :] = acc

        _causal_conv(xa, dw_ref, CONF_KERNEL, PAD_A, T, emit_a)

        xs[0:PAD_B, :] = jnp.zeros((PAD_B, LANES), F32)
        xs[PAD_B:PAD_B + T, :] = cg_ref[...].astype(F32) * bh_ref[...].astype(F32)

        def emit_b(r0, acc):
            s1[r0:r0 + CONV_ROWS, :] = acc

        _causal_conv(xs, sw_ref, SCONV_KERNEL, PAD_B, T, emit_b)
        s2_ref[...] = (bg_ref[...].astype(F32) * s1[...]).astype(BF16)

    col = pl.BlockSpec((T, LANES), lambda i: (0, i))
    return pl.pallas_call(
        body, name="convs_fwd", grid=(nb,),
        in_specs=[_u_block(T, 0), _u_block(T, nb), _u_block(T, 2 * nb), _u_block(T, 3 * nb), _u_block(T, 4 * nb),
                  pl.BlockSpec((CONF_KERNEL, LANES), lambda i: (0, i)),
                  pl.BlockSpec((SCONV_KERNEL, LANES), lambda i: (0, i))],
        out_specs=[col, col],
        out_shape=[jax.ShapeDtypeStruct((T, C), F32), jax.ShapeDtypeStruct((T, C), BF16)],
        scratch_shapes=[pltpu.VMEM((T + PAD_A, LANES), F32), pltpu.VMEM((T + PAD_B, LANES), F32),
                        pltpu.VMEM((T, LANES), F32)],
        compiler_params=_params(("parallel",)))(u, u, u, u, u, conf_dw, sconv_w)


def _convs_bwd(u, conf_dw, sconv_w, d_a1, d_s2, C):
    T = u.shape[0]
    nb = C // LANES

    def body(av_ref, ag_ref, bg_ref, cg_ref, bh_ref, dw_ref, sw_ref, da1_ref, ds2_ref,
             dav_ref, dag_ref, dbg_ref, dcg_ref, dbh_ref, ddw_ref, dsw_ref, xa, ga, xs, gs, tmp):
        av = av_ref[...].astype(F32)
        sg = _sigmoid(ag_ref[...].astype(F32))
        xa[0:PAD_A, :] = jnp.zeros((PAD_A, LANES), F32)
        xa[PAD_A:PAD_A + T, :] = av * sg
        ga[0:T, :] = da1_ref[...]
        ga[T:T + PAD_A, :] = jnp.zeros((PAD_A, LANES), F32)
        _conv_wgrad(xa, ga, ddw_ref, CONF_KERNEL, PAD_A, T)

        def emit_a(r0, acc):
            tmp[r0:r0 + CONV_ROWS, :] = acc

        _anticausal_conv(ga, dw_ref, CONF_KERNEL, T, emit_a)
        da0 = tmp[...]
        dav_ref[...] = (da0 * sg).astype(BF16)
        dag_ref[...] = (da0 * av * sg * (1.0 - sg)).astype(BF16)

        cg = cg_ref[...].astype(F32)
        bh = bh_ref[...].astype(F32)
        ds2 = ds2_ref[...].astype(F32)
        xs[0:PAD_B, :] = jnp.zeros((PAD_B, LANES), F32)
        xs[PAD_B:PAD_B + T, :] = cg * bh

        def emit_s1(r0, acc):
            tmp[r0:r0 + CONV_ROWS, :] = acc

        _causal_conv(xs, sw_ref, SCONV_KERNEL, PAD_B, T, emit_s1)
        dbg_ref[...] = (ds2 * tmp[...]).astype(BF16)
        gs[0:T, :] = ds2 * bg_ref[...].astype(F32)
        gs[T:T + PAD_B, :] = jnp.zeros((PAD_B, LANES), F32)
        _conv_wgrad(xs, gs, dsw_ref, SCONV_KERNEL, PAD_B, T)

        def emit_b(r0, acc):
            tmp[r0:r0 + CONV_ROWS, :] = acc

        _anticausal_conv(gs, sw_ref, SCONV_KERNEL, T, emit_b)
        ds0 = tmp[...]
        dcg_ref[...] = (ds0 * bh).astype(BF16)
        dbh_ref[...] = (ds0 * cg).astype(BF16)

    col = pl.BlockSpec((T, LANES), lambda i: (0, i))
    wa = pl.BlockSpec((CONF_KERNEL, LANES), lambda i: (0, i))
    wb = pl.BlockSpec((SCONV_KERNEL, LANES), lambda i: (0, i))
    act = jax.ShapeDtypeStruct((T, C), BF16)
    return pl.pallas_call(
        body, name="convs_bwd", grid=(nb,),
        in_specs=[_u_block(T, 0), _u_block(T, nb), _u_block(T, 2 * nb), _u_block(T, 3 * nb), _u_block(T, 4 * nb),
                  wa, wb, col, col],
        out_specs=[col, col, col, col, col, wa, wb],
        out_shape=[act, act, act, act, act, jax.ShapeDtypeStruct((CONF_KERNEL, C), F32),
                   jax.ShapeDtypeStruct((SCONV_KERNEL, C), F32)],
        scratch_shapes=[pltpu.VMEM((T + PAD_A, LANES), F32), pltpu.VMEM((T + PAD_A, LANES), F32),
                        pltpu.VMEM((T + PAD_B, LANES), F32), pltpu.VMEM((T + PAD_B, LANES), F32),
                        pltpu.VMEM((T, LANES), F32)],
        compiler_params=_params(("parallel",)))(u, u, u, u, u, conf_dw, sconv_w, d_a1, d_s2)


def _attn_specs(T, C, q_off_blocks):
    kvw = N_KV_HEADS * HEAD_DIM
    kb = (5 * C + C) // kvw
    qs = pl.BlockSpec((ATTN_BLOCK, C), lambda n: (n, 5))
    kc = pl.BlockSpec((ATTN_BLOCK, kvw), lambda n: (n, kb))
    kp = pl.BlockSpec((ATTN_BLOCK, kvw), lambda n: (jnp.maximum(n - 1, 0), kb))
    vc = pl.BlockSpec((ATTN_BLOCK, kvw), lambda n: (n, kb + 1))
    vp = pl.BlockSpec((ATTN_BLOCK, kvw), lambda n: (jnp.maximum(n - 1, 0), kb + 1))
    return qs, kc, kp, vc, vp


def _attn_masks(n):
    row = lax.broadcasted_iota(jnp.int32, (ATTN_BLOCK, ATTN_BLOCK), 0)
    col = lax.broadcasted_iota(jnp.int32, (ATTN_BLOCK, ATTN_BLOCK), 1)
    return col <= row, jnp.logical_and(col > row, n > 0)


def _attn_scores(qh, kc, kp, mask_c, mask_p):
    scale = HEAD_DIM ** -0.5
    s_c = lax.dot_general(qh, kc, NT, preferred_element_type=F32) * scale
    s_p = lax.dot_general(qh, kp, NT, preferred_element_type=F32) * scale
    return jnp.where(mask_c, s_c, NEG_BIG), jnp.where(mask_p, s_p, NEG_BIG)


def _attn_fwd(u, sinks, C):
    T = u.shape[0]
    H = C // HEAD_DIM
    grp = H // N_KV_HEADS

    def body(sink_ref, q_ref, kc_ref, kp_ref, vc_ref, vp_ref, o_ref, lse_ref):
        n = pl.program_id(0)
        mask_c, mask_p = _attn_masks(n)
        for h in range(H):
            kv = slice((h // grp) * HEAD_DIM, (h // grp + 1) * HEAD_DIM)
            hs = slice(h * HEAD_DIM, (h + 1) * HEAD_DIM)
            sink = sink_ref[h]
            s_c, s_p = _attn_scores(q_ref[:, hs], kc_ref[:, kv], kp_ref[:, kv], mask_c, mask_p)
            m = jnp.maximum(jnp.maximum(jnp.max(s_c, axis=-1, keepdims=True), jnp.max(s_p, axis=-1, keepdims=True)), sink)
            p_c = jnp.exp(s_c - m)
            p_p = jnp.exp(s_p - m)
            den = jnp.sum(p_c, axis=-1, keepdims=True) + jnp.sum(p_p, axis=-1, keepdims=True) + jnp.exp(sink - m)
            acc = jnp.dot(p_c.astype(BF16), vc_ref[:, kv], preferred_element_type=F32)
            acc = acc + jnp.dot(p_p.astype(BF16), vp_ref[:, kv], preferred_element_type=F32)
            o_ref[:, hs] = (acc / den).astype(BF16)
            lse_ref[:, h:h + 1] = m + jnp.log(den)

    qs, kc, kp, vc, vp = _attn_specs(T, C, 5)
    return pl.pallas_call(
        body, name="attn_fwd", grid=(T // ATTN_BLOCK,),
        in_specs=[pl.BlockSpec(memory_space=pltpu.SMEM), qs, kc, kp, vc, vp],
        out_specs=[pl.BlockSpec((ATTN_BLOCK, C), lambda n: (n, 0)), pl.BlockSpec((ATTN_BLOCK, H), lambda n: (n, 0))],
        out_shape=[jax.ShapeDtypeStruct((T, C), BF16), jax.ShapeDtypeStruct((T, H), F32)],
        compiler_params=_params(("parallel",)))(sinks, u, u, u, u, u)


def _attn_bwd(u, o, lse, d_o, sinks, C):
    T = u.shape[0]
    H = C // HEAD_DIM
    grp = H // N_KV_HEADS
    kvw = N_KV_HEADS * HEAD_DIM
    nblk = T // ATTN_BLOCK
    scale = HEAD_DIM ** -0.5

    def body(sink_ref, q_ref, kc_ref, kp_ref, vc_ref, vp_ref, o_ref, lse_ref, do_ref,
             dq_ref, dk_ref, dv_ref, ds_ref, dk_acc, dv_acc):
        n = pl.program_id(0)

        @pl.when(n == 0)
        def _():
            dk_acc[...] = jnp.zeros_like(dk_acc)
            dv_acc[...] = jnp.zeros_like(dv_acc)
            ds_ref[...] = jnp.zeros_like(ds_ref)

        mask_c, mask_p = _attn_masks(n)
        cur = pl.ds(pl.multiple_of(n * ATTN_BLOCK, ATTN_BLOCK), ATTN_BLOCK)
        prev = pl.ds(pl.multiple_of(jnp.maximum(n - 1, 0) * ATTN_BLOCK, ATTN_BLOCK), ATTN_BLOCK)
        for g in range(N_KV_HEADS):
            kv = slice(g * HEAD_DIM, (g + 1) * HEAD_DIM)
            kc, kp, vc, vp = kc_ref[:, kv], kp_ref[:, kv], vc_ref[:, kv], vp_ref[:, kv]
            dk_c = dk_p = dv_c = dv_p = None
            for h in range(g * grp, (g + 1) * grp):
                hs = slice(h * HEAD_DIM, (h + 1) * HEAD_DIM)
                qh, doh = q_ref[:, hs], do_ref[:, hs]
                lse_h = lse_ref[:, h:h + 1]
                s_c, s_p = _attn_scores(qh, kc, kp, mask_c, mask_p)
                p_c = jnp.exp(s_c - lse_h)
                p_p = jnp.exp(s_p - lse_h)
                delta = jnp.sum(doh.astype(F32) * o_ref[:, hs].astype(F32), axis=-1, keepdims=True)
                dp_c = lax.dot_general(doh, vc, NT, preferred_element_type=F32)
                dp_p = lax.dot_general(doh, vp, NT, preferred_element_type=F32)
                ds_c = (p_c * (dp_c - delta) * scale).astype(BF16)
                ds_p = (p_p * (dp_p - delta) * scale).astype(BF16)
                dq = jnp.dot(ds_c, kc, preferred_element_type=F32) + jnp.dot(ds_p, kp, preferred_element_type=F32)
                dq_ref[:, hs] = dq.astype(BF16)
                t_kc = lax.dot_general(ds_c, qh, TN, preferred_element_type=F32)
                t_kp = lax.dot_general(ds_p, qh, TN, preferred_element_type=F32)
                t_vc = lax.dot_general(p_c.astype(BF16), doh, TN, preferred_element_type=F32)
                t_vp = lax.dot_general(p_p.astype(BF16), doh, TN, preferred_element_type=F32)
                dk_c = t_kc if dk_c is None else dk_c + t_kc
                dk_p = t_kp if dk_p is None else dk_p + t_kp
                dv_c = t_vc if dv_c is None else dv_c + t_vc
                dv_p = t_vp if dv_p is None else dv_p + t_vp
                p_sink = jnp.exp(sink_ref[h] - lse_h)
                dsink = -jnp.sum(p_sink * delta, axis=0, keepdims=True)
                ds_ref[:, h:h + 1] += jnp.broadcast_to(dsink, (8, 1))
            dk_acc[cur, kv] += dk_c
            dk_acc[prev, kv] += dk_p
            dv_acc[cur, kv] += dv_c
            dv_acc[prev, kv] += dv_p

        @pl.when(n == nblk - 1)
        def _():
            dk_ref[...] = dk_acc[...].astype(BF16)
            dv_ref[...] = dv_acc[...].astype(BF16)

    qs, kc, kp, vc, vp = _attn_specs(T, C, 5)
    blk = pl.BlockSpec((ATTN_BLOCK, C), lambda n: (n, 0))
    full = pl.BlockSpec((T, kvw), lambda n: (0, 0))
    return pl.pallas_call(
        body, name="attn_bwd", grid=(nblk,),
        in_specs=[pl.BlockSpec(memory_space=pltpu.SMEM), qs, kc, kp, vc, vp, blk,
                  pl.BlockSpec((ATTN_BLOCK, H), lambda n: (n, 0)), blk],
        out_specs=[blk, full, full, pl.BlockSpec((8, H), lambda n: (0, 0))],
        out_shape=[jax.ShapeDtypeStruct((T, C), BF16), jax.ShapeDtypeStruct((T, kvw), BF16),
                   jax.ShapeDtypeStruct((T, kvw), BF16), jax.ShapeDtypeStruct((8, H), F32)],
        scratch_shapes=[pltpu.VMEM((T, kvw), F32), pltpu.VMEM((T, kvw), F32)],
        compiler_params=_params(("arbitrary",)))(sinks, u, u, u, u, u, o, lse, d_o)


MERGE_COLS = 512


def _merge_specs(T, D, I):
    tr = _pick(T, ROW_TILE, 8)
    tc = _pick(D, MERGE_COLS)
    g0 = (I - N_BRANCH * D) // tc
    per = D // tc
    gspecs = [pl.BlockSpec((tr, tc), functools.partial(lambda j, i, b: (i, g0 + b * per + j), b=b)) for b in range(N_BRANCH)]
    tile = pl.BlockSpec((tr, tc), lambda j, i: (i, j))
    bias = pl.BlockSpec((N_BRANCH, tc), lambda j, i: (0, j))
    return tr, tc, gspecs, tile, bias


def _merge_fwd(u, gate_b, ya, yb, yc):
    T, I = u.shape
    D = ya.shape[1]
    tr, tc, gspecs, tile, bias = _merge_specs(T, D, I)

    def body(g0_ref, g1_ref, g2_ref, b_ref, ya_ref, yb_ref, yc_ref, o_ref):
        acc = None
        for b, (g_ref, y_ref) in enumerate(((g0_ref, ya_ref), (g1_ref, yb_ref), (g2_ref, yc_ref))):
            gate = _sigmoid(g_ref[...].astype(F32) + b_ref[b:b + 1, :])
            term = gate * y_ref[...].astype(F32)
            acc = term if acc is None else acc + term
        o_ref[...] = acc.astype(BF16)

    return pl.pallas_call(
        body, name="merge_fwd", grid=(D // tc, T // tr), in_specs=[*gspecs, bias, tile, tile, tile], out_specs=tile,
        out_shape=jax.ShapeDtypeStruct((T, D), BF16),
        compiler_params=_params(("parallel", "parallel")))(u, u, u, gate_b, ya, yb, yc)


def _merge_bwd(u, gate_b, ya, yb, yc, dm):
    T, I = u.shape
    D = ya.shape[1]
    tr, tc, gspecs, tile, bias = _merge_specs(T, D, I)

    def body(g0_ref, g1_ref, g2_ref, b_ref, ya_ref, yb_ref, yc_ref, dm_ref,
             dya_ref, dyb_ref, dyc_ref, dg0_ref, dg1_ref, dg2_ref, db_ref):
        dmv = dm_ref[...].astype(F32)
        first = pl.program_id(1) == 0
        for b, (g_ref, y_ref, dy_ref, dg_ref) in enumerate(((g0_ref, ya_ref, dya_ref, dg0_ref),
                                                           (g1_ref, yb_ref, dyb_ref, dg1_ref),
                                                           (g2_ref, yc_ref, dyc_ref, dg2_ref))):
            gate = _sigmoid(g_ref[...].astype(F32) + b_ref[b:b + 1, :])
            dy_ref[...] = (dmv * gate).astype(BF16)
            dpre = dmv * y_ref[...].astype(F32) * gate * (1.0 - gate)
            dg_ref[...] = dpre.astype(BF16)
            part = jnp.sum(dpre, axis=0, keepdims=True)

            @pl.when(first)
            def _():
                db_ref[b:b + 1, :] = part

            @pl.when(jnp.logical_not(first))
            def _():
                db_ref[b:b + 1, :] += part

    act = jax.ShapeDtypeStruct((T, D), BF16)
    return pl.pallas_call(
        body, name="merge_bwd", grid=(D // tc, T // tr), in_specs=[*gspecs, bias, tile, tile, tile, tile],
        out_specs=[tile] * 6 + [bias], out_shape=[act] * 6 + [jax.ShapeDtypeStruct((N_BRANCH, D), F32)],
        compiler_params=_params(("parallel", "arbitrary")))(u, u, u, gate_b, ya, yb, yc, dm)


ELEMS_PER_TILE = 256 * 1024


def _row_tile(r, c):
    return _pick(r, max(16, ELEMS_PER_TILE // c), 16) if r % 16 == 0 else r


def _cast_bf16(w):
    L, r, c = w.shape
    tr = _row_tile(r, c)
    spec = pl.BlockSpec((None, tr, c), lambda l, i: (l, i, 0))

    def body(w_ref, o_ref):
        o_ref[...] = w_ref[...].astype(BF16)

    return pl.pallas_call(body, name="cast_bf16", grid=(L, r // tr), in_specs=[spec], out_specs=spec,
                          out_shape=jax.ShapeDtypeStruct(w.shape, BF16), compiler_params=_params(("parallel", "parallel")))(w)


def _adamw_math(w, g, m, v):
    m = ADAM_B1 * m + (1.0 - ADAM_B1) * g
    v = ADAM_B2 * v + (1.0 - ADAM_B2) * (g * g)
    m_hat = m / (1.0 - ADAM_B1 ** ADAM_STEP)
    v_hat = v / (1.0 - ADAM_B2 ** ADAM_STEP)
    delta = -ADAM_LR * (m_hat / (jnp.sqrt(v_hat) + ADAM_EPS) + ADAM_WD * w)
    return delta, m, v


def _sum_parts(part_ref):
    acc = part_ref[0].astype(F32)
    for s in range(1, N_DEV):
        acc = acc + part_ref[s].astype(F32)
    return acc


def _sum8(parts):
    L, _, r, c = parts.shape
    tr = _row_tile(r, c)

    def body(p_ref, o_ref):
        o_ref[...] = _sum_parts(p_ref)

    return pl.pallas_call(
        body, name="sum8", grid=(L, r // tr),
        in_specs=[pl.BlockSpec((None, N_DEV, tr, c), lambda l, i: (l, 0, i, 0))],
        out_specs=pl.BlockSpec((None, tr, c), lambda l, i: (l, i, 0)),
        out_shape=jax.ShapeDtypeStruct((L, r, c), F32), compiler_params=_params(("parallel", "parallel")))(parts)


def _adamw(w, g, m, v):
    L, r, c = w.shape
    tr = _row_tile(r, c)
    spec = pl.BlockSpec((None, tr, c), lambda l, i: (l, i, 0))

    def body(w_ref, g_ref, m_ref, v_ref, d_ref, mo_ref, vo_ref):
        d, mn, vn = _adamw_math(w_ref[...], g_ref[...], m_ref[...], v_ref[...])
        d_ref[...] = d
        mo_ref[...] = mn
        vo_ref[...] = vn

    shp = jax.ShapeDtypeStruct(w.shape, F32)
    return pl.pallas_call(body, name="adamw", grid=(L, r // tr), in_specs=[spec] * 4, out_specs=[spec] * 3,
                          out_shape=[shp] * 3, compiler_params=_params(("parallel", "parallel")))(w, g, m, v)


def _sum8_adamw(parts, w, m, v):
    L, r, c = w.shape
    tr = _row_tile(r, c)
    spec = pl.BlockSpec((None, tr, c), lambda l, i: (l, i, 0))

    def body(p_ref, w_ref, m_ref, v_ref, g_ref, d_ref, mo_ref, vo_ref):
        g = _sum_parts(p_ref)
        d, mn, vn = _adamw_math(w_ref[...], g, m_ref[...], v_ref[...])
        g_ref[...] = g
        d_ref[...] = d
        mo_ref[...] = mn
        vo_ref[...] = vn

    shp = jax.ShapeDtypeStruct(w.shape, F32)
    return pl.pallas_call(
        body, name="sum8_adamw", grid=(L, r // tr),
        in_specs=[pl.BlockSpec((None, N_DEV, tr, c), lambda l, i: (l, 0, i, 0)), spec, spec, spec],
        out_specs=[spec] * 4, out_shape=[shp] * 4,
        compiler_params=_params(("parallel", "parallel")))(parts, w, m, v)


def _me():
    return lax.axis_index("x"), lax.axis_index("y"), lax.axis_index("c")


def _flip(pos, k):
    x, y, c = pos
    return (1 - x if k & 4 else x, 1 - y if k & 2 else y, 1 - c if k & 1 else c)


def _block_of(pos):
    return 4 * pos[0] + 2 * pos[1] + pos[2]


ANY = pl.BlockSpec(memory_space=pl.ANY)
SIBLING = 1
OTHER_CHIPS = (4, 2, 6)


def _all_gather(name, arrays, layer=None):
    n = len(arrays)
    shapes = [a.shape[-2:] for a in arrays]

    def body(*refs):
        srcs, outs = refs[:n], refs[n:2 * n]
        send_sems, recv_sems, local_sems = refs[2 * n:]
        me = _me()
        sib = _flip(me, SIBLING)

        def src_of(a):
            return srcs[a] if layer is None else srcs[a].at[layer]

        def copy(a, k, block_pos, to, src=None):
            dst = outs[a].at[_block_of(block_pos)]
            return pltpu.make_async_remote_copy(
                src_ref=dst if src is None else src, dst_ref=dst, send_sem=send_sems.at[a, k],
                recv_sem=recv_sems.at[a, k], device_id=to, device_id_type=MESH)

        mine = [pltpu.make_async_copy(src_of(a), outs[a].at[_block_of(me)], local_sems.at[a]) for a in range(n)]
        for cp in mine:
            cp.start()
        first = []
        for a in range(n):
            first.append(copy(a, 0, me, sib, src=src_of(a)))
            for j, k in enumerate(OTHER_CHIPS):
                first.append(copy(a, 1 + j, me, _flip(me, k), src=src_of(a)))
        for cp in first:
            cp.start()
        passed = []
        for j, k in enumerate(OTHER_CHIPS):
            for a in range(n):
                copy(a, 1 + j, _flip(me, k), me).wait_recv()
                fw = copy(a, 4 + j, _flip(me, k), sib)
                fw.start()
                passed.append(fw)
        for a in range(n):
            copy(a, 0, sib, me).wait_recv()
            for j, k in enumerate(OTHER_CHIPS):
                copy(a, 4 + j, _flip(sib, k), me).wait_recv()
        for cp in first + passed:
            cp.wait_send()
        for cp in mine:
            cp.wait()

    return pl.pallas_call(
        body, name=name, in_specs=[ANY] * n, out_specs=[ANY] * n,
        out_shape=[jax.ShapeDtypeStruct((N_DEV, *s), a.dtype) for s, a in zip(shapes, arrays)],
        scratch_shapes=[pltpu.SemaphoreType.DMA((n, 7)), pltpu.SemaphoreType.DMA((n, 7)), pltpu.SemaphoreType.DMA((n,))],
    )(*arrays)


def _exchange_blocks(name, parts, recvs, layer):
    n = len(parts)

    def body(*refs):
        srcs, outs = refs[:n], refs[2 * n:3 * n]
        send_sems, recv_sems, local_sems = refs[3 * n:]
        me = _me()
        my_block = _block_of(me)

        def copy(a, k):
            peer = _flip(me, k)
            return pltpu.make_async_remote_copy(
                src_ref=srcs[a].at[_block_of(peer)], dst_ref=outs[a].at[layer, my_block],
                send_sem=send_sems.at[a, k - 1], recv_sem=recv_sems.at[a, k - 1], device_id=peer, device_id_type=MESH)

        def landing(a, k):
            peer = _flip(me, k)
            dst = outs[a].at[layer, _block_of(peer)]
            return pltpu.make_async_remote_copy(
                src_ref=dst, dst_ref=dst, send_sem=send_sems.at[a, k - 1], recv_sem=recv_sems.at[a, k - 1],
                device_id=peer, device_id_type=MESH)

        mine = [pltpu.make_async_copy(srcs[a].at[my_block], outs[a].at[layer, my_block], local_sems.at[a]) for a in range(n)]
        for cp in mine:
            cp.start()
        sends = [copy(a, k) for a in range(n) for k in range(1, N_DEV)]
        for cp in sends:
            cp.start()
        for a in range(n):
            for k in range(1, N_DEV):
                landing(a, k).wait_recv()
        for cp in sends:
            cp.wait_send()
        for cp in mine:
            cp.wait()

    return pl.pallas_call(
        body, name=name, in_specs=[ANY] * (2 * n), out_specs=[ANY] * n,
        out_shape=[jax.ShapeDtypeStruct(r.shape, r.dtype) for r in recvs],
        input_output_aliases={n + a: a for a in range(n)},
        scratch_shapes=[pltpu.SemaphoreType.DMA((n, 7)), pltpu.SemaphoreType.DMA((n, 7)), pltpu.SemaphoreType.DMA((n,))],
    )(*parts, *recvs)


def _layer_fwd(xc, l, W, P, dims):
    T, D, C, I, F = dims
    h = _rms_fwd(xc, P["norm_mix_g"][l:l + 1])
    u = _mm_plain("mm_u", h, W["w_in"][l].reshape(I, D), "NT", [BF16], tn=1280)[0]
    a1, s2 = _convs_fwd(u, P["conf_dw"][l], P["sconv_w"][l], C)
    a3 = _ln_silu(a1, P["conf_ln_g"][l:l + 1], P["conf_ln_b"][l:l + 1])
    o, lse = _attn_fwd(u, P["sinks"][l], C)
    ya = _mm_nn_colblocked("mm_branch_out", a3, W["w_conf_out"][l], [BF16], tm=2048)[0]
    yb = _mm_nn_colblocked("mm_branch_out", s2, W["w_sconv_out"][l], [BF16], tm=2048)[0]
    yc = _mm_nn_colblocked("mm_branch_out", o, W["w_attn_out"][l], [BF16], tm=2048)[0]
    merged = _merge_fwd(u, P["gate_b"][l], ya, yb, yc)
    x1 = _mm_plain("mm_mix", merged, W["w_mix_out"][l].reshape(D, D), "NN", [F32], _epi_resid, [xc], tk=1024)[0]
    h2 = _rms_fwd(x1, P["norm_ffn_g"][l:l + 1])
    up, act = _mm_nn_colblocked("mm_up", h2, W["w_up"][l], [BF16, BF16], _epi_relu2)
    x2 = _mm_plain("mm_down", act, W["w_down"][l].reshape(F, D), "NN", [F32], _epi_resid, [x1], tk=1024)[0]
    saved = dict(xc=xc, h=h, u=u, a1=a1, a3=a3, s2=s2, o=o, lse=lse, ya=ya, yb=yb, yc=yc, merged=merged, x1=x1, h2=h2,
                 up=up, act=act)
    return x2, saved


def _layer_bwd(dx2, dx2_b, l, W, P, S, dims):
    T, D, C, I, F = dims
    d_up = _mm_plain("mm_d_up", dx2_b, W["w_down"][l].reshape(F, D), "NT", [BF16], _epi_drelu2, [S["up"]])[0]
    g_down = _mm_plain("mm_g_down", S["act"], dx2_b, "TN", [BF16])[0]
    dh2 = _mm_nt_colblocked("mm_d_h2", d_up, W["w_up"][l], F32)
    g_up = _mm_tn_colblocked_out("mm_g_up", S["h2"], d_up, BF16)
    dx1, dx1_b, dg_ffn = _rms_bwd(dh2, S["x1"], P["norm_ffn_g"][l:l + 1], dx2)
    dm = _mm_plain("mm_d_merged", dx1_b, W["w_mix_out"][l].reshape(D, D), "NT", [BF16])[0]
    g_mix = _mm_plain("mm_g_mix", S["merged"], dx1_b, "TN", [BF16])[0]
    d_ya, d_yb, d_yc, dg0, dg1, dg2, d_gate_b = _merge_bwd(S["u"], P["gate_b"][l], S["ya"], S["yb"], S["yc"], dm)
    d_a3 = _mm_nt_colblocked("mm_d_branch", d_ya, W["w_conf_out"][l], BF16, tm=2048)
    d_s2 = _mm_nt_colblocked("mm_d_branch", d_yb, W["w_sconv_out"][l], BF16, tm=2048)
    d_o = _mm_nt_colblocked("mm_d_branch", d_yc, W["w_attn_out"][l], BF16, tm=2048)
    g_conf = _mm_tn_colblocked_out("mm_g_branch", S["a3"], d_ya, BF16)
    g_sconv = _mm_tn_colblocked_out("mm_g_branch", S["s2"], d_yb, BF16)
    g_attn = _mm_tn_colblocked_out("mm_g_branch", S["o"], d_yc, BF16)
    d_a1, d_ln_g, d_ln_b = _ln_silu_bwd(S["a1"], P["conf_ln_g"][l:l + 1], P["conf_ln_b"][l:l + 1], d_a3)
    d_av, d_ag, d_bg, d_cg, d_bh, d_conf_dw, d_sconv_w = _convs_bwd(S["u"], P["conf_dw"][l], P["sconv_w"][l], d_a1, d_s2, C)
    dq, dk, dv, d_sinks = _attn_bwd(S["u"], S["o"], S["lse"], d_o, P["sinks"][l], C)
    du = jnp.concatenate([d_av, d_ag, d_bg, d_cg, d_bh, dq, dk, dv, dg0, dg1, dg2], axis=1)
    g_in = _mm_plain("mm_g_in", du, S["h"], "TN", [BF16], tm=1280)[0]
    dh = _mm_plain("mm_d_h", du, W["w_in"][l].reshape(I, D), "NN", [F32], tk=1280)[0]
    dx, dx_b, dg_mix = _rms_bwd(dh, S["xc"], P["norm_mix_g"][l:l + 1], dx1)
    big = dict(w_in=g_in.reshape(N_DEV, I // N_DEV, D), w_conf_out=g_conf, w_sconv_out=g_sconv, w_attn_out=g_attn,
               w_mix_out=g_mix.reshape(N_DEV, D // N_DEV, D), w_up=g_up, w_down=g_down.reshape(N_DEV, F // N_DEV, D))
    small = dict(norm_mix_g=dg_mix, gate_b=d_gate_b, conf_dw=d_conf_dw, conf_ln_g=d_ln_g, conf_ln_b=d_ln_b,
                 sconv_w=d_sconv_w, sinks=d_sinks[0:1], norm_ffn_g=dg_ffn)
    return dx, dx_b, big, small


BIG = ("w_in", "w_conf_out", "w_sconv_out", "w_attn_out", "w_mix_out", "w_up", "w_down")
SMALL_PER_LAYER = ("norm_mix_g", "gate_b", "conf_dw", "conf_ln_g", "conf_ln_b", "sconv_w", "sinks", "norm_ffn_g")
WEIGHTS = ("norm_mix_g", "w_in", "gate_b", "conf_dw", "conf_ln_g", "conf_ln_b", "w_conf_out", "sconv_w", "w_sconv_out",
           "sinks", "w_attn_out", "w_mix_out", "norm_ffn_g", "w_up", "w_down", "final_g")


def _rows(a, width):
    flat = a.reshape(-1)
    pad = (-flat.shape[0]) % width
    if pad:
        flat = jnp.concatenate([flat, jnp.zeros((pad,), flat.dtype)])
    return flat.reshape(-1, width)


def _as3d(a):
    if a.ndim == 1:
        return a.reshape(1, 1, -1)
    if a.ndim == 2:
        return a.reshape(1, *a.shape)
    return a


def kernel(x, norm_mix_g, w_in, gate_b, conf_dw, conf_ln_g, conf_ln_b, w_conf_out, sconv_w, w_sconv_out, sinks, w_attn_out, w_mix_out, norm_ffn_g, w_up, w_down, final_g, loss_target, m_norm_mix_g, m_w_in, m_gate_b, m_conf_dw, m_conf_ln_g, m_conf_ln_b, m_w_conf_out, m_sconv_w, m_w_sconv_out, m_sinks, m_w_attn_out, m_w_mix_out, m_norm_ffn_g, m_w_up, m_w_down, m_final_g, v_norm_mix_g, v_w_in, v_gate_b, v_conf_dw, v_conf_ln_g, v_conf_ln_b, v_w_conf_out, v_sconv_w, v_w_sconv_out, v_sinks, v_w_attn_out, v_w_mix_out, v_norm_ffn_g, v_w_up, v_w_down, v_final_g):
    w = dict(norm_mix_g=norm_mix_g, w_in=w_in, gate_b=gate_b, conf_dw=conf_dw, conf_ln_g=conf_ln_g, conf_ln_b=conf_ln_b,
             w_conf_out=w_conf_out, sconv_w=sconv_w, w_sconv_out=w_sconv_out, sinks=sinks, w_attn_out=w_attn_out,
             w_mix_out=w_mix_out, norm_ffn_g=norm_ffn_g, w_up=w_up, w_down=w_down, final_g=final_g)
    mom = dict(norm_mix_g=m_norm_mix_g, w_in=m_w_in, gate_b=m_gate_b, conf_dw=m_conf_dw, conf_ln_g=m_conf_ln_g,
               conf_ln_b=m_conf_ln_b, w_conf_out=m_w_conf_out, sconv_w=m_sconv_w, w_sconv_out=m_w_sconv_out,
               sinks=m_sinks, w_attn_out=m_w_attn_out, w_mix_out=m_w_mix_out, norm_ffn_g=m_norm_ffn_g, w_up=m_w_up,
               w_down=m_w_down, final_g=m_final_g)
    var = dict(norm_mix_g=v_norm_mix_g, w_in=v_w_in, gate_b=v_gate_b, conf_dw=v_conf_dw, conf_ln_g=v_conf_ln_g,
               conf_ln_b=v_conf_ln_b, w_conf_out=v_w_conf_out, sconv_w=v_sconv_w, w_sconv_out=v_w_sconv_out,
               sinks=v_sinks, w_attn_out=v_w_attn_out, w_mix_out=v_w_mix_out, norm_ffn_g=v_norm_ffn_g, w_up=v_w_up,
               w_down=v_w_down, final_g=v_final_g)

    _, T, D = x.shape
    L = w_in.shape[0]
    C = D // 2
    I = w_in.shape[2] * N_DEV
    F = w_up.shape[2] * N_DEV
    dims = (T, D, C, I, F)
    my_block = _block_of(_me())

    local = {k: _cast_bf16(w[k]) for k in BIG}
    local["w_in"] = jnp.swapaxes(local["w_in"], 1, 2)
    W = {k: [] for k in BIG}
    for l in range(L):
        got = _all_gather("gather_weights", [local[k] for k in BIG], layer=l)
        for k, g in zip(BIG, got):
            W[k].append(g)

    shard_names = ("gate_b", "conf_dw", "sconv_w")
    packed = jnp.concatenate([_rows(w[k], LANES) for k in shard_names], axis=0)
    pad_rows = (-packed.shape[0]) % 8
    if pad_rows:
        packed = jnp.concatenate([packed, jnp.zeros((pad_rows, LANES), F32)], axis=0)
    gathered = _all_gather("gather_small", [packed])[0]
    P = dict(norm_mix_g=norm_mix_g, conf_ln_g=conf_ln_g, conf_ln_b=conf_ln_b, sinks=sinks, norm_ffn_g=norm_ffn_g)
    row0 = 0
    for k in shard_names:
        n_el = w[k].size
        nrow = -(-n_el // LANES)
        part = gathered[:, row0:row0 + nrow].reshape(N_DEV, -1)[:, :n_el].reshape(N_DEV, *w[k].shape)
        P[k] = jnp.moveaxis(part, 0, 2).reshape(*w[k].shape[:2], N_DEV * w[k].shape[2])
        row0 += nrow

    xc = x.reshape(T, D)
    saved = []
    for l in range(L):
        xc, S = _layer_fwd(xc, l, W, P, dims)
        saved.append(S)
    dx, dx_b, d_final_g, loss_tile = _loss_head(xc, final_g.reshape(1, D), loss_target.reshape(T, D))

    recv = {k: lax.empty((L, *W[k][0].shape), BF16) for k in BIG}
    small_grads = [None] * L
    for l in reversed(range(L)):
        dx, dx_b, big, small = _layer_bwd(dx, dx_b, l, W, P, saved[l], dims)
        got = _exchange_blocks("exchange_grads", [big[k] for k in BIG], [recv[k] for k in BIG], l)
        for k, g in zip(BIG, got):
            recv[k] = g
        small_grads[l] = small

    width = C
    pieces = [_rows(small_grads[l][k], width) for l in range(L) for k in SMALL_PER_LAYER]
    pieces += [_rows(d_final_g, width), _rows(loss_tile[0:1, 0:1], width)]
    partial = jnp.concatenate(pieces, axis=0)
    pad_rows = (-partial.shape[0]) % 16
    if pad_rows:
        partial = jnp.concatenate([partial, jnp.zeros((pad_rows, width), F32)], axis=0)
    everyone = _all_gather("gather_small_grads", [partial])[0]
    total = _sum8(everyone.reshape(1, *everyone.shape))[0]
    grads = {}
    row0 = 0
    per_layer = {k: [] for k in SMALL_PER_LAYER}
    for l in range(L):
        for k in SMALL_PER_LAYER:
            shape = small_grads[l][k].shape
            n_el = small_grads[l][k].size
            nrow = -(-n_el // width)
            per_layer[k].append(total[row0:row0 + nrow].reshape(-1)[:n_el].reshape(shape))
            row0 += nrow
    nrow = -(-D // width)
    grads["final_g"] = total[row0:row0 + nrow].reshape(-1)[:D]
    row0 += nrow
    loss = total[row0, 0]
    for k in SMALL_PER_LAYER:
        full = jnp.stack(per_layer[k], axis=0)
        if k in shard_names:
            shard = w[k].shape[2]
            full = lax.dynamic_slice_in_dim(full, my_block * shard, shard, axis=2)
        grads[k] = full.reshape(w[k].shape)

    delta, new_m, new_v = {}, {}, {}
    for k in BIG:
        if k == "w_in":
            g_t = _sum8(recv[k])
            grads[k] = jnp.swapaxes(g_t, 1, 2)
            delta[k], new_m[k], new_v[k] = _adamw(w[k], grads[k], mom[k], var[k])
        else:
            grads[k], delta[k], new_m[k], new_v[k] = _sum8_adamw(recv[k], w[k], mom[k], var[k])
    for k in WEIGHTS:
        if k in BIG:
            continue
        d, mn, vn = _adamw(_as3d(w[k]), _as3d(grads[k]), _as3d(mom[k]), _as3d(var[k]))
        delta[k], new_m[k], new_v[k] = d.reshape(w[k].shape), mn.reshape(w[k].shape), vn.reshape(w[k].shape)

    return (loss, dx.reshape(1, T, D), *[grads[k] for k in WEIGHTS], *[delta[k] for k in WEIGHTS],
            *[new_m[k] for k in WEIGHTS], *[new_v[k] for k in WEIGHTS])
```

```python
import functools

import jax
import jax.numpy as jnp
from jax import lax
from jax.experimental import pallas as pl
from jax.experimental.pallas import tpu as pltpu

F32 = jnp.float32
BF16 = jnp.bfloat16

N_DEV = 8
HEAD_DIM = 64
N_KV_HEADS = 4
ATTN_BLOCK = 128
CONF_KERNEL = 31
SCONV_KERNEL = 3
N_BRANCH = 3
RMS_EPS = 1e-6
LN_EPS = 1e-5
ADAM_LR = 0.001
ADAM_B1 = 0.9
ADAM_B2 = 0.999
ADAM_EPS = 1e-08
ADAM_WD = 0.01
ADAM_STEP = 10
LANES = 128
NEG_BIG = -1e30
VMEM_LIMIT_BYTES = 56 * 1024 * 1024
MESH = pl.DeviceIdType.MESH

NN = (((1,), (0,)), ((), ()))
NT = (((1,), (1,)), ((), ()))
TN = (((0,), (0,)), ((), ()))


def _pick(n, cap, mult=LANES):
    best = None
    for d in range(mult, min(n, cap) + 1, mult):
        if n % d == 0:
            best = d
    assert best is not None, (n, cap, mult)
    return best


def _sigmoid(x):
    return 1.0 / (1.0 + jnp.exp(-x))


def _params(sem):
    return pltpu.CompilerParams(dimension_semantics=sem, vmem_limit_bytes=VMEM_LIMIT_BYTES)


def _epi_cast(p, ex, outs):
    outs[0][...] = p.astype(outs[0].dtype)


def _epi_resid(p, ex, outs):
    outs[0][...] = ex[0][...] + p


def _epi_relu2(p, ex, outs):
    outs[0][...] = p.astype(outs[0].dtype)
    r = jnp.maximum(p, 0.0)
    outs[1][...] = (r * r).astype(outs[1].dtype)


def _epi_drelu2(p, ex, outs):
    up = ex[0][...].astype(F32)
    outs[0][...] = (p * (2.0 * jnp.maximum(up, 0.0))).astype(outs[0].dtype)


TOKEN = (8, LANES)


def _matmul(name, a, b, dnums, grid, a_spec, b_spec, out_shape, out_specs, epi, acc_shape, extra=(), extra_specs=(),
            dep=None):
    if dep is not None:
        extra = [*extra, dep]
        extra_specs = [*extra_specs, pl.BlockSpec(TOKEN, lambda j, i, k: (0, 0))]
    nk = grid[2]
    n_extra, n_out = len(extra), len(out_shape)

    def body(*refs):
        a_ref, b_ref = refs[0], refs[1]
        ex = refs[2:2 + n_extra]
        outs = refs[2 + n_extra:2 + n_extra + n_out]
        p = lax.dot_general(a_ref[...], b_ref[...], dnums, preferred_element_type=F32)
        if nk == 1:
            epi(p, ex, outs)
        else:
            acc = refs[-1]
            k = pl.program_id(2)

            @pl.when(k == 0)
            def _():
                acc[...] = p

            @pl.when(k > 0)
            def _():
                acc[...] += p

            @pl.when(k == nk - 1)
            def _():
                epi(acc[...], ex, outs)

    scratch = [pltpu.VMEM(acc_shape, F32)] if nk > 1 else []
    return pl.pallas_call(
        body, name=name, grid=grid, in_specs=[a_spec, b_spec, *extra_specs], out_specs=list(out_specs),
        out_shape=list(out_shape), scratch_shapes=scratch,
        compiler_params=_params(("parallel", "parallel", "arbitrary")))(a, b, *extra)


def _mm_plain(name, a, b, form, out_dtypes, epi=_epi_cast, extra=(), tm=1024, tn=1024, tk=2048, dep=None):
    if form == "NN":
        (M, K), N = a.shape, b.shape[1]
    elif form == "NT":
        (M, K), N = a.shape, b.shape[0]
    else:
        (K, M), N = a.shape, b.shape[1]
    tm, tn, tk = _pick(M, tm, 8), _pick(N, tn), _pick(K, tk)
    grid = (N // tn, M // tm, K // tk)
    if form == "TN":
        a_spec = pl.BlockSpec((tk, tm), lambda j, i, k: (k, i))
    else:
        a_spec = pl.BlockSpec((tm, tk), lambda j, i, k: (i, k))
    if form == "NT":
        b_spec = pl.BlockSpec((tn, tk), lambda j, i, k: (j, k))
    else:
        b_spec = pl.BlockSpec((tk, tn), lambda j, i, k: (k, j))
    o_spec = pl.BlockSpec((tm, tn), lambda j, i, k: (i, j))
    dn = {"NN": NN, "NT": NT, "TN": TN}[form]
    return _matmul(name, a, b, dn, grid, a_spec, b_spec,
                   [jax.ShapeDtypeStruct((M, N), dt) for dt in out_dtypes], [o_spec] * len(out_dtypes), epi,
                   (tm, tn), extra, [o_spec] * len(extra), dep)


def _mm_nn_colblocked(name, a, bb, out_dtypes, epi=_epi_cast, tm=1024, tn=1024, tk=2048):
    M, K = a.shape
    ns = bb.shape[2]
    N = N_DEV * ns
    tm, tn, tk = _pick(M, tm, 8), _pick(ns, tn), _pick(K, tk)
    q = ns // tn
    grid = (N // tn, M // tm, K // tk)
    a_spec = pl.BlockSpec((tm, tk), lambda j, i, k: (i, k))
    b_spec = pl.BlockSpec((None, tk, tn), lambda j, i, k: (j // q, k, j % q))
    o_spec = pl.BlockSpec((tm, tn), lambda j, i, k: (i, j))
    return _matmul(name, a, bb, NN, grid, a_spec, b_spec,
                   [jax.ShapeDtypeStruct((M, N), dt) for dt in out_dtypes], [o_spec] * len(out_dtypes), epi, (tm, tn))


def _mm_nt_colblocked(name, a, bb, out_dtype, tm=1024, tn=1024, tk=1024):
    M, N = a.shape
    K, ns = bb.shape[1], bb.shape[2]
    tm, tn, tk = _pick(M, tm, 8), _pick(K, tn), _pick(ns, tk)
    q = ns // tk
    grid = (K // tn, M // tm, N // tk)
    a_spec = pl.BlockSpec((tm, tk), lambda j, i, k: (i, k))
    b_spec = pl.BlockSpec((None, tn, tk), lambda j, i, k: (k // q, j, k % q))
    o_spec = pl.BlockSpec((tm, tn), lambda j, i, k: (i, j))
    return _matmul(name, a, bb, NT, grid, a_spec, b_spec, [jax.ShapeDtypeStruct((M, K), out_dtype)], [o_spec],
                   _epi_cast, (tm, tn))[0]


def _mm_tn_colblocked_out(name, a, b, out_dtype, tm=1024, tn=1024, tk=2048):
    T, M = a.shape
    N = b.shape[1]
    ns = N // N_DEV
    tm, tn, tk = _pick(M, tm, 8), _pick(ns, tn), _pick(T, tk)
    q = ns // tn
    grid = (N // tn, M // tm, T // tk)
    a_spec = pl.BlockSpec((tk, tm), lambda j, i, k: (k, i))
    b_spec = pl.BlockSpec((tk, tn), lambda j, i, k: (k, j))
    o_spec = pl.BlockSpec((None, tm, tn), lambda j, i, k: (j // q, i, j % q))
    return _matmul(name, a, b, TN, grid, a_spec, b_spec, [jax.ShapeDtypeStruct((N_DEV, M, ns), out_dtype)], [o_spec],
                   _epi_cast, (tm, tn))[0]


ROW_TILE = 256


def _rms_fwd(x, g):
    T, D = x.shape
    tr = _pick(T, ROW_TILE, 8)

    def body(x_ref, g_ref, h_ref):
        xv = x_ref[...]
        r = lax.rsqrt(jnp.mean(xv * xv, axis=-1, keepdims=True) + RMS_EPS)
        h_ref[...] = (xv * r * g_ref[...]).astype(BF16)

    return pl.pallas_call(
        body, name="rms_fwd", grid=(T // tr,),
        in_specs=[pl.BlockSpec((tr, D), lambda i: (i, 0)), pl.BlockSpec((1, D), lambda i: (0, 0))],
        out_specs=pl.BlockSpec((tr, D), lambda i: (i, 0)),
        out_shape=jax.ShapeDtypeStruct((T, D), BF16), compiler_params=_params(("parallel",)))(x, g)


def _rms_bwd_math(dh, xv, g):
    r = lax.rsqrt(jnp.mean(xv * xv, axis=-1, keepdims=True) + RMS_EPS)
    gdh = dh * g
    dot = jnp.mean(gdh * xv, axis=-1, keepdims=True)
    dx = r * gdh - xv * (r * r * r * dot)
    return dx, dh * xv * r


def _rms_bwd(dh, x, g, dres):
    T, D = x.shape
    tr = _pick(T, ROW_TILE, 8)

    def body(dh_ref, x_ref, g_ref, dres_ref, dx_ref, dxb_ref, dg_ref):
        dx, dgrow = _rms_bwd_math(dh_ref[...], x_ref[...], g_ref[...])
        dx = dx + dres_ref[...]
        dx_ref[...] = dx
        dxb_ref[...] = dx.astype(BF16)
        part = jnp.sum(dgrow, axis=0, keepdims=True)

        @pl.when(pl.program_id(0) == 0)
        def _():
            dg_ref[...] = part

        @pl.when(pl.program_id(0) > 0)
        def _():
            dg_ref[...] += part

    row = pl.BlockSpec((tr, D), lambda i: (i, 0))
    vec = pl.BlockSpec((1, D), lambda i: (0, 0))
    return pl.pallas_call(
        body, name="rms_bwd", grid=(T // tr,), in_specs=[row, row, vec, row], out_specs=[row, row, vec],
        out_shape=[jax.ShapeDtypeStruct((T, D), F32), jax.ShapeDtypeStruct((T, D), BF16),
                   jax.ShapeDtypeStruct((1, D), F32)],
        compiler_params=_params(("arbitrary",)))(dh, x, g, dres)


def _loss_head(x, g, target):
    T, D = x.shape
    tr = _pick(T, ROW_TILE, 8)

    def body(x_ref, g_ref, t_ref, dx_ref, dxb_ref, dg_ref, loss_ref):
        xv, gv = x_ref[...], g_ref[...]
        r = lax.rsqrt(jnp.mean(xv * xv, axis=-1, keepdims=True) + RMS_EPS)
        err = xv * r * gv - t_ref[...]
        part_loss = 0.5 * jnp.sum(jnp.mean(err * err, axis=-1, keepdims=True), axis=0, keepdims=True)
        dx, dgrow = _rms_bwd_math(err * (1.0 / D), xv, gv)
        dx_ref[...] = dx
        dxb_ref[...] = dx.astype(BF16)
        part = jnp.sum(dgrow, axis=0, keepdims=True)
        lpart = jnp.broadcast_to(part_loss, (8, LANES))

        @pl.when(pl.program_id(0) == 0)
        def _():
            dg_ref[...] = part
            loss_ref[...] = lpart

        @pl.when(pl.program_id(0) > 0)
        def _():
            dg_ref[...] += part
            loss_ref[...] += lpart

    row = pl.BlockSpec((tr, D), lambda i: (i, 0))
    vec = pl.BlockSpec((1, D), lambda i: (0, 0))
    lsp = pl.BlockSpec((8, LANES), lambda i: (0, 0))
    return pl.pallas_call(
        body, name="loss_head", grid=(T // tr,), in_specs=[row, vec, row], out_specs=[row, row, vec, lsp],
        out_shape=[jax.ShapeDtypeStruct((T, D), F32), jax.ShapeDtypeStruct((T, D), BF16),
                   jax.ShapeDtypeStruct((1, D), F32), jax.ShapeDtypeStruct((8, LANES), F32)],
        compiler_params=_params(("arbitrary",)))(x, g, target)


def _ln_math(a1, g, b):
    mu = jnp.mean(a1, axis=-1, keepdims=True)
    xc = a1 - mu
    rstd = lax.rsqrt(jnp.mean(xc * xc, axis=-1, keepdims=True) + LN_EPS)
    xhat = xc * rstd
    return xhat, rstd, xhat * g + b


def _ln_silu(a1, g, b):
    T, C = a1.shape
    tr = _pick(T, ROW_TILE, 8)

    def body(a_ref, g_ref, b_ref, o_ref):
        _, _, y = _ln_math(a_ref[...], g_ref[...], b_ref[...])
        o_ref[...] = (y * _sigmoid(y)).astype(BF16)

    row = pl.BlockSpec((tr, C), lambda i: (i, 0))
    vec = pl.BlockSpec((1, C), lambda i: (0, 0))
    return pl.pallas_call(body, name="ln_silu", grid=(T // tr,), in_specs=[row, vec, vec], out_specs=row,
                          out_shape=jax.ShapeDtypeStruct((T, C), BF16), compiler_params=_params(("parallel",)))(a1, g, b)


def _ln_silu_bwd(a1, g, b, d_a3):
    T, C = a1.shape
    tr = _pick(T, ROW_TILE, 8)

    def body(a_ref, g_ref, b_ref, d_ref, da_ref, dg_ref, db_ref):
        gv = g_ref[...]
        xhat, rstd, y = _ln_math(a_ref[...], gv, b_ref[...])
        s = _sigmoid(y)
        dy = d_ref[...].astype(F32) * (s * (1.0 + y * (1.0 - s)))
        dxh = dy * gv
        m1 = jnp.mean(dxh, axis=-1, keepdims=True)
        m2 = jnp.mean(dxh * xhat, axis=-1, keepdims=True)
        da_ref[...] = rstd * (dxh - m1 - xhat * m2)
        pg = jnp.sum(dy * xhat, axis=0, keepdims=True)
        pb = jnp.sum(dy, axis=0, keepdims=True)

        @pl.when(pl.program_id(0) == 0)
        def _():
            dg_ref[...] = pg
            db_ref[...] = pb

        @pl.when(pl.program_id(0) > 0)
        def _():
            dg_ref[...] += pg
            db_ref[...] += pb

    row = pl.BlockSpec((tr, C), lambda i: (i, 0))
    vec = pl.BlockSpec((1, C), lambda i: (0, 0))
    return pl.pallas_call(
        body, name="ln_silu_bwd", grid=(T // tr,), in_specs=[row, vec, vec, row], out_specs=[row, vec, vec],
        out_shape=[jax.ShapeDtypeStruct((T, C), F32), jax.ShapeDtypeStruct((1, C), F32),
                   jax.ShapeDtypeStruct((1, C), F32)],
        compiler_params=_params(("arbitrary",)))(a1, g, b, d_a3)


CONV_ROWS = 128
PAD_A = 32
PAD_B = 8


def _u_block(T, first):
    return pl.BlockSpec((T, LANES), lambda i: (0, first + i))


def _causal_conv(xpad_ref, w_ref, ksize, pad, T, emit):
    for r0 in range(0, T, CONV_ROWS):
        acc = None
        for j in range(ksize):
            off = pad - (ksize - 1) + j + r0
            term = w_ref[j:j + 1, :] * xpad_ref[off:off + CONV_ROWS, :]
            acc = term if acc is None else acc + term
        emit(r0, acc)


def _anticausal_conv(gpad_ref, w_ref, ksize, T, emit):
    for r0 in range(0, T, CONV_ROWS):
        acc = None
        for j in range(ksize):
            off = (ksize - 1) - j + r0
            term = w_ref[j:j + 1, :] * gpad_ref[off:off + CONV_ROWS, :]
            acc = term if acc is None else acc + term
        emit(r0, acc)


def _conv_wgrad(xpad_ref, g_ref, dw_ref, ksize, pad, T):
    for j in range(ksize):
        acc = None
        for r0 in range(0, T, CONV_ROWS):
            off = pad - (ksize - 1) + j + r0
            term = g_ref[r0:r0 + CONV_ROWS, :] * xpad_ref[off:off + CONV_ROWS, :]
            term = jnp.sum(term.reshape(CONV_ROWS // 8, 8, LANES), axis=0)
            acc = term if acc is None else acc + term
        dw_ref[j:j + 1, :] = jnp.sum(acc, axis=0, keepdims=True)


def _convs_fwd(u, conf_dw, sconv_w, C):
    T = u.shape[0]
    nb = C // LANES

    def body(av_ref, ag_ref, bg_ref, cg_ref, bh_ref, dw_ref, sw_ref, a1_ref, s2_ref, xa, xs, s1):
        xa[0:PAD_A, :] = jnp.zeros((PAD_A, LANES), F32)
        xa[PAD_A:PAD_A + T, :] = av_ref[...].astype(F32) * _sigmoid(ag_ref[...].astype(F32))

        def emit_a(r0, acc):
            a1_ref[r0:r0 + CONV_ROWS, :] = acc

        _causal_conv(xa, dw_ref, CONF_KERNEL, PAD_A, T, emit_a)

        xs[0:PAD_B, :] = jnp.zeros((PAD_B, LANES), F32)
        xs[PAD_B:PAD_B + T, :] = cg_ref[...].astype(F32) * bh_ref[...].astype(F32)

        def emit_b(r0, acc):
            s1[r0:r0 + CONV_ROWS, :] = acc

        _causal_conv(xs, sw_ref, SCONV_KERNEL, PAD_B, T, emit_b)
        s2_ref[...] = (bg_ref[...].astype(F32) * s1[...]).astype(BF16)

    col = pl.BlockSpec((T, LANES), lambda i: (0, i))
    return pl.pallas_call(
        body, name="convs_fwd", grid=(nb,),
        in_specs=[_u_block(T, 0), _u_block(T, nb), _u_block(T, 2 * nb), _u_block(T, 3 * nb), _u_block(T, 4 * nb),
                  pl.BlockSpec((CONF_KERNEL, LANES), lambda i: (0, i)),
                  pl.BlockSpec((SCONV_KERNEL, LANES), lambda i: (0, i))],
        out_specs=[col, col],
        out_shape=[jax.ShapeDtypeStruct((T, C), F32), jax.ShapeDtypeStruct((T, C), BF16)],
        scratch_shapes=[pltpu.VMEM((T + PAD_A, LANES), F32), pltpu.VMEM((T + PAD_B, LANES), F32),
                        pltpu.VMEM((T, LANES), F32)],
        compiler_params=_params(("parallel",)))(u, u, u, u, u, conf_dw, sconv_w)


def _convs_bwd(u, conf_dw, sconv_w, d_a1, d_s2, C):
    T = u.shape[0]
    nb = C // LANES

    def body(av_ref, ag_ref, bg_ref, cg_ref, bh_ref, dw_ref, sw_ref, da1_ref, ds2_ref,
             dav_ref, dag_ref, dbg_ref, dcg_ref, dbh_ref, ddw_ref, dsw_ref, xa, ga, xs, gs, tmp):
        av = av_ref[...].astype(F32)
        sg = _sigmoid(ag_ref[...].astype(F32))
        xa[0:PAD_A, :] = jnp.zeros((PAD_A, LANES), F32)
        xa[PAD_A:PAD_A + T, :] = av * sg
        ga[0:T, :] = da1_ref[...]
        ga[T:T + PAD_A, :] = jnp.zeros((PAD_A, LANES), F32)
        _conv_wgrad(xa, ga, ddw_ref, CONF_KERNEL, PAD_A, T)

        def emit_a(r0, acc):
            tmp[r0:r0 + CONV_ROWS, :] = acc

        _anticausal_conv(ga, dw_ref, CONF_KERNEL, T, emit_a)
        da0 = tmp[...]
        dav_ref[...] = (da0 * sg).astype(BF16)
        dag_ref[...] = (da0 * av * sg * (1.0 - sg)).astype(BF16)

        cg = cg_ref[...].astype(F32)
        bh = bh_ref[...].astype(F32)
        ds2 = ds2_ref[...].astype(F32)
        xs[0:PAD_B, :] = jnp.zeros((PAD_B, LANES), F32)
        xs[PAD_B:PAD_B + T, :] = cg * bh

        def emit_s1(r0, acc):
            tmp[r0:r0 + CONV_ROWS, :] = acc

        _causal_conv(xs, sw_ref, SCONV_KERNEL, PAD_B, T, emit_s1)
        dbg_ref[...] = (ds2 * tmp[...]).astype(BF16)
        gs[0:T, :] = ds2 * bg_ref[...].astype(F32)
        gs[T:T + PAD_B, :] = jnp.zeros((PAD_B, LANES), F32)
        _conv_wgrad(xs, gs, dsw_ref, SCONV_KERNEL, PAD_B, T)

        def emit_b(r0, acc):
            tmp[r0:r0 + CONV_ROWS, :] = acc

        _anticausal_conv(gs, sw_ref, SCONV_KERNEL, T, emit_b)
        ds0 = tmp[...]
        dcg_ref[...] = (ds0 * bh).astype(BF16)
        dbh_ref[...] = (ds0 * cg).astype(BF16)

    col = pl.BlockSpec((T, LANES), lambda i: (0, i))
    wa = pl.BlockSpec((CONF_KERNEL, LANES), lambda i: (0, i))
    wb = pl.BlockSpec((SCONV_KERNEL, LANES), lambda i: (0, i))
    act = jax.ShapeDtypeStruct((T, C), BF16)
    return pl.pallas_call(
        body, name="convs_bwd", grid=(nb,),
        in_specs=[_u_block(T, 0), _u_block(T, nb), _u_block(T, 2 * nb), _u_block(T, 3 * nb), _u_block(T, 4 * nb),
                  wa, wb, col, col],
        out_specs=[col, col, col, col, col, wa, wb],
        out_shape=[act, act, act, act, act, jax.ShapeDtypeStruct((CONF_KERNEL, C), F32),
                   jax.ShapeDtypeStruct((SCONV_KERNEL, C), F32)],
        scratch_shapes=[pltpu.VMEM((T + PAD_A, LANES), F32), pltpu.VMEM((T + PAD_A, LANES), F32),
                        pltpu.VMEM((T + PAD_B, LANES), F32), pltpu.VMEM((T + PAD_B, LANES), F32),
                        pltpu.VMEM((T, LANES), F32)],
        compiler_params=_params(("parallel",)))(u, u, u, u, u, conf_dw, sconv_w, d_a1, d_s2)


def _attn_specs(T, C, q_off_blocks):
    kvw = N_KV_HEADS * HEAD_DIM
    kb = (5 * C + C) // kvw
    qs = pl.BlockSpec((ATTN_BLOCK, C), lambda n: (n, 5))
    kc = pl.BlockSpec((ATTN_BLOCK, kvw), lambda n: (n, kb))
    kp = pl.BlockSpec((ATTN_BLOCK, kvw), lambda n: (jnp.maximum(n - 1, 0), kb))
    vc = pl.BlockSpec((ATTN_BLOCK, kvw), lambda n: (n, kb + 1))
    vp = pl.BlockSpec((ATTN_BLOCK, kvw), lambda n: (jnp.maximum(n - 1, 0), kb + 1))
    return qs, kc, kp, vc, vp


def _attn_masks(n):
    row = lax.broadcasted_iota(jnp.int32, (ATTN_BLOCK, ATTN_BLOCK), 0)
    col = lax.broadcasted_iota(jnp.int32, (ATTN_BLOCK, ATTN_BLOCK), 1)
    return col <= row, jnp.logical_and(col > row, n > 0)


def _attn_scores(qh, kc, kp, mask_c, mask_p):
    scale = HEAD_DIM ** -0.5
    s_c = lax.dot_general(qh, kc, NT, preferred_element_type=F32) * scale
    s_p = lax.dot_general(qh, kp, NT, preferred_element_type=F32) * scale
    return jnp.where(mask_c, s_c, NEG_BIG), jnp.where(mask_p, s_p, NEG_BIG)


def _attn_fwd(u, sinks, C):
    T = u.shape[0]
    H = C // HEAD_DIM
    grp = H // N_KV_HEADS

    def body(sink_ref, q_ref, kc_ref, kp_ref, vc_ref, vp_ref, o_ref, lse_ref):
        n = pl.program_id(0)
        mask_c, mask_p = _attn_masks(n)
        for h in range(H):
            kv = slice((h // grp) * HEAD_DIM, (h // grp + 1) * HEAD_DIM)
            hs = slice(h * HEAD_DIM, (h + 1) * HEAD_DIM)
            sink = sink_ref[h]
            s_c, s_p = _attn_scores(q_ref[:, hs], kc_ref[:, kv], kp_ref[:, kv], mask_c, mask_p)
            m = jnp.maximum(jnp.maximum(jnp.max(s_c, axis=-1, keepdims=True), jnp.max(s_p, axis=-1, keepdims=True)), sink)
            p_c = jnp.exp(s_c - m)
            p_p = jnp.exp(s_p - m)
            den = jnp.sum(p_c, axis=-1, keepdims=True) + jnp.sum(p_p, axis=-1, keepdims=True) + jnp.exp(sink - m)
            acc = jnp.dot(p_c.astype(BF16), vc_ref[:, kv], preferred_element_type=F32)
            acc = acc + jnp.dot(p_p.astype(BF16), vp_ref[:, kv], preferred_element_type=F32)
            o_ref[:, hs] = (acc / den).astype(BF16)
            lse_ref[:, h:h + 1] = m + jnp.log(den)

    qs, kc, kp, vc, vp = _attn_specs(T, C, 5)
    return pl.pallas_call(
        body, name="attn_fwd", grid=(T // ATTN_BLOCK,),
        in_specs=[pl.BlockSpec(memory_space=pltpu.SMEM), qs, kc, kp, vc, vp],
        out_specs=[pl.BlockSpec((ATTN_BLOCK, C), lambda n: (n, 0)), pl.BlockSpec((ATTN_BLOCK, H), lambda n: (n, 0))],
        out_shape=[jax.ShapeDtypeStruct((T, C), BF16), jax.ShapeDtypeStruct((T, H), F32)],
        compiler_params=_params(("parallel",)))(sinks, u, u, u, u, u)


def _attn_bwd(u, o, lse, d_o, sinks, C):
    T = u.shape[0]
    H = C // HEAD_DIM
    grp = H // N_KV_HEADS
    kvw = N_KV_HEADS * HEAD_DIM
    nblk = T // ATTN_BLOCK
    scale = HEAD_DIM ** -0.5

    def body(sink_ref, q_ref, kc_ref, kp_ref, vc_ref, vp_ref, o_ref, lse_ref, do_ref,
             dq_ref, dk_ref, dv_ref, ds_ref, dk_acc, dv_acc):
        n = pl.program_id(0)

        @pl.when(n == 0)
        def _():
            dk_acc[...] = jnp.zeros_like(dk_acc)
            dv_acc[...] = jnp.zeros_like(dv_acc)
            ds_ref[...] = jnp.zeros_like(ds_ref)

        mask_c, mask_p = _attn_masks(n)
        cur = pl.ds(pl.multiple_of(n * ATTN_BLOCK, ATTN_BLOCK), ATTN_BLOCK)
        prev = pl.ds(pl.multiple_of(jnp.maximum(n - 1, 0) * ATTN_BLOCK, ATTN_BLOCK), ATTN_BLOCK)
        for g in range(N_KV_HEADS):
            kv = slice(g * HEAD_DIM, (g + 1) * HEAD_DIM)
            kc, kp, vc, vp = kc_ref[:, kv], kp_ref[:, kv], vc_ref[:, kv], vp_ref[:, kv]
            dk_c = dk_p = dv_c = dv_p = None
            for h in range(g * grp, (g + 1) * grp):
                hs = slice(h * HEAD_DIM, (h + 1) * HEAD_DIM)
                qh, doh = q_ref[:, hs], do_ref[:, hs]
                lse_h = lse_ref[:, h:h + 1]
                s_c, s_p = _attn_scores(qh, kc, kp, mask_c, mask_p)
                p_c = jnp.exp(s_c - lse_h)
                p_p = jnp.exp(s_p - lse_h)
                delta = jnp.sum(doh.astype(F32) * o_ref[:, hs].astype(F32), axis=-1, keepdims=True)
                dp_c = lax.dot_general(doh, vc, NT, preferred_element_type=F32)
                dp_p = lax.dot_general(doh, vp, NT, preferred_element_type=F32)
                ds_c = (p_c * (dp_c - delta) * scale).astype(BF16)
                ds_p = (p_p * (dp_p - delta) * scale).astype(BF16)
                dq = jnp.dot(ds_c, kc, preferred_element_type=F32) + jnp.dot(ds_p, kp, preferred_element_type=F32)
                dq_ref[:, hs] = dq.astype(BF16)
                t_kc = lax.dot_general(ds_c, qh, TN, preferred_element_type=F32)
                t_kp = lax.dot_general(ds_p, qh, TN, preferred_element_type=F32)
                t_vc = lax.dot_general(p_c.astype(BF16), doh, TN, preferred_element_type=F32)
                t_vp = lax.dot_general(p_p.astype(BF16), doh, TN, preferred_element_type=F32)
                dk_c = t_kc if dk_c is None else dk_c + t_kc
                dk_p = t_kp if dk_p is None else dk_p + t_kp
                dv_c = t_vc if dv_c is None else dv_c + t_vc
                dv_p = t_vp if dv_p is None else dv_p + t_vp
                p_sink = jnp.exp(sink_ref[h] - lse_h)
                dsink = -jnp.sum(p_sink * delta, axis=0, keepdims=True)
                ds_ref[:, h:h + 1] += jnp.broadcast_to(dsink, (8, 1))
            dk_acc[cur, kv] += dk_c
            dk_acc[prev, kv] += dk_p
            dv_acc[cur, kv] += dv_c
            dv_acc[prev, kv] += dv_p

        @pl.when(n == nblk - 1)
        def _():
            dk_ref[...] = dk_acc[...].astype(BF16)
            dv_ref[...] = dv_acc[...].astype(BF16)

    qs, kc, kp, vc, vp = _attn_specs(T, C, 5)
    blk = pl.BlockSpec((ATTN_BLOCK, C), lambda n: (n, 0))
    full = pl.BlockSpec((T, kvw), lambda n: (0, 0))
    return pl.pallas_call(
        body, name="attn_bwd", grid=(nblk,),
        in_specs=[pl.BlockSpec(memory_space=pltpu.SMEM), qs, kc, kp, vc, vp, blk,
                  pl.BlockSpec((ATTN_BLOCK, H), lambda n: (n, 0)), blk],
        out_specs=[blk, full, full, pl.BlockSpec((8, H), lambda n: (0, 0))],
        out_shape=[jax.ShapeDtypeStruct((T, C), BF16), jax.ShapeDtypeStruct((T, kvw), BF16),
                   jax.ShapeDtypeStruct((T, kvw), BF16), jax.ShapeDtypeStruct((8, H), F32)],
        scratch_shapes=[pltpu.VMEM((T, kvw), F32), pltpu.VMEM((T, kvw), F32)],
        compiler_params=_params(("arbitrary",)))(sinks, u, u, u, u, u, o, lse, d_o)


MERGE_COLS = 512


def _merge_specs(T, D, I):
    tr = _pick(T, ROW_TILE, 8)
    tc = _pick(D, MERGE_COLS)
    g0 = (I - N_BRANCH * D) // tc
    per = D // tc
    gspecs = [pl.BlockSpec((tr, tc), functools.partial(lambda j, i, b: (i, g0 + b * per + j), b=b)) for b in range(N_BRANCH)]
    tile = pl.BlockSpec((tr, tc), lambda j, i: (i, j))
    bias = pl.BlockSpec((N_BRANCH, tc), lambda j, i: (0, j))
    return tr, tc, gspecs, tile, bias


def _merge_fwd(u, gate_b, ya, yb, yc):
    T, I = u.shape
    D = ya.shape[1]
    tr, tc, gspecs, tile, bias = _merge_specs(T, D, I)

    def body(g0_ref, g1_ref, g2_ref, b_ref, ya_ref, yb_ref, yc_ref, o_ref):
        acc = None
        for b, (g_ref, y_ref) in enumerate(((g0_ref, ya_ref), (g1_ref, yb_ref), (g2_ref, yc_ref))):
            gate = _sigmoid(g_ref[...].astype(F32) + b_ref[b:b + 1, :])
            term = gate * y_ref[...].astype(F32)
            acc = term if acc is None else acc + term
        o_ref[...] = acc.astype(BF16)

    return pl.pallas_call(
        body, name="merge_fwd", grid=(D // tc, T // tr), in_specs=[*gspecs, bias, tile, tile, tile], out_specs=tile,
        out_shape=jax.ShapeDtypeStruct((T, D), BF16),
        compiler_params=_params(("parallel", "parallel")))(u, u, u, gate_b, ya, yb, yc)


def _merge_bwd(u, gate_b, ya, yb, yc, dm):
    T, I = u.shape
    D = ya.shape[1]
    tr, tc, gspecs, tile, bias = _merge_specs(T, D, I)

    def body(g0_ref, g1_ref, g2_ref, b_ref, ya_ref, yb_ref, yc_ref, dm_ref,
             dya_ref, dyb_ref, dyc_ref, dg0_ref, dg1_ref, dg2_ref, db_ref):
        dmv = dm_ref[...].astype(F32)
        first = pl.program_id(1) == 0
        for b, (g_ref, y_ref, dy_ref, dg_ref) in enumerate(((g0_ref, ya_ref, dya_ref, dg0_ref),
                                                           (g1_ref, yb_ref, dyb_ref, dg1_ref),
                                                           (g2_ref, yc_ref, dyc_ref, dg2_ref))):
            gate = _sigmoid(g_ref[...].astype(F32) + b_ref[b:b + 1, :])
            dy_ref[...] = (dmv * gate).astype(BF16)
            dpre = dmv * y_ref[...].astype(F32) * gate * (1.0 - gate)
            dg_ref[...] = dpre.astype(BF16)
            part = jnp.sum(dpre, axis=0, keepdims=True)

            @pl.when(first)
            def _():
                db_ref[b:b + 1, :] = part

            @pl.when(jnp.logical_not(first))
            def _():
                db_ref[b:b + 1, :] += part

    act = jax.ShapeDtypeStruct((T, D), BF16)
    return pl.pallas_call(
        body, name="merge_bwd", grid=(D // tc, T // tr), in_specs=[*gspecs, bias, tile, tile, tile, tile],
        out_specs=[tile] * 6 + [bias], out_shape=[act] * 6 + [jax.ShapeDtypeStruct((N_BRANCH, D), F32)],
        compiler_params=_params(("parallel", "arbitrary")))(u, u, u, gate_b, ya, yb, yc, dm)


ELEMS_PER_TILE = 256 * 1024


def _row_tile(r, c):
    return _pick(r, max(16, ELEMS_PER_TILE // c), 16) if r % 16 == 0 else r


TRANSPOSE_TILE = 256


def _cast_place(w, layer, my_block, transpose=False):
    L, r, c = w.shape
    if transpose:
        tr = _pick(r, TRANSPOSE_TILE)
        out_shape = (N_DEV, c, r)
        out_spec = pl.BlockSpec((None, c, tr), lambda i, blk: (blk[0], 0, i))
    else:
        tr = _row_tile(r, c)
        out_shape = (N_DEV, r, c)
        out_spec = pl.BlockSpec((None, tr, c), lambda i, blk: (blk[0], i, 0))

    def body(blk_ref, w_ref, o_ref):
        wv = w_ref[...].astype(BF16)
        if transpose:
            eye = (lax.broadcasted_iota(jnp.int32, (tr, tr), 0) == lax.broadcasted_iota(jnp.int32, (tr, tr), 1))
            o_ref[...] = lax.dot_general(wv, eye.astype(BF16), TN, preferred_element_type=F32).astype(BF16)
        else:
            o_ref[...] = wv

    return pl.pallas_call(
        body, name="cast_place_t" if transpose else "cast_place",
        grid_spec=pltpu.PrefetchScalarGridSpec(
            num_scalar_prefetch=1, grid=(r // tr,),
            in_specs=[pl.BlockSpec((None, tr, c), lambda i, blk: (layer, i, 0))], out_specs=out_spec),
        out_shape=jax.ShapeDtypeStruct(out_shape, BF16), compiler_params=_params(("parallel",)))(my_block, w)


def _adamw_math(w, g, m, v):
    m = ADAM_B1 * m + (1.0 - ADAM_B1) * g
    v = ADAM_B2 * v + (1.0 - ADAM_B2) * (g * g)
    m_hat = m / (1.0 - ADAM_B1 ** ADAM_STEP)
    v_hat = v / (1.0 - ADAM_B2 ** ADAM_STEP)
    delta = -ADAM_LR * (m_hat / (jnp.sqrt(v_hat) + ADAM_EPS) + ADAM_WD * w)
    return delta, m, v


def _sum_parts(part_ref):
    acc = part_ref[0].astype(F32)
    for s in range(1, N_DEV):
        acc = acc + part_ref[s].astype(F32)
    return acc


def _sum8(parts):
    L, _, r, c = parts.shape
    tr = _row_tile(r, c)

    def body(p_ref, o_ref):
        o_ref[...] = _sum_parts(p_ref)

    return pl.pallas_call(
        body, name="sum8", grid=(L, r // tr),
        in_specs=[pl.BlockSpec((None, N_DEV, tr, c), lambda l, i: (l, 0, i, 0))],
        out_specs=pl.BlockSpec((None, tr, c), lambda l, i: (l, i, 0)),
        out_shape=jax.ShapeDtypeStruct((L, r, c), F32), compiler_params=_params(("parallel", "parallel")))(parts)


def _adamw(w, g, m, v):
    L, r, c = w.shape
    tr = _row_tile(r, c)
    spec = pl.BlockSpec((None, tr, c), lambda l, i: (l, i, 0))

    def body(w_ref, g_ref, m_ref, v_ref, d_ref, mo_ref, vo_ref):
        d, mn, vn = _adamw_math(w_ref[...], g_ref[...], m_ref[...], v_ref[...])
        d_ref[...] = d
        mo_ref[...] = mn
        vo_ref[...] = vn

    shp = jax.ShapeDtypeStruct(w.shape, F32)
    return pl.pallas_call(body, name="adamw", grid=(L, r // tr), in_specs=[spec] * 4, out_specs=[spec] * 3,
                          out_shape=[shp] * 3, compiler_params=_params(("parallel", "parallel")))(w, g, m, v)


N_CHIPS = 4
CHIP_XOR = (0, 2, 1, 3)


def _chip_sum(part4, sib4, own_all, layer, pos):
    _, _, r, c = part4.shape
    tr = _row_tile(r, c)

    def chip(p, s):
        return jnp.bitwise_xor(2 * p[0] + p[1], CHIP_XOR[s])

    mine = [pl.BlockSpec((None, None, tr, c), functools.partial(lambda i, p, s: (chip(p, s), p[2], i, 0), s=s))
            for s in range(N_CHIPS)]
    theirs = [pl.BlockSpec((None, None, tr, c), functools.partial(lambda i, p, s: (chip(p, s), 0, i, 0), s=s))
              for s in range(N_CHIPS)]

    def body(pos_ref, *refs):
        a, b = refs[:N_CHIPS], refs[N_CHIPS:2 * N_CHIPS]
        out_ref, own_ref = refs[2 * N_CHIPS + 1], refs[2 * N_CHIPS + 2]
        own_ref[...] = a[0][...].astype(F32) + b[0][...].astype(F32)
        for s in range(1, N_CHIPS):
            out_ref[s - 1] = (a[s][...].astype(F32) + b[s][...].astype(F32)).astype(BF16)

    return pl.pallas_call(
        body, name="chip_sum",
        grid_spec=pltpu.PrefetchScalarGridSpec(
            num_scalar_prefetch=1, grid=(r // tr,),
            in_specs=[*mine, *theirs, pl.BlockSpec(memory_space=pl.ANY)],
            out_specs=[pl.BlockSpec((N_CHIPS - 1, tr, c), lambda i, p: (0, i, 0)),
                       pl.BlockSpec((None, tr, c), lambda i, p: (layer, i, 0))]),
        out_shape=[jax.ShapeDtypeStruct((N_CHIPS - 1, r, c), BF16), jax.ShapeDtypeStruct(own_all.shape, F32)],
        input_output_aliases={1 + 2 * N_CHIPS: 1},
        compiler_params=_params(("parallel",)))(pos, *([part4] * N_CHIPS), *([sib4] * N_CHIPS), own_all)


def _sum_chips(own_ref, got_ref):
    acc = own_ref[...]
    for s in range(N_CHIPS - 1):
        acc = acc + got_ref[s].astype(F32)
    return acc


def _sum4(own, got):
    L, r, c = own.shape
    tr = _row_tile(r, c)
    spec = pl.BlockSpec((None, tr, c), lambda l, i: (l, i, 0))

    def body(own_ref, got_ref, o_ref):
        o_ref[...] = _sum_chips(own_ref, got_ref)

    return pl.pallas_call(
        body, name="sum4", grid=(L, r // tr),
        in_specs=[spec, pl.BlockSpec((None, N_CHIPS - 1, tr, c), lambda l, i: (l, 0, i, 0))], out_specs=spec,
        out_shape=jax.ShapeDtypeStruct(own.shape, F32), compiler_params=_params(("parallel", "parallel")))(own, got)


def _sum4_adamw(own, got, w, m, v):
    L, r, c = w.shape
    tr = _row_tile(r, c)
    spec = pl.BlockSpec((None, tr, c), lambda l, i: (l, i, 0))

    def body(own_ref, got_ref, w_ref, m_ref, v_ref, g_ref, d_ref, mo_ref, vo_ref):
        g = _sum_chips(own_ref, got_ref)
        d, mn, vn = _adamw_math(w_ref[...], g, m_ref[...], v_ref[...])
        g_ref[...] = g
        d_ref[...] = d
        mo_ref[...] = mn
        vo_ref[...] = vn

    shp = jax.ShapeDtypeStruct(w.shape, F32)
    return pl.pallas_call(
        body, name="sum4_adamw", grid=(L, r // tr),
        in_specs=[spec, pl.BlockSpec((None, N_CHIPS - 1, tr, c), lambda l, i: (l, 0, i, 0)), spec, spec, spec],
        out_specs=[spec] * 4, out_shape=[shp] * 4,
        compiler_params=_params(("parallel", "parallel")))(own, got, w, m, v)


def _me():
    return lax.axis_index("x"), lax.axis_index("y"), lax.axis_index("c")


def _flip(pos, k):
    x, y, c = pos
    return (1 - x if k & 4 else x, 1 - y if k & 2 else y, 1 - c if k & 1 else c)


def _block_of(pos):
    return 4 * pos[0] + 2 * pos[1] + pos[2]


ANY = pl.BlockSpec(memory_space=pl.ANY)
SIBLING = 1
OTHER_CHIPS = (4, 2, 6)


def _all_gather(name, arrays, layer=None):
    n = len(arrays)
    shapes = [a.shape[-2:] for a in arrays]

    def body(*refs):
        srcs, outs = refs[:n], refs[n:2 * n]
        send_sems, recv_sems, local_sems = refs[2 * n:]
        me = _me()
        sib = _flip(me, SIBLING)

        def src_of(a):
            return srcs[a] if layer is None else srcs[a].at[layer]

        def copy(a, k, block_pos, to, src=None):
            dst = outs[a].at[_block_of(block_pos)]
            return pltpu.make_async_remote_copy(
                src_ref=dst if src is None else src, dst_ref=dst, send_sem=send_sems.at[a, k],
                recv_sem=recv_sems.at[a, k], device_id=to, device_id_type=MESH)

        mine = [pltpu.make_async_copy(src_of(a), outs[a].at[_block_of(me)], local_sems.at[a]) for a in range(n)]
        for cp in mine:
            cp.start()
        first = []
        for a in range(n):
            first.append(copy(a, 0, me, sib, src=src_of(a)))
            for j, k in enumerate(OTHER_CHIPS):
                first.append(copy(a, 1 + j, me, _flip(me, k), src=src_of(a)))
        for cp in first:
            cp.start()
        passed = []
        for j, k in enumerate(OTHER_CHIPS):
            for a in range(n):
                copy(a, 1 + j, _flip(me, k), me).wait_recv()
                fw = copy(a, 4 + j, _flip(me, k), sib)
                fw.start()
                passed.append(fw)
        for a in range(n):
            copy(a, 0, sib, me).wait_recv()
            for j, k in enumerate(OTHER_CHIPS):
                copy(a, 4 + j, _flip(sib, k), me).wait_recv()
        for cp in first + passed:
            cp.wait_send()
        for cp in mine:
            cp.wait()

    return pl.pallas_call(
        body, name=name, in_specs=[ANY] * n, out_specs=[ANY] * n,
        out_shape=[jax.ShapeDtypeStruct((N_DEV, *s), a.dtype) for s, a in zip(shapes, arrays)],
        scratch_shapes=[pltpu.SemaphoreType.DMA((n, 7)), pltpu.SemaphoreType.DMA((n, 7)), pltpu.SemaphoreType.DMA((n,))],
    )(*arrays)


HBM = pl.BlockSpec(memory_space=pltpu.HBM)
SEM = pl.BlockSpec(memory_space=pltpu.SEMAPHORE)
DATAFLOW = pltpu.SideEffectType.DATAFLOW_SIDE_EFFECTING
GATHER_RELATIONS = (SIBLING,) + OTHER_CHIPS


def _in_hbm(a):
    return pltpu.with_memory_space_constraint(a, pltpu.HBM)


def _hbm_like(a):
    return pltpu.HBM(a.shape, a.dtype)


def _gather_start(name, lands):
    n = len(lands)
    nrel = len(GATHER_RELATIONS)

    def body(*refs):
        land = refs[:n]
        send_sems, recv_sems = refs[n], refs[n + 1]
        token = refs[2 * n + 2]
        me = _me()
        for a in range(n):
            own = land[a].at[_block_of(me)]
            for j, k in enumerate(GATHER_RELATIONS):
                pltpu.make_async_remote_copy(src_ref=own, dst_ref=own, send_sem=send_sems.at[a * nrel + j],
                                             recv_sem=recv_sems.at[a * nrel + j], device_id=_flip(me, k),
                                             device_id_type=MESH).start()
        token[...] = jnp.zeros(TOKEN, F32)

    out = pl.pallas_call(
        body, name=name, in_specs=[HBM] * n,
        out_specs=[SEM, SEM, *([HBM] * n), pl.BlockSpec(memory_space=pltpu.VMEM)],
        out_shape=[pltpu.SemaphoreType.DMA((n * nrel,)), pltpu.SemaphoreType.DMA((n * nrel,)),
                   *[_hbm_like(a) for a in lands], jax.ShapeDtypeStruct(TOKEN, F32)],
        input_output_aliases={a: 2 + a for a in range(n)},
        compiler_params=pltpu.CompilerParams(has_side_effects=DATAFLOW),
    )(*[_in_hbm(a) for a in lands])
    return out[0], out[1], list(out[2:2 + n]), out[2 + n]


def _gather_wait(name, lands, send_sems, recv_sems, after):
    n = len(lands)
    nrel = len(GATHER_RELATIONS)

    def body(*refs):
        land = refs[:n]
        ssem, rsem = refs[n], refs[n + 1]
        me = _me()
        for a in range(n):
            own = land[a].at[_block_of(me)]
            for j, k in enumerate(GATHER_RELATIONS):
                peer = _flip(me, k)
                theirs = land[a].at[_block_of(peer)]
                s = a * nrel + j
                pltpu.make_async_remote_copy(src_ref=own, dst_ref=own, send_sem=ssem.at[s], recv_sem=rsem.at[s],
                                             device_id=peer, device_id_type=MESH).wait_send()
                pltpu.make_async_remote_copy(src_ref=theirs, dst_ref=theirs, send_sem=ssem.at[s],
                                             recv_sem=rsem.at[s], device_id=peer, device_id_type=MESH).wait_recv()

    out = pl.pallas_call(
        body, name=name, in_specs=[*([HBM] * n), SEM, SEM, ANY], out_specs=[HBM] * n,
        out_shape=[_hbm_like(a) for a in lands], input_output_aliases={a: a for a in range(n)},
        compiler_params=pltpu.CompilerParams(has_side_effects=DATAFLOW),
    )(*lands, send_sems, recv_sems, after)
    return list(out)


def _gather_forward(name, lands):
    n = len(lands)

    def body(*refs):
        land = refs[n:2 * n]
        send_sems, recv_sems = refs[2 * n:]
        me = _me()
        sib = _flip(me, SIBLING)
        sends = []
        for a in range(n):
            for j, k in enumerate(OTHER_CHIPS):
                blk = land[a].at[_block_of(_flip(me, k))]
                cp = pltpu.make_async_remote_copy(src_ref=blk, dst_ref=blk, send_sem=send_sems.at[a, j],
                                                  recv_sem=recv_sems.at[a, j], device_id=sib, device_id_type=MESH)
                cp.start()
                sends.append(cp)
        for a in range(n):
            for j, k in enumerate(OTHER_CHIPS):
                blk = land[a].at[_block_of(_flip(sib, k))]
                pltpu.make_async_remote_copy(src_ref=blk, dst_ref=blk, send_sem=send_sems.at[a, j],
                                             recv_sem=recv_sems.at[a, j], device_id=sib, device_id_type=MESH).wait_recv()
        for cp in sends:
            cp.wait_send()

    nrel = len(OTHER_CHIPS)
    return pl.pallas_call(
        body, name=name, in_specs=[ANY] * n, out_specs=[ANY] * n,
        out_shape=[jax.ShapeDtypeStruct(a.shape, a.dtype) for a in lands],
        input_output_aliases={a: a for a in range(n)},
        scratch_shapes=[pltpu.SemaphoreType.DMA((n, nrel)), pltpu.SemaphoreType.DMA((n, nrel))],
    )(*lands)


def _sibling_exchange(name, parts4):
    n = len(parts4)

    def body(*refs):
        srcs, outs = refs[:n], refs[n:2 * n]
        send_sems, recv_sems = refs[2 * n:]
        me = _me()
        sib = _flip(me, SIBLING)
        copies = [pltpu.make_async_remote_copy(
            src_ref=srcs[a].at[:, pl.ds(1 - me[2], 1)], dst_ref=outs[a], send_sem=send_sems.at[a],
            recv_sem=recv_sems.at[a], device_id=sib, device_id_type=MESH) for a in range(n)]
        for cp in copies:
            cp.start()
        for cp in copies:
            cp.wait_recv()
        for cp in copies:
            cp.wait_send()

    return pl.pallas_call(
        body, name=name, in_specs=[ANY] * n, out_specs=[ANY] * n,
        out_shape=[jax.ShapeDtypeStruct((N_CHIPS, 1, *p.shape[2:]), p.dtype) for p in parts4],
        scratch_shapes=[pltpu.SemaphoreType.DMA((n,)), pltpu.SemaphoreType.DMA((n,))],
    )(*parts4)


def _scatter_start(name, chip_parts, lands, layer):
    n = len(lands)
    nrel = len(OTHER_CHIPS)

    def body(*refs):
        srcs, land = refs[:n], refs[n:2 * n]
        send_sems, recv_sems = refs[2 * n], refs[2 * n + 1]
        token = refs[3 * n + 2]
        me = _me()
        for a in range(n):
            for j, k in enumerate(OTHER_CHIPS):
                pltpu.make_async_remote_copy(src_ref=srcs[a].at[j], dst_ref=land[a].at[layer, j],
                                             send_sem=send_sems.at[a * nrel + j], recv_sem=recv_sems.at[a * nrel + j],
                                             device_id=_flip(me, k), device_id_type=MESH).start()
        token[...] = jnp.zeros(TOKEN, F32)

    out = pl.pallas_call(
        body, name=name, in_specs=[HBM] * (2 * n),
        out_specs=[SEM, SEM, *([HBM] * n), pl.BlockSpec(memory_space=pltpu.VMEM)],
        out_shape=[pltpu.SemaphoreType.DMA((n * nrel,)), pltpu.SemaphoreType.DMA((n * nrel,)),
                   *[_hbm_like(a) for a in lands], jax.ShapeDtypeStruct(TOKEN, F32)],
        input_output_aliases={n + a: 2 + a for a in range(n)},
        compiler_params=pltpu.CompilerParams(has_side_effects=DATAFLOW),
    )(*[_in_hbm(a) for a in chip_parts], *[_in_hbm(a) for a in lands])
    return out[0], out[1], list(out[2:2 + n]), out[2 + n]


def _scatter_wait(name, chip_parts, lands, send_sems, recv_sems, layer, after):
    n = len(lands)
    nrel = len(OTHER_CHIPS)

    def body(*refs):
        srcs, land = refs[:n], refs[n:2 * n]
        ssem, rsem = refs[2 * n], refs[2 * n + 1]
        me = _me()
        for a in range(n):
            for j, k in enumerate(OTHER_CHIPS):
                cp = pltpu.make_async_remote_copy(src_ref=srcs[a].at[j], dst_ref=land[a].at[layer, j],
                                                  send_sem=ssem.at[a * nrel + j], recv_sem=rsem.at[a * nrel + j],
                                                  device_id=_flip(me, k), device_id_type=MESH)
                cp.wait_send()
                cp.wait_recv()

    out = pl.pallas_call(
        body, name=name, in_specs=[*([HBM] * (2 * n)), SEM, SEM, ANY], out_specs=[HBM] * n,
        out_shape=[_hbm_like(a) for a in lands], input_output_aliases={n + a: a for a in range(n)},
        compiler_params=pltpu.CompilerParams(has_side_effects=DATAFLOW),
    )(*chip_parts, *lands, send_sems, recv_sems, after)
    return list(out)


def _after(small, tokens):
    for t in tokens:
        small = small + t[0:1, 0:1]
    return small


def _layer_fwd(xc, l, W, P, dims, tokens=()):
    T, D, C, I, F = dims
    h = _rms_fwd(xc, _after(P["norm_mix_g"][l:l + 1], tokens))
    u = _mm_plain("mm_u", h, W["w_in"][l].reshape(I, D), "NT", [BF16], tn=1280)[0]
    a1, s2 = _convs_fwd(u, P["conf_dw"][l], P["sconv_w"][l], C)
    a3 = _ln_silu(a1, P["conf_ln_g"][l:l + 1], P["conf_ln_b"][l:l + 1])
    o, lse = _attn_fwd(u, P["sinks"][l], C)
    ya = _mm_nn_colblocked("mm_branch_out", a3, W["w_conf_out"][l], [BF16], tm=2048)[0]
    yb = _mm_nn_colblocked("mm_branch_out", s2, W["w_sconv_out"][l], [BF16], tm=2048)[0]
    yc = _mm_nn_colblocked("mm_branch_out", o, W["w_attn_out"][l], [BF16], tm=2048)[0]
    merged = _merge_fwd(u, P["gate_b"][l], ya, yb, yc)
    x1 = _mm_plain("mm_mix", merged, W["w_mix_out"][l].reshape(D, D), "NN", [F32], _epi_resid, [xc], tk=1024)[0]
    h2 = _rms_fwd(x1, P["norm_ffn_g"][l:l + 1])
    up, act = _mm_nn_colblocked("mm_up", h2, W["w_up"][l], [BF16, BF16], _epi_relu2)
    x2 = _mm_plain("mm_down", act, W["w_down"][l].reshape(F, D), "NN", [F32], _epi_resid, [x1], tk=1024)[0]
    saved = dict(xc=xc, h=h, u=u, a1=a1, a3=a3, s2=s2, o=o, lse=lse, ya=ya, yb=yb, yc=yc, merged=merged, x1=x1, h2=h2,
                 up=up, act=act)
    return x2, saved


def _layer_bwd(dx2, dx2_b, l, W, P, S, dims, dep=None):
    T, D, C, I, F = dims
    d_up = _mm_plain("mm_d_up", dx2_b, W["w_down"][l].reshape(F, D), "NT", [BF16], _epi_drelu2, [S["up"]], dep=dep)[0]
    g_down = _mm_plain("mm_g_down", S["act"], dx2_b, "TN", [BF16])[0]
    dh2 = _mm_nt_colblocked("mm_d_h2", d_up, W["w_up"][l], F32)
    g_up = _mm_tn_colblocked_out("mm_g_up", S["h2"], d_up, BF16)
    dx1, dx1_b, dg_ffn = _rms_bwd(dh2, S["x1"], P["norm_ffn_g"][l:l + 1], dx2)
    dm = _mm_plain("mm_d_merged", dx1_b, W["w_mix_out"][l].reshape(D, D), "NT", [BF16])[0]
    g_mix = _mm_plain("mm_g_mix", S["merged"], dx1_b, "TN", [BF16])[0]
    d_ya, d_yb, d_yc, dg0, dg1, dg2, d_gate_b = _merge_bwd(S["u"], P["gate_b"][l], S["ya"], S["yb"], S["yc"], dm)
    d_a3 = _mm_nt_colblocked("mm_d_branch", d_ya, W["w_conf_out"][l], BF16, tm=2048)
    d_s2 = _mm_nt_colblocked("mm_d_branch", d_yb, W["w_sconv_out"][l], BF16, tm=2048)
    d_o = _mm_nt_colblocked("mm_d_branch", d_yc, W["w_attn_out"][l], BF16, tm=2048)
    g_conf = _mm_tn_colblocked_out("mm_g_branch", S["a3"], d_ya, BF16)
    g_sconv = _mm_tn_colblocked_out("mm_g_branch", S["s2"], d_yb, BF16)
    g_attn = _mm_tn_colblocked_out("mm_g_branch", S["o"], d_yc, BF16)
    d_a1, d_ln_g, d_ln_b = _ln_silu_bwd(S["a1"], P["conf_ln_g"][l:l + 1], P["conf_ln_b"][l:l + 1], d_a3)
    d_av, d_ag, d_bg, d_cg, d_bh, d_conf_dw, d_sconv_w = _convs_bwd(S["u"], P["conf_dw"][l], P["sconv_w"][l], d_a1, d_s2, C)
    dq, dk, dv, d_sinks = _attn_bwd(S["u"], S["o"], S["lse"], d_o, P["sinks"][l], C)
    du = jnp.concatenate([d_av, d_ag, d_bg, d_cg, d_bh, dq, dk, dv, dg0, dg1, dg2], axis=1)
    g_in = _mm_plain("mm_g_in", du, S["h"], "TN", [BF16], tm=1280)[0]
    dh = _mm_plain("mm_d_h", du, W["w_in"][l].reshape(I, D), "NN", [F32], tk=1280)[0]
    dx, dx_b, dg_mix = _rms_bwd(dh, S["xc"], P["norm_mix_g"][l:l + 1], dx1)
    big = dict(w_in=g_in.reshape(N_DEV, I // N_DEV, D), w_conf_out=g_conf, w_sconv_out=g_sconv, w_attn_out=g_attn,
               w_mix_out=g_mix.reshape(N_DEV, D // N_DEV, D), w_up=g_up, w_down=g_down.reshape(N_DEV, F // N_DEV, D))
    small = dict(norm_mix_g=dg_mix, gate_b=d_gate_b, conf_dw=d_conf_dw, conf_ln_g=d_ln_g, conf_ln_b=d_ln_b,
                 sconv_w=d_sconv_w, sinks=d_sinks[0:1], norm_ffn_g=dg_ffn)
    return dx, dx_b, big, small


BIG = ("w_in", "w_conf_out", "w_sconv_out", "w_attn_out", "w_mix_out", "w_up", "w_down")
SMALL_PER_LAYER = ("norm_mix_g", "gate_b", "conf_dw", "conf_ln_g", "conf_ln_b", "sconv_w", "sinks", "norm_ffn_g")
WEIGHTS = ("norm_mix_g", "w_in", "gate_b", "conf_dw", "conf_ln_g", "conf_ln_b", "w_conf_out", "sconv_w", "w_sconv_out",
           "sinks", "w_attn_out", "w_mix_out", "norm_ffn_g", "w_up", "w_down", "final_g")


SUBLANES = 8


def _nrows(n_el, width):
    per_tile = SUBLANES * width
    return SUBLANES * (-(-n_el // per_tile))


def _rows(a, width):
    flat = a.reshape(-1)
    nrow = _nrows(flat.shape[0], width)
    return jnp.pad(flat, (0, nrow * width - flat.shape[0])).reshape(nrow, width)


def _as3d(a):
    if a.ndim == 1:
        return a.reshape(1, 1, -1)
    if a.ndim == 2:
        return a.reshape(1, *a.shape)
    return a


def kernel(x, norm_mix_g, w_in, gate_b, conf_dw, conf_ln_g, conf_ln_b, w_conf_out, sconv_w, w_sconv_out, sinks, w_attn_out, w_mix_out, norm_ffn_g, w_up, w_down, final_g, loss_target, m_norm_mix_g, m_w_in, m_gate_b, m_conf_dw, m_conf_ln_g, m_conf_ln_b, m_w_conf_out, m_sconv_w, m_w_sconv_out, m_sinks, m_w_attn_out, m_w_mix_out, m_norm_ffn_g, m_w_up, m_w_down, m_final_g, v_norm_mix_g, v_w_in, v_gate_b, v_conf_dw, v_conf_ln_g, v_conf_ln_b, v_w_conf_out, v_sconv_w, v_w_sconv_out, v_sinks, v_w_attn_out, v_w_mix_out, v_norm_ffn_g, v_w_up, v_w_down, v_final_g):
    w = dict(norm_mix_g=norm_mix_g, w_in=w_in, gate_b=gate_b, conf_dw=conf_dw, conf_ln_g=conf_ln_g, conf_ln_b=conf_ln_b,
             w_conf_out=w_conf_out, sconv_w=sconv_w, w_sconv_out=w_sconv_out, sinks=sinks, w_attn_out=w_attn_out,
             w_mix_out=w_mix_out, norm_ffn_g=norm_ffn_g, w_up=w_up, w_down=w_down, final_g=final_g)
    mom = dict(norm_mix_g=m_norm_mix_g, w_in=m_w_in, gate_b=m_gate_b, conf_dw=m_conf_dw, conf_ln_g=m_conf_ln_g,
               conf_ln_b=m_conf_ln_b, w_conf_out=m_w_conf_out, sconv_w=m_sconv_w, w_sconv_out=m_w_sconv_out,
               sinks=m_sinks, w_attn_out=m_w_attn_out, w_mix_out=m_w_mix_out, norm_ffn_g=m_norm_ffn_g, w_up=m_w_up,
               w_down=m_w_down, final_g=m_final_g)
    var = dict(norm_mix_g=v_norm_mix_g, w_in=v_w_in, gate_b=v_gate_b, conf_dw=v_conf_dw, conf_ln_g=v_conf_ln_g,
               conf_ln_b=v_conf_ln_b, w_conf_out=v_w_conf_out, sconv_w=v_sconv_w, w_sconv_out=v_w_sconv_out,
               sinks=v_sinks, w_attn_out=v_w_attn_out, w_mix_out=v_w_mix_out, norm_ffn_g=v_norm_ffn_g, w_up=v_w_up,
               w_down=v_w_down, final_g=v_final_g)

    _, T, D = x.shape
    L = w_in.shape[0]
    C = D // 2
    I = w_in.shape[2] * N_DEV
    F = w_up.shape[2] * N_DEV
    dims = (T, D, C, I, F)
    my_block = _block_of(_me())

    pos = jnp.stack(_me()).astype(jnp.int32)
    blk = my_block.reshape(1).astype(jnp.int32)
    W = {k: [_cast_place(w[k], l, blk, transpose=(k == "w_in")) for l in range(L)] for k in BIG}
    in_flight, tokens = [], []
    for l in range(L):
        ssem, rsem, lands, tok = _gather_start(f"gather_start_{l}", [W[k][l] for k in BIG])
        in_flight.append((ssem, rsem, lands))
        tokens.append(tok)

    shard_names = ("gate_b", "conf_dw", "sconv_w")
    packed = jnp.concatenate([_rows(w[k], LANES) for k in shard_names], axis=0)
    gathered = _all_gather("gather_small", [packed])[0]
    P = dict(norm_mix_g=norm_mix_g, conf_ln_g=conf_ln_g, conf_ln_b=conf_ln_b, sinks=sinks, norm_ffn_g=norm_ffn_g)
    row0 = 0
    for k in shard_names:
        n_el = w[k].size
        nrow = _nrows(n_el, LANES)
        part = gathered[:, row0:row0 + nrow].reshape(N_DEV, -1)[:, :n_el].reshape(N_DEV, *w[k].shape)
        P[k] = jnp.moveaxis(part, 0, 2).reshape(*w[k].shape[:2], N_DEV * w[k].shape[2])
        row0 += nrow

    xc = x.reshape(T, D)
    saved = []
    for l in range(L):
        ssem, rsem, lands = in_flight[l]
        lands = _gather_wait(f"gather_wait_{l}", lands, ssem, rsem, xc if l else tokens[-1])
        lands = _gather_forward("gather_forward", lands)
        for k, g in zip(BIG, lands):
            W[k][l] = g
        xc, S = _layer_fwd(xc, l, W, P, dims, tokens if l == 0 else ())
        saved.append(S)
    dx, dx_b, d_final_g, loss_tile = _loss_head(xc, final_g.reshape(1, D), loss_target.reshape(T, D))

    own_all = {k: lax.empty((L, *W[k][0].shape[1:]), F32) for k in BIG}
    recv = {k: lax.empty((L, N_CHIPS - 1, *W[k][0].shape[1:]), BF16) for k in BIG}
    small_grads = [None] * L
    pending, dep = [], None
    for l in reversed(range(L)):
        dx, dx_b, big, small = _layer_bwd(dx, dx_b, l, W, P, saved[l], dims, dep)
        small_grads[l] = small
        part4 = [big[k].reshape(N_CHIPS, 2, *big[k].shape[1:]) for k in BIG]
        sib4 = _sibling_exchange("sibling_exchange", part4)
        chip_parts = []
        for k, p4, s4 in zip(BIG, part4, sib4):
            cp, own_all[k] = _chip_sum(p4, s4, own_all[k], l, pos)
            chip_parts.append(cp)
        ssem, rsem, lands, dep = _scatter_start(f"scatter_start_{l}", chip_parts, [recv[k] for k in BIG], l)
        for k, g in zip(BIG, lands):
            recv[k] = g
        pending.append((l, chip_parts, ssem, rsem))
    for l, chip_parts, ssem, rsem in pending:
        lands = _scatter_wait(f"scatter_wait_{l}", chip_parts, [recv[k] for k in BIG], ssem, rsem, l, dx)
        for k, g in zip(BIG, lands):
            recv[k] = g

    width = C
    pieces = [_rows(small_grads[l][k], width) for l in range(L) for k in SMALL_PER_LAYER]
    pieces += [_rows(d_final_g, width), _rows(loss_tile[0:1, 0:1], width)]
    partial = jnp.concatenate(pieces, axis=0)
    everyone = _all_gather("gather_small_grads", [partial])[0]
    total = _sum8(everyone.reshape(1, *everyone.shape))[0]
    grads = {}
    row0 = 0
    per_layer = {k: [] for k in SMALL_PER_LAYER}
    for l in range(L):
        for k in SMALL_PER_LAYER:
            shape = small_grads[l][k].shape
            n_el = small_grads[l][k].size
            nrow = _nrows(n_el, width)
            per_layer[k].append(total[row0:row0 + nrow].reshape(-1)[:n_el].reshape(shape))
            row0 += nrow
    nrow = _nrows(D, width)
    grads["final_g"] = total[row0:row0 + nrow].reshape(-1)[:D]
    row0 += nrow
    loss = total[row0, 0]
    for k in SMALL_PER_LAYER:
        full = jnp.stack(per_layer[k], axis=0)
        if k in shard_names:
            shard = w[k].shape[2]
            full = lax.dynamic_slice_in_dim(full, my_block * shard, shard, axis=2)
        grads[k] = full.reshape(w[k].shape)

    delta, new_m, new_v = {}, {}, {}
    for k in BIG:
        if k == "w_in":
            g_t = _sum4(own_all[k], recv[k])
            grads[k] = jnp.swapaxes(g_t, 1, 2)
            delta[k], new_m[k], new_v[k] = _adamw(w[k], grads[k], mom[k], var[k])
        else:
            grads[k], delta[k], new_m[k], new_v[k] = _sum4_adamw(own_all[k], recv[k], w[k], mom[k], var[k])
    for k in WEIGHTS:
        if k in BIG:
            continue
        d, mn, vn = _adamw(_as3d(w[k]), _as3d(grads[k]), _as3d(mom[k]), _as3d(var[k]))
        delta[k], new_m[k], new_v[k] = d.reshape(w[k].shape), mn.reshape(w[k].shape), vn.reshape(w[k].shape)

    return (loss, dx.reshape(1, T, D), *[grads[k] for k in WEIGHTS], *[delta[k] for k in WEIGHTS],
            *[new_m[k] for k in WEIGHTS], *[new_v[k] for k in WEIGHTS])
```

```python
import functools

import jax
import jax.numpy as jnp
from jax import lax
from jax.experimental import pallas as pl
from jax.experimental.pallas import tpu as pltpu

F32 = jnp.float32
BF16 = jnp.bfloat16

N_DEV = 8
HEAD_DIM = 64
N_KV_HEADS = 4
ATTN_BLOCK = 128
CONF_KERNEL = 31
SCONV_KERNEL = 3
N_BRANCH = 3
RMS_EPS = 1e-6
LN_EPS = 1e-5
ADAM_LR = 0.001
ADAM_B1 = 0.9
ADAM_B2 = 0.999
ADAM_EPS = 1e-08
ADAM_WD = 0.01
ADAM_STEP = 10
LANES = 128
NEG_BIG = -1e30
VMEM_LIMIT_BYTES = 56 * 1024 * 1024
MESH = pl.DeviceIdType.MESH

NN = (((1,), (0,)), ((), ()))
NT = (((1,), (1,)), ((), ()))
TN = (((0,), (0,)), ((), ()))


def _pick(n, cap, mult=LANES):
    best = None
    for d in range(mult, min(n, cap) + 1, mult):
        if n % d == 0:
            best = d
    assert best is not None, (n, cap, mult)
    return best


def _sigmoid(x):
    return 1.0 / (1.0 + jnp.exp(-x))


def _params(sem):
    return pltpu.CompilerParams(dimension_semantics=sem, vmem_limit_bytes=VMEM_LIMIT_BYTES)


def _epi_cast(p, ex, outs):
    outs[0][...] = p.astype(outs[0].dtype)


def _epi_resid(p, ex, outs):
    outs[0][...] = ex[0][...] + p


def _epi_relu2(p, ex, outs):
    outs[0][...] = p.astype(outs[0].dtype)
    r = jnp.maximum(p, 0.0)
    outs[1][...] = (r * r).astype(outs[1].dtype)


def _epi_drelu2(p, ex, outs):
    up = ex[0][...].astype(F32)
    outs[0][...] = (p * (2.0 * jnp.maximum(up, 0.0))).astype(outs[0].dtype)


TOKEN = (8, LANES)


def _matmul(name, a, b, dnums, grid, a_spec, b_spec, out_shape, out_specs, epi, acc_shape, extra=(), extra_specs=(),
            dep=None):
    if dep is not None:
        extra = [*extra, dep]
        extra_specs = [*extra_specs, pl.BlockSpec(TOKEN, lambda j, i, k: (0, 0))]
    nk = grid[2]
    n_extra, n_out = len(extra), len(out_shape)

    def body(*refs):
        a_ref, b_ref = refs[0], refs[1]
        ex = refs[2:2 + n_extra]
        outs = refs[2 + n_extra:2 + n_extra + n_out]
        p = lax.dot_general(a_ref[...], b_ref[...], dnums, preferred_element_type=F32)
        if nk == 1:
            epi(p, ex, outs)
        else:
            acc = refs[-1]
            k = pl.program_id(2)

            @pl.when(k == 0)
            def _():
                acc[...] = p

            @pl.when(k > 0)
            def _():
                acc[...] += p

            @pl.when(k == nk - 1)
            def _():
                epi(acc[...], ex, outs)

    scratch = [pltpu.VMEM(acc_shape, F32)] if nk > 1 else []
    return pl.pallas_call(
        body, name=name, grid=grid, in_specs=[a_spec, b_spec, *extra_specs], out_specs=list(out_specs),
        out_shape=list(out_shape), scratch_shapes=scratch,
        compiler_params=_params(("parallel", "parallel", "arbitrary")))(a, b, *extra)


def _mm_plain(name, a, b, form, out_dtypes, epi=_epi_cast, extra=(), tm=1024, tn=1024, tk=2048, dep=None):
    if form == "NN":
        (M, K), N = a.shape, b.shape[1]
    elif form == "NT":
        (M, K), N = a.shape, b.shape[0]
    else:
        (K, M), N = a.shape, b.shape[1]
    tm, tn, tk = _pick(M, tm, 8), _pick(N, tn), _pick(K, tk)
    grid = (N // tn, M // tm, K // tk)
    if form == "TN":
        a_spec = pl.BlockSpec((tk, tm), lambda j, i, k: (k, i))
    else:
        a_spec = pl.BlockSpec((tm, tk), lambda j, i, k: (i, k))
    if form == "NT":
        b_spec = pl.BlockSpec((tn, tk), lambda j, i, k: (j, k))
    else:
        b_spec = pl.BlockSpec((tk, tn), lambda j, i, k: (k, j))
    o_spec = pl.BlockSpec((tm, tn), lambda j, i, k: (i, j))
    dn = {"NN": NN, "NT": NT, "TN": TN}[form]
    return _matmul(name, a, b, dn, grid, a_spec, b_spec,
                   [jax.ShapeDtypeStruct((M, N), dt) for dt in out_dtypes], [o_spec] * len(out_dtypes), epi,
                   (tm, tn), extra, [o_spec] * len(extra), dep)


def _mm_nn_colblocked(name, a, bb, out_dtypes, epi=_epi_cast, tm=1024, tn=1024, tk=2048):
    M, K = a.shape
    ns = bb.shape[2]
    N = N_DEV * ns
    tm, tn, tk = _pick(M, tm, 8), _pick(ns, tn), _pick(K, tk)
    q = ns // tn
    grid = (N // tn, M // tm, K // tk)
    a_spec = pl.BlockSpec((tm, tk), lambda j, i, k: (i, k))
    b_spec = pl.BlockSpec((None, tk, tn), lambda j, i, k: (j // q, k, j % q))
    o_spec = pl.BlockSpec((tm, tn), lambda j, i, k: (i, j))
    return _matmul(name, a, bb, NN, grid, a_spec, b_spec,
                   [jax.ShapeDtypeStruct((M, N), dt) for dt in out_dtypes], [o_spec] * len(out_dtypes), epi, (tm, tn))


def _mm_nt_colblocked(name, a, bb, out_dtype, tm=1024, tn=1024, tk=1024, dep=None):
    M, N = a.shape
    K, ns = bb.shape[1], bb.shape[2]
    tm, tn, tk = _pick(M, tm, 8), _pick(K, tn), _pick(ns, tk)
    q = ns // tk
    grid = (K // tn, M // tm, N // tk)
    a_spec = pl.BlockSpec((tm, tk), lambda j, i, k: (i, k))
    b_spec = pl.BlockSpec((None, tn, tk), lambda j, i, k: (k // q, j, k % q))
    o_spec = pl.BlockSpec((tm, tn), lambda j, i, k: (i, j))
    return _matmul(name, a, bb, NT, grid, a_spec, b_spec, [jax.ShapeDtypeStruct((M, K), out_dtype)], [o_spec],
                   _epi_cast, (tm, tn), dep=dep)[0]


def _mm_tn_colblocked_out(name, a, b, out_dtype, tm=1024, tn=1024, tk=2048):
    T, M = a.shape
    N = b.shape[1]
    ns = N // N_DEV
    tm, tn, tk = _pick(M, tm, 8), _pick(ns, tn), _pick(T, tk)
    q = ns // tn
    grid = (N // tn, M // tm, T // tk)
    a_spec = pl.BlockSpec((tk, tm), lambda j, i, k: (k, i))
    b_spec = pl.BlockSpec((tk, tn), lambda j, i, k: (k, j))
    o_spec = pl.BlockSpec((None, tm, tn), lambda j, i, k: (j // q, i, j % q))
    return _matmul(name, a, b, TN, grid, a_spec, b_spec, [jax.ShapeDtypeStruct((N_DEV, M, ns), out_dtype)], [o_spec],
                   _epi_cast, (tm, tn))[0]


ROW_TILE = 256


def _rms_fwd(x, g):
    T, D = x.shape
    tr = _pick(T, ROW_TILE, 8)

    def body(x_ref, g_ref, h_ref):
        xv = x_ref[...]
        r = lax.rsqrt(jnp.mean(xv * xv, axis=-1, keepdims=True) + RMS_EPS)
        h_ref[...] = (xv * r * g_ref[...]).astype(BF16)

    return pl.pallas_call(
        body, name="rms_fwd", grid=(T // tr,),
        in_specs=[pl.BlockSpec((tr, D), lambda i: (i, 0)), pl.BlockSpec((1, D), lambda i: (0, 0))],
        out_specs=pl.BlockSpec((tr, D), lambda i: (i, 0)),
        out_shape=jax.ShapeDtypeStruct((T, D), BF16), compiler_params=_params(("parallel",)))(x, g)


def _rms_bwd_math(dh, xv, g):
    r = lax.rsqrt(jnp.mean(xv * xv, axis=-1, keepdims=True) + RMS_EPS)
    gdh = dh * g
    dot = jnp.mean(gdh * xv, axis=-1, keepdims=True)
    dx = r * gdh - xv * (r * r * r * dot)
    return dx, dh * xv * r


def _rms_bwd(dh, x, g, dres):
    T, D = x.shape
    tr = _pick(T, ROW_TILE, 8)

    def body(dh_ref, x_ref, g_ref, dres_ref, dx_ref, dxb_ref, dg_ref):
        dx, dgrow = _rms_bwd_math(dh_ref[...], x_ref[...], g_ref[...])
        dx = dx + dres_ref[...]
        dx_ref[...] = dx
        dxb_ref[...] = dx.astype(BF16)
        part = jnp.sum(dgrow, axis=0, keepdims=True)

        @pl.when(pl.program_id(0) == 0)
        def _():
            dg_ref[...] = part

        @pl.when(pl.program_id(0) > 0)
        def _():
            dg_ref[...] += part

    row = pl.BlockSpec((tr, D), lambda i: (i, 0))
    vec = pl.BlockSpec((1, D), lambda i: (0, 0))
    return pl.pallas_call(
        body, name="rms_bwd", grid=(T // tr,), in_specs=[row, row, vec, row], out_specs=[row, row, vec],
        out_shape=[jax.ShapeDtypeStruct((T, D), F32), jax.ShapeDtypeStruct((T, D), BF16),
                   jax.ShapeDtypeStruct((1, D), F32)],
        compiler_params=_params(("arbitrary",)))(dh, x, g, dres)


def _loss_head(x, g, target):
    T, D = x.shape
    tr = _pick(T, ROW_TILE, 8)

    def body(x_ref, g_ref, t_ref, dx_ref, dxb_ref, dg_ref, loss_ref):
        xv, gv = x_ref[...], g_ref[...]
        r = lax.rsqrt(jnp.mean(xv * xv, axis=-1, keepdims=True) + RMS_EPS)
        err = xv * r * gv - t_ref[...]
        part_loss = 0.5 * jnp.sum(jnp.mean(err * err, axis=-1, keepdims=True), axis=0, keepdims=True)
        dx, dgrow = _rms_bwd_math(err * (1.0 / D), xv, gv)
        dx_ref[...] = dx
        dxb_ref[...] = dx.astype(BF16)
        part = jnp.sum(dgrow, axis=0, keepdims=True)
        lpart = jnp.broadcast_to(part_loss, (8, LANES))

        @pl.when(pl.program_id(0) == 0)
        def _():
            dg_ref[...] = part
            loss_ref[...] = lpart

        @pl.when(pl.program_id(0) > 0)
        def _():
            dg_ref[...] += part
            loss_ref[...] += lpart

    row = pl.BlockSpec((tr, D), lambda i: (i, 0))
    vec = pl.BlockSpec((1, D), lambda i: (0, 0))
    lsp = pl.BlockSpec((8, LANES), lambda i: (0, 0))
    return pl.pallas_call(
        body, name="loss_head", grid=(T // tr,), in_specs=[row, vec, row], out_specs=[row, row, vec, lsp],
        out_shape=[jax.ShapeDtypeStruct((T, D), F32), jax.ShapeDtypeStruct((T, D), BF16),
                   jax.ShapeDtypeStruct((1, D), F32), jax.ShapeDtypeStruct((8, LANES), F32)],
        compiler_params=_params(("arbitrary",)))(x, g, target)


def _ln_math(a1, g, b):
    mu = jnp.mean(a1, axis=-1, keepdims=True)
    xc = a1 - mu
    rstd = lax.rsqrt(jnp.mean(xc * xc, axis=-1, keepdims=True) + LN_EPS)
    xhat = xc * rstd
    return xhat, rstd, xhat * g + b


def _ln_silu(a1, g, b):
    T, C = a1.shape
    tr = _pick(T, ROW_TILE, 8)

    def body(a_ref, g_ref, b_ref, o_ref):
        _, _, y = _ln_math(a_ref[...], g_ref[...], b_ref[...])
        o_ref[...] = (y * _sigmoid(y)).astype(BF16)

    row = pl.BlockSpec((tr, C), lambda i: (i, 0))
    vec = pl.BlockSpec((1, C), lambda i: (0, 0))
    return pl.pallas_call(body, name="ln_silu", grid=(T // tr,), in_specs=[row, vec, vec], out_specs=row,
                          out_shape=jax.ShapeDtypeStruct((T, C), BF16), compiler_params=_params(("parallel",)))(a1, g, b)


def _ln_silu_bwd(a1, g, b, d_a3):
    T, C = a1.shape
    tr = _pick(T, ROW_TILE, 8)

    def body(a_ref, g_ref, b_ref, d_ref, da_ref, dg_ref, db_ref):
        gv = g_ref[...]
        xhat, rstd, y = _ln_math(a_ref[...], gv, b_ref[...])
        s = _sigmoid(y)
        dy = d_ref[...].astype(F32) * (s * (1.0 + y * (1.0 - s)))
        dxh = dy * gv
        m1 = jnp.mean(dxh, axis=-1, keepdims=True)
        m2 = jnp.mean(dxh * xhat, axis=-1, keepdims=True)
        da_ref[...] = rstd * (dxh - m1 - xhat * m2)
        pg = jnp.sum(dy * xhat, axis=0, keepdims=True)
        pb = jnp.sum(dy, axis=0, keepdims=True)

        @pl.when(pl.program_id(0) == 0)
        def _():
            dg_ref[...] = pg
            db_ref[...] = pb

        @pl.when(pl.program_id(0) > 0)
        def _():
            dg_ref[...] += pg
            db_ref[...] += pb

    row = pl.BlockSpec((tr, C), lambda i: (i, 0))
    vec = pl.BlockSpec((1, C), lambda i: (0, 0))
    return pl.pallas_call(
        body, name="ln_silu_bwd", grid=(T // tr,), in_specs=[row, vec, vec, row], out_specs=[row, vec, vec],
        out_shape=[jax.ShapeDtypeStruct((T, C), F32), jax.ShapeDtypeStruct((1, C), F32),
                   jax.ShapeDtypeStruct((1, C), F32)],
        compiler_params=_params(("arbitrary",)))(a1, g, b, d_a3)


CONV_ROWS = 128
PAD_A = 32
PAD_B = 8


def _u_block(T, first):
    return pl.BlockSpec((T, LANES), lambda i: (0, first + i))


def _causal_conv(xpad_ref, w_ref, ksize, pad, T, emit):
    for r0 in range(0, T, CONV_ROWS):
        acc = None
        for j in range(ksize):
            off = pad - (ksize - 1) + j + r0
            term = w_ref[j:j + 1, :] * xpad_ref[off:off + CONV_ROWS, :]
            acc = term if acc is None else acc + term
        emit(r0, acc)


def _anticausal_conv(gpad_ref, w_ref, ksize, T, emit):
    for r0 in range(0, T, CONV_ROWS):
        acc = None
        for j in range(ksize):
            off = (ksize - 1) - j + r0
            term = w_ref[j:j + 1, :] * gpad_ref[off:off + CONV_ROWS, :]
            acc = term if acc is None else acc + term
        emit(r0, acc)


def _conv_wgrad(xpad_ref, g_ref, dw_ref, ksize, pad, T):
    for j in range(ksize):
        acc = None
        for r0 in range(0, T, CONV_ROWS):
            off = pad - (ksize - 1) + j + r0
            term = g_ref[r0:r0 + CONV_ROWS, :] * xpad_ref[off:off + CONV_ROWS, :]
            term = jnp.sum(term.reshape(CONV_ROWS // 8, 8, LANES), axis=0)
            acc = term if acc is None else acc + term
        dw_ref[j:j + 1, :] = jnp.sum(acc, axis=0, keepdims=True)


def _convs_fwd(u, conf_dw, sconv_w, C):
    T = u.shape[0]
    nb = C // LANES

    def body(av_ref, ag_ref, bg_ref, cg_ref, bh_ref, dw_ref, sw_ref, a1_ref, s2_ref, xa, xs, s1):
        xa[0:PAD_A, :] = jnp.zeros((PAD_A, LANES), F32)
        xa[PAD_A:PAD_A + T, :] = av_ref[...].astype(F32) * _sigmoid(ag_ref[...].astype(F32))

        def emit_a(r0, acc):
            a1_ref[r0:r0 + CONV_ROWS, :] = acc

        _causal_conv(xa, dw_ref, CONF_KERNEL, PAD_A, T, emit_a)

        xs[0:PAD_B, :] = jnp.zeros((PAD_B, LANES), F32)
        xs[PAD_B:PAD_B + T, :] = cg_ref[...].astype(F32) * bh_ref[...].astype(F32)

        def emit_b(r0, acc):
            s1[r0:r0 + CONV_ROWS, :] = acc

        _causal_conv(xs, sw_ref, SCONV_KERNEL, PAD_B, T, emit_b)
        s2_ref[...] = (bg_ref[...].astype(F32) * s1[...]).astype(BF16)

    col = pl.BlockSpec((T, LANES), lambda i: (0, i))
    return pl.pallas_call(
        body, name="convs_fwd", grid=(nb,),
        in_specs=[_u_block(T, 0), _u_block(T, nb), _u_block(T, 2 * nb), _u_block(T, 3 * nb), _u_block(T, 4 * nb),
                  pl.BlockSpec((CONF_KERNEL, LANES), lambda i: (0, i)),
                  pl.BlockSpec((SCONV_KERNEL, LANES), lambda i: (0, i))],
        out_specs=[col, col],
        out_shape=[jax.ShapeDtypeStruct((T, C), F32), jax.ShapeDtypeStruct((T, C), BF16)],
        scratch_shapes=[pltpu.VMEM((T + PAD_A, LANES), F32), pltpu.VMEM((T + PAD_B, LANES), F32),
                        pltpu.VMEM((T, LANES), F32)],
        compiler_params=_params(("parallel",)))(u, u, u, u, u, conf_dw, sconv_w)


def _convs_bwd(u, conf_dw, sconv_w, d_a1, d_s2, C):
    T = u.shape[0]
    nb = C // LANES

    def body(av_ref, ag_ref, bg_ref, cg_ref, bh_ref, dw_ref, sw_ref, da1_ref, ds2_ref,
             dav_ref, dag_ref, dbg_ref, dcg_ref, dbh_ref, ddw_ref, dsw_ref, xa, ga, xs, gs, tmp):
        av = av_ref[...].astype(F32)
        sg = _sigmoid(ag_ref[...].astype(F32))
        xa[0:PAD_A, :] = jnp.zeros((PAD_A, LANES), F32)
        xa[PAD_A:PAD_A + T, :] = av * sg
        ga[0:T, :] = da1_ref[...]
        ga[T:T + PAD_A, :] = jnp.zeros((PAD_A, LANES), F32)
        _conv_wgrad(xa, ga, ddw_ref, CONF_KERNEL, PAD_A, T)

        def emit_a(r0, acc):
            tmp[r0:r0 + CONV_ROWS, :] = acc

        _anticausal_conv(ga, dw_ref, CONF_KERNEL, T, emit_a)
        da0 = tmp[...]
        dav_ref[...] = (da0 * sg).astype(BF16)
        dag_ref[...] = (da0 * av * sg * (1.0 - sg)).astype(BF16)

        cg = cg_ref[...].astype(F32)
        bh = bh_ref[...].astype(F32)
        ds2 = ds2_ref[...].astype(F32)
        xs[0:PAD_B, :] = jnp.zeros((PAD_B, LANES), F32)
        xs[PAD_B:PAD_B + T, :] = cg * bh

        def emit_s1(r0, acc):
            tmp[r0:r0 + CONV_ROWS, :] = acc

        _causal_conv(xs, sw_ref, SCONV_KERNEL, PAD_B, T, emit_s1)
        dbg_ref[...] = (ds2 * tmp[...]).astype(BF16)
        gs[0:T, :] = ds2 * bg_ref[...].astype(F32)
        gs[T:T + PAD_B, :] = jnp.zeros((PAD_B, LANES), F32)
        _conv_wgrad(xs, gs, dsw_ref, SCONV_KERNEL, PAD_B, T)

        def emit_b(r0, acc):
            tmp[r0:r0 + CONV_ROWS, :] = acc

        _anticausal_conv(gs, sw_ref, SCONV_KERNEL, T, emit_b)
        ds0 = tmp[...]
        dcg_ref[...] = (ds0 * bh).astype(BF16)
        dbh_ref[...] = (ds0 * cg).astype(BF16)

    col = pl.BlockSpec((T, LANES), lambda i: (0, i))
    wa = pl.BlockSpec((CONF_KERNEL, LANES), lambda i: (0, i))
    wb = pl.BlockSpec((SCONV_KERNEL, LANES), lambda i: (0, i))
    act = jax.ShapeDtypeStruct((T, C), BF16)
    return pl.pallas_call(
        body, name="convs_bwd", grid=(nb,),
        in_specs=[_u_block(T, 0), _u_block(T, nb), _u_block(T, 2 * nb), _u_block(T, 3 * nb), _u_block(T, 4 * nb),
                  wa, wb, col, col],
        out_specs=[col, col, col, col, col, wa, wb],
        out_shape=[act, act, act, act, act, jax.ShapeDtypeStruct((CONF_KERNEL, C), F32),
                   jax.ShapeDtypeStruct((SCONV_KERNEL, C), F32)],
        scratch_shapes=[pltpu.VMEM((T + PAD_A, LANES), F32), pltpu.VMEM((T + PAD_A, LANES), F32),
                        pltpu.VMEM((T + PAD_B, LANES), F32), pltpu.VMEM((T + PAD_B, LANES), F32),
                        pltpu.VMEM((T, LANES), F32)],
        compiler_params=_params(("parallel",)))(u, u, u, u, u, conf_dw, sconv_w, d_a1, d_s2)


def _attn_specs(T, C, q_off_blocks):
    kvw = N_KV_HEADS * HEAD_DIM
    kb = (5 * C + C) // kvw
    qs = pl.BlockSpec((ATTN_BLOCK, C), lambda n: (n, 5))
    kc = pl.BlockSpec((ATTN_BLOCK, kvw), lambda n: (n, kb))
    kp = pl.BlockSpec((ATTN_BLOCK, kvw), lambda n: (jnp.maximum(n - 1, 0), kb))
    vc = pl.BlockSpec((ATTN_BLOCK, kvw), lambda n: (n, kb + 1))
    vp = pl.BlockSpec((ATTN_BLOCK, kvw), lambda n: (jnp.maximum(n - 1, 0), kb + 1))
    return qs, kc, kp, vc, vp


def _attn_masks(n):
    row = lax.broadcasted_iota(jnp.int32, (ATTN_BLOCK, ATTN_BLOCK), 0)
    col = lax.broadcasted_iota(jnp.int32, (ATTN_BLOCK, ATTN_BLOCK), 1)
    return col <= row, jnp.logical_and(col > row, n > 0)


def _attn_scores(qh, kc, kp, mask_c, mask_p):
    scale = HEAD_DIM ** -0.5
    s_c = lax.dot_general(qh, kc, NT, preferred_element_type=F32) * scale
    s_p = lax.dot_general(qh, kp, NT, preferred_element_type=F32) * scale
    return jnp.where(mask_c, s_c, NEG_BIG), jnp.where(mask_p, s_p, NEG_BIG)


def _attn_fwd(u, sinks, C):
    T = u.shape[0]
    H = C // HEAD_DIM
    grp = H // N_KV_HEADS

    def body(sink_ref, q_ref, kc_ref, kp_ref, vc_ref, vp_ref, o_ref, lse_ref):
        n = pl.program_id(0)
        mask_c, mask_p = _attn_masks(n)
        for h in range(H):
            kv = slice((h // grp) * HEAD_DIM, (h // grp + 1) * HEAD_DIM)
            hs = slice(h * HEAD_DIM, (h + 1) * HEAD_DIM)
            sink = sink_ref[h]
            s_c, s_p = _attn_scores(q_ref[:, hs], kc_ref[:, kv], kp_ref[:, kv], mask_c, mask_p)
            m = jnp.maximum(jnp.maximum(jnp.max(s_c, axis=-1, keepdims=True), jnp.max(s_p, axis=-1, keepdims=True)), sink)
            p_c = jnp.exp(s_c - m)
            p_p = jnp.exp(s_p - m)
            den = jnp.sum(p_c, axis=-1, keepdims=True) + jnp.sum(p_p, axis=-1, keepdims=True) + jnp.exp(sink - m)
            acc = jnp.dot(p_c.astype(BF16), vc_ref[:, kv], preferred_element_type=F32)
            acc = acc + jnp.dot(p_p.astype(BF16), vp_ref[:, kv], preferred_element_type=F32)
            o_ref[:, hs] = (acc / den).astype(BF16)
            lse_ref[:, h:h + 1] = m + jnp.log(den)

    qs, kc, kp, vc, vp = _attn_specs(T, C, 5)
    return pl.pallas_call(
        body, name="attn_fwd", grid=(T // ATTN_BLOCK,),
        in_specs=[pl.BlockSpec(memory_space=pltpu.SMEM), qs, kc, kp, vc, vp],
        out_specs=[pl.BlockSpec((ATTN_BLOCK, C), lambda n: (n, 0)), pl.BlockSpec((ATTN_BLOCK, H), lambda n: (n, 0))],
        out_shape=[jax.ShapeDtypeStruct((T, C), BF16), jax.ShapeDtypeStruct((T, H), F32)],
        compiler_params=_params(("parallel",)))(sinks, u, u, u, u, u)


def _attn_bwd(u, o, lse, d_o, sinks, C):
    T = u.shape[0]
    H = C // HEAD_DIM
    grp = H // N_KV_HEADS
    kvw = N_KV_HEADS * HEAD_DIM
    nblk = T // ATTN_BLOCK
    scale = HEAD_DIM ** -0.5

    def body(sink_ref, q_ref, kc_ref, kp_ref, vc_ref, vp_ref, o_ref, lse_ref, do_ref,
             dq_ref, dk_ref, dv_ref, ds_ref, dk_acc, dv_acc):
        n = pl.program_id(0)

        @pl.when(n == 0)
        def _():
            dk_acc[...] = jnp.zeros_like(dk_acc)
            dv_acc[...] = jnp.zeros_like(dv_acc)
            ds_ref[...] = jnp.zeros_like(ds_ref)

        mask_c, mask_p = _attn_masks(n)
        cur = pl.ds(pl.multiple_of(n * ATTN_BLOCK, ATTN_BLOCK), ATTN_BLOCK)
        prev = pl.ds(pl.multiple_of(jnp.maximum(n - 1, 0) * ATTN_BLOCK, ATTN_BLOCK), ATTN_BLOCK)
        for g in range(N_KV_HEADS):
            kv = slice(g * HEAD_DIM, (g + 1) * HEAD_DIM)
            kc, kp, vc, vp = kc_ref[:, kv], kp_ref[:, kv], vc_ref[:, kv], vp_ref[:, kv]
            dk_c = dk_p = dv_c = dv_p = None
            for h in range(g * grp, (g + 1) * grp):
                hs = slice(h * HEAD_DIM, (h + 1) * HEAD_DIM)
                qh, doh = q_ref[:, hs], do_ref[:, hs]
                lse_h = lse_ref[:, h:h + 1]
                s_c, s_p = _attn_scores(qh, kc, kp, mask_c, mask_p)
                p_c = jnp.exp(s_c - lse_h)
                p_p = jnp.exp(s_p - lse_h)
                delta = jnp.sum(doh.astype(F32) * o_ref[:, hs].astype(F32), axis=-1, keepdims=True)
                dp_c = lax.dot_general(doh, vc, NT, preferred_element_type=F32)
                dp_p = lax.dot_general(doh, vp, NT, preferred_element_type=F32)
                ds_c = (p_c * (dp_c - delta) * scale).astype(BF16)
                ds_p = (p_p * (dp_p - delta) * scale).astype(BF16)
                dq = jnp.dot(ds_c, kc, preferred_element_type=F32) + jnp.dot(ds_p, kp, preferred_element_type=F32)
                dq_ref[:, hs] = dq.astype(BF16)
                t_kc = lax.dot_general(ds_c, qh, TN, preferred_element_type=F32)
                t_kp = lax.dot_general(ds_p, qh, TN, preferred_element_type=F32)
                t_vc = lax.dot_general(p_c.astype(BF16), doh, TN, preferred_element_type=F32)
                t_vp = lax.dot_general(p_p.astype(BF16), doh, TN, preferred_element_type=F32)
                dk_c = t_kc if dk_c is None else dk_c + t_kc
                dk_p = t_kp if dk_p is None else dk_p + t_kp
                dv_c = t_vc if dv_c is None else dv_c + t_vc
                dv_p = t_vp if dv_p is None else dv_p + t_vp
                p_sink = jnp.exp(sink_ref[h] - lse_h)
                dsink = -jnp.sum(p_sink * delta, axis=0, keepdims=True)
                ds_ref[:, h:h + 1] += jnp.broadcast_to(dsink, (8, 1))
            dk_acc[cur, kv] += dk_c
            dk_acc[prev, kv] += dk_p
            dv_acc[cur, kv] += dv_c
            dv_acc[prev, kv] += dv_p

        @pl.when(n == nblk - 1)
        def _():
            dk_ref[...] = dk_acc[...].astype(BF16)
            dv_ref[...] = dv_acc[...].astype(BF16)

    qs, kc, kp, vc, vp = _attn_specs(T, C, 5)
    blk = pl.BlockSpec((ATTN_BLOCK, C), lambda n: (n, 0))
    full = pl.BlockSpec((T, kvw), lambda n: (0, 0))
    return pl.pallas_call(
        body, name="attn_bwd", grid=(nblk,),
        in_specs=[pl.BlockSpec(memory_space=pltpu.SMEM), qs, kc, kp, vc, vp, blk,
                  pl.BlockSpec((ATTN_BLOCK, H), lambda n: (n, 0)), blk],
        out_specs=[blk, full, full, pl.BlockSpec((8, H), lambda n: (0, 0))],
        out_shape=[jax.ShapeDtypeStruct((T, C), BF16), jax.ShapeDtypeStruct((T, kvw), BF16),
                   jax.ShapeDtypeStruct((T, kvw), BF16), jax.ShapeDtypeStruct((8, H), F32)],
        scratch_shapes=[pltpu.VMEM((T, kvw), F32), pltpu.VMEM((T, kvw), F32)],
        compiler_params=_params(("arbitrary",)))(sinks, u, u, u, u, u, o, lse, d_o)


MERGE_COLS = 512


def _merge_specs(T, D, I):
    tr = _pick(T, ROW_TILE, 8)
    tc = _pick(D, MERGE_COLS)
    g0 = (I - N_BRANCH * D) // tc
    per = D // tc
    gspecs = [pl.BlockSpec((tr, tc), functools.partial(lambda j, i, b: (i, g0 + b * per + j), b=b)) for b in range(N_BRANCH)]
    tile = pl.BlockSpec((tr, tc), lambda j, i: (i, j))
    bias = pl.BlockSpec((N_BRANCH, tc), lambda j, i: (0, j))
    return tr, tc, gspecs, tile, bias


def _merge_fwd(u, gate_b, ya, yb, yc):
    T, I = u.shape
    D = ya.shape[1]
    tr, tc, gspecs, tile, bias = _merge_specs(T, D, I)

    def body(g0_ref, g1_ref, g2_ref, b_ref, ya_ref, yb_ref, yc_ref, o_ref):
        acc = None
        for b, (g_ref, y_ref) in enumerate(((g0_ref, ya_ref), (g1_ref, yb_ref), (g2_ref, yc_ref))):
            gate = _sigmoid(g_ref[...].astype(F32) + b_ref[b:b + 1, :])
            term = gate * y_ref[...].astype(F32)
            acc = term if acc is None else acc + term
        o_ref[...] = acc.astype(BF16)

    return pl.pallas_call(
        body, name="merge_fwd", grid=(D // tc, T // tr), in_specs=[*gspecs, bias, tile, tile, tile], out_specs=tile,
        out_shape=jax.ShapeDtypeStruct((T, D), BF16),
        compiler_params=_params(("parallel", "parallel")))(u, u, u, gate_b, ya, yb, yc)


def _merge_bwd(u, gate_b, ya, yb, yc, dm):
    T, I = u.shape
    D = ya.shape[1]
    tr, tc, gspecs, tile, bias = _merge_specs(T, D, I)

    def body(g0_ref, g1_ref, g2_ref, b_ref, ya_ref, yb_ref, yc_ref, dm_ref,
             dya_ref, dyb_ref, dyc_ref, dg0_ref, dg1_ref, dg2_ref, db_ref):
        dmv = dm_ref[...].astype(F32)
        first = pl.program_id(1) == 0
        for b, (g_ref, y_ref, dy_ref, dg_ref) in enumerate(((g0_ref, ya_ref, dya_ref, dg0_ref),
                                                           (g1_ref, yb_ref, dyb_ref, dg1_ref),
                                                           (g2_ref, yc_ref, dyc_ref, dg2_ref))):
            gate = _sigmoid(g_ref[...].astype(F32) + b_ref[b:b + 1, :])
            dy_ref[...] = (dmv * gate).astype(BF16)
            dpre = dmv * y_ref[...].astype(F32) * gate * (1.0 - gate)
            dg_ref[...] = dpre.astype(BF16)
            part = jnp.sum(dpre, axis=0, keepdims=True)

            @pl.when(first)
            def _():
                db_ref[b:b + 1, :] = part

            @pl.when(jnp.logical_not(first))
            def _():
                db_ref[b:b + 1, :] += part

    act = jax.ShapeDtypeStruct((T, D), BF16)
    return pl.pallas_call(
        body, name="merge_bwd", grid=(D // tc, T // tr), in_specs=[*gspecs, bias, tile, tile, tile, tile],
        out_specs=[tile] * 6 + [bias], out_shape=[act] * 6 + [jax.ShapeDtypeStruct((N_BRANCH, D), F32)],
        compiler_params=_params(("parallel", "arbitrary")))(u, u, u, gate_b, ya, yb, yc, dm)


ELEMS_PER_TILE = 256 * 1024


def _row_tile(r, c):
    return _pick(r, max(16, ELEMS_PER_TILE // c), 16) if r % 16 == 0 else r


def _cast_place(w, layer, my_block):
    L, r, c = w.shape
    tr = _row_tile(r, c)

    def body(blk_ref, w_ref, o_ref):
        o_ref[...] = w_ref[...].astype(BF16)

    return pl.pallas_call(
        body, name="cast_place",
        grid_spec=pltpu.PrefetchScalarGridSpec(
            num_scalar_prefetch=1, grid=(r // tr,),
            in_specs=[pl.BlockSpec((None, tr, c), lambda i, blk: (layer, i, 0))],
            out_specs=pl.BlockSpec((None, tr, c), lambda i, blk: (blk[0], i, 0))),
        out_shape=jax.ShapeDtypeStruct((N_DEV, r, c), BF16), compiler_params=_params(("parallel",)))(my_block, w)


def _adamw_math(w, g, m, v):
    m = ADAM_B1 * m + (1.0 - ADAM_B1) * g
    v = ADAM_B2 * v + (1.0 - ADAM_B2) * (g * g)
    m_hat = m / (1.0 - ADAM_B1 ** ADAM_STEP)
    v_hat = v / (1.0 - ADAM_B2 ** ADAM_STEP)
    delta = -ADAM_LR * (m_hat / (jnp.sqrt(v_hat) + ADAM_EPS) + ADAM_WD * w)
    return delta, m, v


def _sum_parts(part_ref):
    acc = part_ref[0].astype(F32)
    for s in range(1, N_DEV):
        acc = acc + part_ref[s].astype(F32)
    return acc


def _sum8(parts):
    L, _, r, c = parts.shape
    tr = _row_tile(r, c)

    def body(p_ref, o_ref):
        o_ref[...] = _sum_parts(p_ref)

    return pl.pallas_call(
        body, name="sum8", grid=(L, r // tr),
        in_specs=[pl.BlockSpec((None, N_DEV, tr, c), lambda l, i: (l, 0, i, 0))],
        out_specs=pl.BlockSpec((None, tr, c), lambda l, i: (l, i, 0)),
        out_shape=jax.ShapeDtypeStruct((L, r, c), F32), compiler_params=_params(("parallel", "parallel")))(parts)


def _adamw(w, g, m, v):
    L, r, c = w.shape
    tr = _row_tile(r, c)
    spec = pl.BlockSpec((None, tr, c), lambda l, i: (l, i, 0))

    def body(w_ref, g_ref, m_ref, v_ref, d_ref, mo_ref, vo_ref):
        d, mn, vn = _adamw_math(w_ref[...], g_ref[...], m_ref[...], v_ref[...])
        d_ref[...] = d
        mo_ref[...] = mn
        vo_ref[...] = vn

    shp = jax.ShapeDtypeStruct(w.shape, F32)
    return pl.pallas_call(body, name="adamw", grid=(L, r // tr), in_specs=[spec] * 4, out_specs=[spec] * 3,
                          out_shape=[shp] * 3, compiler_params=_params(("parallel", "parallel")))(w, g, m, v)


N_CHIPS = 4
CHIP_XOR = (0, 2, 1, 3)


def _chip_sum(part4, sib4, own_all, layer, pos):
    _, _, r, c = part4.shape
    tr = _row_tile(r, c)

    def chip(p, s):
        return jnp.bitwise_xor(2 * p[0] + p[1], CHIP_XOR[s])

    mine = [pl.BlockSpec((None, None, tr, c), functools.partial(lambda i, p, s: (chip(p, s), p[2], i, 0), s=s))
            for s in range(N_CHIPS)]
    theirs = [pl.BlockSpec((None, None, tr, c), functools.partial(lambda i, p, s: (chip(p, s), 0, i, 0), s=s))
              for s in range(N_CHIPS)]

    def body(pos_ref, *refs):
        a, b = refs[:N_CHIPS], refs[N_CHIPS:2 * N_CHIPS]
        out_ref, own_ref = refs[2 * N_CHIPS + 1], refs[2 * N_CHIPS + 2]
        own_ref[...] = a[0][...].astype(F32) + b[0][...].astype(F32)
        for s in range(1, N_CHIPS):
            out_ref[s - 1] = (a[s][...].astype(F32) + b[s][...].astype(F32)).astype(BF16)

    return pl.pallas_call(
        body, name="chip_sum",
        grid_spec=pltpu.PrefetchScalarGridSpec(
            num_scalar_prefetch=1, grid=(r // tr,),
            in_specs=[*mine, *theirs, pl.BlockSpec(memory_space=pl.ANY)],
            out_specs=[pl.BlockSpec((N_CHIPS - 1, tr, c), lambda i, p: (0, i, 0)),
                       pl.BlockSpec((None, tr, c), lambda i, p: (layer, i, 0))]),
        out_shape=[jax.ShapeDtypeStruct((N_CHIPS - 1, r, c), BF16), jax.ShapeDtypeStruct(own_all.shape, F32)],
        input_output_aliases={1 + 2 * N_CHIPS: 1},
        compiler_params=_params(("parallel",)))(pos, *([part4] * N_CHIPS), *([sib4] * N_CHIPS), own_all)


def _sum_chips(own_ref, got_ref):
    acc = own_ref[...]
    for s in range(N_CHIPS - 1):
        acc = acc + got_ref[s].astype(F32)
    return acc


def _sum4_adamw(own, got, w, m, v):
    L, r, c = w.shape
    tr = _row_tile(r, c)
    spec = pl.BlockSpec((None, tr, c), lambda l, i: (l, i, 0))

    def body(own_ref, got_ref, w_ref, m_ref, v_ref, g_ref, d_ref, mo_ref, vo_ref):
        g = _sum_chips(own_ref, got_ref)
        d, mn, vn = _adamw_math(w_ref[...], g, m_ref[...], v_ref[...])
        g_ref[...] = g
        d_ref[...] = d
        mo_ref[...] = mn
        vo_ref[...] = vn

    shp = jax.ShapeDtypeStruct(w.shape, F32)
    return pl.pallas_call(
        body, name="sum4_adamw", grid=(L, r // tr),
        in_specs=[spec, pl.BlockSpec((None, N_CHIPS - 1, tr, c), lambda l, i: (l, 0, i, 0)), spec, spec, spec],
        out_specs=[spec] * 4, out_shape=[shp] * 4,
        compiler_params=_params(("parallel", "parallel")))(own, got, w, m, v)


def _me():
    return lax.axis_index("x"), lax.axis_index("y"), lax.axis_index("c")


def _flip(pos, k):
    x, y, c = pos
    return (1 - x if k & 4 else x, 1 - y if k & 2 else y, 1 - c if k & 1 else c)


def _block_of(pos):
    return 4 * pos[0] + 2 * pos[1] + pos[2]


ANY = pl.BlockSpec(memory_space=pl.ANY)
SIBLING = 1
OTHER_CHIPS = (4, 2, 6)


def _all_gather(name, arrays, layer=None):
    n = len(arrays)
    shapes = [a.shape[-2:] for a in arrays]

    def body(*refs):
        srcs, outs = refs[:n], refs[n:2 * n]
        send_sems, recv_sems, local_sems = refs[2 * n:]
        me = _me()
        sib = _flip(me, SIBLING)

        def src_of(a):
            return srcs[a] if layer is None else srcs[a].at[layer]

        def copy(a, k, block_pos, to, src=None):
            dst = outs[a].at[_block_of(block_pos)]
            return pltpu.make_async_remote_copy(
                src_ref=dst if src is None else src, dst_ref=dst, send_sem=send_sems.at[a, k],
                recv_sem=recv_sems.at[a, k], device_id=to, device_id_type=MESH)

        mine = [pltpu.make_async_copy(src_of(a), outs[a].at[_block_of(me)], local_sems.at[a]) for a in range(n)]
        for cp in mine:
            cp.start()
        first = []
        for a in range(n):
            first.append(copy(a, 0, me, sib, src=src_of(a)))
            for j, k in enumerate(OTHER_CHIPS):
                first.append(copy(a, 1 + j, me, _flip(me, k), src=src_of(a)))
        for cp in first:
            cp.start()
        passed = []
        for j, k in enumerate(OTHER_CHIPS):
            for a in range(n):
                copy(a, 1 + j, _flip(me, k), me).wait_recv()
                fw = copy(a, 4 + j, _flip(me, k), sib)
                fw.start()
                passed.append(fw)
        for a in range(n):
            copy(a, 0, sib, me).wait_recv()
            for j, k in enumerate(OTHER_CHIPS):
                copy(a, 4 + j, _flip(sib, k), me).wait_recv()
        for cp in first + passed:
            cp.wait_send()
        for cp in mine:
            cp.wait()

    return pl.pallas_call(
        body, name=name, in_specs=[ANY] * n, out_specs=[ANY] * n,
        out_shape=[jax.ShapeDtypeStruct((N_DEV, *s), a.dtype) for s, a in zip(shapes, arrays)],
        scratch_shapes=[pltpu.SemaphoreType.DMA((n, 7)), pltpu.SemaphoreType.DMA((n, 7)), pltpu.SemaphoreType.DMA((n,))],
    )(*arrays)


HBM = pl.BlockSpec(memory_space=pltpu.HBM)
SEM = pl.BlockSpec(memory_space=pltpu.SEMAPHORE)
DATAFLOW = pltpu.SideEffectType.DATAFLOW_SIDE_EFFECTING
GATHER_RELATIONS = (SIBLING,) + OTHER_CHIPS


def _in_hbm(a):
    return pltpu.with_memory_space_constraint(a, pltpu.HBM)


def _hbm_like(a):
    return pltpu.HBM(a.shape, a.dtype)


def _copies_start(name, srcs, lands, plan, n_copies, after=None):
    ns, n = len(srcs), len(lands)
    deps = [] if after is None else [after]

    def body(*refs):
        k0 = ns + n + len(deps)
        send_sems, recv_sems, token = refs[k0], refs[k0 + 1], refs[k0 + 2 + n]
        for s, (src, dst, peer, _) in enumerate(plan(_me(), refs[:ns], refs[ns:ns + n])):
            pltpu.make_async_remote_copy(src_ref=src, dst_ref=dst, send_sem=send_sems.at[s], recv_sem=recv_sems.at[s],
                                         device_id=peer, device_id_type=MESH).start()
        token[...] = jnp.zeros(TOKEN, F32)

    out = pl.pallas_call(
        body, name=name, in_specs=[*([HBM] * (ns + n)), *([ANY] * len(deps))],
        out_specs=[SEM, SEM, *([HBM] * n), pl.BlockSpec(memory_space=pltpu.VMEM)],
        out_shape=[pltpu.SemaphoreType.DMA((n_copies,)), pltpu.SemaphoreType.DMA((n_copies,)),
                   *[_hbm_like(a) for a in lands], jax.ShapeDtypeStruct(TOKEN, F32)],
        input_output_aliases={ns + a: 2 + a for a in range(n)},
        compiler_params=pltpu.CompilerParams(has_side_effects=DATAFLOW),
    )(*[_in_hbm(a) for a in srcs], *[_in_hbm(a) for a in lands], *deps)
    return dict(send=out[0], recv=out[1], srcs=list(srcs), plan=plan), list(out[2:2 + n]), out[2 + n]


def _copies_wait(name, flight, lands, after):
    srcs, plan = flight["srcs"], flight["plan"]
    ns, n = len(srcs), len(lands)
    after = list(after) if isinstance(after, (list, tuple)) else [after]

    def body(*refs):
        send_sems, recv_sems = refs[ns + n], refs[ns + n + 1]
        for s, (src, dst, peer, landing) in enumerate(plan(_me(), refs[:ns], refs[ns:ns + n])):
            pltpu.make_async_remote_copy(src_ref=src, dst_ref=dst, send_sem=send_sems.at[s], recv_sem=recv_sems.at[s],
                                         device_id=peer, device_id_type=MESH).wait_send()
            pltpu.make_async_remote_copy(src_ref=landing, dst_ref=landing, send_sem=send_sems.at[s],
                                         recv_sem=recv_sems.at[s], device_id=peer, device_id_type=MESH).wait_recv()

    out = pl.pallas_call(
        body, name=name, in_specs=[*([HBM] * (ns + n)), SEM, SEM, *([ANY] * len(after))], out_specs=[HBM] * n,
        out_shape=[_hbm_like(a) for a in lands], input_output_aliases={ns + a: a for a in range(n)},
        compiler_params=pltpu.CompilerParams(has_side_effects=DATAFLOW),
    )(*srcs, *lands, flight["send"], flight["recv"], *after)
    return list(out)


def _gather_plan(me, srcs, lands):
    plan = []
    for land in lands:
        own = land.at[_block_of(me)]
        for k in GATHER_RELATIONS:
            peer = _flip(me, k)
            plan.append((own, own, peer, land.at[_block_of(peer)]))
    return plan


def _sibling_plan(me, srcs, lands):
    sib = _flip(me, SIBLING)
    return [(src.at[:, pl.ds(1 - me[2], 1)], land, sib, land) for src, land in zip(srcs, lands)]


def _scatter_plan(layer):
    def plan(me, srcs, lands):
        out = []
        for src, land in zip(srcs, lands):
            for j, k in enumerate(OTHER_CHIPS):
                out.append((src.at[j], land.at[layer, j], _flip(me, k), land.at[layer, j]))
        return out
    return plan


def _gather_forward(name, lands, dep=None):
    n = len(lands)
    deps = [] if dep is None else [dep]

    def body(*refs):
        land = refs[n + len(deps):2 * n + len(deps)]
        send_sems, recv_sems = refs[2 * n + len(deps):]
        me = _me()
        sib = _flip(me, SIBLING)
        sends = []
        for a in range(n):
            for j, k in enumerate(OTHER_CHIPS):
                blk = land[a].at[_block_of(_flip(me, k))]
                cp = pltpu.make_async_remote_copy(src_ref=blk, dst_ref=blk, send_sem=send_sems.at[a, j],
                                                  recv_sem=recv_sems.at[a, j], device_id=sib, device_id_type=MESH)
                cp.start()
                sends.append(cp)
        for a in range(n):
            for j, k in enumerate(OTHER_CHIPS):
                blk = land[a].at[_block_of(_flip(sib, k))]
                pltpu.make_async_remote_copy(src_ref=blk, dst_ref=blk, send_sem=send_sems.at[a, j],
                                             recv_sem=recv_sems.at[a, j], device_id=sib, device_id_type=MESH).wait_recv()
        for cp in sends:
            cp.wait_send()

    nrel = len(OTHER_CHIPS)
    return pl.pallas_call(
        body, name=name, in_specs=[ANY] * (n + len(deps)), out_specs=[ANY] * n,
        out_shape=[jax.ShapeDtypeStruct(a.shape, a.dtype) for a in lands],
        input_output_aliases={a: a for a in range(n)},
        scratch_shapes=[pltpu.SemaphoreType.DMA((n, nrel)), pltpu.SemaphoreType.DMA((n, nrel))],
    )(*lands, *deps)


def _after(small, tokens):
    for t in tokens:
        small = small + t[0:1, 0:1]
    return small


def _layer_fwd(xc, l, W, P, dims, tokens=()):
    T, D, C, I, F = dims
    h = _rms_fwd(xc, _after(P["norm_mix_g"][l:l + 1], tokens))
    u = _mm_plain("mm_u", h, W["w_in"][l].reshape(I, D), "NT", [BF16], tn=1280)[0]
    a1, s2 = _convs_fwd(u, P["conf_dw"][l], P["sconv_w"][l], C)
    a3 = _ln_silu(a1, P["conf_ln_g"][l:l + 1], P["conf_ln_b"][l:l + 1])
    o, lse = _attn_fwd(u, P["sinks"][l], C)
    ya = _mm_nn_colblocked("mm_branch_out", a3, W["w_conf_out"][l], [BF16], tm=2048)[0]
    yb = _mm_nn_colblocked("mm_branch_out", s2, W["w_sconv_out"][l], [BF16], tm=2048)[0]
    yc = _mm_nn_colblocked("mm_branch_out", o, W["w_attn_out"][l], [BF16], tm=2048)[0]
    merged = _merge_fwd(u, P["gate_b"][l], ya, yb, yc)
    x1 = _mm_plain("mm_mix", merged, W["w_mix_out"][l].reshape(D, D), "NN", [F32], _epi_resid, [xc], tk=1024)[0]
    h2 = _rms_fwd(x1, P["norm_ffn_g"][l:l + 1])
    up, act = _mm_nn_colblocked("mm_up", h2, W["w_up"][l], [BF16, BF16], _epi_relu2)
    x2 = _mm_plain("mm_down", act, W["w_down"][l].reshape(F, D), "NN", [F32], _epi_resid, [x1], tk=1024)[0]
    saved = dict(xc=xc, h=h, u=u, a1=a1, a3=a3, s2=s2, o=o, lse=lse, ya=ya, yb=yb, yc=yc, merged=merged, x1=x1, h2=h2,
                 up=up, act=act)
    return x2, saved


def _bwd_mlp(dx2, dx2_b, l, W, P, S, dims, dep=None):
    T, D, C, I, F = dims
    d_up = _mm_plain("mm_d_up", dx2_b, W["w_down"][l].reshape(F, D), "NT", [BF16], _epi_drelu2, [S["up"]], dep=dep)[0]
    g_down = _mm_plain("mm_g_down", S["act"], dx2_b, "TN", [BF16])[0]
    dh2 = _mm_nt_colblocked("mm_d_h2", d_up, W["w_up"][l], F32)
    g_up = _mm_tn_colblocked_out("mm_g_up", S["h2"], d_up, BF16)
    dx1, dx1_b, dg_ffn = _rms_bwd(dh2, S["x1"], P["norm_ffn_g"][l:l + 1], dx2)
    return dx1, dx1_b, dict(w_up=g_up, w_down=g_down.reshape(N_DEV, F // N_DEV, D)), dg_ffn


def _bwd_mix_out(dx1_b, l, W, P, S, dims, dep=None):
    T, D, C, I, F = dims
    dm = _mm_plain("mm_d_merged", dx1_b, W["w_mix_out"][l].reshape(D, D), "NT", [BF16], dep=dep)[0]
    g_mix = _mm_plain("mm_g_mix", S["merged"], dx1_b, "TN", [BF16])[0]
    merge = _merge_bwd(S["u"], P["gate_b"][l], S["ya"], S["yb"], S["yc"], dm)
    return g_mix.reshape(N_DEV, D // N_DEV, D), merge


def _bwd_mixers(dx1, merge, dg_ffn, g_mix, l, W, P, S, dims, dep=None):
    T, D, C, I, F = dims
    d_ya, d_yb, d_yc, dg0, dg1, dg2, d_gate_b = merge
    d_a3 = _mm_nt_colblocked("mm_d_branch", d_ya, W["w_conf_out"][l], BF16, tm=2048, dep=dep)
    d_s2 = _mm_nt_colblocked("mm_d_branch", d_yb, W["w_sconv_out"][l], BF16, tm=2048)
    d_o = _mm_nt_colblocked("mm_d_branch", d_yc, W["w_attn_out"][l], BF16, tm=2048)
    g_conf = _mm_tn_colblocked_out("mm_g_branch", S["a3"], d_ya, BF16)
    g_sconv = _mm_tn_colblocked_out("mm_g_branch", S["s2"], d_yb, BF16)
    g_attn = _mm_tn_colblocked_out("mm_g_branch", S["o"], d_yc, BF16)
    d_a1, d_ln_g, d_ln_b = _ln_silu_bwd(S["a1"], P["conf_ln_g"][l:l + 1], P["conf_ln_b"][l:l + 1], d_a3)
    d_av, d_ag, d_bg, d_cg, d_bh, d_conf_dw, d_sconv_w = _convs_bwd(S["u"], P["conf_dw"][l], P["sconv_w"][l], d_a1, d_s2, C)
    dq, dk, dv, d_sinks = _attn_bwd(S["u"], S["o"], S["lse"], d_o, P["sinks"][l], C)
    du = jnp.concatenate([d_av, d_ag, d_bg, d_cg, d_bh, dq, dk, dv, dg0, dg1, dg2], axis=1)
    g_in = _mm_plain("mm_g_in", du, S["h"], "TN", [BF16], tm=1280)[0]
    dh = _mm_plain("mm_d_h", du, W["w_in"][l].reshape(I, D), "NN", [F32], tk=1280)[0]
    dx, dx_b, dg_mix = _rms_bwd(dh, S["xc"], P["norm_mix_g"][l:l + 1], dx1)
    big = dict(w_in=g_in.reshape(N_DEV, I // N_DEV, D), w_conf_out=g_conf, w_sconv_out=g_sconv, w_attn_out=g_attn,
               w_mix_out=g_mix)
    small = dict(norm_mix_g=dg_mix, gate_b=d_gate_b, conf_dw=d_conf_dw, conf_ln_g=d_ln_g, conf_ln_b=d_ln_b,
                 sconv_w=d_sconv_w, sinks=d_sinks[0:1], norm_ffn_g=dg_ffn)
    return dx, dx_b, big, small


BIG = ("w_in", "w_conf_out", "w_sconv_out", "w_attn_out", "w_mix_out", "w_up", "w_down")
MLP_WEIGHTS = ("w_down", "w_up")
MIXER_WEIGHTS = ("w_mix_out", "w_conf_out", "w_sconv_out", "w_attn_out", "w_in")
SMALL_PER_LAYER = ("norm_mix_g", "gate_b", "conf_dw", "conf_ln_g", "conf_ln_b", "sconv_w", "sinks", "norm_ffn_g")
WEIGHTS = ("norm_mix_g", "w_in", "gate_b", "conf_dw", "conf_ln_g", "conf_ln_b", "w_conf_out", "sconv_w", "w_sconv_out",
           "sinks", "w_attn_out", "w_mix_out", "norm_ffn_g", "w_up", "w_down", "final_g")


SUBLANES = 8


def _nrows(n_el, width):
    per_tile = SUBLANES * width
    return SUBLANES * (-(-n_el // per_tile))


def _rows(a, width):
    flat = a.reshape(-1)
    nrow = _nrows(flat.shape[0], width)
    return jnp.pad(flat, (0, nrow * width - flat.shape[0])).reshape(nrow, width)


def _as3d(a):
    if a.ndim == 1:
        return a.reshape(1, 1, -1)
    if a.ndim == 2:
        return a.reshape(1, *a.shape)
    return a


def kernel(x, norm_mix_g, w_in, gate_b, conf_dw, conf_ln_g, conf_ln_b, w_conf_out, sconv_w, w_sconv_out, sinks, w_attn_out, w_mix_out, norm_ffn_g, w_up, w_down, final_g, loss_target, m_norm_mix_g, m_w_in, m_gate_b, m_conf_dw, m_conf_ln_g, m_conf_ln_b, m_w_conf_out, m_sconv_w, m_w_sconv_out, m_sinks, m_w_attn_out, m_w_mix_out, m_norm_ffn_g, m_w_up, m_w_down, m_final_g, v_norm_mix_g, v_w_in, v_gate_b, v_conf_dw, v_conf_ln_g, v_conf_ln_b, v_w_conf_out, v_sconv_w, v_w_sconv_out, v_sinks, v_w_attn_out, v_w_mix_out, v_norm_ffn_g, v_w_up, v_w_down, v_final_g):
    w = dict(norm_mix_g=norm_mix_g, w_in=w_in, gate_b=gate_b, conf_dw=conf_dw, conf_ln_g=conf_ln_g, conf_ln_b=conf_ln_b,
             w_conf_out=w_conf_out, sconv_w=sconv_w, w_sconv_out=w_sconv_out, sinks=sinks, w_attn_out=w_attn_out,
             w_mix_out=w_mix_out, norm_ffn_g=norm_ffn_g, w_up=w_up, w_down=w_down, final_g=final_g)
    mom = dict(norm_mix_g=m_norm_mix_g, w_in=m_w_in, gate_b=m_gate_b, conf_dw=m_conf_dw, conf_ln_g=m_conf_ln_g,
               conf_ln_b=m_conf_ln_b, w_conf_out=m_w_conf_out, sconv_w=m_sconv_w, w_sconv_out=m_w_sconv_out,
               sinks=m_sinks, w_attn_out=m_w_attn_out, w_mix_out=m_w_mix_out, norm_ffn_g=m_norm_ffn_g, w_up=m_w_up,
               w_down=m_w_down, final_g=m_final_g)
    var = dict(norm_mix_g=v_norm_mix_g, w_in=v_w_in, gate_b=v_gate_b, conf_dw=v_conf_dw, conf_ln_g=v_conf_ln_g,
               conf_ln_b=v_conf_ln_b, w_conf_out=v_w_conf_out, sconv_w=v_sconv_w, w_sconv_out=v_w_sconv_out,
               sinks=v_sinks, w_attn_out=v_w_attn_out, w_mix_out=v_w_mix_out, norm_ffn_g=v_norm_ffn_g, w_up=v_w_up,
               w_down=v_w_down, final_g=v_final_g)

    _, T, D = x.shape
    L = w_in.shape[0]
    C = D // 2
    I = w_in.shape[2] * N_DEV
    F = w_up.shape[2] * N_DEV
    dims = (T, D, C, I, F)
    my_block = _block_of(_me())

    w["w_in"], mom["w_in"], var["w_in"] = (jnp.swapaxes(a, 1, 2) for a in (w_in, m_w_in, v_w_in))

    shard_names = ("gate_b", "conf_dw", "sconv_w")
    packed = jnp.concatenate([_rows(w[k], LANES) for k in shard_names], axis=0)
    gathered = _all_gather("gather_small", [packed])[0]

    pos = jnp.stack(_me()).astype(jnp.int32)
    blk = my_block.reshape(1).astype(jnp.int32)
    W = {k: [_cast_place(w[k], l, blk) for l in range(L)] for k in BIG}
    n_gather = len(BIG) * len(GATHER_RELATIONS)
    flight, lands, tok = _copies_start("gather_start_0", [], [W[k][0] for k in BIG], _gather_plan, n_gather,
                                       after=gathered)
    later_layers = [W[k][l] for l in range(1, L) for k in BIG]

    P = dict(norm_mix_g=norm_mix_g, conf_ln_g=conf_ln_g, conf_ln_b=conf_ln_b, sinks=sinks, norm_ffn_g=norm_ffn_g)
    row0 = 0
    for k in shard_names:
        n_el = w[k].size
        nrow = _nrows(n_el, LANES)
        part = gathered[:, row0:row0 + nrow].reshape(N_DEV, -1)[:, :n_el].reshape(N_DEV, *w[k].shape)
        P[k] = jnp.moveaxis(part, 0, 2).reshape(*w[k].shape[:2], N_DEV * w[k].shape[2])
        row0 += nrow

    xc = x.reshape(T, D)
    saved = []
    for l in range(L):
        lands = _copies_wait(f"gather_wait_{l}", flight, lands, xc if l else [tok, *later_layers])
        dep = None
        if l + 1 < L:
            flight, next_lands, dep = _copies_start(f"gather_start_{l + 1}", [], [W[k][l + 1] for k in BIG],
                                                    _gather_plan, n_gather, after=lands[0])
        lands = _gather_forward("gather_forward", lands, dep)
        for k, g in zip(BIG, lands):
            W[k][l] = g
        xc, S = _layer_fwd(xc, l, W, P, dims)
        saved.append(S)
        lands = next_lands if l + 1 < L else None
    dx, dx_b, d_final_g, loss_tile = _loss_head(xc, final_g.reshape(1, D), loss_target.reshape(T, D))

    own_all = {k: lax.empty((L, *W[k][0].shape[1:]), F32) for k in BIG}
    recv = {k: lax.empty((L, N_CHIPS - 1, *W[k][0].shape[1:]), BF16) for k in BIG}
    scatters = []

    def to_sibling(names, grads, l, after):
        part4 = [grads[k].reshape(N_CHIPS, 2, *grads[k].shape[1:]) for k in names]
        zone = [lax.empty((N_CHIPS, 1, *p.shape[2:]), BF16) for p in part4]
        fl, zone, token = _copies_start(f"sibling_start_{l}_{names[0]}", part4, zone, _sibling_plan, len(names), after)
        return dict(names=names, l=l, part4=part4, flight=fl, zone=zone), token

    def to_owners(group, after):
        names, l = group["names"], group["l"]
        sib4 = _copies_wait(f"sibling_wait_{l}_{names[0]}", group["flight"], group["zone"], after)
        chip_parts = []
        for k, p4, s4 in zip(names, group["part4"], sib4):
            cp, own_all[k] = _chip_sum(p4, s4, own_all[k], l, pos)
            chip_parts.append(cp)
        fl, zone, token = _copies_start(f"scatter_start_{l}_{names[0]}", chip_parts, [recv[k] for k in names],
                                        _scatter_plan(l), len(names) * len(OTHER_CHIPS))
        for k, g in zip(names, zone):
            recv[k] = g
        scatters.append((f"scatter_wait_{l}_{names[0]}", fl, names))
        return token

    small_grads = [None] * L
    dep, late_group = None, None
    for l in reversed(range(L)):
        S = saved[l]
        dx1, dx1_b, g_mlp, dg_ffn = _bwd_mlp(dx, dx_b, l, W, P, S, dims, dep)
        mlp_group, dep = to_sibling(MLP_WEIGHTS, g_mlp, l, dx1)
        if late_group is not None:
            dep = dep + to_owners(late_group, dx1)
        g_mix, merge = _bwd_mix_out(dx1_b, l, W, P, S, dims, dep)
        dep = to_owners(mlp_group, merge[0])
        dx, dx_b, g_mix_in, small_grads[l] = _bwd_mixers(dx1, merge, dg_ffn, g_mix, l, W, P, S, dims, dep)
        late_group, dep = to_sibling(MIXER_WEIGHTS, g_mix_in, l, dx)
    to_owners(late_group, dx)
    for name, fl, names in scatters:
        for k, g in zip(names, _copies_wait(name, fl, [recv[k] for k in names], dx)):
            recv[k] = g

    width = C
    pieces = [_rows(small_grads[l][k], width) for l in range(L) for k in SMALL_PER_LAYER]
    pieces += [_rows(d_final_g, width), _rows(loss_tile[0:1, 0:1], width)]
    partial = jnp.concatenate(pieces, axis=0)
    everyone = _all_gather("gather_small_grads", [partial])[0]
    total = _sum8(everyone.reshape(1, *everyone.shape))[0]
    grads = {}
    row0 = 0
    per_layer = {k: [] for k in SMALL_PER_LAYER}
    for l in range(L):
        for k in SMALL_PER_LAYER:
            shape = small_grads[l][k].shape
            n_el = small_grads[l][k].size
            nrow = _nrows(n_el, width)
            per_layer[k].append(total[row0:row0 + nrow].reshape(-1)[:n_el].reshape(shape))
            row0 += nrow
    nrow = _nrows(D, width)
    grads["final_g"] = total[row0:row0 + nrow].reshape(-1)[:D]
    row0 += nrow
    loss = total[row0, 0]
    for k in SMALL_PER_LAYER:
        full = jnp.stack(per_layer[k], axis=0)
        if k in shard_names:
            shard = w[k].shape[2]
            full = lax.dynamic_slice_in_dim(full, my_block * shard, shard, axis=2)
        grads[k] = full.reshape(w[k].shape)

    delta, new_m, new_v = {}, {}, {}
    for k in BIG:
        grads[k], delta[k], new_m[k], new_v[k] = _sum4_adamw(own_all[k], recv[k], w[k], mom[k], var[k])
    for out in (grads, delta, new_m, new_v):
        out["w_in"] = jnp.swapaxes(out["w_in"], 1, 2)
    for k in WEIGHTS:
        if k in BIG:
            continue
        d, mn, vn = _adamw(_as3d(w[k]), _as3d(grads[k]), _as3d(mom[k]), _as3d(var[k]))
        delta[k], new_m[k], new_v[k] = d.reshape(w[k].shape), mn.reshape(w[k].shape), vn.reshape(w[k].shape)

    return (loss, dx.reshape(1, T, D), *[grads[k] for k in WEIGHTS], *[delta[k] for k in WEIGHTS],
            *[new_m[k] for k in WEIGHTS], *[new_v[k] for k in WEIGHTS])
```

```python
import functools

import jax
import jax.numpy as jnp
from jax import lax
from jax.experimental import pallas as pl
from jax.experimental.pallas import tpu as pltpu

F32 = jnp.float32
BF16 = jnp.bfloat16

N_DEV = 8
HEAD_DIM = 64
N_KV_HEADS = 4
ATTN_BLOCK = 128
CONF_KERNEL = 31
SCONV_KERNEL = 3
N_BRANCH = 3
RMS_EPS = 1e-6
LN_EPS = 1e-5
ADAM_LR = 0.001
ADAM_B1 = 0.9
ADAM_B2 = 0.999
ADAM_EPS = 1e-08
ADAM_WD = 0.01
ADAM_STEP = 10
LANES = 128
NEG_BIG = -1e30
VMEM_LIMIT_BYTES = 56 * 1024 * 1024
MESH = pl.DeviceIdType.MESH

NN = (((1,), (0,)), ((), ()))
NT = (((1,), (1,)), ((), ()))
TN = (((0,), (0,)), ((), ()))


def _pick(n, cap, mult=LANES):
    best = None
    for d in range(mult, min(n, cap) + 1, mult):
        if n % d == 0:
            best = d
    assert best is not None, (n, cap, mult)
    return best


def _sigmoid(x):
    return 1.0 / (1.0 + jnp.exp(-x))


def _params(sem):
    return pltpu.CompilerParams(dimension_semantics=sem, vmem_limit_bytes=VMEM_LIMIT_BYTES)


def _epi_cast(p, ex, outs):
    outs[0][...] = p.astype(outs[0].dtype)


def _epi_resid(p, ex, outs):
    outs[0][...] = ex[0][...] + p


def _epi_relu2(p, ex, outs):
    outs[0][...] = p.astype(outs[0].dtype)
    r = jnp.maximum(p, 0.0)
    outs[1][...] = (r * r).astype(outs[1].dtype)


def _epi_drelu2(p, ex, outs):
    up = ex[0][...].astype(F32)
    outs[0][...] = (p * (2.0 * jnp.maximum(up, 0.0))).astype(outs[0].dtype)


TOKEN = (8, LANES)


def _matmul(name, a, b, dnums, grid, a_spec, b_spec, out_shape, out_specs, epi, acc_shape, extra=(), extra_specs=(),
            dep=None):
    if dep is not None:
        extra = [*extra, dep]
        extra_specs = [*extra_specs, pl.BlockSpec(TOKEN, lambda j, i, k: (0, 0))]
    nk = grid[2]
    n_extra, n_out = len(extra), len(out_shape)

    def body(*refs):
        a_ref, b_ref = refs[0], refs[1]
        ex = refs[2:2 + n_extra]
        outs = refs[2 + n_extra:2 + n_extra + n_out]
        p = lax.dot_general(a_ref[...], b_ref[...], dnums, preferred_element_type=F32)
        if nk == 1:
            epi(p, ex, outs)
        else:
            acc = refs[-1]
            k = pl.program_id(2)

            @pl.when(k == 0)
            def _():
                acc[...] = p

            @pl.when(k > 0)
            def _():
                acc[...] += p

            @pl.when(k == nk - 1)
            def _():
                epi(acc[...], ex, outs)

    scratch = [pltpu.VMEM(acc_shape, F32)] if nk > 1 else []
    return pl.pallas_call(
        body, name=name, grid=grid, in_specs=[a_spec, b_spec, *extra_specs], out_specs=list(out_specs),
        out_shape=list(out_shape), scratch_shapes=scratch,
        compiler_params=_params(("parallel", "parallel", "arbitrary")))(a, b, *extra)


def _mm_plain(name, a, b, form, out_dtypes, epi=_epi_cast, extra=(), tm=1024, tn=1024, tk=2048, dep=None):
    if form == "NN":
        (M, K), N = a.shape, b.shape[1]
    elif form == "NT":
        (M, K), N = a.shape, b.shape[0]
    else:
        (K, M), N = a.shape, b.shape[1]
    tm, tn, tk = _pick(M, tm, 8), _pick(N, tn), _pick(K, tk)
    grid = (N // tn, M // tm, K // tk)
    if form == "TN":
        a_spec = pl.BlockSpec((tk, tm), lambda j, i, k: (k, i))
    else:
        a_spec = pl.BlockSpec((tm, tk), lambda j, i, k: (i, k))
    if form == "NT":
        b_spec = pl.BlockSpec((tn, tk), lambda j, i, k: (j, k))
    else:
        b_spec = pl.BlockSpec((tk, tn), lambda j, i, k: (k, j))
    o_spec = pl.BlockSpec((tm, tn), lambda j, i, k: (i, j))
    dn = {"NN": NN, "NT": NT, "TN": TN}[form]
    return _matmul(name, a, b, dn, grid, a_spec, b_spec,
                   [jax.ShapeDtypeStruct((M, N), dt) for dt in out_dtypes], [o_spec] * len(out_dtypes), epi,
                   (tm, tn), extra, [o_spec] * len(extra), dep)


def _mm_nn_colblocked(name, a, bb, out_dtypes, epi=_epi_cast, tm=1024, tn=1024, tk=2048):
    M, K = a.shape
    ns = bb.shape[2]
    N = N_DEV * ns
    tm, tn, tk = _pick(M, tm, 8), _pick(ns, tn), _pick(K, tk)
    q = ns // tn
    grid = (N // tn, M // tm, K // tk)
    a_spec = pl.BlockSpec((tm, tk), lambda j, i, k: (i, k))
    b_spec = pl.BlockSpec((None, tk, tn), lambda j, i, k: (j // q, k, j % q))
    o_spec = pl.BlockSpec((tm, tn), lambda j, i, k: (i, j))
    return _matmul(name, a, bb, NN, grid, a_spec, b_spec,
                   [jax.ShapeDtypeStruct((M, N), dt) for dt in out_dtypes], [o_spec] * len(out_dtypes), epi, (tm, tn))


def _mm_nt_colblocked(name, a, bb, out_dtype, tm=1024, tn=1024, tk=1024, dep=None):
    M, N = a.shape
    K, ns = bb.shape[1], bb.shape[2]
    tm, tn, tk = _pick(M, tm, 8), _pick(K, tn), _pick(ns, tk)
    q = ns // tk
    grid = (K // tn, M // tm, N // tk)
    a_spec = pl.BlockSpec((tm, tk), lambda j, i, k: (i, k))
    b_spec = pl.BlockSpec((None, tn, tk), lambda j, i, k: (k // q, j, k % q))
    o_spec = pl.BlockSpec((tm, tn), lambda j, i, k: (i, j))
    return _matmul(name, a, bb, NT, grid, a_spec, b_spec, [jax.ShapeDtypeStruct((M, K), out_dtype)], [o_spec],
                   _epi_cast, (tm, tn), dep=dep)[0]


def _mm_tn_colblocked_out(name, a, b, out_dtype, tm=1024, tn=1024, tk=2048):
    T, M = a.shape
    N = b.shape[1]
    ns = N // N_DEV
    tm, tn, tk = _pick(M, tm, 8), _pick(ns, tn), _pick(T, tk)
    q = ns // tn
    grid = (N // tn, M // tm, T // tk)
    a_spec = pl.BlockSpec((tk, tm), lambda j, i, k: (k, i))
    b_spec = pl.BlockSpec((tk, tn), lambda j, i, k: (k, j))
    o_spec = pl.BlockSpec((None, tm, tn), lambda j, i, k: (j // q, i, j % q))
    return _matmul(name, a, b, TN, grid, a_spec, b_spec, [jax.ShapeDtypeStruct((N_DEV, M, ns), out_dtype)], [o_spec],
                   _epi_cast, (tm, tn))[0]


ROW_TILE = 256


def _rms_fwd(x, g):
    T, D = x.shape
    tr = _pick(T, ROW_TILE, 8)

    def body(x_ref, g_ref, h_ref):
        xv = x_ref[...]
        r = lax.rsqrt(jnp.mean(xv * xv, axis=-1, keepdims=True) + RMS_EPS)
        h_ref[...] = (xv * r * g_ref[...]).astype(BF16)

    return pl.pallas_call(
        body, name="rms_fwd", grid=(T // tr,),
        in_specs=[pl.BlockSpec((tr, D), lambda i: (i, 0)), pl.BlockSpec((1, D), lambda i: (0, 0))],
        out_specs=pl.BlockSpec((tr, D), lambda i: (i, 0)),
        out_shape=jax.ShapeDtypeStruct((T, D), BF16), compiler_params=_params(("parallel",)))(x, g)


def _rms_bwd_math(dh, xv, g):
    r = lax.rsqrt(jnp.mean(xv * xv, axis=-1, keepdims=True) + RMS_EPS)
    gdh = dh * g
    dot = jnp.mean(gdh * xv, axis=-1, keepdims=True)
    dx = r * gdh - xv * (r * r * r * dot)
    return dx, dh * xv * r


def _rms_bwd(dh, x, g, dres):
    T, D = x.shape
    tr = _pick(T, ROW_TILE, 8)

    def body(dh_ref, x_ref, g_ref, dres_ref, dx_ref, dxb_ref, dg_ref):
        dx, dgrow = _rms_bwd_math(dh_ref[...], x_ref[...], g_ref[...])
        dx = dx + dres_ref[...]
        dx_ref[...] = dx
        dxb_ref[...] = dx.astype(BF16)
        part = jnp.sum(dgrow, axis=0, keepdims=True)

        @pl.when(pl.program_id(0) == 0)
        def _():
            dg_ref[...] = part

        @pl.when(pl.program_id(0) > 0)
        def _():
            dg_ref[...] += part

    row = pl.BlockSpec((tr, D), lambda i: (i, 0))
    vec = pl.BlockSpec((1, D), lambda i: (0, 0))
    return pl.pallas_call(
        body, name="rms_bwd", grid=(T // tr,), in_specs=[row, row, vec, row], out_specs=[row, row, vec],
        out_shape=[jax.ShapeDtypeStruct((T, D), F32), jax.ShapeDtypeStruct((T, D), BF16),
                   jax.ShapeDtypeStruct((1, D), F32)],
        compiler_params=_params(("arbitrary",)))(dh, x, g, dres)


def _loss_head(x, g, target):
    T, D = x.shape
    tr = _pick(T, ROW_TILE, 8)

    def body(x_ref, g_ref, t_ref, dx_ref, dxb_ref, dg_ref, loss_ref):
        xv, gv = x_ref[...], g_ref[...]
        r = lax.rsqrt(jnp.mean(xv * xv, axis=-1, keepdims=True) + RMS_EPS)
        err = xv * r * gv - t_ref[...]
        part_loss = 0.5 * jnp.sum(jnp.mean(err * err, axis=-1, keepdims=True), axis=0, keepdims=True)
        dx, dgrow = _rms_bwd_math(err * (1.0 / D), xv, gv)
        dx_ref[...] = dx
        dxb_ref[...] = dx.astype(BF16)
        part = jnp.sum(dgrow, axis=0, keepdims=True)
        lpart = jnp.broadcast_to(part_loss, (8, LANES))

        @pl.when(pl.program_id(0) == 0)
        def _():
            dg_ref[...] = part
            loss_ref[...] = lpart

        @pl.when(pl.program_id(0) > 0)
        def _():
            dg_ref[...] += part
            loss_ref[...] += lpart

    row = pl.BlockSpec((tr, D), lambda i: (i, 0))
    vec = pl.BlockSpec((1, D), lambda i: (0, 0))
    lsp = pl.BlockSpec((8, LANES), lambda i: (0, 0))
    return pl.pallas_call(
        body, name="loss_head", grid=(T // tr,), in_specs=[row, vec, row], out_specs=[row, row, vec, lsp],
        out_shape=[jax.ShapeDtypeStruct((T, D), F32), jax.ShapeDtypeStruct((T, D), BF16),
                   jax.ShapeDtypeStruct((1, D), F32), jax.ShapeDtypeStruct((8, LANES), F32)],
        compiler_params=_params(("arbitrary",)))(x, g, target)


def _ln_math(a1, g, b):
    mu = jnp.mean(a1, axis=-1, keepdims=True)
    xc = a1 - mu
    rstd = lax.rsqrt(jnp.mean(xc * xc, axis=-1, keepdims=True) + LN_EPS)
    xhat = xc * rstd
    return xhat, rstd, xhat * g + b


def _ln_silu(a1, g, b):
    T, C = a1.shape
    tr = _pick(T, ROW_TILE, 8)

    def body(a_ref, g_ref, b_ref, o_ref):
        _, _, y = _ln_math(a_ref[...], g_ref[...], b_ref[...])
        o_ref[...] = (y * _sigmoid(y)).astype(BF16)

    row = pl.BlockSpec((tr, C), lambda i: (i, 0))
    vec = pl.BlockSpec((1, C), lambda i: (0, 0))
    return pl.pallas_call(body, name="ln_silu", grid=(T // tr,), in_specs=[row, vec, vec], out_specs=row,
                          out_shape=jax.ShapeDtypeStruct((T, C), BF16), compiler_params=_params(("parallel",)))(a1, g, b)


def _ln_silu_bwd(a1, g, b, d_a3):
    T, C = a1.shape
    tr = _pick(T, ROW_TILE, 8)

    def body(a_ref, g_ref, b_ref, d_ref, da_ref, dg_ref, db_ref):
        gv = g_ref[...]
        xhat, rstd, y = _ln_math(a_ref[...], gv, b_ref[...])
        s = _sigmoid(y)
        dy = d_ref[...].astype(F32) * (s * (1.0 + y * (1.0 - s)))
        dxh = dy * gv
        m1 = jnp.mean(dxh, axis=-1, keepdims=True)
        m2 = jnp.mean(dxh * xhat, axis=-1, keepdims=True)
        da_ref[...] = rstd * (dxh - m1 - xhat * m2)
        pg = jnp.sum(dy * xhat, axis=0, keepdims=True)
        pb = jnp.sum(dy, axis=0, keepdims=True)

        @pl.when(pl.program_id(0) == 0)
        def _():
            dg_ref[...] = pg
            db_ref[...] = pb

        @pl.when(pl.program_id(0) > 0)
        def _():
            dg_ref[...] += pg
            db_ref[...] += pb

    row = pl.BlockSpec((tr, C), lambda i: (i, 0))
    vec = pl.BlockSpec((1, C), lambda i: (0, 0))
    return pl.pallas_call(
        body, name="ln_silu_bwd", grid=(T // tr,), in_specs=[row, vec, vec, row], out_specs=[row, vec, vec],
        out_shape=[jax.ShapeDtypeStruct((T, C), F32), jax.ShapeDtypeStruct((1, C), F32),
                   jax.ShapeDtypeStruct((1, C), F32)],
        compiler_params=_params(("arbitrary",)))(a1, g, b, d_a3)


CONV_ROWS = 128
PAD_A = 32
PAD_B = 8


def _u_block(T, first):
    return pl.BlockSpec((T, LANES), lambda i: (0, first + i))


def _causal_conv(xpad_ref, w_ref, ksize, pad, T, emit):
    for r0 in range(0, T, CONV_ROWS):
        acc = None
        for j in range(ksize):
            off = pad - (ksize - 1) + j + r0
            term = w_ref[j:j + 1, :] * xpad_ref[off:off + CONV_ROWS, :]
            acc = term if acc is None else acc + term
        emit(r0, acc)


def _anticausal_conv(gpad_ref, w_ref, ksize, T, emit):
    for r0 in range(0, T, CONV_ROWS):
        acc = None
        for j in range(ksize):
            off = (ksize - 1) - j + r0
            term = w_ref[j:j + 1, :] * gpad_ref[off:off + CONV_ROWS, :]
            acc = term if acc is None else acc + term
        emit(r0, acc)


def _conv_wgrad(xpad_ref, g_ref, dw_ref, ksize, pad, T):
    for j in range(ksize):
        acc = None
        for r0 in range(0, T, CONV_ROWS):
            off = pad - (ksize - 1) + j + r0
            term = g_ref[r0:r0 + CONV_ROWS, :] * xpad_ref[off:off + CONV_ROWS, :]
            term = jnp.sum(term.reshape(CONV_ROWS // 8, 8, LANES), axis=0)
            acc = term if acc is None else acc + term
        dw_ref[j:j + 1, :] = jnp.sum(acc, axis=0, keepdims=True)


def _convs_fwd(u, conf_dw, sconv_w, C):
    T = u.shape[0]
    nb = C // LANES

    def body(av_ref, ag_ref, bg_ref, cg_ref, bh_ref, dw_ref, sw_ref, a1_ref, s2_ref, xa, xs, s1):
        xa[0:PAD_A, :] = jnp.zeros((PAD_A, LANES), F32)
        xa[PAD_A:PAD_A + T, :] = av_ref[...].astype(F32) * _sigmoid(ag_ref[...].astype(F32))

        def emit_a(r0, acc):
            a1_ref[r0:r0 + CONV_ROWS, :] = acc

        _causal_conv(xa, dw_ref, CONF_KERNEL, PAD_A, T, emit_a)

        xs[0:PAD_B, :] = jnp.zeros((PAD_B, LANES), F32)
        xs[PAD_B:PAD_B + T, :] = cg_ref[...].astype(F32) * bh_ref[...].astype(F32)

        def emit_b(r0, acc):
            s1[r0:r0 + CONV_ROWS, :] = acc

        _causal_conv(xs, sw_ref, SCONV_KERNEL, PAD_B, T, emit_b)
        s2_ref[...] = (bg_ref[...].astype(F32) * s1[...]).astype(BF16)

    col = pl.BlockSpec((T, LANES), lambda i: (0, i))
    return pl.pallas_call(
        body, name="convs_fwd", grid=(nb,),
        in_specs=[_u_block(T, 0), _u_block(T, nb), _u_block(T, 2 * nb), _u_block(T, 3 * nb), _u_block(T, 4 * nb),
                  pl.BlockSpec((CONF_KERNEL, LANES), lambda i: (0, i)),
                  pl.BlockSpec((SCONV_KERNEL, LANES), lambda i: (0, i))],
        out_specs=[col, col],
        out_shape=[jax.ShapeDtypeStruct((T, C), F32), jax.ShapeDtypeStruct((T, C), BF16)],
        scratch_shapes=[pltpu.VMEM((T + PAD_A, LANES), F32), pltpu.VMEM((T + PAD_B, LANES), F32),
                        pltpu.VMEM((T, LANES), F32)],
        compiler_params=_params(("parallel",)))(u, u, u, u, u, conf_dw, sconv_w)


def _convs_bwd(u, conf_dw, sconv_w, d_a1, d_s2, C):
    T = u.shape[0]
    nb = C // LANES

    def body(av_ref, ag_ref, bg_ref, cg_ref, bh_ref, dw_ref, sw_ref, da1_ref, ds2_ref,
             dav_ref, dag_ref, dbg_ref, dcg_ref, dbh_ref, ddw_ref, dsw_ref, xa, ga, xs, gs, tmp):
        av = av_ref[...].astype(F32)
        sg = _sigmoid(ag_ref[...].astype(F32))
        xa[0:PAD_A, :] = jnp.zeros((PAD_A, LANES), F32)
        xa[PAD_A:PAD_A + T, :] = av * sg
        ga[0:T, :] = da1_ref[...]
        ga[T:T + PAD_A, :] = jnp.zeros((PAD_A, LANES), F32)
        _conv_wgrad(xa, ga, ddw_ref, CONF_KERNEL, PAD_A, T)

        def emit_a(r0, acc):
            tmp[r0:r0 + CONV_ROWS, :] = acc

        _anticausal_conv(ga, dw_ref, CONF_KERNEL, T, emit_a)
        da0 = tmp[...]
        dav_ref[...] = (da0 * sg).astype(BF16)
        dag_ref[...] = (da0 * av * sg * (1.0 - sg)).astype(BF16)

        cg = cg_ref[...].astype(F32)
        bh = bh_ref[...].astype(F32)
        ds2 = ds2_ref[...].astype(F32)
        xs[0:PAD_B, :] = jnp.zeros((PAD_B, LANES), F32)
        xs[PAD_B:PAD_B + T, :] = cg * bh

        def emit_s1(r0, acc):
            tmp[r0:r0 + CONV_ROWS, :] = acc

        _causal_conv(xs, sw_ref, SCONV_KERNEL, PAD_B, T, emit_s1)
        dbg_ref[...] = (ds2 * tmp[...]).astype(BF16)
        gs[0:T, :] = ds2 * bg_ref[...].astype(F32)
        gs[T:T + PAD_B, :] = jnp.zeros((PAD_B, LANES), F32)
        _conv_wgrad(xs, gs, dsw_ref, SCONV_KERNEL, PAD_B, T)

        def emit_b(r0, acc):
            tmp[r0:r0 + CONV_ROWS, :] = acc

        _anticausal_conv(gs, sw_ref, SCONV_KERNEL, T, emit_b)
        ds0 = tmp[...]
        dcg_ref[...] = (ds0 * bh).astype(BF16)
        dbh_ref[...] = (ds0 * cg).astype(BF16)

    col = pl.BlockSpec((T, LANES), lambda i: (0, i))
    wa = pl.BlockSpec((CONF_KERNEL, LANES), lambda i: (0, i))
    wb = pl.BlockSpec((SCONV_KERNEL, LANES), lambda i: (0, i))
    act = jax.ShapeDtypeStruct((T, C), BF16)
    return pl.pallas_call(
        body, name="convs_bwd", grid=(nb,),
        in_specs=[_u_block(T, 0), _u_block(T, nb), _u_block(T, 2 * nb), _u_block(T, 3 * nb), _u_block(T, 4 * nb),
                  wa, wb, col, col],
        out_specs=[col, col, col, col, col, wa, wb],
        out_shape=[act, act, act, act, act, jax.ShapeDtypeStruct((CONF_KERNEL, C), F32),
                   jax.ShapeDtypeStruct((SCONV_KERNEL, C), F32)],
        scratch_shapes=[pltpu.VMEM((T + PAD_A, LANES), F32), pltpu.VMEM((T + PAD_A, LANES), F32),
                        pltpu.VMEM((T + PAD_B, LANES), F32), pltpu.VMEM((T + PAD_B, LANES), F32),
                        pltpu.VMEM((T, LANES), F32)],
        compiler_params=_params(("parallel",)))(u, u, u, u, u, conf_dw, sconv_w, d_a1, d_s2)


def _attn_specs(T, C, q_off_blocks):
    kvw = N_KV_HEADS * HEAD_DIM
    kb = (5 * C + C) // kvw
    qs = pl.BlockSpec((ATTN_BLOCK, C), lambda n: (n, 5))
    kc = pl.BlockSpec((ATTN_BLOCK, kvw), lambda n: (n, kb))
    kp = pl.BlockSpec((ATTN_BLOCK, kvw), lambda n: (jnp.maximum(n - 1, 0), kb))
    vc = pl.BlockSpec((ATTN_BLOCK, kvw), lambda n: (n, kb + 1))
    vp = pl.BlockSpec((ATTN_BLOCK, kvw), lambda n: (jnp.maximum(n - 1, 0), kb + 1))
    return qs, kc, kp, vc, vp


def _attn_masks(n):
    row = lax.broadcasted_iota(jnp.int32, (ATTN_BLOCK, ATTN_BLOCK), 0)
    col = lax.broadcasted_iota(jnp.int32, (ATTN_BLOCK, ATTN_BLOCK), 1)
    return col <= row, jnp.logical_and(col > row, n > 0)


def _attn_scores(qh, kc, kp, mask_c, mask_p):
    scale = HEAD_DIM ** -0.5
    s_c = lax.dot_general(qh, kc, NT, preferred_element_type=F32) * scale
    s_p = lax.dot_general(qh, kp, NT, preferred_element_type=F32) * scale
    return jnp.where(mask_c, s_c, NEG_BIG), jnp.where(mask_p, s_p, NEG_BIG)


def _attn_fwd(u, sinks, C):
    T = u.shape[0]
    H = C // HEAD_DIM
    grp = H // N_KV_HEADS

    def body(sink_ref, q_ref, kc_ref, kp_ref, vc_ref, vp_ref, o_ref, lse_ref):
        n = pl.program_id(0)
        mask_c, mask_p = _attn_masks(n)
        for h in range(H):
            kv = slice((h // grp) * HEAD_DIM, (h // grp + 1) * HEAD_DIM)
            hs = slice(h * HEAD_DIM, (h + 1) * HEAD_DIM)
            sink = sink_ref[h]
            s_c, s_p = _attn_scores(q_ref[:, hs], kc_ref[:, kv], kp_ref[:, kv], mask_c, mask_p)
            m = jnp.maximum(jnp.maximum(jnp.max(s_c, axis=-1, keepdims=True), jnp.max(s_p, axis=-1, keepdims=True)), sink)
            p_c = jnp.exp(s_c - m)
            p_p = jnp.exp(s_p - m)
            den = jnp.sum(p_c, axis=-1, keepdims=True) + jnp.sum(p_p, axis=-1, keepdims=True) + jnp.exp(sink - m)
            acc = jnp.dot(p_c.astype(BF16), vc_ref[:, kv], preferred_element_type=F32)
            acc = acc + jnp.dot(p_p.astype(BF16), vp_ref[:, kv], preferred_element_type=F32)
            o_ref[:, hs] = (acc / den).astype(BF16)
            lse_ref[:, h:h + 1] = m + jnp.log(den)

    qs, kc, kp, vc, vp = _attn_specs(T, C, 5)
    return pl.pallas_call(
        body, name="attn_fwd", grid=(T // ATTN_BLOCK,),
        in_specs=[pl.BlockSpec(memory_space=pltpu.SMEM), qs, kc, kp, vc, vp],
        out_specs=[pl.BlockSpec((ATTN_BLOCK, C), lambda n: (n, 0)), pl.BlockSpec((ATTN_BLOCK, H), lambda n: (n, 0))],
        out_shape=[jax.ShapeDtypeStruct((T, C), BF16), jax.ShapeDtypeStruct((T, H), F32)],
        compiler_params=_params(("parallel",)))(sinks, u, u, u, u, u)


def _attn_bwd(u, o, lse, d_o, sinks, C):
    T = u.shape[0]
    H = C // HEAD_DIM
    grp = H // N_KV_HEADS
    kvw = N_KV_HEADS * HEAD_DIM
    nblk = T // ATTN_BLOCK
    scale = HEAD_DIM ** -0.5

    def body(sink_ref, q_ref, kc_ref, kp_ref, vc_ref, vp_ref, o_ref, lse_ref, do_ref,
             dq_ref, dk_ref, dv_ref, ds_ref, dk_acc, dv_acc):
        n = pl.program_id(0)

        @pl.when(n == 0)
        def _():
            dk_acc[...] = jnp.zeros_like(dk_acc)
            dv_acc[...] = jnp.zeros_like(dv_acc)
            ds_ref[...] = jnp.zeros_like(ds_ref)

        mask_c, mask_p = _attn_masks(n)
        cur = pl.ds(pl.multiple_of(n * ATTN_BLOCK, ATTN_BLOCK), ATTN_BLOCK)
        prev = pl.ds(pl.multiple_of(jnp.maximum(n - 1, 0) * ATTN_BLOCK, ATTN_BLOCK), ATTN_BLOCK)
        for g in range(N_KV_HEADS):
            kv = slice(g * HEAD_DIM, (g + 1) * HEAD_DIM)
            kc, kp, vc, vp = kc_ref[:, kv], kp_ref[:, kv], vc_ref[:, kv], vp_ref[:, kv]
            dk_c = dk_p = dv_c = dv_p = None
            for h in range(g * grp, (g + 1) * grp):
                hs = slice(h * HEAD_DIM, (h + 1) * HEAD_DIM)
                qh, doh = q_ref[:, hs], do_ref[:, hs]
                lse_h = lse_ref[:, h:h + 1]
                s_c, s_p = _attn_scores(qh, kc, kp, mask_c, mask_p)
                p_c = jnp.exp(s_c - lse_h)
                p_p = jnp.exp(s_p - lse_h)
                delta = jnp.sum(doh.astype(F32) * o_ref[:, hs].astype(F32), axis=-1, keepdims=True)
                dp_c = lax.dot_general(doh, vc, NT, preferred_element_type=F32)
                dp_p = lax.dot_general(doh, vp, NT, preferred_element_type=F32)
                ds_c = (p_c * (dp_c - delta) * scale).astype(BF16)
                ds_p = (p_p * (dp_p - delta) * scale).astype(BF16)
                dq = jnp.dot(ds_c, kc, preferred_element_type=F32) + jnp.dot(ds_p, kp, preferred_element_type=F32)
                dq_ref[:, hs] = dq.astype(BF16)
                t_kc = lax.dot_general(ds_c, qh, TN, preferred_element_type=F32)
                t_kp = lax.dot_general(ds_p, qh, TN, preferred_element_type=F32)
                t_vc = lax.dot_general(p_c.astype(BF16), doh, TN, preferred_element_type=F32)
                t_vp = lax.dot_general(p_p.astype(BF16), doh, TN, preferred_element_type=F32)
                dk_c = t_kc if dk_c is None else dk_c + t_kc
                dk_p = t_kp if dk_p is None else dk_p + t_kp
                dv_c = t_vc if dv_c is None else dv_c + t_vc
                dv_p = t_vp if dv_p is None else dv_p + t_vp
                p_sink = jnp.exp(sink_ref[h] - lse_h)
                dsink = -jnp.sum(p_sink * delta, axis=0, keepdims=True)
                ds_ref[:, h:h + 1] += jnp.broadcast_to(dsink, (8, 1))
            dk_acc[cur, kv] += dk_c
            dk_acc[prev, kv] += dk_p
            dv_acc[cur, kv] += dv_c
            dv_acc[prev, kv] += dv_p

        @pl.when(n == nblk - 1)
        def _():
            dk_ref[...] = dk_acc[...].astype(BF16)
            dv_ref[...] = dv_acc[...].astype(BF16)

    qs, kc, kp, vc, vp = _attn_specs(T, C, 5)
    blk = pl.BlockSpec((ATTN_BLOCK, C), lambda n: (n, 0))
    full = pl.BlockSpec((T, kvw), lambda n: (0, 0))
    return pl.pallas_call(
        body, name="attn_bwd", grid=(nblk,),
        in_specs=[pl.BlockSpec(memory_space=pltpu.SMEM), qs, kc, kp, vc, vp, blk,
                  pl.BlockSpec((ATTN_BLOCK, H), lambda n: (n, 0)), blk],
        out_specs=[blk, full, full, pl.BlockSpec((8, H), lambda n: (0, 0))],
        out_shape=[jax.ShapeDtypeStruct((T, C), BF16), jax.ShapeDtypeStruct((T, kvw), BF16),
                   jax.ShapeDtypeStruct((T, kvw), BF16), jax.ShapeDtypeStruct((8, H), F32)],
        scratch_shapes=[pltpu.VMEM((T, kvw), F32), pltpu.VMEM((T, kvw), F32)],
        compiler_params=_params(("arbitrary",)))(sinks, u, u, u, u, u, o, lse, d_o)


MERGE_COLS = 512


def _merge_specs(T, D, I):
    tr = _pick(T, ROW_TILE, 8)
    tc = _pick(D, MERGE_COLS)
    g0 = (I - N_BRANCH * D) // tc
    per = D // tc
    gspecs = [pl.BlockSpec((tr, tc), functools.partial(lambda j, i, b: (i, g0 + b * per + j), b=b)) for b in range(N_BRANCH)]
    tile = pl.BlockSpec((tr, tc), lambda j, i: (i, j))
    bias = pl.BlockSpec((N_BRANCH, tc), lambda j, i: (0, j))
    return tr, tc, gspecs, tile, bias


def _merge_fwd(u, gate_b, ya, yb, yc):
    T, I = u.shape
    D = ya.shape[1]
    tr, tc, gspecs, tile, bias = _merge_specs(T, D, I)

    def body(g0_ref, g1_ref, g2_ref, b_ref, ya_ref, yb_ref, yc_ref, o_ref):
        acc = None
        for b, (g_ref, y_ref) in enumerate(((g0_ref, ya_ref), (g1_ref, yb_ref), (g2_ref, yc_ref))):
            gate = _sigmoid(g_ref[...].astype(F32) + b_ref[b:b + 1, :])
            term = gate * y_ref[...].astype(F32)
            acc = term if acc is None else acc + term
        o_ref[...] = acc.astype(BF16)

    return pl.pallas_call(
        body, name="merge_fwd", grid=(D // tc, T // tr), in_specs=[*gspecs, bias, tile, tile, tile], out_specs=tile,
        out_shape=jax.ShapeDtypeStruct((T, D), BF16),
        compiler_params=_params(("parallel", "parallel")))(u, u, u, gate_b, ya, yb, yc)


def _merge_bwd(u, gate_b, ya, yb, yc, dm):
    T, I = u.shape
    D = ya.shape[1]
    tr, tc, gspecs, tile, bias = _merge_specs(T, D, I)

    def body(g0_ref, g1_ref, g2_ref, b_ref, ya_ref, yb_ref, yc_ref, dm_ref,
             dya_ref, dyb_ref, dyc_ref, dg0_ref, dg1_ref, dg2_ref, db_ref):
        dmv = dm_ref[...].astype(F32)
        first = pl.program_id(1) == 0
        for b, (g_ref, y_ref, dy_ref, dg_ref) in enumerate(((g0_ref, ya_ref, dya_ref, dg0_ref),
                                                           (g1_ref, yb_ref, dyb_ref, dg1_ref),
                                                           (g2_ref, yc_ref, dyc_ref, dg2_ref))):
            gate = _sigmoid(g_ref[...].astype(F32) + b_ref[b:b + 1, :])
            dy_ref[...] = (dmv * gate).astype(BF16)
            dpre = dmv * y_ref[...].astype(F32) * gate * (1.0 - gate)
            dg_ref[...] = dpre.astype(BF16)
            part = jnp.sum(dpre, axis=0, keepdims=True)

            @pl.when(first)
            def _():
                db_ref[b:b + 1, :] = part

            @pl.when(jnp.logical_not(first))
            def _():
                db_ref[b:b + 1, :] += part

    act = jax.ShapeDtypeStruct((T, D), BF16)
    return pl.pallas_call(
        body, name="merge_bwd", grid=(D // tc, T // tr), in_specs=[*gspecs, bias, tile, tile, tile, tile],
        out_specs=[tile] * 6 + [bias], out_shape=[act] * 6 + [jax.ShapeDtypeStruct((N_BRANCH, D), F32)],
        compiler_params=_params(("parallel", "arbitrary")))(u, u, u, gate_b, ya, yb, yc, dm)


ELEMS_PER_TILE = 256 * 1024


def _row_tile(r, c):
    return _pick(r, max(16, ELEMS_PER_TILE // c), 16) if r % 16 == 0 else r


def _cast_place(w, layer, my_block):
    L, r, c = w.shape
    tr = _row_tile(r, c)

    def body(blk_ref, w_ref, o_ref):
        o_ref[...] = w_ref[...].astype(BF16)

    return pl.pallas_call(
        body, name="cast_place",
        grid_spec=pltpu.PrefetchScalarGridSpec(
            num_scalar_prefetch=1, grid=(r // tr,),
            in_specs=[pl.BlockSpec((None, tr, c), lambda i, blk: (layer, i, 0))],
            out_specs=pl.BlockSpec((None, tr, c), lambda i, blk: (blk[0], i, 0))),
        out_shape=jax.ShapeDtypeStruct((N_DEV, r, c), BF16), compiler_params=_params(("parallel",)))(my_block, w)


def _adamw_math(w, g, m, v):
    m = ADAM_B1 * m + (1.0 - ADAM_B1) * g
    v = ADAM_B2 * v + (1.0 - ADAM_B2) * (g * g)
    m_hat = m / (1.0 - ADAM_B1 ** ADAM_STEP)
    v_hat = v / (1.0 - ADAM_B2 ** ADAM_STEP)
    delta = -ADAM_LR * (m_hat / (jnp.sqrt(v_hat) + ADAM_EPS) + ADAM_WD * w)
    return delta, m, v


def _sum_parts(part_ref):
    acc = part_ref[0].astype(F32)
    for s in range(1, N_DEV):
        acc = acc + part_ref[s].astype(F32)
    return acc


def _sum8(parts):
    L, _, r, c = parts.shape
    tr = _row_tile(r, c)

    def body(p_ref, o_ref):
        o_ref[...] = _sum_parts(p_ref)

    return pl.pallas_call(
        body, name="sum8", grid=(L, r // tr),
        in_specs=[pl.BlockSpec((None, N_DEV, tr, c), lambda l, i: (l, 0, i, 0))],
        out_specs=pl.BlockSpec((None, tr, c), lambda l, i: (l, i, 0)),
        out_shape=jax.ShapeDtypeStruct((L, r, c), F32), compiler_params=_params(("parallel", "parallel")))(parts)


def _adamw(w, g, m, v):
    L, r, c = w.shape
    tr = _row_tile(r, c)
    spec = pl.BlockSpec((None, tr, c), lambda l, i: (l, i, 0))

    def body(w_ref, g_ref, m_ref, v_ref, d_ref, mo_ref, vo_ref):
        d, mn, vn = _adamw_math(w_ref[...], g_ref[...], m_ref[...], v_ref[...])
        d_ref[...] = d
        mo_ref[...] = mn
        vo_ref[...] = vn

    shp = jax.ShapeDtypeStruct(w.shape, F32)
    return pl.pallas_call(body, name="adamw", grid=(L, r // tr), in_specs=[spec] * 4, out_specs=[spec] * 3,
                          out_shape=[shp] * 3, compiler_params=_params(("parallel", "parallel")))(w, g, m, v)


N_CHIPS = 4
CHIP_XOR = (0, 2, 1, 3)


def _chip_sum(part4, sib4, own_all, layer, pos):
    _, _, r, c = part4.shape
    tr = _row_tile(r, c)

    def chip(p, s):
        return jnp.bitwise_xor(2 * p[0] + p[1], CHIP_XOR[s])

    mine = [pl.BlockSpec((None, None, tr, c), functools.partial(lambda i, p, s: (chip(p, s), p[2], i, 0), s=s))
            for s in range(N_CHIPS)]
    theirs = [pl.BlockSpec((None, None, tr, c), functools.partial(lambda i, p, s: (chip(p, s), 0, i, 0), s=s))
              for s in range(N_CHIPS)]

    def body(pos_ref, *refs):
        a, b = refs[:N_CHIPS], refs[N_CHIPS:2 * N_CHIPS]
        out_ref, own_ref = refs[2 * N_CHIPS + 1], refs[2 * N_CHIPS + 2]
        own_ref[...] = a[0][...].astype(F32) + b[0][...].astype(F32)
        for s in range(1, N_CHIPS):
            out_ref[s - 1] = (a[s][...].astype(F32) + b[s][...].astype(F32)).astype(BF16)

    return pl.pallas_call(
        body, name="chip_sum",
        grid_spec=pltpu.PrefetchScalarGridSpec(
            num_scalar_prefetch=1, grid=(r // tr,),
            in_specs=[*mine, *theirs, pl.BlockSpec(memory_space=pl.ANY)],
            out_specs=[pl.BlockSpec((N_CHIPS - 1, tr, c), lambda i, p: (0, i, 0)),
                       pl.BlockSpec((None, tr, c), lambda i, p: (layer, i, 0))]),
        out_shape=[jax.ShapeDtypeStruct((N_CHIPS - 1, r, c), BF16), jax.ShapeDtypeStruct(own_all.shape, F32)],
        input_output_aliases={1 + 2 * N_CHIPS: 1},
        compiler_params=_params(("parallel",)))(pos, *([part4] * N_CHIPS), *([sib4] * N_CHIPS), own_all)


def _sum_chips(own_ref, got_ref):
    acc = own_ref[...]
    for s in range(N_CHIPS - 1):
        acc = acc + got_ref[s].astype(F32)
    return acc


def _sum4_adamw(own, got, w, m, v, first, count, earlier=None):
    L, r, c = w.shape
    tr = _row_tile(r, c)
    spec = pl.BlockSpec((None, tr, c), lambda l, i: (first + l, i, 0))
    earlier = [] if earlier is None else list(earlier)

    def body(own_ref, got_ref, w_ref, m_ref, v_ref, *rest):
        g_ref, d_ref, mo_ref, vo_ref = rest[len(earlier):]
        g = _sum_chips(own_ref, got_ref)
        d, mn, vn = _adamw_math(w_ref[...], g, m_ref[...], v_ref[...])
        g_ref[...] = g
        d_ref[...] = d
        mo_ref[...] = mn
        vo_ref[...] = vn

    shp = jax.ShapeDtypeStruct(w.shape, F32)
    return pl.pallas_call(
        body, name="sum4_adamw", grid=(count, r // tr),
        in_specs=[spec, pl.BlockSpec((None, N_CHIPS - 1, tr, c), lambda l, i: (first + l, 0, i, 0)), spec, spec, spec,
                  *([ANY] * len(earlier))],
        out_specs=[spec] * 4, out_shape=[shp] * 4, input_output_aliases={5 + a: a for a in range(len(earlier))},
        compiler_params=_params(("parallel", "parallel")))(own, got, w, m, v, *earlier)


def _me():
    return lax.axis_index("x"), lax.axis_index("y"), lax.axis_index("c")


def _flip(pos, k):
    x, y, c = pos
    return (1 - x if k & 4 else x, 1 - y if k & 2 else y, 1 - c if k & 1 else c)


def _block_of(pos):
    return 4 * pos[0] + 2 * pos[1] + pos[2]


ANY = pl.BlockSpec(memory_space=pl.ANY)
SIBLING = 1
OTHER_CHIPS = (4, 2, 6)


def _all_gather(name, arrays, layer=None):
    n = len(arrays)
    shapes = [a.shape[-2:] for a in arrays]

    def body(*refs):
        srcs, outs = refs[:n], refs[n:2 * n]
        send_sems, recv_sems, local_sems = refs[2 * n:]
        me = _me()
        sib = _flip(me, SIBLING)

        def src_of(a):
            return srcs[a] if layer is None else srcs[a].at[layer]

        def copy(a, k, block_pos, to, src=None):
            dst = outs[a].at[_block_of(block_pos)]
            return pltpu.make_async_remote_copy(
                src_ref=dst if src is None else src, dst_ref=dst, send_sem=send_sems.at[a, k],
                recv_sem=recv_sems.at[a, k], device_id=to, device_id_type=MESH)

        mine = [pltpu.make_async_copy(src_of(a), outs[a].at[_block_of(me)], local_sems.at[a]) for a in range(n)]
        for cp in mine:
            cp.start()
        first = []
        for a in range(n):
            first.append(copy(a, 0, me, sib, src=src_of(a)))
            for j, k in enumerate(OTHER_CHIPS):
                first.append(copy(a, 1 + j, me, _flip(me, k), src=src_of(a)))
        for cp in first:
            cp.start()
        passed = []
        for j, k in enumerate(OTHER_CHIPS):
            for a in range(n):
                copy(a, 1 + j, _flip(me, k), me).wait_recv()
                fw = copy(a, 4 + j, _flip(me, k), sib)
                fw.start()
                passed.append(fw)
        for a in range(n):
            copy(a, 0, sib, me).wait_recv()
            for j, k in enumerate(OTHER_CHIPS):
                copy(a, 4 + j, _flip(sib, k), me).wait_recv()
        for cp in first + passed:
            cp.wait_send()
        for cp in mine:
            cp.wait()

    return pl.pallas_call(
        body, name=name, in_specs=[ANY] * n, out_specs=[ANY] * n,
        out_shape=[jax.ShapeDtypeStruct((N_DEV, *s), a.dtype) for s, a in zip(shapes, arrays)],
        scratch_shapes=[pltpu.SemaphoreType.DMA((n, 7)), pltpu.SemaphoreType.DMA((n, 7)), pltpu.SemaphoreType.DMA((n,))],
    )(*arrays)


HBM = pl.BlockSpec(memory_space=pltpu.HBM)
SEM = pl.BlockSpec(memory_space=pltpu.SEMAPHORE)
DATAFLOW = pltpu.SideEffectType.DATAFLOW_SIDE_EFFECTING
X_NEIGHBOUR, Y_NEIGHBOUR = 4, 2
NEAR = (SIBLING, X_NEIGHBOUR, Y_NEIGHBOUR)


def _in_hbm(a):
    return pltpu.with_memory_space_constraint(a, pltpu.HBM)


def _hbm_like(a):
    return pltpu.HBM(a.shape, a.dtype)


def _copies_start(name, srcs, lands, plan, n_copies, after=None):
    ns, n = len(srcs), len(lands)
    deps = [] if after is None else [after]

    def body(*refs):
        k0 = ns + n + len(deps)
        send_sems, recv_sems, token = refs[k0], refs[k0 + 1], refs[k0 + 2 + n]
        for s, (src, dst, peer, _) in enumerate(plan(_me(), refs[:ns], refs[ns:ns + n])):
            pltpu.make_async_remote_copy(src_ref=src, dst_ref=dst, send_sem=send_sems.at[s], recv_sem=recv_sems.at[s],
                                         device_id=peer, device_id_type=MESH).start()
        token[...] = jnp.zeros(TOKEN, F32)

    out = pl.pallas_call(
        body, name=name, in_specs=[*([HBM] * (ns + n)), *([ANY] * len(deps))],
        out_specs=[SEM, SEM, *([HBM] * n), pl.BlockSpec(memory_space=pltpu.VMEM)],
        out_shape=[pltpu.SemaphoreType.DMA((n_copies,)), pltpu.SemaphoreType.DMA((n_copies,)),
                   *[_hbm_like(a) for a in lands], jax.ShapeDtypeStruct(TOKEN, F32)],
        input_output_aliases={ns + a: 2 + a for a in range(n)},
        compiler_params=pltpu.CompilerParams(has_side_effects=DATAFLOW),
    )(*[_in_hbm(a) for a in srcs], *[_in_hbm(a) for a in lands], *deps)
    return dict(send=out[0], recv=out[1], srcs=list(srcs), plan=plan), list(out[2:2 + n]), out[2 + n]


def _copies_wait(name, flight, lands, after):
    srcs, plan = flight["srcs"], flight["plan"]
    ns, n = len(srcs), len(lands)
    after = list(after) if isinstance(after, (list, tuple)) else [after]

    def body(*refs):
        send_sems, recv_sems = refs[ns + n], refs[ns + n + 1]
        for s, (src, dst, peer, landing) in enumerate(plan(_me(), refs[:ns], refs[ns:ns + n])):
            pltpu.make_async_remote_copy(src_ref=src, dst_ref=dst, send_sem=send_sems.at[s], recv_sem=recv_sems.at[s],
                                         device_id=peer, device_id_type=MESH).wait_send()
            pltpu.make_async_remote_copy(src_ref=landing, dst_ref=landing, send_sem=send_sems.at[s],
                                         recv_sem=recv_sems.at[s], device_id=peer, device_id_type=MESH).wait_recv()

    out = pl.pallas_call(
        body, name=name, in_specs=[*([HBM] * (ns + n)), SEM, SEM, *([ANY] * len(after))], out_specs=[HBM] * n,
        out_shape=[_hbm_like(a) for a in lands], input_output_aliases={ns + a: a for a in range(n)},
        compiler_params=pltpu.CompilerParams(has_side_effects=DATAFLOW),
    )(*srcs, *lands, flight["send"], flight["recv"], *after)
    return list(out)


def _gather_plan_near(me, srcs, lands):
    plan = []
    for land in lands:
        own = land.at[_block_of(me)]
        for k in NEAR:
            peer = _flip(me, k)
            plan.append((own, own, peer, land.at[_block_of(peer)]))
    return plan


def _gather_plan_far(me, srcs, lands):
    x_nbr, y_nbr, far = _flip(me, X_NEIGHBOUR), _flip(me, Y_NEIGHBOUR), _flip(me, X_NEIGHBOUR | Y_NEIGHBOUR)
    plan = []
    for land in lands:
        half = land.shape[1] // 2
        first, second = pl.ds(0, half), pl.ds(half, half)
        passed = land.at[_block_of(y_nbr), first]
        plan.append((passed, passed, x_nbr, land.at[_block_of(far), first]))
        passed = land.at[_block_of(x_nbr), second]
        plan.append((passed, passed, y_nbr, land.at[_block_of(far), second]))
    return plan


def _sibling_plan(me, srcs, lands):
    sib = _flip(me, SIBLING)
    return [(src.at[:, pl.ds(1 - me[2], 1)], land, sib, land) for src, land in zip(srcs, lands)]


def _scatter_plan(layer):
    def plan(me, srcs, lands):
        out = []
        for src, land in zip(srcs, lands):
            for j, k in enumerate(OTHER_CHIPS):
                out.append((src.at[j], land.at[layer, j], _flip(me, k), land.at[layer, j]))
        return out
    return plan


def _pass_on_plan(relations):
    def plan(me, srcs, lands):
        sib = _flip(me, SIBLING)
        out = []
        for land in lands:
            for k in relations:
                blk = land.at[_block_of(_flip(me, k))]
                out.append((blk, blk, sib, land.at[_block_of(_flip(sib, k))]))
        return out
    return plan


def _after(small, tokens):
    for t in tokens:
        small = small + t[0:1, 0:1]
    return small


def _layer_fwd(xc, l, W, P, dims, deps=(), mid_layer=None):
    T, D, C, I, F = dims
    h = _rms_fwd(xc, _after(P["norm_mix_g"][l:l + 1], deps))
    u = _mm_plain("mm_u", h, W["w_in"][l].reshape(I, D), "NT", [BF16], tn=1280)[0]
    a1, s2 = _convs_fwd(u, P["conf_dw"][l], P["sconv_w"][l], C)
    a3 = _ln_silu(a1, P["conf_ln_g"][l:l + 1], P["conf_ln_b"][l:l + 1])
    o, lse = _attn_fwd(u, P["sinks"][l], C)
    ya = _mm_nn_colblocked("mm_branch_out", a3, W["w_conf_out"][l], [BF16], tm=2048)[0]
    yb = _mm_nn_colblocked("mm_branch_out", s2, W["w_sconv_out"][l], [BF16], tm=2048)[0]
    yc = _mm_nn_colblocked("mm_branch_out", o, W["w_attn_out"][l], [BF16], tm=2048)[0]
    merged = _merge_fwd(u, P["gate_b"][l], ya, yb, yc)
    x1 = _mm_plain("mm_mix", merged, W["w_mix_out"][l].reshape(D, D), "NN", [F32], _epi_resid, [xc], tk=1024)[0]
    norm_ffn_g = P["norm_ffn_g"][l:l + 1]
    h2 = _rms_fwd(x1, norm_ffn_g if mid_layer is None else _after(norm_ffn_g, [mid_layer(x1)]))
    up, act = _mm_nn_colblocked("mm_up", h2, W["w_up"][l], [BF16, BF16], _epi_relu2)
    x2 = _mm_plain("mm_down", act, W["w_down"][l].reshape(F, D), "NN", [F32], _epi_resid, [x1], tk=1024)[0]
    saved = dict(xc=xc, h=h, u=u, a1=a1, a3=a3, s2=s2, o=o, lse=lse, ya=ya, yb=yb, yc=yc, merged=merged, x1=x1, h2=h2,
                 up=up, act=act)
    return x2, saved


def _bwd_mlp(dx2, dx2_b, l, W, P, S, dims, dep=None):
    T, D, C, I, F = dims
    d_up = _mm_plain("mm_d_up", dx2_b, W["w_down"][l].reshape(F, D), "NT", [BF16], _epi_drelu2, [S["up"]], dep=dep)[0]
    g_down = _mm_plain("mm_g_down", S["act"], dx2_b, "TN", [BF16])[0]
    dh2 = _mm_nt_colblocked("mm_d_h2", d_up, W["w_up"][l], F32)
    g_up = _mm_tn_colblocked_out("mm_g_up", S["h2"], d_up, BF16)
    dx1, dx1_b, dg_ffn = _rms_bwd(dh2, S["x1"], P["norm_ffn_g"][l:l + 1], dx2)
    return dx1, dx1_b, dict(w_up=g_up, w_down=g_down.reshape(N_DEV, F // N_DEV, D)), dg_ffn


def _bwd_mix_out(dx1_b, l, W, P, S, dims, dep=None):
    T, D, C, I, F = dims
    dm = _mm_plain("mm_d_merged", dx1_b, W["w_mix_out"][l].reshape(D, D), "NT", [BF16], dep=dep)[0]
    g_mix = _mm_plain("mm_g_mix", S["merged"], dx1_b, "TN", [BF16])[0]
    merge = _merge_bwd(S["u"], P["gate_b"][l], S["ya"], S["yb"], S["yc"], dm)
    return g_mix.reshape(N_DEV, D // N_DEV, D), merge


def _bwd_mixers(dx1, merge, dg_ffn, g_mix, l, W, P, S, dims, dep, send_mid, send_in):
    T, D, C, I, F = dims
    d_ya, d_yb, d_yc, dg0, dg1, dg2, d_gate_b = merge
    d_a3 = _mm_nt_colblocked("mm_d_branch", d_ya, W["w_conf_out"][l], BF16, tm=2048, dep=dep)
    d_s2 = _mm_nt_colblocked("mm_d_branch", d_yb, W["w_sconv_out"][l], BF16, tm=2048)
    d_o = _mm_nt_colblocked("mm_d_branch", d_yc, W["w_attn_out"][l], BF16, tm=2048)
    g_conf = _mm_tn_colblocked_out("mm_g_branch", S["a3"], d_ya, BF16)
    g_sconv = _mm_tn_colblocked_out("mm_g_branch", S["s2"], d_yb, BF16)
    g_attn = _mm_tn_colblocked_out("mm_g_branch", S["o"], d_yc, BF16)
    tok = send_mid(dict(w_mix_out=g_mix, w_conf_out=g_conf, w_sconv_out=g_sconv, w_attn_out=g_attn), g_attn)
    d_a1, d_ln_g, d_ln_b = _ln_silu_bwd(S["a1"], _after(P["conf_ln_g"][l:l + 1], [tok]), P["conf_ln_b"][l:l + 1], d_a3)
    d_av, d_ag, d_bg, d_cg, d_bh, d_conf_dw, d_sconv_w = _convs_bwd(S["u"], P["conf_dw"][l], P["sconv_w"][l], d_a1, d_s2, C)
    dq, dk, dv, d_sinks = _attn_bwd(S["u"], S["o"], S["lse"], d_o, P["sinks"][l], C)
    du = jnp.concatenate([d_av, d_ag, d_bg, d_cg, d_bh, dq, dk, dv, dg0, dg1, dg2], axis=1)
    g_in = _mm_plain("mm_g_in", du, S["h"], "TN", [BF16], tm=1280)[0]
    tok = send_in(dict(w_in=g_in.reshape(N_DEV, I // N_DEV, D)), g_in)
    dh = _mm_plain("mm_d_h", du, W["w_in"][l].reshape(I, D), "NN", [F32], tk=1280, dep=tok)[0]
    dx, dx_b, dg_mix = _rms_bwd(dh, S["xc"], P["norm_mix_g"][l:l + 1], dx1)
    small = dict(norm_mix_g=dg_mix, gate_b=d_gate_b, conf_dw=d_conf_dw, conf_ln_g=d_ln_g, conf_ln_b=d_ln_b,
                 sconv_w=d_sconv_w, sinks=d_sinks[0:1], norm_ffn_g=dg_ffn)
    return dx, dx_b, small


BIG = ("w_in", "w_conf_out", "w_sconv_out", "w_attn_out", "w_mix_out", "w_up", "w_down")
MLP_WEIGHTS = ("w_down", "w_up")
MID_WEIGHTS = ("w_mix_out", "w_conf_out", "w_sconv_out", "w_attn_out")
IN_WEIGHTS = ("w_in",)
SMALL_PER_LAYER = ("norm_mix_g", "gate_b", "conf_dw", "conf_ln_g", "conf_ln_b", "sconv_w", "sinks", "norm_ffn_g")
WEIGHTS = ("norm_mix_g", "w_in", "gate_b", "conf_dw", "conf_ln_g", "conf_ln_b", "w_conf_out", "sconv_w", "w_sconv_out",
           "sinks", "w_attn_out", "w_mix_out", "norm_ffn_g", "w_up", "w_down", "final_g")


SUBLANES = 8


def _nrows(n_el, width):
    per_tile = SUBLANES * width
    return SUBLANES * (-(-n_el // per_tile))


def _rows(a, width):
    flat = a.reshape(-1)
    nrow = _nrows(flat.shape[0], width)
    return jnp.pad(flat, (0, nrow * width - flat.shape[0])).reshape(nrow, width)


def _as3d(a):
    if a.ndim == 1:
        return a.reshape(1, 1, -1)
    if a.ndim == 2:
        return a.reshape(1, *a.shape)
    return a


def kernel(x, norm_mix_g, w_in, gate_b, conf_dw, conf_ln_g, conf_ln_b, w_conf_out, sconv_w, w_sconv_out, sinks, w_attn_out, w_mix_out, norm_ffn_g, w_up, w_down, final_g, loss_target, m_norm_mix_g, m_w_in, m_gate_b, m_conf_dw, m_conf_ln_g, m_conf_ln_b, m_w_conf_out, m_sconv_w, m_w_sconv_out, m_sinks, m_w_attn_out, m_w_mix_out, m_norm_ffn_g, m_w_up, m_w_down, m_final_g, v_norm_mix_g, v_w_in, v_gate_b, v_conf_dw, v_conf_ln_g, v_conf_ln_b, v_w_conf_out, v_sconv_w, v_w_sconv_out, v_sinks, v_w_attn_out, v_w_mix_out, v_norm_ffn_g, v_w_up, v_w_down, v_final_g):
    w = dict(norm_mix_g=norm_mix_g, w_in=w_in, gate_b=gate_b, conf_dw=conf_dw, conf_ln_g=conf_ln_g, conf_ln_b=conf_ln_b,
             w_conf_out=w_conf_out, sconv_w=sconv_w, w_sconv_out=w_sconv_out, sinks=sinks, w_attn_out=w_attn_out,
             w_mix_out=w_mix_out, norm_ffn_g=norm_ffn_g, w_up=w_up, w_down=w_down, final_g=final_g)
    mom = dict(norm_mix_g=m_norm_mix_g, w_in=m_w_in, gate_b=m_gate_b, conf_dw=m_conf_dw, conf_ln_g=m_conf_ln_g,
               conf_ln_b=m_conf_ln_b, w_conf_out=m_w_conf_out, sconv_w=m_sconv_w, w_sconv_out=m_w_sconv_out,
               sinks=m_sinks, w_attn_out=m_w_attn_out, w_mix_out=m_w_mix_out, norm_ffn_g=m_norm_ffn_g, w_up=m_w_up,
               w_down=m_w_down, final_g=m_final_g)
    var = dict(norm_mix_g=v_norm_mix_g, w_in=v_w_in, gate_b=v_gate_b, conf_dw=v_conf_dw, conf_ln_g=v_conf_ln_g,
               conf_ln_b=v_conf_ln_b, w_conf_out=v_w_conf_out, sconv_w=v_sconv_w, w_sconv_out=v_w_sconv_out,
               sinks=v_sinks, w_attn_out=v_w_attn_out, w_mix_out=v_w_mix_out, norm_ffn_g=v_norm_ffn_g, w_up=v_w_up,
               w_down=v_w_down, final_g=v_final_g)

    _, T, D = x.shape
    L = w_in.shape[0]
    C = D // 2
    I = w_in.shape[2] * N_DEV
    F = w_up.shape[2] * N_DEV
    dims = (T, D, C, I, F)
    my_block = _block_of(_me())

    w["w_in"], mom["w_in"], var["w_in"] = (jnp.swapaxes(a, 1, 2) for a in (w_in, m_w_in, v_w_in))

    shard_names = ("gate_b", "conf_dw", "sconv_w")
    packed = jnp.concatenate([_rows(w[k], LANES) for k in shard_names], axis=0)
    gathered = _all_gather("gather_small", [packed])[0]

    pos = jnp.stack(_me()).astype(jnp.int32)
    blk = my_block.reshape(1).astype(jnp.int32)
    W = {k: [_cast_place(w[k], l, blk) for l in range(L)] for k in BIG}
    n_near, n_far = len(BIG) * len(NEAR), len(BIG) * 2

    def gather_near(l, after):
        return _copies_start(f"gather_near_start_{l}", [], [W[k][l] for k in BIG], _gather_plan_near, n_near, after)

    def gather_far(l, g, after):
        lands = _copies_wait(f"gather_near_wait_{l}", g["near"], g["lands"], after)
        g["far"], lands, tok_far = _copies_start(f"gather_far_start_{l}", [], lands, _gather_plan_far, n_far)
        g["pass_near"], g["lands"], tok_pass = _copies_start(
            f"pass_near_start_{l}", [], lands, _pass_on_plan((X_NEIGHBOUR, Y_NEIGHBOUR)), 2 * len(BIG))
        return tok_far + tok_pass

    def gather_finish(l, g, after):
        lands = _copies_wait(f"gather_far_wait_{l}", g["far"], g["lands"], after)
        pass_far, lands, _ = _copies_start(f"pass_far_start_{l}", [], lands,
                                           _pass_on_plan((X_NEIGHBOUR | Y_NEIGHBOUR,)), len(BIG))
        lands = _copies_wait(f"pass_near_wait_{l}", g["pass_near"], lands, after)
        return _copies_wait(f"pass_far_wait_{l}", pass_far, lands, after)

    gathering = {}
    gathering["near"], gathering["lands"], tok = gather_near(0, gathered)
    later_layers = [W[k][l] for l in range(1, L) for k in BIG]
    tok = gather_far(0, gathering, [tok, *later_layers])

    P = dict(norm_mix_g=norm_mix_g, conf_ln_g=conf_ln_g, conf_ln_b=conf_ln_b, sinks=sinks, norm_ffn_g=norm_ffn_g)
    row0 = 0
    for k in shard_names:
        n_el = w[k].size
        nrow = _nrows(n_el, LANES)
        part = gathered[:, row0:row0 + nrow].reshape(N_DEV, -1)[:, :n_el].reshape(N_DEV, *w[k].shape)
        P[k] = jnp.moveaxis(part, 0, 2).reshape(*w[k].shape[:2], N_DEV * w[k].shape[2])
        row0 += nrow

    xc = x.reshape(T, D)
    saved = []
    for l in range(L):
        lands = gather_finish(l, gathering, xc if l else tok)
        for k, g in zip(BIG, lands):
            W[k][l] = g
        deps, mid_layer = (), None
        if l + 1 < L:
            gathering = {}
            gathering["near"], gathering["lands"], tok = gather_near(l + 1, lands[0])
            deps = (tok,)
            mid_layer = functools.partial(gather_far, l + 1, gathering)
        xc, S = _layer_fwd(xc, l, W, P, dims, deps, mid_layer)
        saved.append(S)
    dx, dx_b, d_final_g, loss_tile = _loss_head(xc, final_g.reshape(1, D), loss_target.reshape(T, D))

    own_all = {k: lax.empty((L, *W[k][0].shape[1:]), F32) for k in BIG}
    recv = {k: lax.empty((L, N_CHIPS - 1, *W[k][0].shape[1:]), BF16) for k in BIG}
    scatters = []

    def to_sibling(names, grads, l, after):
        part4 = [grads[k].reshape(N_CHIPS, 2, *grads[k].shape[1:]) for k in names]
        zone = [lax.empty((N_CHIPS, 1, *p.shape[2:]), BF16) for p in part4]
        fl, zone, token = _copies_start(f"sibling_start_{l}_{names[0]}", part4, zone, _sibling_plan, len(names), after)
        return dict(names=names, l=l, part4=part4, flight=fl, zone=zone), token

    def to_owners(group, after):
        names, l = group["names"], group["l"]
        sib4 = _copies_wait(f"sibling_wait_{l}_{names[0]}", group["flight"], group["zone"], after)
        chip_parts = []
        for k, p4, s4 in zip(names, group["part4"], sib4):
            cp, own_all[k] = _chip_sum(p4, s4, own_all[k], l, pos)
            chip_parts.append(cp)
        fl, zone, token = _copies_start(f"scatter_start_{l}_{names[0]}", chip_parts, [recv[k] for k in names],
                                        _scatter_plan(l), len(names) * len(OTHER_CHIPS))
        for k, g in zip(names, zone):
            recv[k] = g
        scatters.append((f"scatter_wait_{l}_{names[0]}", fl, names, l))
        return token

    small_grads = [None] * L
    dep, groups = None, {}
    for l in reversed(range(L)):
        S = saved[l]
        dx1, dx1_b, g_mlp, dg_ffn = _bwd_mlp(dx, dx_b, l, W, P, S, dims, dep)
        groups["mlp"], dep = to_sibling(MLP_WEIGHTS, g_mlp, l, dx1)
        if "in" in groups:
            dep = dep + to_owners(groups["in"], dx1)
        g_mix, merge = _bwd_mix_out(dx1_b, l, W, P, S, dims, dep)
        dep = to_owners(groups["mlp"], merge[0])

        def send_mid(grads, after, l=l):
            groups["mid"], token = to_sibling(MID_WEIGHTS, grads, l, after)
            return token

        def send_in(grads, after, l=l):
            token = to_owners(groups["mid"], after)
            groups["in"], token2 = to_sibling(IN_WEIGHTS, grads, l, after)
            return token + token2

        dx, dx_b, small_grads[l] = _bwd_mixers(dx1, merge, dg_ffn, g_mix, l, W, P, S, dims, dep, send_mid, send_in)
    to_owners(groups["in"], dx)

    def await_scatters(layers):
        for name, fl, names, l in scatters:
            if l in layers:
                for k, g in zip(names, _copies_wait(name, fl, [recv[k] for k in names], dx)):
                    recv[k] = g

    await_scatters(range(1, L))
    done = {k: _sum4_adamw(own_all[k], recv[k], w[k], mom[k], var[k], 1, L - 1) for k in BIG} if L > 1 else {}

    width = C
    pieces = [_rows(small_grads[l][k], width) for l in range(L) for k in SMALL_PER_LAYER]
    pieces += [_rows(d_final_g, width), _rows(loss_tile[0:1, 0:1], width)]
    partial = jnp.concatenate(pieces, axis=0)
    everyone = _all_gather("gather_small_grads", [partial])[0]
    total = _sum8(everyone.reshape(1, *everyone.shape))[0]
    grads = {}
    row0 = 0
    per_layer = {k: [] for k in SMALL_PER_LAYER}
    for l in range(L):
        for k in SMALL_PER_LAYER:
            shape = small_grads[l][k].shape
            n_el = small_grads[l][k].size
            nrow = _nrows(n_el, width)
            per_layer[k].append(total[row0:row0 + nrow].reshape(-1)[:n_el].reshape(shape))
            row0 += nrow
    nrow = _nrows(D, width)
    grads["final_g"] = total[row0:row0 + nrow].reshape(-1)[:D]
    row0 += nrow
    loss = total[row0, 0]
    for k in SMALL_PER_LAYER:
        full = jnp.stack(per_layer[k], axis=0)
        if k in shard_names:
            shard = w[k].shape[2]
            full = lax.dynamic_slice_in_dim(full, my_block * shard, shard, axis=2)
        grads[k] = full.reshape(w[k].shape)

    delta, new_m, new_v = {}, {}, {}
    await_scatters([0])
    for k in BIG:
        grads[k], delta[k], new_m[k], new_v[k] = _sum4_adamw(own_all[k], recv[k], w[k], mom[k], var[k], 0, 1,
                                                             done.get(k))
    for out in (grads, delta, new_m, new_v):
        out["w_in"] = jnp.swapaxes(out["w_in"], 1, 2)
    for k in WEIGHTS:
        if k in BIG:
            continue
        d, mn, vn = _adamw(_as3d(w[k]), _as3d(grads[k]), _as3d(mom[k]), _as3d(var[k]))
        delta[k], new_m[k], new_v[k] = d.reshape(w[k].shape), mn.reshape(w[k].shape), vn.reshape(w[k].shape)

    return (loss, dx.reshape(1, T, D), *[grads[k] for k in WEIGHTS], *[delta[k] for k in WEIGHTS],
            *[new_m[k] for k in WEIGHTS], *[new_v[k] for k in WEIGHTS])
```

```python
import functools

import jax
import jax.numpy as jnp
from jax import lax
from jax.experimental import pallas as pl
from jax.experimental.pallas import tpu as pltpu

F32 = jnp.float32
BF16 = jnp.bfloat16

N_DEV = 8
HEAD_DIM = 64
N_KV_HEADS = 4
ATTN_BLOCK = 128
CONF_KERNEL = 31
SCONV_KERNEL = 3
N_BRANCH = 3
RMS_EPS = 1e-6
LN_EPS = 1e-5
ADAM_LR = 0.001
ADAM_B1 = 0.9
ADAM_B2 = 0.999
ADAM_EPS = 1e-08
ADAM_WD = 0.01
ADAM_STEP = 10
LANES = 128
NEG_BIG = -1e30
VMEM_LIMIT_BYTES = 56 * 1024 * 1024
MESH = pl.DeviceIdType.MESH

NN = (((1,), (0,)), ((), ()))
NT = (((1,), (1,)), ((), ()))
TN = (((0,), (0,)), ((), ()))


def _pick(n, cap, mult=LANES):
    best = None
    for d in range(mult, min(n, cap) + 1, mult):
        if n % d == 0:
            best = d
    assert best is not None, (n, cap, mult)
    return best


def _sigmoid(x):
    return 1.0 / (1.0 + jnp.exp(-x))


def _params(sem):
    return pltpu.CompilerParams(dimension_semantics=sem, vmem_limit_bytes=VMEM_LIMIT_BYTES)


def _epi_cast(p, ex, outs):
    outs[0][...] = p.astype(outs[0].dtype)


def _epi_resid(p, ex, outs):
    outs[0][...] = ex[0][...] + p


def _epi_relu2(p, ex, outs):
    outs[0][...] = p.astype(outs[0].dtype)
    r = jnp.maximum(p, 0.0)
    outs[1][...] = (r * r).astype(outs[1].dtype)


def _epi_drelu2(p, ex, outs):
    up = ex[0][...].astype(F32)
    outs[0][...] = (p * (2.0 * jnp.maximum(up, 0.0))).astype(outs[0].dtype)


TOKEN = (8, LANES)


def _matmul(name, a, b, dnums, grid, a_spec, b_spec, out_shape, out_specs, epi, acc_shape, extra=(), extra_specs=(),
            dep=None):
    if dep is not None:
        extra = [*extra, dep]
        extra_specs = [*extra_specs, pl.BlockSpec(TOKEN, lambda j, i, k: (0, 0))]
    nk = grid[2]
    n_extra, n_out = len(extra), len(out_shape)

    def body(*refs):
        a_ref, b_ref = refs[0], refs[1]
        ex = refs[2:2 + n_extra]
        outs = refs[2 + n_extra:2 + n_extra + n_out]
        p = lax.dot_general(a_ref[...], b_ref[...], dnums, preferred_element_type=F32)
        if nk == 1:
            epi(p, ex, outs)
        else:
            acc = refs[-1]
            k = pl.program_id(2)

            @pl.when(k == 0)
            def _():
                acc[...] = p

            @pl.when(k > 0)
            def _():
                acc[...] += p

            @pl.when(k == nk - 1)
            def _():
                epi(acc[...], ex, outs)

    scratch = [pltpu.VMEM(acc_shape, F32)] if nk > 1 else []
    return pl.pallas_call(
        body, name=name, grid=grid, in_specs=[a_spec, b_spec, *extra_specs], out_specs=list(out_specs),
        out_shape=list(out_shape), scratch_shapes=scratch,
        compiler_params=_params(("parallel", "parallel", "arbitrary")))(a, b, *extra)


def _mm_plain(name, a, b, form, out_dtypes, epi=_epi_cast, extra=(), tm=1024, tn=1024, tk=2048, dep=None):
    if form == "NN":
        (M, K), N = a.shape, b.shape[1]
    elif form == "NT":
        (M, K), N = a.shape, b.shape[0]
    else:
        (K, M), N = a.shape, b.shape[1]
    tm, tn, tk = _pick(M, tm, 8), _pick(N, tn), _pick(K, tk)
    grid = (N // tn, M // tm, K // tk)
    if form == "TN":
        a_spec = pl.BlockSpec((tk, tm), lambda j, i, k: (k, i))
    else:
        a_spec = pl.BlockSpec((tm, tk), lambda j, i, k: (i, k))
    if form == "NT":
        b_spec = pl.BlockSpec((tn, tk), lambda j, i, k: (j, k))
    else:
        b_spec = pl.BlockSpec((tk, tn), lambda j, i, k: (k, j))
    o_spec = pl.BlockSpec((tm, tn), lambda j, i, k: (i, j))
    dn = {"NN": NN, "NT": NT, "TN": TN}[form]
    return _matmul(name, a, b, dn, grid, a_spec, b_spec,
                   [jax.ShapeDtypeStruct((M, N), dt) for dt in out_dtypes], [o_spec] * len(out_dtypes), epi,
                   (tm, tn), extra, [o_spec] * len(extra), dep)


def _mm_nn_colblocked(name, a, bb, out_dtypes, epi=_epi_cast, tm=1024, tn=1024, tk=2048):
    M, K = a.shape
    ns = bb.shape[2]
    N = N_DEV * ns
    tm, tn, tk = _pick(M, tm, 8), _pick(ns, tn), _pick(K, tk)
    q = ns // tn
    grid = (N // tn, M // tm, K // tk)
    a_spec = pl.BlockSpec((tm, tk), lambda j, i, k: (i, k))
    b_spec = pl.BlockSpec((None, tk, tn), lambda j, i, k: (j // q, k, j % q))
    o_spec = pl.BlockSpec((tm, tn), lambda j, i, k: (i, j))
    return _matmul(name, a, bb, NN, grid, a_spec, b_spec,
                   [jax.ShapeDtypeStruct((M, N), dt) for dt in out_dtypes], [o_spec] * len(out_dtypes), epi, (tm, tn))


def _mm_nt_colblocked(name, a, bb, out_dtype, tm=1024, tn=1024, tk=1024, dep=None):
    M, N = a.shape
    K, ns = bb.shape[1], bb.shape[2]
    tm, tn, tk = _pick(M, tm, 8), _pick(K, tn), _pick(ns, tk)
    q = ns // tk
    grid = (K // tn, M // tm, N // tk)
    a_spec = pl.BlockSpec((tm, tk), lambda j, i, k: (i, k))
    b_spec = pl.BlockSpec((None, tn, tk), lambda j, i, k: (k // q, j, k % q))
    o_spec = pl.BlockSpec((tm, tn), lambda j, i, k: (i, j))
    return _matmul(name, a, bb, NT, grid, a_spec, b_spec, [jax.ShapeDtypeStruct((M, K), out_dtype)], [o_spec],
                   _epi_cast, (tm, tn), dep=dep)[0]


def _mm_tn_colblocked_out(name, a, b, out_dtype, tm=1024, tn=1024, tk=2048):
    T, M = a.shape
    N = b.shape[1]
    ns = N // N_DEV
    tm, tn, tk = _pick(M, tm, 8), _pick(ns, tn), _pick(T, tk)
    q = ns // tn
    grid = (N // tn, M // tm, T // tk)
    a_spec = pl.BlockSpec((tk, tm), lambda j, i, k: (k, i))
    b_spec = pl.BlockSpec((tk, tn), lambda j, i, k: (k, j))
    o_spec = pl.BlockSpec((None, tm, tn), lambda j, i, k: (j // q, i, j % q))
    return _matmul(name, a, b, TN, grid, a_spec, b_spec, [jax.ShapeDtypeStruct((N_DEV, M, ns), out_dtype)], [o_spec],
                   _epi_cast, (tm, tn))[0]


ROW_TILE = 256


def _rms_fwd(x, g):
    T, D = x.shape
    tr = _pick(T, ROW_TILE, 8)

    def body(x_ref, g_ref, h_ref):
        xv = x_ref[...]
        r = lax.rsqrt(jnp.mean(xv * xv, axis=-1, keepdims=True) + RMS_EPS)
        h_ref[...] = (xv * r * g_ref[...]).astype(BF16)

    return pl.pallas_call(
        body, name="rms_fwd", grid=(T // tr,),
        in_specs=[pl.BlockSpec((tr, D), lambda i: (i, 0)), pl.BlockSpec((1, D), lambda i: (0, 0))],
        out_specs=pl.BlockSpec((tr, D), lambda i: (i, 0)),
        out_shape=jax.ShapeDtypeStruct((T, D), BF16), compiler_params=_params(("parallel",)))(x, g)


def _rms_bwd_math(dh, xv, g):
    r = lax.rsqrt(jnp.mean(xv * xv, axis=-1, keepdims=True) + RMS_EPS)
    gdh = dh * g
    dot = jnp.mean(gdh * xv, axis=-1, keepdims=True)
    dx = r * gdh - xv * (r * r * r * dot)
    return dx, dh * xv * r


def _rms_bwd(dh, x, g, dres):
    T, D = x.shape
    tr = _pick(T, ROW_TILE, 8)

    def body(dh_ref, x_ref, g_ref, dres_ref, dx_ref, dxb_ref, dg_ref):
        dx, dgrow = _rms_bwd_math(dh_ref[...], x_ref[...], g_ref[...])
        dx = dx + dres_ref[...]
        dx_ref[...] = dx
        dxb_ref[...] = dx.astype(BF16)
        part = jnp.sum(dgrow, axis=0, keepdims=True)

        @pl.when(pl.program_id(0) == 0)
        def _():
            dg_ref[...] = part

        @pl.when(pl.program_id(0) > 0)
        def _():
            dg_ref[...] += part

    row = pl.BlockSpec((tr, D), lambda i: (i, 0))
    vec = pl.BlockSpec((1, D), lambda i: (0, 0))
    return pl.pallas_call(
        body, name="rms_bwd", grid=(T // tr,), in_specs=[row, row, vec, row], out_specs=[row, row, vec],
        out_shape=[jax.ShapeDtypeStruct((T, D), F32), jax.ShapeDtypeStruct((T, D), BF16),
                   jax.ShapeDtypeStruct((1, D), F32)],
        compiler_params=_params(("arbitrary",)))(dh, x, g, dres)


def _loss_head(x, g, target):
    T, D = x.shape
    tr = _pick(T, ROW_TILE, 8)

    def body(x_ref, g_ref, t_ref, dx_ref, dxb_ref, dg_ref, loss_ref):
        xv, gv = x_ref[...], g_ref[...]
        r = lax.rsqrt(jnp.mean(xv * xv, axis=-1, keepdims=True) + RMS_EPS)
        err = xv * r * gv - t_ref[...]
        part_loss = 0.5 * jnp.sum(jnp.mean(err * err, axis=-1, keepdims=True), axis=0, keepdims=True)
        dx, dgrow = _rms_bwd_math(err * (1.0 / D), xv, gv)
        dx_ref[...] = dx
        dxb_ref[...] = dx.astype(BF16)
        part = jnp.sum(dgrow, axis=0, keepdims=True)
        lpart = jnp.broadcast_to(part_loss, (8, LANES))

        @pl.when(pl.program_id(0) == 0)
        def _():
            dg_ref[...] = part
            loss_ref[...] = lpart

        @pl.when(pl.program_id(0) > 0)
        def _():
            dg_ref[...] += part
            loss_ref[...] += lpart

    row = pl.BlockSpec((tr, D), lambda i: (i, 0))
    vec = pl.BlockSpec((1, D), lambda i: (0, 0))
    lsp = pl.BlockSpec((8, LANES), lambda i: (0, 0))
    return pl.pallas_call(
        body, name="loss_head", grid=(T // tr,), in_specs=[row, vec, row], out_specs=[row, row, vec, lsp],
        out_shape=[jax.ShapeDtypeStruct((T, D), F32), jax.ShapeDtypeStruct((T, D), BF16),
                   jax.ShapeDtypeStruct((1, D), F32), jax.ShapeDtypeStruct((8, LANES), F32)],
        compiler_params=_params(("arbitrary",)))(x, g, target)


def _ln_math(a1, g, b):
    mu = jnp.mean(a1, axis=-1, keepdims=True)
    xc = a1 - mu
    rstd = lax.rsqrt(jnp.mean(xc * xc, axis=-1, keepdims=True) + LN_EPS)
    xhat = xc * rstd
    return xhat, rstd, xhat * g + b


def _ln_silu(a1, g, b):
    T, C = a1.shape
    tr = _pick(T, ROW_TILE, 8)

    def body(a_ref, g_ref, b_ref, o_ref):
        _, _, y = _ln_math(a_ref[...], g_ref[...], b_ref[...])
        o_ref[...] = (y * _sigmoid(y)).astype(BF16)

    row = pl.BlockSpec((tr, C), lambda i: (i, 0))
    vec = pl.BlockSpec((1, C), lambda i: (0, 0))
    return pl.pallas_call(body, name="ln_silu", grid=(T // tr,), in_specs=[row, vec, vec], out_specs=row,
                          out_shape=jax.ShapeDtypeStruct((T, C), BF16), compiler_params=_params(("parallel",)))(a1, g, b)


def _ln_silu_bwd(a1, g, b, d_a3):
    T, C = a1.shape
    tr = _pick(T, ROW_TILE, 8)

    def body(a_ref, g_ref, b_ref, d_ref, da_ref, dg_ref, db_ref):
        gv = g_ref[...]
        xhat, rstd, y = _ln_math(a_ref[...], gv, b_ref[...])
        s = _sigmoid(y)
        dy = d_ref[...].astype(F32) * (s * (1.0 + y * (1.0 - s)))
        dxh = dy * gv
        m1 = jnp.mean(dxh, axis=-1, keepdims=True)
        m2 = jnp.mean(dxh * xhat, axis=-1, keepdims=True)
        da_ref[...] = rstd * (dxh - m1 - xhat * m2)
        pg = jnp.sum(dy * xhat, axis=0, keepdims=True)
        pb = jnp.sum(dy, axis=0, keepdims=True)

        @pl.when(pl.program_id(0) == 0)
        def _():
            dg_ref[...] = pg
            db_ref[...] = pb

        @pl.when(pl.program_id(0) > 0)
        def _():
            dg_ref[...] += pg
            db_ref[...] += pb

    row = pl.BlockSpec((tr, C), lambda i: (i, 0))
    vec = pl.BlockSpec((1, C), lambda i: (0, 0))
    return pl.pallas_call(
        body, name="ln_silu_bwd", grid=(T // tr,), in_specs=[row, vec, vec, row], out_specs=[row, vec, vec],
        out_shape=[jax.ShapeDtypeStruct((T, C), F32), jax.ShapeDtypeStruct((1, C), F32),
                   jax.ShapeDtypeStruct((1, C), F32)],
        compiler_params=_params(("arbitrary",)))(a1, g, b, d_a3)


CONV_ROWS = 128
PAD_A = 32
PAD_B = 8


def _u_block(T, first):
    return pl.BlockSpec((T, LANES), lambda i: (0, first + i))


def _causal_conv(xpad_ref, w_ref, ksize, pad, T, emit):
    for r0 in range(0, T, CONV_ROWS):
        acc = None
        for j in range(ksize):
            off = pad - (ksize - 1) + j + r0
            term = w_ref[j:j + 1, :] * xpad_ref[off:off + CONV_ROWS, :]
            acc = term if acc is None else acc + term
        emit(r0, acc)


def _anticausal_conv(gpad_ref, w_ref, ksize, T, emit):
    for r0 in range(0, T, CONV_ROWS):
        acc = None
        for j in range(ksize):
            off = (ksize - 1) - j + r0
            term = w_ref[j:j + 1, :] * gpad_ref[off:off + CONV_ROWS, :]
            acc = term if acc is None else acc + term
        emit(r0, acc)


def _conv_wgrad(xpad_ref, g_ref, dw_ref, ksize, pad, T):
    for j in range(ksize):
        acc = None
        for r0 in range(0, T, CONV_ROWS):
            off = pad - (ksize - 1) + j + r0
            term = g_ref[r0:r0 + CONV_ROWS, :] * xpad_ref[off:off + CONV_ROWS, :]
            term = jnp.sum(term.reshape(CONV_ROWS // 8, 8, LANES), axis=0)
            acc = term if acc is None else acc + term
        dw_ref[j:j + 1, :] = jnp.sum(acc, axis=0, keepdims=True)


def _convs_fwd(u, conf_dw, sconv_w, C):
    T = u.shape[0]
    nb = C // LANES

    def body(av_ref, ag_ref, bg_ref, cg_ref, bh_ref, dw_ref, sw_ref, a1_ref, s2_ref, xa, xs, s1):
        xa[0:PAD_A, :] = jnp.zeros((PAD_A, LANES), F32)
        xa[PAD_A:PAD_A + T, :] = av_ref[...].astype(F32) * _sigmoid(ag_ref[...].astype(F32))

        def emit_a(r0, acc):
            a1_ref[r0:r0 + CONV_ROWS, :] = acc

        _causal_conv(xa, dw_ref, CONF_KERNEL, PAD_A, T, emit_a)

        xs[0:PAD_B, :] = jnp.zeros((PAD_B, LANES), F32)
        xs[PAD_B:PAD_B + T, :] = cg_ref[...].astype(F32) * bh_ref[...].astype(F32)

        def emit_b(r0, acc):
            s1[r0:r0 + CONV_ROWS, :] = acc

        _causal_conv(xs, sw_ref, SCONV_KERNEL, PAD_B, T, emit_b)
        s2_ref[...] = (bg_ref[...].astype(F32) * s1[...]).astype(BF16)

    col = pl.BlockSpec((T, LANES), lambda i: (0, i))
    return pl.pallas_call(
        body, name="convs_fwd", grid=(nb,),
        in_specs=[_u_block(T, 0), _u_block(T, nb), _u_block(T, 2 * nb), _u_block(T, 3 * nb), _u_block(T, 4 * nb),
                  pl.BlockSpec((CONF_KERNEL, LANES), lambda i: (0, i)),
                  pl.BlockSpec((SCONV_KERNEL, LANES), lambda i: (0, i))],
        out_specs=[col, col],
        out_shape=[jax.ShapeDtypeStruct((T, C), F32), jax.ShapeDtypeStruct((T, C), BF16)],
        scratch_shapes=[pltpu.VMEM((T + PAD_A, LANES), F32), pltpu.VMEM((T + PAD_B, LANES), F32),
                        pltpu.VMEM((T, LANES), F32)],
        compiler_params=_params(("parallel",)))(u, u, u, u, u, conf_dw, sconv_w)


def _convs_bwd(u, conf_dw, sconv_w, d_a1, d_s2, C):
    T = u.shape[0]
    nb = C // LANES

    def body(av_ref, ag_ref, bg_ref, cg_ref, bh_ref, dw_ref, sw_ref, da1_ref, ds2_ref,
             dav_ref, dag_ref, dbg_ref, dcg_ref, dbh_ref, ddw_ref, dsw_ref, xa, ga, xs, gs, tmp):
        av = av_ref[...].astype(F32)
        sg = _sigmoid(ag_ref[...].astype(F32))
        xa[0:PAD_A, :] = jnp.zeros((PAD_A, LANES), F32)
        xa[PAD_A:PAD_A + T, :] = av * sg
        ga[0:T, :] = da1_ref[...]
        ga[T:T + PAD_A, :] = jnp.zeros((PAD_A, LANES), F32)
        _conv_wgrad(xa, ga, ddw_ref, CONF_KERNEL, PAD_A, T)

        def emit_a(r0, acc):
            tmp[r0:r0 + CONV_ROWS, :] = acc

        _anticausal_conv(ga, dw_ref, CONF_KERNEL, T, emit_a)
        da0 = tmp[...]
        dav_ref[...] = (da0 * sg).astype(BF16)
        dag_ref[...] = (da0 * av * sg * (1.0 - sg)).astype(BF16)

        cg = cg_ref[...].astype(F32)
        bh = bh_ref[...].astype(F32)
        ds2 = ds2_ref[...].astype(F32)
        xs[0:PAD_B, :] = jnp.zeros((PAD_B, LANES), F32)
        xs[PAD_B:PAD_B + T, :] = cg * bh

        def emit_s1(r0, acc):
            tmp[r0:r0 + CONV_ROWS, :] = acc

        _causal_conv(xs, sw_ref, SCONV_KERNEL, PAD_B, T, emit_s1)
        dbg_ref[...] = (ds2 * tmp[...]).astype(BF16)
        gs[0:T, :] = ds2 * bg_ref[...].astype(F32)
        gs[T:T + PAD_B, :] = jnp.zeros((PAD_B, LANES), F32)
        _conv_wgrad(xs, gs, dsw_ref, SCONV_KERNEL, PAD_B, T)

        def emit_b(r0, acc):
            tmp[r0:r0 + CONV_ROWS, :] = acc

        _anticausal_conv(gs, sw_ref, SCONV_KERNEL, T, emit_b)
        ds0 = tmp[...]
        dcg_ref[...] = (ds0 * bh).astype(BF16)
        dbh_ref[...] = (ds0 * cg).astype(BF16)

    col = pl.BlockSpec((T, LANES), lambda i: (0, i))
    wa = pl.BlockSpec((CONF_KERNEL, LANES), lambda i: (0, i))
    wb = pl.BlockSpec((SCONV_KERNEL, LANES), lambda i: (0, i))
    act = jax.ShapeDtypeStruct((T, C), BF16)
    return pl.pallas_call(
        body, name="convs_bwd", grid=(nb,),
        in_specs=[_u_block(T, 0), _u_block(T, nb), _u_block(T, 2 * nb), _u_block(T, 3 * nb), _u_block(T, 4 * nb),
                  wa, wb, col, col],
        out_specs=[col, col, col, col, col, wa, wb],
        out_shape=[act, act, act, act, act, jax.ShapeDtypeStruct((CONF_KERNEL, C), F32),
                   jax.ShapeDtypeStruct((SCONV_KERNEL, C), F32)],
        scratch_shapes=[pltpu.VMEM((T + PAD_A, LANES), F32), pltpu.VMEM((T + PAD_A, LANES), F32),
                        pltpu.VMEM((T + PAD_B, LANES), F32), pltpu.VMEM((T + PAD_B, LANES), F32),
                        pltpu.VMEM((T, LANES), F32)],
        compiler_params=_params(("parallel",)))(u, u, u, u, u, conf_dw, sconv_w, d_a1, d_s2)


def _attn_specs(T, C, q_off_blocks):
    kvw = N_KV_HEADS * HEAD_DIM
    kb = (5 * C + C) // kvw
    qs = pl.BlockSpec((ATTN_BLOCK, C), lambda n: (n, 5))
    kc = pl.BlockSpec((ATTN_BLOCK, kvw), lambda n: (n, kb))
    kp = pl.BlockSpec((ATTN_BLOCK, kvw), lambda n: (jnp.maximum(n - 1, 0), kb))
    vc = pl.BlockSpec((ATTN_BLOCK, kvw), lambda n: (n, kb + 1))
    vp = pl.BlockSpec((ATTN_BLOCK, kvw), lambda n: (jnp.maximum(n - 1, 0), kb + 1))
    return qs, kc, kp, vc, vp


def _attn_masks(n, rows):
    row = lax.broadcasted_iota(jnp.int32, (rows, ATTN_BLOCK), 0) % ATTN_BLOCK
    col = lax.broadcasted_iota(jnp.int32, (rows, ATTN_BLOCK), 1)
    from_cur = col <= row
    return from_cur, jnp.logical_or(from_cur, n > 0)


def _stack_heads(ref, heads, width=HEAD_DIM):
    return jnp.concatenate([ref[:, h * width:(h + 1) * width] for h in heads], axis=0)


def _unstack_heads(t, count):
    return [t[j * ATTN_BLOCK:(j + 1) * ATTN_BLOCK] for j in range(count)]


def _attn_scores(qh, kc, kp, from_cur, valid):
    qs = qh * (HEAD_DIM ** -0.5)
    s_c = lax.dot_general(qs, kc, NT, preferred_element_type=F32)
    s_p = lax.dot_general(qs, kp, NT, preferred_element_type=F32)
    return jnp.where(valid, jnp.where(from_cur, s_c, s_p), NEG_BIG)


def _attn_split(t, from_cur):
    t = t.astype(BF16)
    zero = jnp.zeros_like(t)
    return jnp.where(from_cur, t, zero), jnp.where(from_cur, zero, t)


def _attn_fwd(u, sinks, C):
    T = u.shape[0]
    H = C // HEAD_DIM
    grp = H // N_KV_HEADS

    def body(sink_ref, q_ref, kc_ref, kp_ref, vc_ref, vp_ref, o_ref, lse_ref):
        n = pl.program_id(0)
        from_cur, valid = _attn_masks(n, grp * ATTN_BLOCK)
        outs, lses = [], []
        for g in range(N_KV_HEADS):
            kv = slice(g * HEAD_DIM, (g + 1) * HEAD_DIM)
            heads = range(g * grp, (g + 1) * grp)
            sink = jnp.concatenate([jnp.full((ATTN_BLOCK, 1), sink_ref[h], F32) for h in heads], axis=0)
            s = _attn_scores(_stack_heads(q_ref, heads), kc_ref[:, kv], kp_ref[:, kv], from_cur, valid)
            m = jnp.maximum(jnp.max(s, axis=-1, keepdims=True), sink)
            p = jnp.exp(s - m)
            den = jnp.sum(p, axis=-1, keepdims=True) + jnp.exp(sink - m)
            p_c, p_p = _attn_split(p, from_cur)
            acc = jnp.dot(p_c, vc_ref[:, kv], preferred_element_type=F32)
            acc = acc + jnp.dot(p_p, vp_ref[:, kv], preferred_element_type=F32)
            outs += _unstack_heads((acc / den).astype(BF16), grp)
            lses += _unstack_heads(m + jnp.log(den), grp)
        o_ref[...] = jnp.concatenate(outs, axis=1)
        lse_ref[...] = jnp.concatenate(lses, axis=1)

    qs, kc, kp, vc, vp = _attn_specs(T, C, 5)
    return pl.pallas_call(
        body, name="attn_fwd", grid=(T // ATTN_BLOCK,),
        in_specs=[pl.BlockSpec(memory_space=pltpu.SMEM), qs, kc, kp, vc, vp],
        out_specs=[pl.BlockSpec((ATTN_BLOCK, C), lambda n: (n, 0)), pl.BlockSpec((ATTN_BLOCK, H), lambda n: (n, 0))],
        out_shape=[jax.ShapeDtypeStruct((T, C), BF16), jax.ShapeDtypeStruct((T, H), F32)],
        compiler_params=_params(("parallel",)))(sinks, u, u, u, u, u)


def _attn_bwd(u, o, lse, d_o, sinks, C):
    T = u.shape[0]
    H = C // HEAD_DIM
    grp = H // N_KV_HEADS
    kvw = N_KV_HEADS * HEAD_DIM
    nblk = T // ATTN_BLOCK
    scale = HEAD_DIM ** -0.5

    def body(sink_ref, q_ref, kc_ref, kp_ref, vc_ref, vp_ref, o_ref, lse_ref, do_ref,
             dq_ref, dk_ref, dv_ref, ds_ref, dk_acc, dv_acc):
        n = pl.program_id(0)

        @pl.when(n == 0)
        def _():
            dk_acc[...] = jnp.zeros_like(dk_acc)
            dv_acc[...] = jnp.zeros_like(dv_acc)
            ds_ref[...] = jnp.zeros_like(ds_ref)

        from_cur, valid = _attn_masks(n, grp * ATTN_BLOCK)
        cur = pl.ds(pl.multiple_of(n * ATTN_BLOCK, ATTN_BLOCK), ATTN_BLOCK)
        prev = pl.ds(pl.multiple_of(jnp.maximum(n - 1, 0) * ATTN_BLOCK, ATTN_BLOCK), ATTN_BLOCK)
        dqs, dsinks, dk_cs, dk_ps, dv_cs, dv_ps = [], [], [], [], [], []
        for g in range(N_KV_HEADS):
            kv = slice(g * HEAD_DIM, (g + 1) * HEAD_DIM)
            kc, kp, vc, vp = kc_ref[:, kv], kp_ref[:, kv], vc_ref[:, kv], vp_ref[:, kv]
            heads = range(g * grp, (g + 1) * grp)
            qg, dog, og = _stack_heads(q_ref, heads), _stack_heads(do_ref, heads), _stack_heads(o_ref, heads)
            lse_g = _stack_heads(lse_ref, heads, 1)
            sink = jnp.concatenate([jnp.full((ATTN_BLOCK, 1), sink_ref[h], F32) for h in heads], axis=0)
            p = jnp.exp(_attn_scores(qg, kc, kp, from_cur, valid) - lse_g)
            delta = jnp.sum(dog.astype(F32) * og.astype(F32), axis=-1, keepdims=True)
            dp = jnp.where(from_cur, lax.dot_general(dog, vc, NT, preferred_element_type=F32),
                           lax.dot_general(dog, vp, NT, preferred_element_type=F32))
            ds_c, ds_p = _attn_split(p * (dp - delta) * scale, from_cur)
            p_c, p_p = _attn_split(p, from_cur)
            dq = jnp.dot(ds_c, kc, preferred_element_type=F32) + jnp.dot(ds_p, kp, preferred_element_type=F32)
            dqs += _unstack_heads(dq.astype(BF16), grp)
            dk_cs.append(lax.dot_general(ds_c, qg, TN, preferred_element_type=F32))
            dk_ps.append(lax.dot_general(ds_p, qg, TN, preferred_element_type=F32))
            dv_cs.append(lax.dot_general(p_c, dog, TN, preferred_element_type=F32))
            dv_ps.append(lax.dot_general(p_p, dog, TN, preferred_element_type=F32))
            for t in _unstack_heads(jnp.exp(sink - lse_g) * delta, grp):
                dsinks.append(jnp.broadcast_to(-jnp.sum(t, axis=0, keepdims=True), (8, 1)))
        dq_ref[...] = jnp.concatenate(dqs, axis=1)
        ds_ref[...] += jnp.concatenate(dsinks, axis=1)
        dk_acc[cur, :] += jnp.concatenate(dk_cs, axis=1)
        dk_acc[prev, :] += jnp.concatenate(dk_ps, axis=1)
        dv_acc[cur, :] += jnp.concatenate(dv_cs, axis=1)
        dv_acc[prev, :] += jnp.concatenate(dv_ps, axis=1)

        @pl.when(n == nblk - 1)
        def _():
            dk_ref[...] = dk_acc[...].astype(BF16)
            dv_ref[...] = dv_acc[...].astype(BF16)

    qs, kc, kp, vc, vp = _attn_specs(T, C, 5)
    blk = pl.BlockSpec((ATTN_BLOCK, C), lambda n: (n, 0))
    full = pl.BlockSpec((T, kvw), lambda n: (0, 0))
    return pl.pallas_call(
        body, name="attn_bwd", grid=(nblk,),
        in_specs=[pl.BlockSpec(memory_space=pltpu.SMEM), qs, kc, kp, vc, vp, blk,
                  pl.BlockSpec((ATTN_BLOCK, H), lambda n: (n, 0)), blk],
        out_specs=[blk, full, full, pl.BlockSpec((8, H), lambda n: (0, 0))],
        out_shape=[jax.ShapeDtypeStruct((T, C), BF16), jax.ShapeDtypeStruct((T, kvw), BF16),
                   jax.ShapeDtypeStruct((T, kvw), BF16), jax.ShapeDtypeStruct((8, H), F32)],
        scratch_shapes=[pltpu.VMEM((T, kvw), F32), pltpu.VMEM((T, kvw), F32)],
        compiler_params=_params(("arbitrary",)))(sinks, u, u, u, u, u, o, lse, d_o)


MERGE_COLS = 512


def _merge_specs(T, D, I):
    tr = _pick(T, ROW_TILE, 8)
    tc = _pick(D, MERGE_COLS)
    g0 = (I - N_BRANCH * D) // tc
    per = D // tc
    gspecs = [pl.BlockSpec((tr, tc), functools.partial(lambda j, i, b: (i, g0 + b * per + j), b=b)) for b in range(N_BRANCH)]
    tile = pl.BlockSpec((tr, tc), lambda j, i: (i, j))
    bias = pl.BlockSpec((N_BRANCH, tc), lambda j, i: (0, j))
    return tr, tc, gspecs, tile, bias


def _merge_fwd(u, gate_b, ya, yb, yc):
    T, I = u.shape
    D = ya.shape[1]
    tr, tc, gspecs, tile, bias = _merge_specs(T, D, I)

    def body(g0_ref, g1_ref, g2_ref, b_ref, ya_ref, yb_ref, yc_ref, o_ref):
        acc = None
        for b, (g_ref, y_ref) in enumerate(((g0_ref, ya_ref), (g1_ref, yb_ref), (g2_ref, yc_ref))):
            gate = _sigmoid(g_ref[...].astype(F32) + b_ref[b:b + 1, :])
            term = gate * y_ref[...].astype(F32)
            acc = term if acc is None else acc + term
        o_ref[...] = acc.astype(BF16)

    return pl.pallas_call(
        body, name="merge_fwd", grid=(D // tc, T // tr), in_specs=[*gspecs, bias, tile, tile, tile], out_specs=tile,
        out_shape=jax.ShapeDtypeStruct((T, D), BF16),
        compiler_params=_params(("parallel", "parallel")))(u, u, u, gate_b, ya, yb, yc)


def _merge_bwd(u, gate_b, ya, yb, yc, dm):
    T, I = u.shape
    D = ya.shape[1]
    tr, tc, gspecs, tile, bias = _merge_specs(T, D, I)

    def body(g0_ref, g1_ref, g2_ref, b_ref, ya_ref, yb_ref, yc_ref, dm_ref,
             dya_ref, dyb_ref, dyc_ref, dg0_ref, dg1_ref, dg2_ref, db_ref):
        dmv = dm_ref[...].astype(F32)
        first = pl.program_id(1) == 0
        for b, (g_ref, y_ref, dy_ref, dg_ref) in enumerate(((g0_ref, ya_ref, dya_ref, dg0_ref),
                                                           (g1_ref, yb_ref, dyb_ref, dg1_ref),
                                                           (g2_ref, yc_ref, dyc_ref, dg2_ref))):
            gate = _sigmoid(g_ref[...].astype(F32) + b_ref[b:b + 1, :])
            dy_ref[...] = (dmv * gate).astype(BF16)
            dpre = dmv * y_ref[...].astype(F32) * gate * (1.0 - gate)
            dg_ref[...] = dpre.astype(BF16)
            part = jnp.sum(dpre, axis=0, keepdims=True)

            @pl.when(first)
            def _():
                db_ref[b:b + 1, :] = part

            @pl.when(jnp.logical_not(first))
            def _():
                db_ref[b:b + 1, :] += part

    act = jax.ShapeDtypeStruct((T, D), BF16)
    return pl.pallas_call(
        body, name="merge_bwd", grid=(D // tc, T // tr), in_specs=[*gspecs, bias, tile, tile, tile, tile],
        out_specs=[tile] * 6 + [bias], out_shape=[act] * 6 + [jax.ShapeDtypeStruct((N_BRANCH, D), F32)],
        compiler_params=_params(("parallel", "arbitrary")))(u, u, u, gate_b, ya, yb, yc, dm)


ELEMS_PER_TILE = 256 * 1024


def _row_tile(r, c):
    return _pick(r, max(16, ELEMS_PER_TILE // c), 16) if r % 16 == 0 else r


def _cast_place(w, layer, my_block):
    L, r, c = w.shape
    tr = _row_tile(r, c)

    def body(blk_ref, w_ref, o_ref):
        o_ref[...] = w_ref[...].astype(BF16)

    return pl.pallas_call(
        body, name="cast_place",
        grid_spec=pltpu.PrefetchScalarGridSpec(
            num_scalar_prefetch=1, grid=(r // tr,),
            in_specs=[pl.BlockSpec((None, tr, c), lambda i, blk: (layer, i, 0))],
            out_specs=pl.BlockSpec((None, tr, c), lambda i, blk: (blk[0], i, 0))),
        out_shape=jax.ShapeDtypeStruct((N_DEV, r, c), BF16), compiler_params=_params(("parallel",)))(my_block, w)


def _adamw_math(w, g, m, v):
    m = ADAM_B1 * m + (1.0 - ADAM_B1) * g
    v = ADAM_B2 * v + (1.0 - ADAM_B2) * (g * g)
    m_hat = m / (1.0 - ADAM_B1 ** ADAM_STEP)
    v_hat = v / (1.0 - ADAM_B2 ** ADAM_STEP)
    delta = -ADAM_LR * (m_hat / (jnp.sqrt(v_hat) + ADAM_EPS) + ADAM_WD * w)
    return delta, m, v


def _sum_parts(part_ref):
    acc = part_ref[0].astype(F32)
    for s in range(1, N_DEV):
        acc = acc + part_ref[s].astype(F32)
    return acc


def _sum8(parts):
    L, _, r, c = parts.shape
    tr = _row_tile(r, c)

    def body(p_ref, o_ref):
        o_ref[...] = _sum_parts(p_ref)

    return pl.pallas_call(
        body, name="sum8", grid=(L, r // tr),
        in_specs=[pl.BlockSpec((None, N_DEV, tr, c), lambda l, i: (l, 0, i, 0))],
        out_specs=pl.BlockSpec((None, tr, c), lambda l, i: (l, i, 0)),
        out_shape=jax.ShapeDtypeStruct((L, r, c), F32), compiler_params=_params(("parallel", "parallel")))(parts)


def _adamw(w, g, m, v):
    L, r, c = w.shape
    tr = _row_tile(r, c)
    spec = pl.BlockSpec((None, tr, c), lambda l, i: (l, i, 0))

    def body(w_ref, g_ref, m_ref, v_ref, d_ref, mo_ref, vo_ref):
        d, mn, vn = _adamw_math(w_ref[...], g_ref[...], m_ref[...], v_ref[...])
        d_ref[...] = d
        mo_ref[...] = mn
        vo_ref[...] = vn

    shp = jax.ShapeDtypeStruct(w.shape, F32)
    return pl.pallas_call(body, name="adamw", grid=(L, r // tr), in_specs=[spec] * 4, out_specs=[spec] * 3,
                          out_shape=[shp] * 3, compiler_params=_params(("parallel", "parallel")))(w, g, m, v)


N_CHIPS = 4
CHIP_XOR = (0, 2, 1, 3)


def _chip_sum(part4, sib4, own_all, layer, pos):
    _, _, r, c = part4.shape
    tr = _row_tile(r, c)

    def chip(p, s):
        return jnp.bitwise_xor(2 * p[0] + p[1], CHIP_XOR[s])

    mine = [pl.BlockSpec((None, None, tr, c), functools.partial(lambda i, p, s: (chip(p, s), p[2], i, 0), s=s))
            for s in range(N_CHIPS)]
    theirs = [pl.BlockSpec((None, None, tr, c), functools.partial(lambda i, p, s: (chip(p, s), 0, i, 0), s=s))
              for s in range(N_CHIPS)]

    def body(pos_ref, *refs):
        a, b = refs[:N_CHIPS], refs[N_CHIPS:2 * N_CHIPS]
        out_ref, own_ref = refs[2 * N_CHIPS + 1], refs[2 * N_CHIPS + 2]
        own_ref[...] = a[0][...].astype(F32) + b[0][...].astype(F32)
        for s in range(1, N_CHIPS):
            out_ref[s - 1] = (a[s][...].astype(F32) + b[s][...].astype(F32)).astype(BF16)

    return pl.pallas_call(
        body, name="chip_sum",
        grid_spec=pltpu.PrefetchScalarGridSpec(
            num_scalar_prefetch=1, grid=(r // tr,),
            in_specs=[*mine, *theirs, pl.BlockSpec(memory_space=pl.ANY)],
            out_specs=[pl.BlockSpec((N_CHIPS - 1, tr, c), lambda i, p: (0, i, 0)),
                       pl.BlockSpec((None, tr, c), lambda i, p: (layer, i, 0))]),
        out_shape=[jax.ShapeDtypeStruct((N_CHIPS - 1, r, c), BF16), jax.ShapeDtypeStruct(own_all.shape, F32)],
        input_output_aliases={1 + 2 * N_CHIPS: 1},
        compiler_params=_params(("parallel",)))(pos, *([part4] * N_CHIPS), *([sib4] * N_CHIPS), own_all)


def _sum_chips(own_ref, got_ref):
    acc = own_ref[...]
    for s in range(N_CHIPS - 1):
        acc = acc + got_ref[s].astype(F32)
    return acc


def _sum4_adamw(own, got, w, m, v, first, count, earlier=None):
    L, r, c = w.shape
    tr = _row_tile(r, c)
    spec = pl.BlockSpec((None, tr, c), lambda l, i: (first + l, i, 0))
    earlier = [] if earlier is None else list(earlier)

    def body(own_ref, got_ref, w_ref, m_ref, v_ref, *rest):
        g_ref, d_ref, mo_ref, vo_ref = rest[len(earlier):]
        g = _sum_chips(own_ref, got_ref)
        d, mn, vn = _adamw_math(w_ref[...], g, m_ref[...], v_ref[...])
        g_ref[...] = g
        d_ref[...] = d
        mo_ref[...] = mn
        vo_ref[...] = vn

    shp = jax.ShapeDtypeStruct(w.shape, F32)
    return pl.pallas_call(
        body, name="sum4_adamw", grid=(count, r // tr),
        in_specs=[spec, pl.BlockSpec((None, N_CHIPS - 1, tr, c), lambda l, i: (first + l, 0, i, 0)), spec, spec, spec,
                  *([ANY] * len(earlier))],
        out_specs=[spec] * 4, out_shape=[shp] * 4, input_output_aliases={5 + a: a for a in range(len(earlier))},
        compiler_params=_params(("parallel", "parallel")))(own, got, w, m, v, *earlier)


def _me():
    return lax.axis_index("x"), lax.axis_index("y"), lax.axis_index("c")


def _flip(pos, k):
    x, y, c = pos
    return (1 - x if k & 4 else x, 1 - y if k & 2 else y, 1 - c if k & 1 else c)


def _block_of(pos):
    return 4 * pos[0] + 2 * pos[1] + pos[2]


ANY = pl.BlockSpec(memory_space=pl.ANY)
SIBLING = 1
OTHER_CHIPS = (4, 2, 6)


def _all_gather(name, arrays, after=None):
    n = len(arrays)
    shapes = [a.shape[-2:] for a in arrays]
    deps = [] if after is None else [after]

    def body(*refs):
        srcs, outs = refs[:n], refs[n + len(deps):2 * n + len(deps)]
        send_sems, recv_sems, local_sems = refs[2 * n + len(deps):]
        me = _me()
        sib = _flip(me, SIBLING)

        def src_of(a):
            return srcs[a]

        def copy(a, k, block_pos, to, src=None):
            dst = outs[a].at[_block_of(block_pos)]
            return pltpu.make_async_remote_copy(
                src_ref=dst if src is None else src, dst_ref=dst, send_sem=send_sems.at[a, k],
                recv_sem=recv_sems.at[a, k], device_id=to, device_id_type=MESH)

        mine = [pltpu.make_async_copy(src_of(a), outs[a].at[_block_of(me)], local_sems.at[a]) for a in range(n)]
        for cp in mine:
            cp.start()
        first = []
        for a in range(n):
            first.append(copy(a, 0, me, sib, src=src_of(a)))
            for j, k in enumerate(OTHER_CHIPS):
                first.append(copy(a, 1 + j, me, _flip(me, k), src=src_of(a)))
        for cp in first:
            cp.start()
        passed = []
        for j, k in enumerate(OTHER_CHIPS):
            for a in range(n):
                copy(a, 1 + j, _flip(me, k), me).wait_recv()
                fw = copy(a, 4 + j, _flip(me, k), sib)
                fw.start()
                passed.append(fw)
        for a in range(n):
            copy(a, 0, sib, me).wait_recv()
            for j, k in enumerate(OTHER_CHIPS):
                copy(a, 4 + j, _flip(sib, k), me).wait_recv()
        for cp in first + passed:
            cp.wait_send()
        for cp in mine:
            cp.wait()

    return pl.pallas_call(
        body, name=name, in_specs=[ANY] * (n + len(deps)), out_specs=[ANY] * n,
        out_shape=[jax.ShapeDtypeStruct((N_DEV, *s), a.dtype) for s, a in zip(shapes, arrays)],
        scratch_shapes=[pltpu.SemaphoreType.DMA((n, 7)), pltpu.SemaphoreType.DMA((n, 7)), pltpu.SemaphoreType.DMA((n,))],
    )(*arrays, *deps)


HBM = pl.BlockSpec(memory_space=pltpu.HBM)
SEM = pl.BlockSpec(memory_space=pltpu.SEMAPHORE)
DATAFLOW = pltpu.SideEffectType.DATAFLOW_SIDE_EFFECTING
X_NEIGHBOUR, Y_NEIGHBOUR = 4, 2
NEAR = (SIBLING, X_NEIGHBOUR, Y_NEIGHBOUR)


def _in_hbm(a):
    return pltpu.with_memory_space_constraint(a, pltpu.HBM)


def _hbm_like(a):
    return pltpu.HBM(a.shape, a.dtype)


def _copies_start(name, srcs, lands, plan, n_copies, after=None):
    ns, n = len(srcs), len(lands)
    deps = [] if after is None else [after]

    def body(*refs):
        k0 = ns + n + len(deps)
        send_sems, recv_sems, token = refs[k0], refs[k0 + 1], refs[k0 + 2 + n]
        for s, (src, dst, peer, _) in enumerate(plan(_me(), refs[:ns], refs[ns:ns + n])):
            pltpu.make_async_remote_copy(src_ref=src, dst_ref=dst, send_sem=send_sems.at[s], recv_sem=recv_sems.at[s],
                                         device_id=peer, device_id_type=MESH).start()
        token[...] = jnp.zeros(TOKEN, F32)

    out = pl.pallas_call(
        body, name=name, in_specs=[*([HBM] * (ns + n)), *([ANY] * len(deps))],
        out_specs=[SEM, SEM, *([HBM] * n), pl.BlockSpec(memory_space=pltpu.VMEM)],
        out_shape=[pltpu.SemaphoreType.DMA((n_copies,)), pltpu.SemaphoreType.DMA((n_copies,)),
                   *[_hbm_like(a) for a in lands], jax.ShapeDtypeStruct(TOKEN, F32)],
        input_output_aliases={ns + a: 2 + a for a in range(n)},
        compiler_params=pltpu.CompilerParams(has_side_effects=DATAFLOW),
    )(*[_in_hbm(a) for a in srcs], *[_in_hbm(a) for a in lands], *deps)
    return dict(send=out[0], recv=out[1], srcs=list(srcs), plan=plan), list(out[2:2 + n]), out[2 + n]


def _copies_wait(name, flight, lands, after):
    srcs, plan = flight["srcs"], flight["plan"]
    ns, n = len(srcs), len(lands)
    after = list(after) if isinstance(after, (list, tuple)) else [after]

    def body(*refs):
        send_sems, recv_sems = refs[ns + n], refs[ns + n + 1]
        for s, (src, dst, peer, landing) in enumerate(plan(_me(), refs[:ns], refs[ns:ns + n])):
            pltpu.make_async_remote_copy(src_ref=src, dst_ref=dst, send_sem=send_sems.at[s], recv_sem=recv_sems.at[s],
                                         device_id=peer, device_id_type=MESH).wait_send()
            pltpu.make_async_remote_copy(src_ref=landing, dst_ref=landing, send_sem=send_sems.at[s],
                                         recv_sem=recv_sems.at[s], device_id=peer, device_id_type=MESH).wait_recv()

    out = pl.pallas_call(
        body, name=name, in_specs=[*([HBM] * (ns + n)), SEM, SEM, *([ANY] * len(after))], out_specs=[HBM] * n,
        out_shape=[_hbm_like(a) for a in lands], input_output_aliases={ns + a: a for a in range(n)},
        compiler_params=pltpu.CompilerParams(has_side_effects=DATAFLOW),
    )(*srcs, *lands, flight["send"], flight["recv"], *after)
    return list(out)


def _gather_plan_near(me, srcs, lands):
    plan = []
    for land in lands:
        own = land.at[_block_of(me)]
        for k in NEAR:
            peer = _flip(me, k)
            plan.append((own, own, peer, land.at[_block_of(peer)]))
    return plan


def _gather_plan_far(me, srcs, lands):
    x_nbr, y_nbr, far = _flip(me, X_NEIGHBOUR), _flip(me, Y_NEIGHBOUR), _flip(me, X_NEIGHBOUR | Y_NEIGHBOUR)
    plan = []
    for land in lands:
        half = land.shape[1] // 2
        first, second = pl.ds(0, half), pl.ds(half, half)
        passed = land.at[_block_of(y_nbr), first]
        plan.append((passed, passed, x_nbr, land.at[_block_of(far), first]))
        passed = land.at[_block_of(x_nbr), second]
        plan.append((passed, passed, y_nbr, land.at[_block_of(far), second]))
    return plan


def _broadcast_plan(me, srcs, lands):
    plan = []
    for land in lands:
        own = land.at[_block_of(me)]
        for k in range(1, N_DEV):
            peer = _flip(me, k)
            plan.append((own, own, peer, land.at[_block_of(peer)]))
    return plan


def _sibling_plan(me, srcs, lands):
    sib = _flip(me, SIBLING)
    return [(src.at[:, pl.ds(1 - me[2], 1)], land, sib, land) for src, land in zip(srcs, lands)]


def _scatter_plan(layer):
    def plan(me, srcs, lands):
        out = []
        for src, land in zip(srcs, lands):
            for j, k in enumerate(OTHER_CHIPS):
                out.append((src.at[j], land.at[layer, j], _flip(me, k), land.at[layer, j]))
        return out
    return plan


def _pass_on_plan(relations):
    def plan(me, srcs, lands):
        sib = _flip(me, SIBLING)
        out = []
        for land in lands:
            for k in relations:
                blk = land.at[_block_of(_flip(me, k))]
                out.append((blk, blk, sib, land.at[_block_of(_flip(sib, k))]))
        return out
    return plan


def _after(small, tokens):
    for t in tokens:
        small = small + t[0:1, 0:1]
    return small


def _layer_fwd(xc, l, W, P, dims, deps=(), mid_layer=None):
    T, D, C, I, F = dims
    h = _rms_fwd(xc, _after(P["norm_mix_g"][l:l + 1], deps))
    u = _mm_plain("mm_u", h, W["w_in"][l].reshape(I, D), "NT", [BF16], tn=1280)[0]
    a1, s2 = _convs_fwd(u, P["conf_dw"][l], P["sconv_w"][l], C)
    a3 = _ln_silu(a1, P["conf_ln_g"][l:l + 1], P["conf_ln_b"][l:l + 1])
    o, lse = _attn_fwd(u, P["sinks"][l], C)
    ya = _mm_nn_colblocked("mm_branch_out", a3, W["w_conf_out"][l], [BF16], tm=2048)[0]
    yb = _mm_nn_colblocked("mm_branch_out", s2, W["w_sconv_out"][l], [BF16], tm=2048)[0]
    yc = _mm_nn_colblocked("mm_branch_out", o, W["w_attn_out"][l], [BF16], tm=2048)[0]
    merged = _merge_fwd(u, P["gate_b"][l], ya, yb, yc)
    x1 = _mm_plain("mm_mix", merged, W["w_mix_out"][l].reshape(D, D), "NN", [F32], _epi_resid, [xc], tk=1024)[0]
    norm_ffn_g = P["norm_ffn_g"][l:l + 1]
    h2 = _rms_fwd(x1, norm_ffn_g if mid_layer is None else _after(norm_ffn_g, [mid_layer(x1)]))
    up, act = _mm_nn_colblocked("mm_up", h2, W["w_up"][l], [BF16, BF16], _epi_relu2)
    x2 = _mm_plain("mm_down", act, W["w_down"][l].reshape(F, D), "NN", [F32], _epi_resid, [x1], tk=2048)[0]
    saved = dict(xc=xc, h=h, u=u, a1=a1, a3=a3, s2=s2, o=o, lse=lse, ya=ya, yb=yb, yc=yc, merged=merged, x1=x1, h2=h2,
                 up=up, act=act)
    return x2, saved


def _bwd_mlp(dx2, dx2_b, l, W, P, S, dims, dep=None):
    T, D, C, I, F = dims
    d_up = _mm_plain("mm_d_up", dx2_b, W["w_down"][l].reshape(F, D), "NT", [BF16], _epi_drelu2, [S["up"]], dep=dep)[0]
    g_down = _mm_plain("mm_g_down", S["act"], dx2_b, "TN", [BF16])[0]
    dh2 = _mm_nt_colblocked("mm_d_h2", d_up, W["w_up"][l], F32)
    g_up = _mm_tn_colblocked_out("mm_g_up", S["h2"], d_up, BF16)
    dx1, dx1_b, dg_ffn = _rms_bwd(dh2, S["x1"], P["norm_ffn_g"][l:l + 1], dx2)
    return dx1, dx1_b, dict(w_up=g_up, w_down=g_down.reshape(N_DEV, F // N_DEV, D)), dg_ffn


def _bwd_mix_out(dx1_b, l, W, P, S, dims, dep=None):
    T, D, C, I, F = dims
    dm = _mm_plain("mm_d_merged", dx1_b, W["w_mix_out"][l].reshape(D, D), "NT", [BF16], dep=dep)[0]
    g_mix = _mm_plain("mm_g_mix", S["merged"], dx1_b, "TN", [BF16])[0]
    merge = _merge_bwd(S["u"], P["gate_b"][l], S["ya"], S["yb"], S["yc"], dm)
    return g_mix.reshape(N_DEV, D // N_DEV, D), merge


def _bwd_mixers(dx1, merge, dg_ffn, g_mix, l, W, P, S, dims, dep, send_mid, send_in):
    T, D, C, I, F = dims
    d_ya, d_yb, d_yc, dg0, dg1, dg2, d_gate_b = merge
    d_a3 = _mm_nt_colblocked("mm_d_branch", d_ya, W["w_conf_out"][l], BF16, tm=2048, dep=dep)
    d_s2 = _mm_nt_colblocked("mm_d_branch", d_yb, W["w_sconv_out"][l], BF16, tm=2048)
    d_o = _mm_nt_colblocked("mm_d_branch", d_yc, W["w_attn_out"][l], BF16, tm=2048)
    g_conf = _mm_tn_colblocked_out("mm_g_branch", S["a3"], d_ya, BF16)
    g_sconv = _mm_tn_colblocked_out("mm_g_branch", S["s2"], d_yb, BF16)
    g_attn = _mm_tn_colblocked_out("mm_g_branch", S["o"], d_yc, BF16)
    tok = send_mid(dict(w_mix_out=g_mix, w_conf_out=g_conf, w_sconv_out=g_sconv, w_attn_out=g_attn), g_attn)
    d_a1, d_ln_g, d_ln_b = _ln_silu_bwd(S["a1"], _after(P["conf_ln_g"][l:l + 1], [tok]), P["conf_ln_b"][l:l + 1], d_a3)
    d_av, d_ag, d_bg, d_cg, d_bh, d_conf_dw, d_sconv_w = _convs_bwd(S["u"], P["conf_dw"][l], P["sconv_w"][l], d_a1, d_s2, C)
    dq, dk, dv, d_sinks = _attn_bwd(S["u"], S["o"], S["lse"], d_o, P["sinks"][l], C)
    du = jnp.concatenate([d_av, d_ag, d_bg, d_cg, d_bh, dq, dk, dv, dg0, dg1, dg2], axis=1)
    g_in = _mm_plain("mm_g_in", du, S["h"], "TN", [BF16], tm=1280)[0]
    tok = send_in(dict(w_in=g_in.reshape(N_DEV, I // N_DEV, D)), g_in)
    dh = _mm_plain("mm_d_h", du, W["w_in"][l].reshape(I, D), "NN", [F32], tk=2560, dep=tok)[0]
    dx, dx_b, dg_mix = _rms_bwd(dh, S["xc"], P["norm_mix_g"][l:l + 1], dx1)
    small = dict(norm_mix_g=dg_mix, gate_b=d_gate_b, conf_dw=d_conf_dw, conf_ln_g=d_ln_g, conf_ln_b=d_ln_b,
                 sconv_w=d_sconv_w, sinks=d_sinks[0:1], norm_ffn_g=dg_ffn)
    return dx, dx_b, small


BIG = ("w_in", "w_conf_out", "w_sconv_out", "w_attn_out", "w_mix_out", "w_up", "w_down")
MLP_WEIGHTS = ("w_down", "w_up")
MID_WEIGHTS = ("w_mix_out", "w_conf_out", "w_sconv_out", "w_attn_out")
IN_WEIGHTS = ("w_in",)
SMALL_PER_LAYER = ("norm_mix_g", "gate_b", "conf_dw", "conf_ln_g", "conf_ln_b", "sconv_w", "sinks", "norm_ffn_g")
WEIGHTS = ("norm_mix_g", "w_in", "gate_b", "conf_dw", "conf_ln_g", "conf_ln_b", "w_conf_out", "sconv_w", "w_sconv_out",
           "sinks", "w_attn_out", "w_mix_out", "norm_ffn_g", "w_up", "w_down", "final_g")


SUBLANES = 8


def _nrows(n_el, width):
    per_tile = SUBLANES * width
    return SUBLANES * (-(-n_el // per_tile))


def _rows(a, width):
    flat = a.reshape(-1)
    nrow = _nrows(flat.shape[0], width)
    return jnp.pad(flat, (0, nrow * width - flat.shape[0])).reshape(nrow, width)


def _as3d(a):
    if a.ndim == 1:
        return a.reshape(1, 1, -1)
    if a.ndim == 2:
        return a.reshape(1, *a.shape)
    return a


def kernel(x, norm_mix_g, w_in, gate_b, conf_dw, conf_ln_g, conf_ln_b, w_conf_out, sconv_w, w_sconv_out, sinks, w_attn_out, w_mix_out, norm_ffn_g, w_up, w_down, final_g, loss_target, m_norm_mix_g, m_w_in, m_gate_b, m_conf_dw, m_conf_ln_g, m_conf_ln_b, m_w_conf_out, m_sconv_w, m_w_sconv_out, m_sinks, m_w_attn_out, m_w_mix_out, m_norm_ffn_g, m_w_up, m_w_down, m_final_g, v_norm_mix_g, v_w_in, v_gate_b, v_conf_dw, v_conf_ln_g, v_conf_ln_b, v_w_conf_out, v_sconv_w, v_w_sconv_out, v_sinks, v_w_attn_out, v_w_mix_out, v_norm_ffn_g, v_w_up, v_w_down, v_final_g):
    w = dict(norm_mix_g=norm_mix_g, w_in=w_in, gate_b=gate_b, conf_dw=conf_dw, conf_ln_g=conf_ln_g, conf_ln_b=conf_ln_b,
             w_conf_out=w_conf_out, sconv_w=sconv_w, w_sconv_out=w_sconv_out, sinks=sinks, w_attn_out=w_attn_out,
             w_mix_out=w_mix_out, norm_ffn_g=norm_ffn_g, w_up=w_up, w_down=w_down, final_g=final_g)
    mom = dict(norm_mix_g=m_norm_mix_g, w_in=m_w_in, gate_b=m_gate_b, conf_dw=m_conf_dw, conf_ln_g=m_conf_ln_g,
               conf_ln_b=m_conf_ln_b, w_conf_out=m_w_conf_out, sconv_w=m_sconv_w, w_sconv_out=m_w_sconv_out,
               sinks=m_sinks, w_attn_out=m_w_attn_out, w_mix_out=m_w_mix_out, norm_ffn_g=m_norm_ffn_g, w_up=m_w_up,
               w_down=m_w_down, final_g=m_final_g)
    var = dict(norm_mix_g=v_norm_mix_g, w_in=v_w_in, gate_b=v_gate_b, conf_dw=v_conf_dw, conf_ln_g=v_conf_ln_g,
               conf_ln_b=v_conf_ln_b, w_conf_out=v_w_conf_out, sconv_w=v_sconv_w, w_sconv_out=v_w_sconv_out,
               sinks=v_sinks, w_attn_out=v_w_attn_out, w_mix_out=v_w_mix_out, norm_ffn_g=v_norm_ffn_g, w_up=v_w_up,
               w_down=v_w_down, final_g=v_final_g)

    _, T, D = x.shape
    L = w_in.shape[0]
    C = D // 2
    I = w_in.shape[2] * N_DEV
    F = w_up.shape[2] * N_DEV
    dims = (T, D, C, I, F)
    my_block = _block_of(_me())

    w["w_in"], mom["w_in"], var["w_in"] = (jnp.swapaxes(a, 1, 2) for a in (w_in, m_w_in, v_w_in))

    shard_names = ("gate_b", "conf_dw", "sconv_w")
    packed = jnp.concatenate([_rows(w[k], LANES) for k in shard_names], axis=0)
    gathered = _all_gather("gather_small", [packed])[0]

    pos = jnp.stack(_me()).astype(jnp.int32)
    blk = my_block.reshape(1).astype(jnp.int32)
    W = {k: [_cast_place(w[k], l, blk) for l in range(L)] for k in BIG}
    n_near, n_far = len(BIG) * len(NEAR), len(BIG) * 2

    def gather_near(l, after):
        return _copies_start(f"gather_near_start_{l}", [], [W[k][l] for k in BIG], _gather_plan_near, n_near, after)

    def gather_far(l, g, after):
        lands = _copies_wait(f"gather_near_wait_{l}", g["near"], g["lands"], after)
        g["far"], lands, tok_far = _copies_start(f"gather_far_start_{l}", [], lands, _gather_plan_far, n_far)
        g["pass_near"], g["lands"], tok_pass = _copies_start(
            f"pass_near_start_{l}", [], lands, _pass_on_plan((X_NEIGHBOUR, Y_NEIGHBOUR)), 2 * len(BIG))
        return tok_far + tok_pass

    def gather_finish(l, g, after):
        lands = _copies_wait(f"gather_far_wait_{l}", g["far"], g["lands"], after)
        pass_far, lands, _ = _copies_start(f"pass_far_start_{l}", [], lands,
                                           _pass_on_plan((X_NEIGHBOUR | Y_NEIGHBOUR,)), len(BIG))
        lands = _copies_wait(f"pass_near_wait_{l}", g["pass_near"], lands, after)
        return _copies_wait(f"pass_far_wait_{l}", pass_far, lands, after)

    gathering = {}
    gathering["near"], gathering["lands"], tok = gather_near(0, gathered)
    later_layers = [W[k][l] for l in range(1, L) for k in BIG]
    tok = gather_far(0, gathering, [tok, *later_layers])

    P = dict(norm_mix_g=norm_mix_g, conf_ln_g=conf_ln_g, conf_ln_b=conf_ln_b, sinks=sinks, norm_ffn_g=norm_ffn_g)
    row0 = 0
    for k in shard_names:
        n_el = w[k].size
        nrow = _nrows(n_el, LANES)
        part = gathered[:, row0:row0 + nrow].reshape(N_DEV, -1)[:, :n_el].reshape(N_DEV, *w[k].shape)
        P[k] = jnp.moveaxis(part, 0, 2).reshape(*w[k].shape[:2], N_DEV * w[k].shape[2])
        row0 += nrow

    xc = x.reshape(T, D)
    saved = []
    for l in range(L):
        lands = gather_finish(l, gathering, xc if l else tok)
        for k, g in zip(BIG, lands):
            W[k][l] = g
        deps, mid_layer = (), None
        if l + 1 < L:
            gathering = {}
            gathering["near"], gathering["lands"], tok = gather_near(l + 1, lands[0])
            deps = (tok,)
            mid_layer = functools.partial(gather_far, l + 1, gathering)
        xc, S = _layer_fwd(xc, l, W, P, dims, deps, mid_layer)
        saved.append(S)
    dx, dx_b, d_final_g, loss_tile = _loss_head(xc, final_g.reshape(1, D), loss_target.reshape(T, D))

    own_all = {k: lax.empty((L, *W[k][0].shape[1:]), F32) for k in BIG}
    recv = {k: lax.empty((L, N_CHIPS - 1, *W[k][0].shape[1:]), BF16) for k in BIG}
    scatters = []

    def to_sibling(names, grads, l, after):
        part4 = [grads[k].reshape(N_CHIPS, 2, *grads[k].shape[1:]) for k in names]
        zone = [lax.empty((N_CHIPS, 1, *p.shape[2:]), BF16) for p in part4]
        fl, zone, token = _copies_start(f"sibling_start_{l}_{names[0]}", part4, zone, _sibling_plan, len(names), after)
        return dict(names=names, l=l, part4=part4, flight=fl, zone=zone), token

    def to_owners(group, after):
        names, l = group["names"], group["l"]
        sib4 = _copies_wait(f"sibling_wait_{l}_{names[0]}", group["flight"], group["zone"], after)
        chip_parts = []
        for k, p4, s4 in zip(names, group["part4"], sib4):
            cp, own_all[k] = _chip_sum(p4, s4, own_all[k], l, pos)
            chip_parts.append(cp)
        fl, zone, token = _copies_start(f"scatter_start_{l}_{names[0]}", chip_parts, [recv[k] for k in names],
                                        _scatter_plan(l), len(names) * len(OTHER_CHIPS))
        for k, g in zip(names, zone):
            recv[k] = g
        scatters.append((f"scatter_wait_{l}_{names[0]}", fl, names, l))
        return token

    small_grads = [None] * L
    dep, groups = None, {}
    for l in reversed(range(L)):
        S = saved[l]
        dx1, dx1_b, g_mlp, dg_ffn = _bwd_mlp(dx, dx_b, l, W, P, S, dims, dep)
        groups["mlp"], dep = to_sibling(MLP_WEIGHTS, g_mlp, l, dx1)
        if "in" in groups:
            dep = dep + to_owners(groups["in"], dx1)
        g_mix, merge = _bwd_mix_out(dx1_b, l, W, P, S, dims, dep)
        dep = to_owners(groups["mlp"], merge[0])

        def send_mid(grads, after, l=l):
            groups["mid"], token = to_sibling(MID_WEIGHTS, grads, l, after)
            return token

        def send_in(grads, after, l=l):
            token = to_owners(groups["mid"], after)
            groups["in"], token2 = to_sibling(IN_WEIGHTS, grads, l, after)
            return token + token2

        dx, dx_b, small_grads[l] = _bwd_mixers(dx1, merge, dg_ffn, g_mix, l, W, P, S, dims, dep, send_mid, send_in)
    last_start = to_owners(groups["in"], dx)

    width = C
    pieces = [_rows(small_grads[l][k], width) for l in range(L) for k in SMALL_PER_LAYER]
    pieces += [_rows(d_final_g, width), _rows(loss_tile[0:1, 0:1], width)]
    partial = jnp.concatenate(pieces, axis=0)
    everyone = lax.dynamic_update_slice(lax.empty((N_DEV, *partial.shape), F32), partial[None], (my_block, 0, 0))
    small_flight, (everyone,), token = _copies_start("small_grads_start", [], [everyone], _broadcast_plan, N_DEV - 1,
                                                     after=last_start)
    last_start = last_start + token

    def await_scatters(layers):
        for name, fl, names, l in scatters:
            if l in layers:
                for k, g in zip(names, _copies_wait(name, fl, [recv[k] for k in names], [dx, last_start])):
                    recv[k] = g

    await_scatters(range(1, L))
    done = {k: _sum4_adamw(own_all[k], recv[k], w[k], mom[k], var[k], 1, L - 1) for k in BIG} if L > 1 else {}
    await_scatters([0])

    everyone = _copies_wait("small_grads_wait", small_flight, [everyone], recv[BIG[0]])[0]
    total = _sum8(everyone.reshape(1, *everyone.shape))[0]
    grads = {}
    row0 = 0
    per_layer = {k: [] for k in SMALL_PER_LAYER}
    for l in range(L):
        for k in SMALL_PER_LAYER:
            shape = small_grads[l][k].shape
            n_el = small_grads[l][k].size
            nrow = _nrows(n_el, width)
            per_layer[k].append(total[row0:row0 + nrow].reshape(-1)[:n_el].reshape(shape))
            row0 += nrow
    nrow = _nrows(D, width)
    grads["final_g"] = total[row0:row0 + nrow].reshape(-1)[:D]
    row0 += nrow
    loss = total[row0, 0]
    for k in SMALL_PER_LAYER:
        full = jnp.stack(per_layer[k], axis=0)
        if k in shard_names:
            shard = w[k].shape[2]
            full = lax.dynamic_slice_in_dim(full, my_block * shard, shard, axis=2)
        grads[k] = full.reshape(w[k].shape)

    delta, new_m, new_v = {}, {}, {}
    for k in BIG:
        grads[k], delta[k], new_m[k], new_v[k] = _sum4_adamw(own_all[k], recv[k], w[k], mom[k], var[k], 0, 1,
                                                             done.get(k))
    for out in (grads, delta, new_m, new_v):
        out["w_in"] = jnp.swapaxes(out["w_in"], 1, 2)
    for k in WEIGHTS:
        if k in BIG:
            continue
        d, mn, vn = _adamw(_as3d(w[k]), _as3d(grads[k]), _as3d(mom[k]), _as3d(var[k]))
        delta[k], new_m[k], new_v[k] = d.reshape(w[k].shape), mn.reshape(w[k].shape), vn.reshape(w[k].shape)

    return (loss, dx.reshape(1, T, D), *[grads[k] for k in WEIGHTS], *[delta[k] for k in WEIGHTS],
            *[new_m[k] for k in WEIGHTS], *[new_v[k] for k in WEIGHTS])
```

```python
import functools

import jax
import jax.numpy as jnp
from jax import lax
from jax.experimental import pallas as pl
from jax.experimental.pallas import tpu as pltpu

F32 = jnp.float32
BF16 = jnp.bfloat16

N_DEV = 8
HEAD_DIM = 64
N_KV_HEADS = 4
ATTN_BLOCK = 128
CONF_KERNEL = 31
SCONV_KERNEL = 3
N_BRANCH = 3
RMS_EPS = 1e-6
LN_EPS = 1e-5
ADAM_LR = 0.001
ADAM_B1 = 0.9
ADAM_B2 = 0.999
ADAM_EPS = 1e-08
ADAM_WD = 0.01
ADAM_STEP = 10
LANES = 128
NEG_BIG = -1e30
VMEM_LIMIT_BYTES = 56 * 1024 * 1024
MESH = pl.DeviceIdType.MESH

NN = (((1,), (0,)), ((), ()))
NT = (((1,), (1,)), ((), ()))
TN = (((0,), (0,)), ((), ()))


def _pick(n, cap, mult=LANES):
    best = None
    for d in range(mult, min(n, cap) + 1, mult):
        if n % d == 0:
            best = d
    assert best is not None, (n, cap, mult)
    return best


def _sigmoid(x):
    return 0.5 * jnp.tanh(0.5 * x) + 0.5


def _params(sem):
    return pltpu.CompilerParams(dimension_semantics=sem, vmem_limit_bytes=VMEM_LIMIT_BYTES)


def _epi_cast(p, ex, outs):
    outs[0][...] = p.astype(outs[0].dtype)


def _epi_resid(p, ex, outs):
    outs[0][...] = ex[0][...] + p


def _epi_relu2(p, ex, outs):
    outs[0][...] = p.astype(outs[0].dtype)
    r = jnp.maximum(p, 0.0)
    outs[1][...] = (r * r).astype(outs[1].dtype)


def _epi_drelu2(p, ex, outs):
    up = ex[0][...].astype(F32)
    outs[0][...] = (p * (2.0 * jnp.maximum(up, 0.0))).astype(outs[0].dtype)


TOKEN = (8, LANES)


def _matmul(name, a, b, dnums, grid, a_spec, b_spec, out_shape, out_specs, epi, acc_shape, extra=(), extra_specs=(),
            dep=None):
    if dep is not None:
        extra = [*extra, dep]
        extra_specs = [*extra_specs, pl.BlockSpec(TOKEN, lambda j, i, k: (0, 0))]
    nk = grid[2]
    n_extra, n_out = len(extra), len(out_shape)

    def body(*refs):
        a_ref, b_ref = refs[0], refs[1]
        ex = refs[2:2 + n_extra]
        outs = refs[2 + n_extra:2 + n_extra + n_out]
        p = lax.dot_general(a_ref[...], b_ref[...], dnums, preferred_element_type=F32)
        if nk == 1:
            epi(p, ex, outs)
        else:
            acc = refs[-1]
            k = pl.program_id(2)

            @pl.when(k == 0)
            def _():
                acc[...] = p

            @pl.when(k > 0)
            def _():
                acc[...] += p

            @pl.when(k == nk - 1)
            def _():
                epi(acc[...], ex, outs)

    scratch = [pltpu.VMEM(acc_shape, F32)] if nk > 1 else []
    return pl.pallas_call(
        body, name=name, grid=grid, in_specs=[a_spec, b_spec, *extra_specs], out_specs=list(out_specs),
        out_shape=list(out_shape), scratch_shapes=scratch,
        compiler_params=_params(("parallel", "parallel", "arbitrary")))(a, b, *extra)


def _mm_plain(name, a, b, form, out_dtypes, epi=_epi_cast, extra=(), tm=1024, tn=1024, tk=2048, dep=None):
    if form == "NN":
        (M, K), N = a.shape, b.shape[1]
    elif form == "NT":
        (M, K), N = a.shape, b.shape[0]
    else:
        (K, M), N = a.shape, b.shape[1]
    tm, tn, tk = _pick(M, tm, 8), _pick(N, tn), _pick(K, tk)
    grid = (N // tn, M // tm, K // tk)
    if form == "TN":
        a_spec = pl.BlockSpec((tk, tm), lambda j, i, k: (k, i))
    else:
        a_spec = pl.BlockSpec((tm, tk), lambda j, i, k: (i, k))
    if form == "NT":
        b_spec = pl.BlockSpec((tn, tk), lambda j, i, k: (j, k))
    else:
        b_spec = pl.BlockSpec((tk, tn), lambda j, i, k: (k, j))
    o_spec = pl.BlockSpec((tm, tn), lambda j, i, k: (i, j))
    dn = {"NN": NN, "NT": NT, "TN": TN}[form]
    return _matmul(name, a, b, dn, grid, a_spec, b_spec,
                   [jax.ShapeDtypeStruct((M, N), dt) for dt in out_dtypes], [o_spec] * len(out_dtypes), epi,
                   (tm, tn), extra, [o_spec] * len(extra), dep)


def _mm_nn_colblocked(name, a, bb, out_dtypes, epi=_epi_cast, tm=1024, tn=1024, tk=2048):
    M, K = a.shape
    ns = bb.shape[2]
    N = N_DEV * ns
    tm, tn, tk = _pick(M, tm, 8), _pick(ns, tn), _pick(K, tk)
    q = ns // tn
    grid = (N // tn, M // tm, K // tk)
    a_spec = pl.BlockSpec((tm, tk), lambda j, i, k: (i, k))
    b_spec = pl.BlockSpec((None, tk, tn), lambda j, i, k: (j // q, k, j % q))
    o_spec = pl.BlockSpec((tm, tn), lambda j, i, k: (i, j))
    return _matmul(name, a, bb, NN, grid, a_spec, b_spec,
                   [jax.ShapeDtypeStruct((M, N), dt) for dt in out_dtypes], [o_spec] * len(out_dtypes), epi, (tm, tn))


def _mm_nt_colblocked(name, a, bb, out_dtype, tm=1024, tn=1024, tk=1024, dep=None):
    M, N = a.shape
    K, ns = bb.shape[1], bb.shape[2]
    tm, tn, tk = _pick(M, tm, 8), _pick(K, tn), _pick(ns, tk)
    q = ns // tk
    grid = (K // tn, M // tm, N // tk)
    a_spec = pl.BlockSpec((tm, tk), lambda j, i, k: (i, k))
    b_spec = pl.BlockSpec((None, tn, tk), lambda j, i, k: (k // q, j, k % q))
    o_spec = pl.BlockSpec((tm, tn), lambda j, i, k: (i, j))
    return _matmul(name, a, bb, NT, grid, a_spec, b_spec, [jax.ShapeDtypeStruct((M, K), out_dtype)], [o_spec],
                   _epi_cast, (tm, tn), dep=dep)[0]


def _mm_tn_colblocked_out(name, a, b, out_dtype, tm=1024, tn=1024, tk=2048):
    T, M = a.shape
    N = b.shape[1]
    ns = N // N_DEV
    tm, tn, tk = _pick(M, tm, 8), _pick(ns, tn), _pick(T, tk)
    q = ns // tn
    grid = (N // tn, M // tm, T // tk)
    a_spec = pl.BlockSpec((tk, tm), lambda j, i, k: (k, i))
    b_spec = pl.BlockSpec((tk, tn), lambda j, i, k: (k, j))
    o_spec = pl.BlockSpec((None, tm, tn), lambda j, i, k: (j // q, i, j % q))
    return _matmul(name, a, b, TN, grid, a_spec, b_spec, [jax.ShapeDtypeStruct((N_DEV, M, ns), out_dtype)], [o_spec],
                   _epi_cast, (tm, tn))[0]


def _unblock(bb_ref, full_ref):
    ns = bb_ref.shape[2]
    for j in range(N_DEV):
        full_ref[:, j * ns:(j + 1) * ns] = bb_ref[j]


def _mm_branch(name, a, bb, form, out_dtype, tm=512, dep=None):
    M = a.shape[0]
    K, ns = bb.shape[1], bb.shape[2]
    N = N_DEV * ns
    tm = _pick(M, tm, 8)
    out_cols = N if form == "NN" else K
    deps = [] if dep is None else [dep]

    def body(a_ref, b_ref, *rest):
        o_ref, w_full = rest[len(deps):]

        @pl.when(pl.program_id(0) == 0)
        def _():
            _unblock(b_ref, w_full)

        o_ref[...] = lax.dot_general(a_ref[...], w_full[...], NN if form == "NN" else NT,
                                     preferred_element_type=F32).astype(out_dtype)

    return pl.pallas_call(
        body, name=name, grid=(M // tm,),
        in_specs=[pl.BlockSpec((tm, a.shape[1]), lambda i: (i, 0)), pl.BlockSpec(bb.shape, lambda i: (0, 0, 0)),
                  *[pl.BlockSpec(TOKEN, lambda i: (0, 0)) for _ in deps]],
        out_specs=pl.BlockSpec((tm, out_cols), lambda i: (i, 0)),
        out_shape=jax.ShapeDtypeStruct((M, out_cols), out_dtype),
        scratch_shapes=[pltpu.VMEM((K, N), BF16)],
        compiler_params=_params(("arbitrary",)))(a, bb, *deps)


def _mm_branch_grad(name, a, b, out_dtype, tm=256):
    T, M = a.shape
    N = b.shape[1]
    ns = N // N_DEV
    tm = _pick(M, tm, 8)

    def body(a_ref, b_ref, o_ref):
        p = lax.dot_general(a_ref[...], b_ref[...], TN, preferred_element_type=F32)
        for j in range(N_DEV):
            o_ref[j] = p[:, j * ns:(j + 1) * ns].astype(out_dtype)

    return pl.pallas_call(
        body, name=name, grid=(M // tm,),
        in_specs=[pl.BlockSpec((T, tm), lambda i: (0, i)), pl.BlockSpec((T, N), lambda i: (0, 0))],
        out_specs=pl.BlockSpec((N_DEV, tm, ns), lambda i: (0, i, 0)),
        out_shape=jax.ShapeDtypeStruct((N_DEV, M, ns), out_dtype),
        compiler_params=_params(("parallel",)))(a, b)


ROW_TILE = 256


def _rms_fwd(x, g):
    T, D = x.shape
    tr = _pick(T, ROW_TILE, 8)

    def body(x_ref, g_ref, h_ref):
        xv = x_ref[...]
        r = lax.rsqrt(jnp.mean(xv * xv, axis=-1, keepdims=True) + RMS_EPS)
        h_ref[...] = (xv * r * g_ref[...]).astype(BF16)

    return pl.pallas_call(
        body, name="rms_fwd", grid=(T // tr,),
        in_specs=[pl.BlockSpec((tr, D), lambda i: (i, 0)), pl.BlockSpec((1, D), lambda i: (0, 0))],
        out_specs=pl.BlockSpec((tr, D), lambda i: (i, 0)),
        out_shape=jax.ShapeDtypeStruct((T, D), BF16), compiler_params=_params(("parallel",)))(x, g)


def _rms_bwd_math(dh, xv, g):
    r = lax.rsqrt(jnp.mean(xv * xv, axis=-1, keepdims=True) + RMS_EPS)
    gdh = dh * g
    dot = jnp.mean(gdh * xv, axis=-1, keepdims=True)
    dx = r * gdh - xv * (r * r * r * dot)
    return dx, dh * xv * r


def _rms_bwd(dh, x, g, dres):
    T, D = x.shape
    tr = _pick(T, ROW_TILE, 8)

    def body(dh_ref, x_ref, g_ref, dres_ref, dx_ref, dxb_ref, dg_ref):
        dx, dgrow = _rms_bwd_math(dh_ref[...], x_ref[...], g_ref[...])
        dx = dx + dres_ref[...]
        dx_ref[...] = dx
        dxb_ref[...] = dx.astype(BF16)
        part = jnp.sum(dgrow, axis=0, keepdims=True)

        @pl.when(pl.program_id(0) == 0)
        def _():
            dg_ref[...] = part

        @pl.when(pl.program_id(0) > 0)
        def _():
            dg_ref[...] += part

    row = pl.BlockSpec((tr, D), lambda i: (i, 0))
    vec = pl.BlockSpec((1, D), lambda i: (0, 0))
    return pl.pallas_call(
        body, name="rms_bwd", grid=(T // tr,), in_specs=[row, row, vec, row], out_specs=[row, row, vec],
        out_shape=[jax.ShapeDtypeStruct((T, D), F32), jax.ShapeDtypeStruct((T, D), BF16),
                   jax.ShapeDtypeStruct((1, D), F32)],
        compiler_params=_params(("arbitrary",)))(dh, x, g, dres)


def _loss_head(x, g, target):
    T, D = x.shape
    tr = _pick(T, ROW_TILE, 8)

    def body(x_ref, g_ref, t_ref, dx_ref, dxb_ref, dg_ref, loss_ref):
        xv, gv = x_ref[...], g_ref[...]
        r = lax.rsqrt(jnp.mean(xv * xv, axis=-1, keepdims=True) + RMS_EPS)
        err = xv * r * gv - t_ref[...]
        part_loss = 0.5 * jnp.sum(jnp.mean(err * err, axis=-1, keepdims=True), axis=0, keepdims=True)
        dx, dgrow = _rms_bwd_math(err * (1.0 / D), xv, gv)
        dx_ref[...] = dx
        dxb_ref[...] = dx.astype(BF16)
        part = jnp.sum(dgrow, axis=0, keepdims=True)
        lpart = jnp.broadcast_to(part_loss, (8, LANES))

        @pl.when(pl.program_id(0) == 0)
        def _():
            dg_ref[...] = part
            loss_ref[...] = lpart

        @pl.when(pl.program_id(0) > 0)
        def _():
            dg_ref[...] += part
            loss_ref[...] += lpart

    row = pl.BlockSpec((tr, D), lambda i: (i, 0))
    vec = pl.BlockSpec((1, D), lambda i: (0, 0))
    lsp = pl.BlockSpec((8, LANES), lambda i: (0, 0))
    return pl.pallas_call(
        body, name="loss_head", grid=(T // tr,), in_specs=[row, vec, row], out_specs=[row, row, vec, lsp],
        out_shape=[jax.ShapeDtypeStruct((T, D), F32), jax.ShapeDtypeStruct((T, D), BF16),
                   jax.ShapeDtypeStruct((1, D), F32), jax.ShapeDtypeStruct((8, LANES), F32)],
        compiler_params=_params(("arbitrary",)))(x, g, target)


def _ln_math(a1, g, b):
    mu = jnp.mean(a1, axis=-1, keepdims=True)
    xc = a1 - mu
    rstd = lax.rsqrt(jnp.mean(xc * xc, axis=-1, keepdims=True) + LN_EPS)
    xhat = xc * rstd
    return xhat, rstd, xhat * g + b


def _ln_silu(a1, g, b):
    T, C = a1.shape
    tr = _pick(T, ROW_TILE, 8)

    def body(a_ref, g_ref, b_ref, o_ref):
        _, _, y = _ln_math(a_ref[...], g_ref[...], b_ref[...])
        o_ref[...] = (y * _sigmoid(y)).astype(BF16)

    row = pl.BlockSpec((tr, C), lambda i: (i, 0))
    vec = pl.BlockSpec((1, C), lambda i: (0, 0))
    return pl.pallas_call(body, name="ln_silu", grid=(T // tr,), in_specs=[row, vec, vec], out_specs=row,
                          out_shape=jax.ShapeDtypeStruct((T, C), BF16), compiler_params=_params(("parallel",)))(a1, g, b)


def _ln_silu_bwd(a1, g, b, d_a3):
    T, C = a1.shape
    tr = _pick(T, ROW_TILE, 8)

    def body(a_ref, g_ref, b_ref, d_ref, da_ref, dg_ref, db_ref):
        gv = g_ref[...]
        xhat, rstd, y = _ln_math(a_ref[...], gv, b_ref[...])
        s = _sigmoid(y)
        dy = d_ref[...].astype(F32) * (s * (1.0 + y * (1.0 - s)))
        dxh = dy * gv
        m1 = jnp.mean(dxh, axis=-1, keepdims=True)
        m2 = jnp.mean(dxh * xhat, axis=-1, keepdims=True)
        da_ref[...] = rstd * (dxh - m1 - xhat * m2)
        pg = jnp.sum(dy * xhat, axis=0, keepdims=True)
        pb = jnp.sum(dy, axis=0, keepdims=True)

        @pl.when(pl.program_id(0) == 0)
        def _():
            dg_ref[...] = pg
            db_ref[...] = pb

        @pl.when(pl.program_id(0) > 0)
        def _():
            dg_ref[...] += pg
            db_ref[...] += pb

    row = pl.BlockSpec((tr, C), lambda i: (i, 0))
    vec = pl.BlockSpec((1, C), lambda i: (0, 0))
    return pl.pallas_call(
        body, name="ln_silu_bwd", grid=(T // tr,), in_specs=[row, vec, vec, row], out_specs=[row, vec, vec],
        out_shape=[jax.ShapeDtypeStruct((T, C), F32), jax.ShapeDtypeStruct((1, C), F32),
                   jax.ShapeDtypeStruct((1, C), F32)],
        compiler_params=_params(("arbitrary",)))(a1, g, b, d_a3)


CONV_ROWS = 128
PAD_A = 32
PAD_B = 8


def _u_block(T, first):
    return pl.BlockSpec((T, LANES), lambda i: (0, first + i))


def _causal_conv(xpad_ref, w_ref, ksize, pad, T, emit):
    for r0 in range(0, T, CONV_ROWS):
        acc = None
        for j in range(ksize):
            off = pad - (ksize - 1) + j + r0
            term = w_ref[j:j + 1, :] * xpad_ref[off:off + CONV_ROWS, :]
            acc = term if acc is None else acc + term
        emit(r0, acc)


def _anticausal_conv(gpad_ref, w_ref, ksize, T, emit):
    for r0 in range(0, T, CONV_ROWS):
        acc = None
        for j in range(ksize):
            off = (ksize - 1) - j + r0
            term = w_ref[j:j + 1, :] * gpad_ref[off:off + CONV_ROWS, :]
            acc = term if acc is None else acc + term
        emit(r0, acc)


def _conv_wgrad(xpad_ref, g_ref, dw_ref, ksize, pad, T):
    for j in range(ksize):
        acc = None
        for r0 in range(0, T, CONV_ROWS):
            off = pad - (ksize - 1) + j + r0
            term = g_ref[r0:r0 + CONV_ROWS, :] * xpad_ref[off:off + CONV_ROWS, :]
            term = jnp.sum(term.reshape(CONV_ROWS // 8, 8, LANES), axis=0)
            acc = term if acc is None else acc + term
        dw_ref[j:j + 1, :] = jnp.sum(acc, axis=0, keepdims=True)


def _convs_fwd(u, conf_dw, sconv_w, C):
    T = u.shape[0]
    nb = C // LANES

    def body(av_ref, ag_ref, bg_ref, cg_ref, bh_ref, dw_ref, sw_ref, a1_ref, s2_ref, xa, xs, s1):
        xa[0:PAD_A, :] = jnp.zeros((PAD_A, LANES), F32)
        xa[PAD_A:PAD_A + T, :] = av_ref[...].astype(F32) * _sigmoid(ag_ref[...].astype(F32))

        def emit_a(r0, acc):
            a1_ref[r0:r0 + CONV_ROWS, :] = acc

        _causal_conv(xa, dw_ref, CONF_KERNEL, PAD_A, T, emit_a)

        xs[0:PAD_B, :] = jnp.zeros((PAD_B, LANES), F32)
        xs[PAD_B:PAD_B + T, :] = cg_ref[...].astype(F32) * bh_ref[...].astype(F32)

        def emit_b(r0, acc):
            s1[r0:r0 + CONV_ROWS, :] = acc

        _causal_conv(xs, sw_ref, SCONV_KERNEL, PAD_B, T, emit_b)
        s2_ref[...] = (bg_ref[...].astype(F32) * s1[...]).astype(BF16)

    col = pl.BlockSpec((T, LANES), lambda i: (0, i))
    return pl.pallas_call(
        body, name="convs_fwd", grid=(nb,),
        in_specs=[_u_block(T, 0), _u_block(T, nb), _u_block(T, 2 * nb), _u_block(T, 3 * nb), _u_block(T, 4 * nb),
                  pl.BlockSpec((CONF_KERNEL, LANES), lambda i: (0, i)),
                  pl.BlockSpec((SCONV_KERNEL, LANES), lambda i: (0, i))],
        out_specs=[col, col],
        out_shape=[jax.ShapeDtypeStruct((T, C), F32), jax.ShapeDtypeStruct((T, C), BF16)],
        scratch_shapes=[pltpu.VMEM((T + PAD_A, LANES), F32), pltpu.VMEM((T + PAD_B, LANES), F32),
                        pltpu.VMEM((T, LANES), F32)],
        compiler_params=_params(("parallel",)))(u, u, u, u, u, conf_dw, sconv_w)


def _convs_bwd(u, conf_dw, sconv_w, d_a1, d_s2, C):
    T = u.shape[0]
    nb = C // LANES

    def body(*refs):
        _convs_bwd_block(*refs, T)

    col = pl.BlockSpec((T, LANES), lambda i: (0, i))
    wa = pl.BlockSpec((CONF_KERNEL, LANES), lambda i: (0, i))
    wb = pl.BlockSpec((SCONV_KERNEL, LANES), lambda i: (0, i))
    act = jax.ShapeDtypeStruct((T, C), BF16)
    return pl.pallas_call(
        body, name="convs_bwd", grid=(nb,),
        in_specs=[_u_block(T, 0), _u_block(T, nb), _u_block(T, 2 * nb), _u_block(T, 3 * nb), _u_block(T, 4 * nb),
                  wa, wb, col, col],
        out_specs=[col, col, col, col, col, wa, wb],
        out_shape=[act, act, act, act, act, jax.ShapeDtypeStruct((CONF_KERNEL, C), F32),
                   jax.ShapeDtypeStruct((SCONV_KERNEL, C), F32)],
        scratch_shapes=[pltpu.VMEM((T + PAD_A, LANES), F32), pltpu.VMEM((T + PAD_A, LANES), F32),
                        pltpu.VMEM((T + PAD_B, LANES), F32), pltpu.VMEM((T + PAD_B, LANES), F32),
                        pltpu.VMEM((T, LANES), F32)],
        compiler_params=_params(("parallel",)))(u, u, u, u, u, conf_dw, sconv_w, d_a1, d_s2)


def _convs_bwd_block(av_ref, ag_ref, bg_ref, cg_ref, bh_ref, dw_ref, sw_ref, da1_ref, ds2_ref,
                     dav_ref, dag_ref, dbg_ref, dcg_ref, dbh_ref, ddw_ref, dsw_ref, xa, ga, xs, gs, tmp, T):
    def to_tmp(r0, acc):
        tmp[r0:r0 + CONV_ROWS, :] = acc

    av = av_ref[...].astype(F32)
    sg = _sigmoid(ag_ref[...].astype(F32))
    xa[0:PAD_A, :] = jnp.zeros((PAD_A, LANES), F32)
    xa[PAD_A:PAD_A + T, :] = av * sg
    ga[0:T, :] = da1_ref[...]
    ga[T:T + PAD_A, :] = jnp.zeros((PAD_A, LANES), F32)
    _conv_wgrad(xa, ga, ddw_ref, CONF_KERNEL, PAD_A, T)
    _anticausal_conv(ga, dw_ref, CONF_KERNEL, T, to_tmp)
    da0 = tmp[...]
    dav_ref[...] = (da0 * sg).astype(BF16)
    dag_ref[...] = (da0 * av * sg * (1.0 - sg)).astype(BF16)

    cg = cg_ref[...].astype(F32)
    bh = bh_ref[...].astype(F32)
    ds2 = ds2_ref[...].astype(F32)
    xs[0:PAD_B, :] = jnp.zeros((PAD_B, LANES), F32)
    xs[PAD_B:PAD_B + T, :] = cg * bh
    _causal_conv(xs, sw_ref, SCONV_KERNEL, PAD_B, T, to_tmp)
    dbg_ref[...] = (ds2 * tmp[...]).astype(BF16)
    gs[0:T, :] = ds2 * bg_ref[...].astype(F32)
    gs[T:T + PAD_B, :] = jnp.zeros((PAD_B, LANES), F32)
    _conv_wgrad(xs, gs, dsw_ref, SCONV_KERNEL, PAD_B, T)
    _anticausal_conv(gs, sw_ref, SCONV_KERNEL, T, to_tmp)
    ds0 = tmp[...]
    dcg_ref[...] = (ds0 * bh).astype(BF16)
    dbh_ref[...] = (ds0 * cg).astype(BF16)


def _attn_specs(T, C, q_off_blocks):
    kvw = N_KV_HEADS * HEAD_DIM
    kb = (5 * C + C) // kvw
    qs = pl.BlockSpec((ATTN_BLOCK, C), lambda n: (n, 5))
    kc = pl.BlockSpec((ATTN_BLOCK, kvw), lambda n: (n, kb))
    kp = pl.BlockSpec((ATTN_BLOCK, kvw), lambda n: (jnp.maximum(n - 1, 0), kb))
    vc = pl.BlockSpec((ATTN_BLOCK, kvw), lambda n: (n, kb + 1))
    vp = pl.BlockSpec((ATTN_BLOCK, kvw), lambda n: (jnp.maximum(n - 1, 0), kb + 1))
    return qs, kc, kp, vc, vp


def _attn_masks(n, rows):
    row = lax.broadcasted_iota(jnp.int32, (rows, ATTN_BLOCK), 0) % ATTN_BLOCK
    col = lax.broadcasted_iota(jnp.int32, (rows, ATTN_BLOCK), 1)
    from_cur = col <= row
    return from_cur, jnp.logical_or(from_cur, n > 0)


def _stack_heads(ref, heads, width=HEAD_DIM):
    return jnp.concatenate([ref[:, h * width:(h + 1) * width] for h in heads], axis=0)


def _unstack_heads(t, count):
    return [t[j * ATTN_BLOCK:(j + 1) * ATTN_BLOCK] for j in range(count)]


def _attn_scores(qh, kc, kp, from_cur, valid):
    qs = qh * (HEAD_DIM ** -0.5)
    s_c = lax.dot_general(qs, kc, NT, preferred_element_type=F32)
    s_p = lax.dot_general(qs, kp, NT, preferred_element_type=F32)
    return jnp.where(valid, jnp.where(from_cur, s_c, s_p), NEG_BIG)


def _attn_split(t, from_cur):
    t = t.astype(BF16)
    zero = jnp.zeros_like(t)
    return jnp.where(from_cur, t, zero), jnp.where(from_cur, zero, t)


def _attn_fwd(u, sinks, C):
    T = u.shape[0]
    H = C // HEAD_DIM
    grp = H // N_KV_HEADS

    def body(sink_ref, q_ref, kc_ref, kp_ref, vc_ref, vp_ref, o_ref, lse_ref):
        n = pl.program_id(0)
        from_cur, valid = _attn_masks(n, grp * ATTN_BLOCK)
        outs, lses = [], []
        for g in range(N_KV_HEADS):
            kv = slice(g * HEAD_DIM, (g + 1) * HEAD_DIM)
            heads = range(g * grp, (g + 1) * grp)
            sink = jnp.concatenate([jnp.full((ATTN_BLOCK, 1), sink_ref[h], F32) for h in heads], axis=0)
            s = _attn_scores(_stack_heads(q_ref, heads), kc_ref[:, kv], kp_ref[:, kv], from_cur, valid)
            m = jnp.maximum(jnp.max(s, axis=-1, keepdims=True), sink)
            p = jnp.exp(s - m)
            den = jnp.sum(p, axis=-1, keepdims=True) + jnp.exp(sink - m)
            p_c, p_p = _attn_split(p, from_cur)
            acc = jnp.dot(p_c, vc_ref[:, kv], preferred_element_type=F32)
            acc = acc + jnp.dot(p_p, vp_ref[:, kv], preferred_element_type=F32)
            outs += _unstack_heads((acc / den).astype(BF16), grp)
            lses += _unstack_heads(m + jnp.log(den), grp)
        o_ref[...] = jnp.concatenate(outs, axis=1)
        lse_ref[...] = jnp.concatenate(lses, axis=1)

    qs, kc, kp, vc, vp = _attn_specs(T, C, 5)
    return pl.pallas_call(
        body, name="attn_fwd", grid=(T // ATTN_BLOCK,),
        in_specs=[pl.BlockSpec(memory_space=pltpu.SMEM), qs, kc, kp, vc, vp],
        out_specs=[pl.BlockSpec((ATTN_BLOCK, C), lambda n: (n, 0)), pl.BlockSpec((ATTN_BLOCK, H), lambda n: (n, 0))],
        out_shape=[jax.ShapeDtypeStruct((T, C), BF16), jax.ShapeDtypeStruct((T, H), F32)],
        compiler_params=_params(("parallel",)))(sinks, u, u, u, u, u)


def _attn_bwd(u, o, lse, d_o, sinks, C):
    T = u.shape[0]
    H = C // HEAD_DIM
    grp = H // N_KV_HEADS
    kvw = N_KV_HEADS * HEAD_DIM
    nblk = T // ATTN_BLOCK
    scale = HEAD_DIM ** -0.5

    def body(sink_ref, q_ref, kc_ref, kp_ref, vc_ref, vp_ref, o_ref, lse_ref, do_ref,
             dq_ref, dk_ref, dv_ref, ds_ref, dk_acc, dv_acc):
        n = pl.program_id(0)

        @pl.when(n == 0)
        def _():
            dk_acc[...] = jnp.zeros_like(dk_acc)
            dv_acc[...] = jnp.zeros_like(dv_acc)
            ds_ref[...] = jnp.zeros_like(ds_ref)

        from_cur, valid = _attn_masks(n, grp * ATTN_BLOCK)
        cur = pl.ds(pl.multiple_of(n * ATTN_BLOCK, ATTN_BLOCK), ATTN_BLOCK)
        prev = pl.ds(pl.multiple_of(jnp.maximum(n - 1, 0) * ATTN_BLOCK, ATTN_BLOCK), ATTN_BLOCK)
        dqs, dsinks, dk_cs, dk_ps, dv_cs, dv_ps = [], [], [], [], [], []
        for g in range(N_KV_HEADS):
            kv = slice(g * HEAD_DIM, (g + 1) * HEAD_DIM)
            kc, kp, vc, vp = kc_ref[:, kv], kp_ref[:, kv], vc_ref[:, kv], vp_ref[:, kv]
            heads = range(g * grp, (g + 1) * grp)
            qg, dog, og = _stack_heads(q_ref, heads), _stack_heads(do_ref, heads), _stack_heads(o_ref, heads)
            lse_g = _stack_heads(lse_ref, heads, 1)
            sink = jnp.concatenate([jnp.full((ATTN_BLOCK, 1), sink_ref[h], F32) for h in heads], axis=0)
            p = jnp.exp(_attn_scores(qg, kc, kp, from_cur, valid) - lse_g)
            delta = jnp.sum(dog.astype(F32) * og.astype(F32), axis=-1, keepdims=True)
            dp = jnp.where(from_cur, lax.dot_general(dog, vc, NT, preferred_element_type=F32),
                           lax.dot_general(dog, vp, NT, preferred_element_type=F32))
            ds_c, ds_p = _attn_split(p * (dp - delta) * scale, from_cur)
            p_c, p_p = _attn_split(p, from_cur)
            dq = jnp.dot(ds_c, kc, preferred_element_type=F32) + jnp.dot(ds_p, kp, preferred_element_type=F32)
            dqs += _unstack_heads(dq.astype(BF16), grp)
            dk_cs.append(lax.dot_general(ds_c, qg, TN, preferred_element_type=F32))
            dk_ps.append(lax.dot_general(ds_p, qg, TN, preferred_element_type=F32))
            dv_cs.append(lax.dot_general(p_c, dog, TN, preferred_element_type=F32))
            dv_ps.append(lax.dot_general(p_p, dog, TN, preferred_element_type=F32))
            for t in _unstack_heads(jnp.exp(sink - lse_g) * delta, grp):
                dsinks.append(jnp.broadcast_to(-jnp.sum(t, axis=0, keepdims=True), (8, 1)))
        dq_ref[...] = jnp.concatenate(dqs, axis=1)
        ds_ref[...] += jnp.concatenate(dsinks, axis=1)
        dk_acc[cur, :] += jnp.concatenate(dk_cs, axis=1)
        dk_acc[prev, :] += jnp.concatenate(dk_ps, axis=1)
        dv_acc[cur, :] += jnp.concatenate(dv_cs, axis=1)
        dv_acc[prev, :] += jnp.concatenate(dv_ps, axis=1)

        @pl.when(n == nblk - 1)
        def _():
            dk_ref[...] = dk_acc[...].astype(BF16)
            dv_ref[...] = dv_acc[...].astype(BF16)

    qs, kc, kp, vc, vp = _attn_specs(T, C, 5)
    blk = pl.BlockSpec((ATTN_BLOCK, C), lambda n: (n, 0))
    full = pl.BlockSpec((T, kvw), lambda n: (0, 0))
    return pl.pallas_call(
        body, name="attn_bwd", grid=(nblk,),
        in_specs=[pl.BlockSpec(memory_space=pltpu.SMEM), qs, kc, kp, vc, vp, blk,
                  pl.BlockSpec((ATTN_BLOCK, H), lambda n: (n, 0)), blk],
        out_specs=[blk, full, full, pl.BlockSpec((8, H), lambda n: (0, 0))],
        out_shape=[jax.ShapeDtypeStruct((T, C), BF16), jax.ShapeDtypeStruct((T, kvw), BF16),
                   jax.ShapeDtypeStruct((T, kvw), BF16), jax.ShapeDtypeStruct((8, H), F32)],
        scratch_shapes=[pltpu.VMEM((T, kvw), F32), pltpu.VMEM((T, kvw), F32)],
        compiler_params=_params(("arbitrary",)))(sinks, u, u, u, u, u, o, lse, d_o)


MERGE_COLS = 512


def _merge_specs(T, D, I):
    tr = _pick(T, ROW_TILE, 8)
    tc = _pick(D, MERGE_COLS)
    g0 = (I - N_BRANCH * D) // tc
    per = D // tc
    gspecs = [pl.BlockSpec((tr, tc), functools.partial(lambda j, i, b: (i, g0 + b * per + j), b=b)) for b in range(N_BRANCH)]
    tile = pl.BlockSpec((tr, tc), lambda j, i: (i, j))
    bias = pl.BlockSpec((N_BRANCH, tc), lambda j, i: (0, j))
    return tr, tc, gspecs, tile, bias


def _merge_fwd(u, gate_b, ya, yb, yc):
    T, I = u.shape
    D = ya.shape[1]
    tr, tc, gspecs, tile, bias = _merge_specs(T, D, I)

    def body(g0_ref, g1_ref, g2_ref, b_ref, ya_ref, yb_ref, yc_ref, o_ref):
        acc = None
        for b, (g_ref, y_ref) in enumerate(((g0_ref, ya_ref), (g1_ref, yb_ref), (g2_ref, yc_ref))):
            gate = _sigmoid(g_ref[...].astype(F32) + b_ref[b:b + 1, :])
            term = gate * y_ref[...].astype(F32)
            acc = term if acc is None else acc + term
        o_ref[...] = acc.astype(BF16)

    return pl.pallas_call(
        body, name="merge_fwd", grid=(D // tc, T // tr), in_specs=[*gspecs, bias, tile, tile, tile], out_specs=tile,
        out_shape=jax.ShapeDtypeStruct((T, D), BF16),
        compiler_params=_params(("parallel", "parallel")))(u, u, u, gate_b, ya, yb, yc)


def _merge_bwd(u, gate_b, ya, yb, yc, dm):
    T, I = u.shape
    D = ya.shape[1]
    tr, tc, gspecs, tile, bias = _merge_specs(T, D, I)

    def body(g0_ref, g1_ref, g2_ref, b_ref, ya_ref, yb_ref, yc_ref, dm_ref,
             dya_ref, dyb_ref, dyc_ref, dg0_ref, dg1_ref, dg2_ref, db_ref):
        dmv = dm_ref[...].astype(F32)
        first = pl.program_id(1) == 0
        for b, (g_ref, y_ref, dy_ref, dg_ref) in enumerate(((g0_ref, ya_ref, dya_ref, dg0_ref),
                                                           (g1_ref, yb_ref, dyb_ref, dg1_ref),
                                                           (g2_ref, yc_ref, dyc_ref, dg2_ref))):
            gate = _sigmoid(g_ref[...].astype(F32) + b_ref[b:b + 1, :])
            dy_ref[...] = (dmv * gate).astype(BF16)
            dpre = dmv * y_ref[...].astype(F32) * gate * (1.0 - gate)
            dg_ref[...] = dpre.astype(BF16)
            part = jnp.sum(dpre, axis=0, keepdims=True)

            @pl.when(first)
            def _():
                db_ref[b:b + 1, :] = part

            @pl.when(jnp.logical_not(first))
            def _():
                db_ref[b:b + 1, :] += part

    act = jax.ShapeDtypeStruct((T, D), BF16)
    return pl.pallas_call(
        body, name="merge_bwd", grid=(D // tc, T // tr), in_specs=[*gspecs, bias, tile, tile, tile, tile],
        out_specs=[tile] * 6 + [bias], out_shape=[act] * 6 + [jax.ShapeDtypeStruct((N_BRANCH, D), F32)],
        compiler_params=_params(("parallel", "arbitrary")))(u, u, u, gate_b, ya, yb, yc, dm)


ELEMS_PER_TILE = 512 * 1024


def _row_tile(r, c):
    return _pick(r, max(16, ELEMS_PER_TILE // c), 16) if r % 16 == 0 else r


def _cast_place(w, layer, my_block):
    L, r, c = w.shape
    tr = _row_tile(r, c)

    def body(blk_ref, w_ref, o_ref):
        o_ref[...] = w_ref[...].astype(BF16)

    return pl.pallas_call(
        body, name="cast_place",
        grid_spec=pltpu.PrefetchScalarGridSpec(
            num_scalar_prefetch=1, grid=(r // tr,),
            in_specs=[pl.BlockSpec((None, tr, c), lambda i, blk: (layer, i, 0))],
            out_specs=pl.BlockSpec((None, tr, c), lambda i, blk: (blk[0], i, 0))),
        out_shape=jax.ShapeDtypeStruct((N_DEV, r, c), BF16), compiler_params=_params(("parallel",)))(my_block, w)


def _adamw_math(w, g, m, v):
    m = ADAM_B1 * m + (1.0 - ADAM_B1) * g
    v = ADAM_B2 * v + (1.0 - ADAM_B2) * (g * g)
    m_hat = m / (1.0 - ADAM_B1 ** ADAM_STEP)
    v_hat = v / (1.0 - ADAM_B2 ** ADAM_STEP)
    delta = -ADAM_LR * (m_hat / (jnp.sqrt(v_hat) + ADAM_EPS) + ADAM_WD * w)
    return delta, m, v


def _sum_parts(part_ref):
    acc = part_ref[0].astype(F32)
    for s in range(1, N_DEV):
        acc = acc + part_ref[s].astype(F32)
    return acc


def _sum8(parts):
    L, _, r, c = parts.shape
    tr = _row_tile(r, c)

    def body(p_ref, o_ref):
        o_ref[...] = _sum_parts(p_ref)

    return pl.pallas_call(
        body, name="sum8", grid=(L, r // tr),
        in_specs=[pl.BlockSpec((None, N_DEV, tr, c), lambda l, i: (l, 0, i, 0))],
        out_specs=pl.BlockSpec((None, tr, c), lambda l, i: (l, i, 0)),
        out_shape=jax.ShapeDtypeStruct((L, r, c), F32), compiler_params=_params(("parallel", "parallel")))(parts)


def _adamw(w, g, m, v):
    L, r, c = w.shape
    tr = _row_tile(r, c)
    spec = pl.BlockSpec((None, tr, c), lambda l, i: (l, i, 0))

    def body(w_ref, g_ref, m_ref, v_ref, d_ref, mo_ref, vo_ref):
        d, mn, vn = _adamw_math(w_ref[...], g_ref[...], m_ref[...], v_ref[...])
        d_ref[...] = d
        mo_ref[...] = mn
        vo_ref[...] = vn

    shp = jax.ShapeDtypeStruct(w.shape, F32)
    return pl.pallas_call(body, name="adamw", grid=(L, r // tr), in_specs=[spec] * 4, out_specs=[spec] * 3,
                          out_shape=[shp] * 3, compiler_params=_params(("parallel", "parallel")))(w, g, m, v)


N_CHIPS = 4
CHIP_XOR = (0, 2, 1, 3)


def _chip_sum(part4, sib4, own_all, layer, pos):
    _, _, r, c = part4.shape
    tr = _row_tile(r, c)

    def chip(p, s):
        return jnp.bitwise_xor(2 * p[0] + p[1], CHIP_XOR[s])

    mine = [pl.BlockSpec((None, None, tr, c), functools.partial(lambda i, p, s: (chip(p, s), p[2], i, 0), s=s))
            for s in range(N_CHIPS)]
    theirs = [pl.BlockSpec((None, None, tr, c), functools.partial(lambda i, p, s: (chip(p, s), 0, i, 0), s=s))
              for s in range(N_CHIPS)]

    def body(pos_ref, *refs):
        a, b = refs[:N_CHIPS], refs[N_CHIPS:2 * N_CHIPS]
        out_ref, own_ref = refs[2 * N_CHIPS + 1], refs[2 * N_CHIPS + 2]
        own_ref[...] = a[0][...].astype(F32) + b[0][...].astype(F32)
        for s in range(1, N_CHIPS):
            out_ref[s - 1] = (a[s][...].astype(F32) + b[s][...].astype(F32)).astype(BF16)

    return pl.pallas_call(
        body, name="chip_sum",
        grid_spec=pltpu.PrefetchScalarGridSpec(
            num_scalar_prefetch=1, grid=(r // tr,),
            in_specs=[*mine, *theirs, pl.BlockSpec(memory_space=pl.ANY)],
            out_specs=[pl.BlockSpec((N_CHIPS - 1, tr, c), lambda i, p: (0, i, 0)),
                       pl.BlockSpec((None, tr, c), lambda i, p: (layer, i, 0))]),
        out_shape=[jax.ShapeDtypeStruct((N_CHIPS - 1, r, c), BF16), jax.ShapeDtypeStruct(own_all.shape, F32)],
        input_output_aliases={1 + 2 * N_CHIPS: 1},
        compiler_params=_params(("parallel",)))(pos, *([part4] * N_CHIPS), *([sib4] * N_CHIPS), own_all)


def _sum_chips(own_ref, got_ref):
    acc = own_ref[...]
    for s in range(N_CHIPS - 1):
        acc = acc + got_ref[s].astype(F32)
    return acc


def _sum4_adamw(own, got, w, m, v, first, count, earlier=None):
    L, r, c = w.shape
    tr = _row_tile(r, c)
    spec = pl.BlockSpec((None, tr, c), lambda l, i: (first + l, i, 0))
    earlier = [] if earlier is None else list(earlier)

    def body(own_ref, got_ref, w_ref, m_ref, v_ref, *rest):
        g_ref, d_ref, mo_ref, vo_ref = rest[len(earlier):]
        g = _sum_chips(own_ref, got_ref)
        d, mn, vn = _adamw_math(w_ref[...], g, m_ref[...], v_ref[...])
        g_ref[...] = g
        d_ref[...] = d
        mo_ref[...] = mn
        vo_ref[...] = vn

    shp = jax.ShapeDtypeStruct(w.shape, F32)
    return pl.pallas_call(
        body, name="sum4_adamw", grid=(count, r // tr),
        in_specs=[spec, pl.BlockSpec((None, N_CHIPS - 1, tr, c), lambda l, i: (first + l, 0, i, 0)), spec, spec, spec,
                  *([ANY] * len(earlier))],
        out_specs=[spec] * 4, out_shape=[shp] * 4, input_output_aliases={5 + a: a for a in range(len(earlier))},
        compiler_params=_params(("parallel", "parallel")))(own, got, w, m, v, *earlier)


def _me():
    return lax.axis_index("x"), lax.axis_index("y"), lax.axis_index("c")


def _flip(pos, k):
    x, y, c = pos
    return (1 - x if k & 4 else x, 1 - y if k & 2 else y, 1 - c if k & 1 else c)


def _block_of(pos):
    return 4 * pos[0] + 2 * pos[1] + pos[2]


ANY = pl.BlockSpec(memory_space=pl.ANY)
SIBLING = 1
OTHER_CHIPS = (4, 2, 6)


def _all_gather(name, arrays, after=None):
    n = len(arrays)
    shapes = [a.shape[-2:] for a in arrays]
    deps = [] if after is None else [after]

    def body(*refs):
        srcs, outs = refs[:n], refs[n + len(deps):2 * n + len(deps)]
        send_sems, recv_sems, local_sems = refs[2 * n + len(deps):]
        me = _me()
        sib = _flip(me, SIBLING)

        def src_of(a):
            return srcs[a]

        def copy(a, k, block_pos, to, src=None):
            dst = outs[a].at[_block_of(block_pos)]
            return pltpu.make_async_remote_copy(
                src_ref=dst if src is None else src, dst_ref=dst, send_sem=send_sems.at[a, k],
                recv_sem=recv_sems.at[a, k], device_id=to, device_id_type=MESH)

        mine = [pltpu.make_async_copy(src_of(a), outs[a].at[_block_of(me)], local_sems.at[a]) for a in range(n)]
        for cp in mine:
            cp.start()
        first = []
        for a in range(n):
            first.append(copy(a, 0, me, sib, src=src_of(a)))
            for j, k in enumerate(OTHER_CHIPS):
                first.append(copy(a, 1 + j, me, _flip(me, k), src=src_of(a)))
        for cp in first:
            cp.start()
        passed = []
        for j, k in enumerate(OTHER_CHIPS):
            for a in range(n):
                copy(a, 1 + j, _flip(me, k), me).wait_recv()
                fw = copy(a, 4 + j, _flip(me, k), sib)
                fw.start()
                passed.append(fw)
        for a in range(n):
            copy(a, 0, sib, me).wait_recv()
            for j, k in enumerate(OTHER_CHIPS):
                copy(a, 4 + j, _flip(sib, k), me).wait_recv()
        for cp in first + passed:
            cp.wait_send()
        for cp in mine:
            cp.wait()

    return pl.pallas_call(
        body, name=name, in_specs=[ANY] * (n + len(deps)), out_specs=[ANY] * n,
        out_shape=[jax.ShapeDtypeStruct((N_DEV, *s), a.dtype) for s, a in zip(shapes, arrays)],
        scratch_shapes=[pltpu.SemaphoreType.DMA((n, 7)), pltpu.SemaphoreType.DMA((n, 7)), pltpu.SemaphoreType.DMA((n,))],
    )(*arrays, *deps)


HBM = pl.BlockSpec(memory_space=pltpu.HBM)
SEM = pl.BlockSpec(memory_space=pltpu.SEMAPHORE)
DATAFLOW = pltpu.SideEffectType.DATAFLOW_SIDE_EFFECTING
X_NEIGHBOUR, Y_NEIGHBOUR = 4, 2
NEAR = (SIBLING, X_NEIGHBOUR, Y_NEIGHBOUR)


def _in_hbm(a):
    return pltpu.with_memory_space_constraint(a, pltpu.HBM)


def _hbm_like(a):
    return pltpu.HBM(a.shape, a.dtype)


def _copies_start(name, srcs, lands, plan, n_copies, after=None):
    ns, n = len(srcs), len(lands)
    deps = [] if after is None else [after]

    def body(*refs):
        k0 = ns + n + len(deps)
        send_sems, recv_sems, token = refs[k0], refs[k0 + 1], refs[k0 + 2 + n]
        for s, (src, dst, peer, _) in enumerate(plan(_me(), refs[:ns], refs[ns:ns + n])):
            pltpu.make_async_remote_copy(src_ref=src, dst_ref=dst, send_sem=send_sems.at[s], recv_sem=recv_sems.at[s],
                                         device_id=peer, device_id_type=MESH).start()
        token[...] = jnp.zeros(TOKEN, F32)

    out = pl.pallas_call(
        body, name=name, in_specs=[*([HBM] * (ns + n)), *([ANY] * len(deps))],
        out_specs=[SEM, SEM, *([HBM] * n), pl.BlockSpec(memory_space=pltpu.VMEM)],
        out_shape=[pltpu.SemaphoreType.DMA((n_copies,)), pltpu.SemaphoreType.DMA((n_copies,)),
                   *[_hbm_like(a) for a in lands], jax.ShapeDtypeStruct(TOKEN, F32)],
        input_output_aliases={ns + a: 2 + a for a in range(n)},
        compiler_params=pltpu.CompilerParams(has_side_effects=DATAFLOW),
    )(*[_in_hbm(a) for a in srcs], *[_in_hbm(a) for a in lands], *deps)
    return dict(send=out[0], recv=out[1], srcs=list(srcs), plan=plan), list(out[2:2 + n]), out[2 + n]


def _copies_wait(name, flight, lands, after):
    srcs, plan = flight["srcs"], flight["plan"]
    ns, n = len(srcs), len(lands)
    after = list(after) if isinstance(after, (list, tuple)) else [after]

    def body(*refs):
        send_sems, recv_sems = refs[ns + n], refs[ns + n + 1]
        for s, (src, dst, peer, landing) in enumerate(plan(_me(), refs[:ns], refs[ns:ns + n])):
            pltpu.make_async_remote_copy(src_ref=src, dst_ref=dst, send_sem=send_sems.at[s], recv_sem=recv_sems.at[s],
                                         device_id=peer, device_id_type=MESH).wait_send()
            pltpu.make_async_remote_copy(src_ref=landing, dst_ref=landing, send_sem=send_sems.at[s],
                                         recv_sem=recv_sems.at[s], device_id=peer, device_id_type=MESH).wait_recv()

    out = pl.pallas_call(
        body, name=name, in_specs=[*([HBM] * (ns + n)), SEM, SEM, *([ANY] * len(after))], out_specs=[HBM] * n,
        out_shape=[_hbm_like(a) for a in lands], input_output_aliases={ns + a: a for a in range(n)},
        compiler_params=pltpu.CompilerParams(has_side_effects=DATAFLOW),
    )(*srcs, *lands, flight["send"], flight["recv"], *after)
    return list(out)


def _gather_plan_near(me, srcs, lands):
    plan = []
    for land in lands:
        own = land.at[_block_of(me)]
        for k in NEAR:
            peer = _flip(me, k)
            plan.append((own, own, peer, land.at[_block_of(peer)]))
    return plan


def _gather_plan_far(me, srcs, lands):
    x_nbr, y_nbr, far = _flip(me, X_NEIGHBOUR), _flip(me, Y_NEIGHBOUR), _flip(me, X_NEIGHBOUR | Y_NEIGHBOUR)
    plan = []
    for land in lands:
        half = land.shape[1] // 2
        first, second = pl.ds(0, half), pl.ds(half, half)
        passed = land.at[_block_of(y_nbr), first]
        plan.append((passed, passed, x_nbr, land.at[_block_of(far), first]))
        passed = land.at[_block_of(x_nbr), second]
        plan.append((passed, passed, y_nbr, land.at[_block_of(far), second]))
    return plan


def _broadcast_plan(me, srcs, lands):
    plan = []
    for land in lands:
        own = land.at[_block_of(me)]
        for k in range(1, N_DEV):
            peer = _flip(me, k)
            plan.append((own, own, peer, land.at[_block_of(peer)]))
    return plan


def _sibling_plan(me, srcs, lands):
    sib = _flip(me, SIBLING)
    return [(src.at[:, pl.ds(1 - me[2], 1)], land, sib, land) for src, land in zip(srcs, lands)]


def _scatter_plan(layer):
    def plan(me, srcs, lands):
        out = []
        for src, land in zip(srcs, lands):
            for j, k in enumerate(OTHER_CHIPS):
                out.append((src.at[j], land.at[layer, j], _flip(me, k), land.at[layer, j]))
        return out
    return plan


def _pass_on_plan(relations):
    def plan(me, srcs, lands):
        sib = _flip(me, SIBLING)
        out = []
        for land in lands:
            for k in relations:
                blk = land.at[_block_of(_flip(me, k))]
                out.append((blk, blk, sib, land.at[_block_of(_flip(sib, k))]))
        return out
    return plan


def _after(small, tokens):
    for t in tokens:
        small = small + t[0:1, 0:1]
    return small


def _layer_fwd(xc, l, W, P, dims, deps=(), mid_layer=None):
    T, D, C, I, F = dims
    h = _rms_fwd(xc, _after(P["norm_mix_g"][l:l + 1], deps))
    u = _mm_plain("mm_u", h, W["w_in"][l].reshape(I, D), "NT", [BF16], tn=1280)[0]
    a1, s2 = _convs_fwd(u, P["conf_dw"][l], P["sconv_w"][l], C)
    a3 = _ln_silu(a1, P["conf_ln_g"][l:l + 1], P["conf_ln_b"][l:l + 1])
    o, lse = _attn_fwd(u, P["sinks"][l], C)
    ya = _mm_branch("mm_branch_out", a3, W["w_conf_out"][l], "NN", BF16)
    yb = _mm_branch("mm_branch_out", s2, W["w_sconv_out"][l], "NN", BF16)
    yc = _mm_branch("mm_branch_out", o, W["w_attn_out"][l], "NN", BF16)
    merged = _merge_fwd(u, P["gate_b"][l], ya, yb, yc)
    x1 = _mm_plain("mm_mix", merged, W["w_mix_out"][l].reshape(D, D), "NN", [F32], _epi_resid, [xc], tk=2048)[0]
    norm_ffn_g = P["norm_ffn_g"][l:l + 1]
    h2 = _rms_fwd(x1, norm_ffn_g if mid_layer is None else _after(norm_ffn_g, [mid_layer(x1)]))
    up, act = _mm_nn_colblocked("mm_up", h2, W["w_up"][l], [BF16, BF16], _epi_relu2)
    x2 = _mm_plain("mm_down", act, W["w_down"][l].reshape(F, D), "NN", [F32], _epi_resid, [x1], tk=2048)[0]
    saved = dict(xc=xc, h=h, u=u, a1=a1, a3=a3, s2=s2, o=o, lse=lse, ya=ya, yb=yb, yc=yc, merged=merged, x1=x1, h2=h2,
                 up=up, act=act)
    return x2, saved


def _bwd_mlp(dx2, dx2_b, l, W, P, S, dims, dep=None):
    T, D, C, I, F = dims
    d_up = _mm_plain("mm_d_up", dx2_b, W["w_down"][l].reshape(F, D), "NT", [BF16], _epi_drelu2, [S["up"]], dep=dep)[0]
    g_down = _mm_plain("mm_g_down", S["act"], dx2_b, "TN", [BF16])[0]
    dh2 = _mm_nt_colblocked("mm_d_h2", d_up, W["w_up"][l], F32)
    g_up = _mm_tn_colblocked_out("mm_g_up", S["h2"], d_up, BF16)
    dx1, dx1_b, dg_ffn = _rms_bwd(dh2, S["x1"], P["norm_ffn_g"][l:l + 1], dx2)
    return dx1, dx1_b, dict(w_up=g_up, w_down=g_down.reshape(N_DEV, F // N_DEV, D)), dg_ffn


def _bwd_mix_out(dx1_b, l, W, P, S, dims, dep=None):
    T, D, C, I, F = dims
    dm = _mm_plain("mm_d_merged", dx1_b, W["w_mix_out"][l].reshape(D, D), "NT", [BF16], dep=dep)[0]
    g_mix = _mm_plain("mm_g_mix", S["merged"], dx1_b, "TN", [BF16])[0]
    merge = _merge_bwd(S["u"], P["gate_b"][l], S["ya"], S["yb"], S["yc"], dm)
    return g_mix.reshape(N_DEV, D // N_DEV, D), merge


def _bwd_mixers(dx1, merge, dg_ffn, g_mix, l, W, P, S, dims, dep, send_mid, send_in):
    T, D, C, I, F = dims
    d_ya, d_yb, d_yc, dg0, dg1, dg2, d_gate_b = merge
    d_a3 = _mm_branch("mm_d_branch", d_ya, W["w_conf_out"][l], "NT", BF16, dep=dep)
    d_s2 = _mm_branch("mm_d_branch", d_yb, W["w_sconv_out"][l], "NT", BF16)
    d_o = _mm_branch("mm_d_branch", d_yc, W["w_attn_out"][l], "NT", BF16)
    g_conf = _mm_branch_grad("mm_g_branch", S["a3"], d_ya, BF16)
    g_sconv = _mm_branch_grad("mm_g_branch", S["s2"], d_yb, BF16)
    g_attn = _mm_branch_grad("mm_g_branch", S["o"], d_yc, BF16)
    tok = send_mid(dict(w_mix_out=g_mix, w_conf_out=g_conf, w_sconv_out=g_sconv, w_attn_out=g_attn), g_attn)
    d_a1, d_ln_g, d_ln_b = _ln_silu_bwd(S["a1"], _after(P["conf_ln_g"][l:l + 1], [tok]), P["conf_ln_b"][l:l + 1], d_a3)
    d_av, d_ag, d_bg, d_cg, d_bh, d_conf_dw, d_sconv_w = _convs_bwd(S["u"], P["conf_dw"][l], P["sconv_w"][l], d_a1, d_s2, C)
    dq, dk, dv, d_sinks = _attn_bwd(S["u"], S["o"], S["lse"], d_o, P["sinks"][l], C)
    du = jnp.concatenate([d_av, d_ag, d_bg, d_cg, d_bh, dq, dk, dv, dg0, dg1, dg2], axis=1)
    g_in = _mm_plain("mm_g_in", du, S["h"], "TN", [BF16], tm=1280)[0]
    tok = send_in(dict(w_in=g_in.reshape(N_DEV, I // N_DEV, D)), g_in)
    dh = _mm_plain("mm_d_h", du, W["w_in"][l].reshape(I, D), "NN", [F32], tk=2560, dep=tok)[0]
    dx, dx_b, dg_mix = _rms_bwd(dh, S["xc"], P["norm_mix_g"][l:l + 1], dx1)
    small = dict(norm_mix_g=dg_mix, gate_b=d_gate_b, conf_dw=d_conf_dw, conf_ln_g=d_ln_g, conf_ln_b=d_ln_b,
                 sconv_w=d_sconv_w, sinks=d_sinks[0:1], norm_ffn_g=dg_ffn)
    return dx, dx_b, small


BIG = ("w_in", "w_conf_out", "w_sconv_out", "w_attn_out", "w_mix_out", "w_up", "w_down")
MLP_WEIGHTS = ("w_down", "w_up")
MID_WEIGHTS = ("w_mix_out", "w_conf_out", "w_sconv_out", "w_attn_out")
IN_WEIGHTS = ("w_in",)
SMALL_PER_LAYER = ("norm_mix_g", "gate_b", "conf_dw", "conf_ln_g", "conf_ln_b", "sconv_w", "sinks", "norm_ffn_g")
WEIGHTS = ("norm_mix_g", "w_in", "gate_b", "conf_dw", "conf_ln_g", "conf_ln_b", "w_conf_out", "sconv_w", "w_sconv_out",
           "sinks", "w_attn_out", "w_mix_out", "norm_ffn_g", "w_up", "w_down", "final_g")


SUBLANES = 8


def _nrows(n_el, width):
    per_tile = SUBLANES * width
    return SUBLANES * (-(-n_el // per_tile))


def _rows(a, width):
    flat = a.reshape(-1)
    nrow = _nrows(flat.shape[0], width)
    return jnp.pad(flat, (0, nrow * width - flat.shape[0])).reshape(nrow, width)


def _as3d(a):
    if a.ndim == 1:
        return a.reshape(1, 1, -1)
    if a.ndim == 2:
        return a.reshape(1, *a.shape)
    return a


def kernel(x, norm_mix_g, w_in, gate_b, conf_dw, conf_ln_g, conf_ln_b, w_conf_out, sconv_w, w_sconv_out, sinks, w_attn_out, w_mix_out, norm_ffn_g, w_up, w_down, final_g, loss_target, m_norm_mix_g, m_w_in, m_gate_b, m_conf_dw, m_conf_ln_g, m_conf_ln_b, m_w_conf_out, m_sconv_w, m_w_sconv_out, m_sinks, m_w_attn_out, m_w_mix_out, m_norm_ffn_g, m_w_up, m_w_down, m_final_g, v_norm_mix_g, v_w_in, v_gate_b, v_conf_dw, v_conf_ln_g, v_conf_ln_b, v_w_conf_out, v_sconv_w, v_w_sconv_out, v_sinks, v_w_attn_out, v_w_mix_out, v_norm_ffn_g, v_w_up, v_w_down, v_final_g):
    w = dict(norm_mix_g=norm_mix_g, w_in=w_in, gate_b=gate_b, conf_dw=conf_dw, conf_ln_g=conf_ln_g, conf_ln_b=conf_ln_b,
             w_conf_out=w_conf_out, sconv_w=sconv_w, w_sconv_out=w_sconv_out, sinks=sinks, w_attn_out=w_attn_out,
             w_mix_out=w_mix_out, norm_ffn_g=norm_ffn_g, w_up=w_up, w_down=w_down, final_g=final_g)
    mom = dict(norm_mix_g=m_norm_mix_g, w_in=m_w_in, gate_b=m_gate_b, conf_dw=m_conf_dw, conf_ln_g=m_conf_ln_g,
               conf_ln_b=m_conf_ln_b, w_conf_out=m_w_conf_out, sconv_w=m_sconv_w, w_sconv_out=m_w_sconv_out,
               sinks=m_sinks, w_attn_out=m_w_attn_out, w_mix_out=m_w_mix_out, norm_ffn_g=m_norm_ffn_g, w_up=m_w_up,
               w_down=m_w_down, final_g=m_final_g)
    var = dict(norm_mix_g=v_norm_mix_g, w_in=v_w_in, gate_b=v_gate_b, conf_dw=v_conf_dw, conf_ln_g=v_conf_ln_g,
               conf_ln_b=v_conf_ln_b, w_conf_out=v_w_conf_out, sconv_w=v_sconv_w, w_sconv_out=v_w_sconv_out,
               sinks=v_sinks, w_attn_out=v_w_attn_out, w_mix_out=v_w_mix_out, norm_ffn_g=v_norm_ffn_g, w_up=v_w_up,
               w_down=v_w_down, final_g=v_final_g)

    _, T, D = x.shape
    L = w_in.shape[0]
    C = D // 2
    I = w_in.shape[2] * N_DEV
    F = w_up.shape[2] * N_DEV
    dims = (T, D, C, I, F)
    my_block = _block_of(_me())

    w["w_in"], mom["w_in"], var["w_in"] = (jnp.swapaxes(a, 1, 2) for a in (w_in, m_w_in, v_w_in))

    shard_names = ("gate_b", "conf_dw", "sconv_w")
    packed = jnp.concatenate([_rows(w[k], LANES) for k in shard_names], axis=0)
    gathered = _all_gather("gather_small", [packed])[0]

    pos = jnp.stack(_me()).astype(jnp.int32)
    blk = my_block.reshape(1).astype(jnp.int32)
    W = {k: [_cast_place(w[k], l, blk) for l in range(L)] for k in BIG}
    n_near, n_far = len(BIG) * len(NEAR), len(BIG) * 2

    def gather_near(l, after):
        return _copies_start(f"gather_near_start_{l}", [], [W[k][l] for k in BIG], _gather_plan_near, n_near, after)

    def gather_far(l, g, after):
        lands = _copies_wait(f"gather_near_wait_{l}", g["near"], g["lands"], after)
        g["far"], lands, tok_far = _copies_start(f"gather_far_start_{l}", [], lands, _gather_plan_far, n_far)
        g["pass_near"], g["lands"], tok_pass = _copies_start(
            f"pass_near_start_{l}", [], lands, _pass_on_plan((X_NEIGHBOUR, Y_NEIGHBOUR)), 2 * len(BIG))
        return tok_far + tok_pass

    def gather_finish(l, g, after):
        lands = _copies_wait(f"gather_far_wait_{l}", g["far"], g["lands"], after)
        pass_far, lands, _ = _copies_start(f"pass_far_start_{l}", [], lands,
                                           _pass_on_plan((X_NEIGHBOUR | Y_NEIGHBOUR,)), len(BIG))
        lands = _copies_wait(f"pass_near_wait_{l}", g["pass_near"], lands, after)
        return _copies_wait(f"pass_far_wait_{l}", pass_far, lands, after)

    gathering = {}
    gathering["near"], gathering["lands"], tok = gather_near(0, gathered)
    later_layers = [W[k][l] for l in range(1, L) for k in BIG]
    tok = gather_far(0, gathering, [tok, *later_layers])

    P = dict(norm_mix_g=norm_mix_g, conf_ln_g=conf_ln_g, conf_ln_b=conf_ln_b, sinks=sinks, norm_ffn_g=norm_ffn_g)
    row0 = 0
    for k in shard_names:
        n_el = w[k].size
        nrow = _nrows(n_el, LANES)
        part = gathered[:, row0:row0 + nrow].reshape(N_DEV, -1)[:, :n_el].reshape(N_DEV, *w[k].shape)
        P[k] = jnp.moveaxis(part, 0, 2).reshape(*w[k].shape[:2], N_DEV * w[k].shape[2])
        row0 += nrow

    xc = x.reshape(T, D)
    saved = []
    for l in range(L):
        lands = gather_finish(l, gathering, xc if l else tok)
        for k, g in zip(BIG, lands):
            W[k][l] = g
        deps, mid_layer = (), None
        if l + 1 < L:
            gathering = {}
            gathering["near"], gathering["lands"], tok = gather_near(l + 1, lands[0])
            deps = (tok,)
            mid_layer = functools.partial(gather_far, l + 1, gathering)
        xc, S = _layer_fwd(xc, l, W, P, dims, deps, mid_layer)
        saved.append(S)
    dx, dx_b, d_final_g, loss_tile = _loss_head(xc, final_g.reshape(1, D), loss_target.reshape(T, D))

    own_all = {k: lax.empty((L, *W[k][0].shape[1:]), F32) for k in BIG}
    recv = {k: lax.empty((L, N_CHIPS - 1, *W[k][0].shape[1:]), BF16) for k in BIG}
    scatters = []

    def to_sibling(names, grads, l, after):
        part4 = [grads[k].reshape(N_CHIPS, 2, *grads[k].shape[1:]) for k in names]
        zone = [lax.empty((N_CHIPS, 1, *p.shape[2:]), BF16) for p in part4]
        fl, zone, token = _copies_start(f"sibling_start_{l}_{names[0]}", part4, zone, _sibling_plan, len(names), after)
        return dict(names=names, l=l, part4=part4, flight=fl, zone=zone), token

    def to_owners(group, after):
        names, l = group["names"], group["l"]
        sib4 = _copies_wait(f"sibling_wait_{l}_{names[0]}", group["flight"], group["zone"], after)
        chip_parts = []
        for k, p4, s4 in zip(names, group["part4"], sib4):
            cp, own_all[k] = _chip_sum(p4, s4, own_all[k], l, pos)
            chip_parts.append(cp)
        fl, zone, token = _copies_start(f"scatter_start_{l}_{names[0]}", chip_parts, [recv[k] for k in names],
                                        _scatter_plan(l), len(names) * len(OTHER_CHIPS))
        for k, g in zip(names, zone):
            recv[k] = g
        scatters.append((f"scatter_wait_{l}_{names[0]}", fl, names, l))
        return token

    small_grads = [None] * L
    dep, groups = None, {}
    for l in reversed(range(L)):
        S = saved[l]
        dx1, dx1_b, g_mlp, dg_ffn = _bwd_mlp(dx, dx_b, l, W, P, S, dims, dep)
        groups["mlp"], dep = to_sibling(MLP_WEIGHTS, g_mlp, l, dx1)
        if "in" in groups:
            dep = dep + to_owners(groups["in"], dx1)
        g_mix, merge = _bwd_mix_out(dx1_b, l, W, P, S, dims, dep)
        dep = to_owners(groups["mlp"], merge[0])

        def send_mid(grads, after, l=l):
            groups["mid"], token = to_sibling(MID_WEIGHTS, grads, l, after)
            return token

        def send_in(grads, after, l=l):
            token = to_owners(groups["mid"], after)
            groups["in"], token2 = to_sibling(IN_WEIGHTS, grads, l, after)
            return token + token2

        dx, dx_b, small_grads[l] = _bwd_mixers(dx1, merge, dg_ffn, g_mix, l, W, P, S, dims, dep, send_mid, send_in)
    last_start = to_owners(groups["in"], dx)

    width = C
    pieces = [_rows(small_grads[l][k], width) for l in range(L) for k in SMALL_PER_LAYER]
    pieces += [_rows(d_final_g, width), _rows(loss_tile[0:1, 0:1], width)]
    partial = jnp.concatenate(pieces, axis=0)
    everyone = lax.dynamic_update_slice(lax.empty((N_DEV, *partial.shape), F32), partial[None], (my_block, 0, 0))
    small_flight, (everyone,), token = _copies_start("small_grads_start", [], [everyone], _broadcast_plan, N_DEV - 1,
                                                     after=last_start)
    last_start = last_start + token

    def await_scatters(layers):
        for name, fl, names, l in scatters:
            if l in layers:
                for k, g in zip(names, _copies_wait(name, fl, [recv[k] for k in names], [dx, last_start])):
                    recv[k] = g

    await_scatters(range(1, L))
    done = {k: _sum4_adamw(own_all[k], recv[k], w[k], mom[k], var[k], 1, L - 1) for k in BIG} if L > 1 else {}
    await_scatters([0])

    everyone = _copies_wait("small_grads_wait", small_flight, [everyone], recv[BIG[0]])[0]
    total = _sum8(everyone.reshape(1, *everyone.shape))[0]
    grads = {}
    row0 = 0
    per_layer = {k: [] for k in SMALL_PER_LAYER}
    for l in range(L):
        for k in SMALL_PER_LAYER:
            shape = small_grads[l][k].shape
            n_el = small_grads[l][k].size
            nrow = _nrows(n_el, width)
            per_layer[k].append(total[row0:row0 + nrow].reshape(-1)[:n_el].reshape(shape))
            row0 += nrow
    nrow = _nrows(D, width)
    grads["final_g"] = total[row0:row0 + nrow].reshape(-1)[:D]
    row0 += nrow
    loss = total[row0, 0]
    for k in SMALL_PER_LAYER:
        full = jnp.stack(per_layer[k], axis=0)
        if k in shard_names:
            shard = w[k].shape[2]
            full = lax.dynamic_slice_in_dim(full, my_block * shard, shard, axis=2)
        grads[k] = full.reshape(w[k].shape)

    delta, new_m, new_v = {}, {}, {}
    for k in BIG:
        grads[k], delta[k], new_m[k], new_v[k] = _sum4_adamw(own_all[k], recv[k], w[k], mom[k], var[k], 0, 1,
                                                             done.get(k))
    for out in (grads, delta, new_m, new_v):
        out["w_in"] = jnp.swapaxes(out["w_in"], 1, 2)
    for k in WEIGHTS:
        if k in BIG:
            continue
        d, mn, vn = _adamw(_as3d(w[k]), _as3d(grads[k]), _as3d(mom[k]), _as3d(var[k]))
        delta[k], new_m[k], new_v[k] = d.reshape(w[k].shape), mn.reshape(w[k].shape), vn.reshape(w[k].shape)

    return (loss, dx.reshape(1, T, D), *[grads[k] for k in WEIGHTS], *[delta[k] for k in WEIGHTS],
            *[new_m[k] for k in WEIGHTS], *[new_v[k] for k in WEIGHTS])
```

```python
import functools

import jax
import jax.numpy as jnp
from jax import lax
from jax.experimental import pallas as pl
from jax.experimental.pallas import tpu as pltpu

F32 = jnp.float32
BF16 = jnp.bfloat16

N_DEV = 8
HEAD_DIM = 64
N_KV_HEADS = 4
ATTN_BLOCK = 128
CONF_KERNEL = 31
SCONV_KERNEL = 3
N_BRANCH = 3
RMS_EPS = 1e-6
LN_EPS = 1e-5
ADAM_LR = 0.001
ADAM_B1 = 0.9
ADAM_B2 = 0.999
ADAM_EPS = 1e-08
ADAM_WD = 0.01
ADAM_STEP = 10
LANES = 128
NEG_BIG = -1e30
VMEM_LIMIT_BYTES = 56 * 1024 * 1024
MESH = pl.DeviceIdType.MESH

NN = (((1,), (0,)), ((), ()))
NT = (((1,), (1,)), ((), ()))
TN = (((0,), (0,)), ((), ()))


def _pick(n, cap, mult=LANES):
    best = None
    for d in range(mult, min(n, cap) + 1, mult):
        if n % d == 0:
            best = d
    assert best is not None, (n, cap, mult)
    return best


def _sigmoid(x):
    return 0.5 * jnp.tanh(0.5 * x) + 0.5


def _params(sem):
    return pltpu.CompilerParams(dimension_semantics=sem, vmem_limit_bytes=VMEM_LIMIT_BYTES)


def _epi_cast(p, ex, outs):
    outs[0][...] = p.astype(outs[0].dtype)


def _epi_resid(p, ex, outs):
    outs[0][...] = ex[0][...] + p


def _epi_relu2(p, ex, outs):
    outs[0][...] = p.astype(outs[0].dtype)
    r = jnp.maximum(p, 0.0)
    outs[1][...] = (r * r).astype(outs[1].dtype)


def _epi_drelu2(p, ex, outs):
    up = ex[0][...].astype(F32)
    outs[0][...] = (p * (2.0 * jnp.maximum(up, 0.0))).astype(outs[0].dtype)


TOKEN = (8, LANES)


def _matmul(name, a, b, dnums, grid, a_spec, b_spec, out_shape, out_specs, epi, acc_shape, extra=(), extra_specs=(),
            dep=None):
    if dep is not None:
        extra = [*extra, dep]
        extra_specs = [*extra_specs, pl.BlockSpec(TOKEN, lambda j, i, k: (0, 0))]
    nk = grid[2]
    n_extra, n_out = len(extra), len(out_shape)

    def body(*refs):
        a_ref, b_ref = refs[0], refs[1]
        ex = refs[2:2 + n_extra]
        outs = refs[2 + n_extra:2 + n_extra + n_out]
        p = lax.dot_general(a_ref[...], b_ref[...], dnums, preferred_element_type=F32)
        if nk == 1:
            epi(p, ex, outs)
        else:
            acc = refs[-1]
            k = pl.program_id(2)

            @pl.when(k == 0)
            def _():
                acc[...] = p

            @pl.when(k > 0)
            def _():
                acc[...] += p

            @pl.when(k == nk - 1)
            def _():
                epi(acc[...], ex, outs)

    scratch = [pltpu.VMEM(acc_shape, F32)] if nk > 1 else []
    return pl.pallas_call(
        body, name=name, grid=grid, in_specs=[a_spec, b_spec, *extra_specs], out_specs=list(out_specs),
        out_shape=list(out_shape), scratch_shapes=scratch,
        compiler_params=_params(("parallel", "parallel", "arbitrary")))(a, b, *extra)


def _mm_plain(name, a, b, form, out_dtypes, epi=_epi_cast, extra=(), tm=1024, tn=1024, tk=2048, dep=None):
    if form == "NN":
        (M, K), N = a.shape, b.shape[1]
    elif form == "NT":
        (M, K), N = a.shape, b.shape[0]
    else:
        (K, M), N = a.shape, b.shape[1]
    tm, tn, tk = _pick(M, tm, 8), _pick(N, tn), _pick(K, tk)
    grid = (N // tn, M // tm, K // tk)
    if form == "TN":
        a_spec = pl.BlockSpec((tk, tm), lambda j, i, k: (k, i))
    else:
        a_spec = pl.BlockSpec((tm, tk), lambda j, i, k: (i, k))
    if form == "NT":
        b_spec = pl.BlockSpec((tn, tk), lambda j, i, k: (j, k))
    else:
        b_spec = pl.BlockSpec((tk, tn), lambda j, i, k: (k, j))
    o_spec = pl.BlockSpec((tm, tn), lambda j, i, k: (i, j))
    dn = {"NN": NN, "NT": NT, "TN": TN}[form]
    return _matmul(name, a, b, dn, grid, a_spec, b_spec,
                   [jax.ShapeDtypeStruct((M, N), dt) for dt in out_dtypes], [o_spec] * len(out_dtypes), epi,
                   (tm, tn), extra, [o_spec] * len(extra), dep)


def _mm_nn_colblocked(name, a, bb, out_dtypes, epi=_epi_cast, tm=1024, tn=1024, tk=2048):
    M, K = a.shape
    ns = bb.shape[2]
    N = N_DEV * ns
    tm, tn, tk = _pick(M, tm, 8), _pick(ns, tn), _pick(K, tk)
    q = ns // tn
    grid = (N // tn, M // tm, K // tk)
    a_spec = pl.BlockSpec((tm, tk), lambda j, i, k: (i, k))
    b_spec = pl.BlockSpec((None, tk, tn), lambda j, i, k: (j // q, k, j % q))
    o_spec = pl.BlockSpec((tm, tn), lambda j, i, k: (i, j))
    return _matmul(name, a, bb, NN, grid, a_spec, b_spec,
                   [jax.ShapeDtypeStruct((M, N), dt) for dt in out_dtypes], [o_spec] * len(out_dtypes), epi, (tm, tn))


def _mm_nt_colblocked(name, a, bb, out_dtype, tm=1024, tn=1024, tk=1024, dep=None):
    M, N = a.shape
    K, ns = bb.shape[1], bb.shape[2]
    tm, tn, tk = _pick(M, tm, 8), _pick(K, tn), _pick(ns, tk)
    q = ns // tk
    grid = (K // tn, M // tm, N // tk)
    a_spec = pl.BlockSpec((tm, tk), lambda j, i, k: (i, k))
    b_spec = pl.BlockSpec((None, tn, tk), lambda j, i, k: (k // q, j, k % q))
    o_spec = pl.BlockSpec((tm, tn), lambda j, i, k: (i, j))
    return _matmul(name, a, bb, NT, grid, a_spec, b_spec, [jax.ShapeDtypeStruct((M, K), out_dtype)], [o_spec],
                   _epi_cast, (tm, tn), dep=dep)[0]


def _mm_tn_colblocked_out(name, a, b, out_dtype, tm=1024, tn=1024, tk=2048):
    T, M = a.shape
    N = b.shape[1]
    ns = N // N_DEV
    tm, tn, tk = _pick(M, tm, 8), _pick(ns, tn), _pick(T, tk)
    q = ns // tn
    grid = (N // tn, M // tm, T // tk)
    a_spec = pl.BlockSpec((tk, tm), lambda j, i, k: (k, i))
    b_spec = pl.BlockSpec((tk, tn), lambda j, i, k: (k, j))
    o_spec = pl.BlockSpec((None, tm, tn), lambda j, i, k: (j // q, i, j % q))
    return _matmul(name, a, b, TN, grid, a_spec, b_spec, [jax.ShapeDtypeStruct((N_DEV, M, ns), out_dtype)], [o_spec],
                   _epi_cast, (tm, tn))[0]


def _unblock(bb_ref, full_ref):
    ns = bb_ref.shape[2]
    for j in range(N_DEV):
        full_ref[:, j * ns:(j + 1) * ns] = bb_ref[j]


def _mm_branch(name, a, bb, form, out_dtype, tm=512, dep=None):
    M = a.shape[0]
    K, ns = bb.shape[1], bb.shape[2]
    N = N_DEV * ns
    tm = _pick(M, tm, 8)
    out_cols = N if form == "NN" else K
    deps = [] if dep is None else [dep]

    def body(a_ref, b_ref, *rest):
        o_ref, w_full = rest[len(deps):]

        @pl.when(pl.program_id(0) == 0)
        def _():
            _unblock(b_ref, w_full)

        o_ref[...] = lax.dot_general(a_ref[...], w_full[...], NN if form == "NN" else NT,
                                     preferred_element_type=F32).astype(out_dtype)

    return pl.pallas_call(
        body, name=name, grid=(M // tm,),
        in_specs=[pl.BlockSpec((tm, a.shape[1]), lambda i: (i, 0)), pl.BlockSpec(bb.shape, lambda i: (0, 0, 0)),
                  *[pl.BlockSpec(TOKEN, lambda i: (0, 0)) for _ in deps]],
        out_specs=pl.BlockSpec((tm, out_cols), lambda i: (i, 0)),
        out_shape=jax.ShapeDtypeStruct((M, out_cols), out_dtype),
        scratch_shapes=[pltpu.VMEM((K, N), BF16)],
        compiler_params=_params(("arbitrary",)))(a, bb, *deps)


def _mm_branch_grad(name, a, b, out_dtype, tm=256):
    T, M = a.shape
    N = b.shape[1]
    ns = N // N_DEV
    tm = _pick(M, tm, 8)

    def body(a_ref, b_ref, o_ref):
        p = lax.dot_general(a_ref[...], b_ref[...], TN, preferred_element_type=F32)
        for j in range(N_DEV):
            o_ref[j] = p[:, j * ns:(j + 1) * ns].astype(out_dtype)

    return pl.pallas_call(
        body, name=name, grid=(M // tm,),
        in_specs=[pl.BlockSpec((T, tm), lambda i: (0, i)), pl.BlockSpec((T, N), lambda i: (0, 0))],
        out_specs=pl.BlockSpec((N_DEV, tm, ns), lambda i: (0, i, 0)),
        out_shape=jax.ShapeDtypeStruct((N_DEV, M, ns), out_dtype),
        compiler_params=_params(("parallel",)))(a, b)


ROW_TILE = 256


def _rms_fwd(x, g):
    T, D = x.shape
    tr = _pick(T, ROW_TILE, 8)

    def body(x_ref, g_ref, h_ref):
        xv = x_ref[...]
        r = lax.rsqrt(jnp.mean(xv * xv, axis=-1, keepdims=True) + RMS_EPS)
        h_ref[...] = (xv * r * g_ref[...]).astype(BF16)

    return pl.pallas_call(
        body, name="rms_fwd", grid=(T // tr,),
        in_specs=[pl.BlockSpec((tr, D), lambda i: (i, 0)), pl.BlockSpec((1, D), lambda i: (0, 0))],
        out_specs=pl.BlockSpec((tr, D), lambda i: (i, 0)),
        out_shape=jax.ShapeDtypeStruct((T, D), BF16), compiler_params=_params(("parallel",)))(x, g)


def _rms_bwd_math(dh, xv, g):
    r = lax.rsqrt(jnp.mean(xv * xv, axis=-1, keepdims=True) + RMS_EPS)
    gdh = dh * g
    dot = jnp.mean(gdh * xv, axis=-1, keepdims=True)
    dx = r * gdh - xv * (r * r * r * dot)
    return dx, dh * xv * r


def _rms_bwd(dh, x, g, dres):
    T, D = x.shape
    tr = _pick(T, ROW_TILE, 8)

    def body(dh_ref, x_ref, g_ref, dres_ref, dx_ref, dxb_ref, dg_ref):
        dx, dgrow = _rms_bwd_math(dh_ref[...], x_ref[...], g_ref[...])
        dx = dx + dres_ref[...]
        dx_ref[...] = dx
        dxb_ref[...] = dx.astype(BF16)
        part = jnp.sum(dgrow, axis=0, keepdims=True)

        @pl.when(pl.program_id(0) == 0)
        def _():
            dg_ref[...] = part

        @pl.when(pl.program_id(0) > 0)
        def _():
            dg_ref[...] += part

    row = pl.BlockSpec((tr, D), lambda i: (i, 0))
    vec = pl.BlockSpec((1, D), lambda i: (0, 0))
    return pl.pallas_call(
        body, name="rms_bwd", grid=(T // tr,), in_specs=[row, row, vec, row], out_specs=[row, row, vec],
        out_shape=[jax.ShapeDtypeStruct((T, D), F32), jax.ShapeDtypeStruct((T, D), BF16),
                   jax.ShapeDtypeStruct((1, D), F32)],
        compiler_params=_params(("arbitrary",)))(dh, x, g, dres)


def _loss_head(x, g, target):
    T, D = x.shape
    tr = _pick(T, ROW_TILE, 8)

    def body(x_ref, g_ref, t_ref, dx_ref, dxb_ref, dg_ref, loss_ref):
        xv, gv = x_ref[...], g_ref[...]
        r = lax.rsqrt(jnp.mean(xv * xv, axis=-1, keepdims=True) + RMS_EPS)
        err = xv * r * gv - t_ref[...]
        part_loss = 0.5 * jnp.sum(jnp.mean(err * err, axis=-1, keepdims=True), axis=0, keepdims=True)
        dx, dgrow = _rms_bwd_math(err * (1.0 / D), xv, gv)
        dx_ref[...] = dx
        dxb_ref[...] = dx.astype(BF16)
        part = jnp.sum(dgrow, axis=0, keepdims=True)
        lpart = jnp.broadcast_to(part_loss, (8, LANES))

        @pl.when(pl.program_id(0) == 0)
        def _():
            dg_ref[...] = part
            loss_ref[...] = lpart

        @pl.when(pl.program_id(0) > 0)
        def _():
            dg_ref[...] += part
            loss_ref[...] += lpart

    row = pl.BlockSpec((tr, D), lambda i: (i, 0))
    vec = pl.BlockSpec((1, D), lambda i: (0, 0))
    lsp = pl.BlockSpec((8, LANES), lambda i: (0, 0))
    return pl.pallas_call(
        body, name="loss_head", grid=(T // tr,), in_specs=[row, vec, row], out_specs=[row, row, vec, lsp],
        out_shape=[jax.ShapeDtypeStruct((T, D), F32), jax.ShapeDtypeStruct((T, D), BF16),
                   jax.ShapeDtypeStruct((1, D), F32), jax.ShapeDtypeStruct((8, LANES), F32)],
        compiler_params=_params(("arbitrary",)))(x, g, target)


def _ln_math(a1, g, b):
    mu = jnp.mean(a1, axis=-1, keepdims=True)
    xc = a1 - mu
    rstd = lax.rsqrt(jnp.mean(xc * xc, axis=-1, keepdims=True) + LN_EPS)
    xhat = xc * rstd
    return xhat, rstd, xhat * g + b


def _ln_silu(a1, g, b):
    T, C = a1.shape
    tr = _pick(T, ROW_TILE, 8)

    def body(a_ref, g_ref, b_ref, o_ref):
        _, _, y = _ln_math(a_ref[...], g_ref[...], b_ref[...])
        o_ref[...] = (y * _sigmoid(y)).astype(BF16)

    row = pl.BlockSpec((tr, C), lambda i: (i, 0))
    vec = pl.BlockSpec((1, C), lambda i: (0, 0))
    return pl.pallas_call(body, name="ln_silu", grid=(T // tr,), in_specs=[row, vec, vec], out_specs=row,
                          out_shape=jax.ShapeDtypeStruct((T, C), BF16), compiler_params=_params(("parallel",)))(a1, g, b)


def _ln_silu_bwd(a1, g, b, d_a3):
    T, C = a1.shape
    tr = _pick(T, ROW_TILE, 8)

    def body(a_ref, g_ref, b_ref, d_ref, da_ref, dg_ref, db_ref):
        gv = g_ref[...]
        xhat, rstd, y = _ln_math(a_ref[...], gv, b_ref[...])
        s = _sigmoid(y)
        dy = d_ref[...].astype(F32) * (s * (1.0 + y * (1.0 - s)))
        dxh = dy * gv
        m1 = jnp.mean(dxh, axis=-1, keepdims=True)
        m2 = jnp.mean(dxh * xhat, axis=-1, keepdims=True)
        da_ref[...] = rstd * (dxh - m1 - xhat * m2)
        pg = jnp.sum(dy * xhat, axis=0, keepdims=True)
        pb = jnp.sum(dy, axis=0, keepdims=True)

        @pl.when(pl.program_id(0) == 0)
        def _():
            dg_ref[...] = pg
            db_ref[...] = pb

        @pl.when(pl.program_id(0) > 0)
        def _():
            dg_ref[...] += pg
            db_ref[...] += pb

    row = pl.BlockSpec((tr, C), lambda i: (i, 0))
    vec = pl.BlockSpec((1, C), lambda i: (0, 0))
    return pl.pallas_call(
        body, name="ln_silu_bwd", grid=(T // tr,), in_specs=[row, vec, vec, row], out_specs=[row, vec, vec],
        out_shape=[jax.ShapeDtypeStruct((T, C), F32), jax.ShapeDtypeStruct((1, C), F32),
                   jax.ShapeDtypeStruct((1, C), F32)],
        compiler_params=_params(("arbitrary",)))(a1, g, b, d_a3)


CONV_ROWS = 128
PAD_A = 32
PAD_B = 8


def _u_block(T, first):
    return pl.BlockSpec((T, LANES), lambda i: (0, first + i))


def _causal_conv(xpad_ref, w_ref, ksize, pad, T, emit):
    for r0 in range(0, T, CONV_ROWS):
        acc = None
        for j in range(ksize):
            off = pad - (ksize - 1) + j + r0
            term = w_ref[j:j + 1, :] * xpad_ref[off:off + CONV_ROWS, :]
            acc = term if acc is None else acc + term
        emit(r0, acc)


def _anticausal_conv(gpad_ref, w_ref, ksize, T, emit):
    for r0 in range(0, T, CONV_ROWS):
        acc = None
        for j in range(ksize):
            off = (ksize - 1) - j + r0
            term = w_ref[j:j + 1, :] * gpad_ref[off:off + CONV_ROWS, :]
            acc = term if acc is None else acc + term
        emit(r0, acc)


def _conv_wgrad(xpad_ref, g_ref, dw_ref, ksize, pad, T):
    for j in range(ksize):
        acc = None
        for r0 in range(0, T, CONV_ROWS):
            off = pad - (ksize - 1) + j + r0
            term = g_ref[r0:r0 + CONV_ROWS, :] * xpad_ref[off:off + CONV_ROWS, :]
            term = jnp.sum(term.reshape(CONV_ROWS // 8, 8, LANES), axis=0)
            acc = term if acc is None else acc + term
        dw_ref[j:j + 1, :] = jnp.sum(acc, axis=0, keepdims=True)


def _convs_fwd(u, conf_dw, sconv_w, C):
    T = u.shape[0]
    nb = C // LANES

    def body(av_ref, ag_ref, bg_ref, cg_ref, bh_ref, dw_ref, sw_ref, a1_ref, s2_ref, xa, xs, s1):
        xa[0:PAD_A, :] = jnp.zeros((PAD_A, LANES), F32)
        xa[PAD_A:PAD_A + T, :] = av_ref[...].astype(F32) * _sigmoid(ag_ref[...].astype(F32))

        def emit_a(r0, acc):
            a1_ref[r0:r0 + CONV_ROWS, :] = acc

        _causal_conv(xa, dw_ref, CONF_KERNEL, PAD_A, T, emit_a)

        xs[0:PAD_B, :] = jnp.zeros((PAD_B, LANES), F32)
        xs[PAD_B:PAD_B + T, :] = cg_ref[...].astype(F32) * bh_ref[...].astype(F32)

        def emit_b(r0, acc):
            s1[r0:r0 + CONV_ROWS, :] = acc

        _causal_conv(xs, sw_ref, SCONV_KERNEL, PAD_B, T, emit_b)
        s2_ref[...] = (bg_ref[...].astype(F32) * s1[...]).astype(BF16)

    col = pl.BlockSpec((T, LANES), lambda i: (0, i))
    return pl.pallas_call(
        body, name="convs_fwd", grid=(nb,),
        in_specs=[_u_block(T, 0), _u_block(T, nb), _u_block(T, 2 * nb), _u_block(T, 3 * nb), _u_block(T, 4 * nb),
                  pl.BlockSpec((CONF_KERNEL, LANES), lambda i: (0, i)),
                  pl.BlockSpec((SCONV_KERNEL, LANES), lambda i: (0, i))],
        out_specs=[col, col],
        out_shape=[jax.ShapeDtypeStruct((T, C), F32), jax.ShapeDtypeStruct((T, C), BF16)],
        scratch_shapes=[pltpu.VMEM((T + PAD_A, LANES), F32), pltpu.VMEM((T + PAD_B, LANES), F32),
                        pltpu.VMEM((T, LANES), F32)],
        compiler_params=_params(("parallel",)))(u, u, u, u, u, conf_dw, sconv_w)


def _convs_bwd(u, conf_dw, sconv_w, d_a1, d_s2, C):
    T = u.shape[0]
    nb = C // LANES

    def body(*refs):
        _convs_bwd_block(*refs, T)

    col = pl.BlockSpec((T, LANES), lambda i: (0, i))
    wa = pl.BlockSpec((CONF_KERNEL, LANES), lambda i: (0, i))
    wb = pl.BlockSpec((SCONV_KERNEL, LANES), lambda i: (0, i))
    act = jax.ShapeDtypeStruct((T, C), BF16)
    return pl.pallas_call(
        body, name="convs_bwd", grid=(nb,),
        in_specs=[_u_block(T, 0), _u_block(T, nb), _u_block(T, 2 * nb), _u_block(T, 3 * nb), _u_block(T, 4 * nb),
                  wa, wb, col, col],
        out_specs=[col, col, col, col, col, wa, wb],
        out_shape=[act, act, act, act, act, jax.ShapeDtypeStruct((CONF_KERNEL, C), F32),
                   jax.ShapeDtypeStruct((SCONV_KERNEL, C), F32)],
        scratch_shapes=[pltpu.VMEM((T + PAD_A, LANES), F32), pltpu.VMEM((T + PAD_A, LANES), F32),
                        pltpu.VMEM((T + PAD_B, LANES), F32), pltpu.VMEM((T + PAD_B, LANES), F32),
                        pltpu.VMEM((T, LANES), F32)],
        compiler_params=_params(("parallel",)))(u, u, u, u, u, conf_dw, sconv_w, d_a1, d_s2)


def _convs_bwd_block(av_ref, ag_ref, bg_ref, cg_ref, bh_ref, dw_ref, sw_ref, da1_ref, ds2_ref,
                     dav_ref, dag_ref, dbg_ref, dcg_ref, dbh_ref, ddw_ref, dsw_ref, xa, ga, xs, gs, tmp, T):
    def to_tmp(r0, acc):
        tmp[r0:r0 + CONV_ROWS, :] = acc

    av = av_ref[...].astype(F32)
    sg = _sigmoid(ag_ref[...].astype(F32))
    xa[0:PAD_A, :] = jnp.zeros((PAD_A, LANES), F32)
    xa[PAD_A:PAD_A + T, :] = av * sg
    ga[0:T, :] = da1_ref[...]
    ga[T:T + PAD_A, :] = jnp.zeros((PAD_A, LANES), F32)
    _conv_wgrad(xa, ga, ddw_ref, CONF_KERNEL, PAD_A, T)
    _anticausal_conv(ga, dw_ref, CONF_KERNEL, T, to_tmp)
    da0 = tmp[...]
    dav_ref[...] = (da0 * sg).astype(BF16)
    dag_ref[...] = (da0 * av * sg * (1.0 - sg)).astype(BF16)

    cg = cg_ref[...].astype(F32)
    bh = bh_ref[...].astype(F32)
    ds2 = ds2_ref[...].astype(F32)
    xs[0:PAD_B, :] = jnp.zeros((PAD_B, LANES), F32)
    xs[PAD_B:PAD_B + T, :] = cg * bh
    _causal_conv(xs, sw_ref, SCONV_KERNEL, PAD_B, T, to_tmp)
    dbg_ref[...] = (ds2 * tmp[...]).astype(BF16)
    gs[0:T, :] = ds2 * bg_ref[...].astype(F32)
    gs[T:T + PAD_B, :] = jnp.zeros((PAD_B, LANES), F32)
    _conv_wgrad(xs, gs, dsw_ref, SCONV_KERNEL, PAD_B, T)
    _anticausal_conv(gs, sw_ref, SCONV_KERNEL, T, to_tmp)
    ds0 = tmp[...]
    dcg_ref[...] = (ds0 * bh).astype(BF16)
    dbh_ref[...] = (ds0 * cg).astype(BF16)


def _attn_specs(T, C, q_off_blocks):
    kvw = N_KV_HEADS * HEAD_DIM
    kb = (5 * C + C) // kvw
    qs = pl.BlockSpec((ATTN_BLOCK, C), lambda n: (n, 5))
    kc = pl.BlockSpec((ATTN_BLOCK, kvw), lambda n: (n, kb))
    kp = pl.BlockSpec((ATTN_BLOCK, kvw), lambda n: (jnp.maximum(n - 1, 0), kb))
    vc = pl.BlockSpec((ATTN_BLOCK, kvw), lambda n: (n, kb + 1))
    vp = pl.BlockSpec((ATTN_BLOCK, kvw), lambda n: (jnp.maximum(n - 1, 0), kb + 1))
    return qs, kc, kp, vc, vp


def _attn_masks(n, rows):
    row = lax.broadcasted_iota(jnp.int32, (rows, ATTN_BLOCK), 0) % ATTN_BLOCK
    col = lax.broadcasted_iota(jnp.int32, (rows, ATTN_BLOCK), 1)
    from_cur = col <= row
    return from_cur, jnp.logical_or(from_cur, n > 0)


def _stack_heads(ref, heads, width=HEAD_DIM):
    return jnp.concatenate([ref[:, h * width:(h + 1) * width] for h in heads], axis=0)


def _unstack_heads(t, count):
    return [t[j * ATTN_BLOCK:(j + 1) * ATTN_BLOCK] for j in range(count)]


def _attn_scores(qh, kc, kp, from_cur, valid):
    qs = qh * (HEAD_DIM ** -0.5)
    s_c = lax.dot_general(qs, kc, NT, preferred_element_type=F32)
    s_p = lax.dot_general(qs, kp, NT, preferred_element_type=F32)
    return jnp.where(valid, jnp.where(from_cur, s_c, s_p), NEG_BIG)


def _attn_split(t, from_cur):
    t = t.astype(BF16)
    zero = jnp.zeros_like(t)
    return jnp.where(from_cur, t, zero), jnp.where(from_cur, zero, t)


def _attn_fwd(u, sinks, C):
    T = u.shape[0]
    H = C // HEAD_DIM
    grp = H // N_KV_HEADS

    def body(sink_ref, q_ref, kc_ref, kp_ref, vc_ref, vp_ref, o_ref, lse_ref):
        n = pl.program_id(0)
        from_cur, valid = _attn_masks(n, grp * ATTN_BLOCK)
        outs, lses = [], []
        for g in range(N_KV_HEADS):
            kv = slice(g * HEAD_DIM, (g + 1) * HEAD_DIM)
            heads = range(g * grp, (g + 1) * grp)
            sink = jnp.concatenate([jnp.full((ATTN_BLOCK, 1), sink_ref[h], F32) for h in heads], axis=0)
            s = _attn_scores(_stack_heads(q_ref, heads), kc_ref[:, kv], kp_ref[:, kv], from_cur, valid)
            m = jnp.maximum(jnp.max(s, axis=-1, keepdims=True), sink)
            p = jnp.exp(s - m)
            den = jnp.sum(p, axis=-1, keepdims=True) + jnp.exp(sink - m)
            p_c, p_p = _attn_split(p, from_cur)
            acc = jnp.dot(p_c, vc_ref[:, kv], preferred_element_type=F32)
            acc = acc + jnp.dot(p_p, vp_ref[:, kv], preferred_element_type=F32)
            outs += _unstack_heads((acc / den).astype(BF16), grp)
            lses += _unstack_heads(m + jnp.log(den), grp)
        o_ref[...] = jnp.concatenate(outs, axis=1)
        lse_ref[...] = jnp.concatenate(lses, axis=1)

    qs, kc, kp, vc, vp = _attn_specs(T, C, 5)
    return pl.pallas_call(
        body, name="attn_fwd", grid=(T // ATTN_BLOCK,),
        in_specs=[pl.BlockSpec(memory_space=pltpu.SMEM), qs, kc, kp, vc, vp],
        out_specs=[pl.BlockSpec((ATTN_BLOCK, C), lambda n: (n, 0)), pl.BlockSpec((ATTN_BLOCK, H), lambda n: (n, 0))],
        out_shape=[jax.ShapeDtypeStruct((T, C), BF16), jax.ShapeDtypeStruct((T, H), F32)],
        compiler_params=_params(("parallel",)))(sinks, u, u, u, u, u)


def _attn_bwd(u, o, lse, d_o, sinks, C):
    T = u.shape[0]
    H = C // HEAD_DIM
    grp = H // N_KV_HEADS
    kvw = N_KV_HEADS * HEAD_DIM
    nblk = T // ATTN_BLOCK
    scale = HEAD_DIM ** -0.5

    def body(sink_ref, q_ref, kc_ref, kp_ref, vc_ref, vp_ref, o_ref, lse_ref, do_ref,
             dq_ref, dk_ref, dv_ref, ds_ref, dk_acc, dv_acc):
        n = pl.program_id(0)

        @pl.when(n == 0)
        def _():
            dk_acc[...] = jnp.zeros_like(dk_acc)
            dv_acc[...] = jnp.zeros_like(dv_acc)
            ds_ref[...] = jnp.zeros_like(ds_ref)

        from_cur, valid = _attn_masks(n, grp * ATTN_BLOCK)
        cur = pl.ds(pl.multiple_of(n * ATTN_BLOCK, ATTN_BLOCK), ATTN_BLOCK)
        prev = pl.ds(pl.multiple_of(jnp.maximum(n - 1, 0) * ATTN_BLOCK, ATTN_BLOCK), ATTN_BLOCK)
        dqs, dsinks, dk_cs, dk_ps, dv_cs, dv_ps = [], [], [], [], [], []
        for g in range(N_KV_HEADS):
            kv = slice(g * HEAD_DIM, (g + 1) * HEAD_DIM)
            kc, kp, vc, vp = kc_ref[:, kv], kp_ref[:, kv], vc_ref[:, kv], vp_ref[:, kv]
            heads = range(g * grp, (g + 1) * grp)
            qg, dog, og = _stack_heads(q_ref, heads), _stack_heads(do_ref, heads), _stack_heads(o_ref, heads)
            lse_g = _stack_heads(lse_ref, heads, 1)
            sink = jnp.concatenate([jnp.full((ATTN_BLOCK, 1), sink_ref[h], F32) for h in heads], axis=0)
            p = jnp.exp(_attn_scores(qg, kc, kp, from_cur, valid) - lse_g)
            delta = jnp.sum(dog.astype(F32) * og.astype(F32), axis=-1, keepdims=True)
            dp = jnp.where(from_cur, lax.dot_general(dog, vc, NT, preferred_element_type=F32),
                           lax.dot_general(dog, vp, NT, preferred_element_type=F32))
            ds_c, ds_p = _attn_split(p * (dp - delta) * scale, from_cur)
            p_c, p_p = _attn_split(p, from_cur)
            dq = jnp.dot(ds_c, kc, preferred_element_type=F32) + jnp.dot(ds_p, kp, preferred_element_type=F32)
            dqs += _unstack_heads(dq.astype(BF16), grp)
            dk_cs.append(lax.dot_general(ds_c, qg, TN, preferred_element_type=F32))
            dk_ps.append(lax.dot_general(ds_p, qg, TN, preferred_element_type=F32))
            dv_cs.append(lax.dot_general(p_c, dog, TN, preferred_element_type=F32))
            dv_ps.append(lax.dot_general(p_p, dog, TN, preferred_element_type=F32))
            for t in _unstack_heads(jnp.exp(sink - lse_g) * delta, grp):
                dsinks.append(jnp.broadcast_to(-jnp.sum(t, axis=0, keepdims=True), (8, 1)))
        dq_ref[...] = jnp.concatenate(dqs, axis=1)
        ds_ref[...] += jnp.concatenate(dsinks, axis=1)
        dk_acc[cur, :] += jnp.concatenate(dk_cs, axis=1)
        dk_acc[prev, :] += jnp.concatenate(dk_ps, axis=1)
        dv_acc[cur, :] += jnp.concatenate(dv_cs, axis=1)
        dv_acc[prev, :] += jnp.concatenate(dv_ps, axis=1)

        @pl.when(n == nblk - 1)
        def _():
            dk_ref[...] = dk_acc[...].astype(BF16)
            dv_ref[...] = dv_acc[...].astype(BF16)

    qs, kc, kp, vc, vp = _attn_specs(T, C, 5)
    blk = pl.BlockSpec((ATTN_BLOCK, C), lambda n: (n, 0))
    full = pl.BlockSpec((T, kvw), lambda n: (0, 0))
    return pl.pallas_call(
        body, name="attn_bwd", grid=(nblk,),
        in_specs=[pl.BlockSpec(memory_space=pltpu.SMEM), qs, kc, kp, vc, vp, blk,
                  pl.BlockSpec((ATTN_BLOCK, H), lambda n: (n, 0)), blk],
        out_specs=[blk, full, full, pl.BlockSpec((8, H), lambda n: (0, 0))],
        out_shape=[jax.ShapeDtypeStruct((T, C), BF16), jax.ShapeDtypeStruct((T, kvw), BF16),
                   jax.ShapeDtypeStruct((T, kvw), BF16), jax.ShapeDtypeStruct((8, H), F32)],
        scratch_shapes=[pltpu.VMEM((T, kvw), F32), pltpu.VMEM((T, kvw), F32)],
        compiler_params=_params(("arbitrary",)))(sinks, u, u, u, u, u, o, lse, d_o)


MERGE_COLS = 512


def _merge_specs(T, D, I):
    tr = _pick(T, ROW_TILE, 8)
    tc = _pick(D, MERGE_COLS)
    g0 = (I - N_BRANCH * D) // tc
    per = D // tc
    gspecs = [pl.BlockSpec((tr, tc), functools.partial(lambda j, i, b: (i, g0 + b * per + j), b=b)) for b in range(N_BRANCH)]
    tile = pl.BlockSpec((tr, tc), lambda j, i: (i, j))
    bias = pl.BlockSpec((N_BRANCH, tc), lambda j, i: (0, j))
    return tr, tc, gspecs, tile, bias


def _merge_fwd(u, gate_b, ya, yb, yc):
    T, I = u.shape
    D = ya.shape[1]
    tr, tc, gspecs, tile, bias = _merge_specs(T, D, I)

    def body(g0_ref, g1_ref, g2_ref, b_ref, ya_ref, yb_ref, yc_ref, o_ref):
        acc = None
        for b, (g_ref, y_ref) in enumerate(((g0_ref, ya_ref), (g1_ref, yb_ref), (g2_ref, yc_ref))):
            gate = _sigmoid(g_ref[...].astype(F32) + b_ref[b:b + 1, :])
            term = gate * y_ref[...].astype(F32)
            acc = term if acc is None else acc + term
        o_ref[...] = acc.astype(BF16)

    return pl.pallas_call(
        body, name="merge_fwd", grid=(D // tc, T // tr), in_specs=[*gspecs, bias, tile, tile, tile], out_specs=tile,
        out_shape=jax.ShapeDtypeStruct((T, D), BF16),
        compiler_params=_params(("parallel", "parallel")))(u, u, u, gate_b, ya, yb, yc)


def _merge_bwd(u, gate_b, ya, yb, yc, dm):
    T, I = u.shape
    D = ya.shape[1]
    tr, tc, gspecs, tile, bias = _merge_specs(T, D, I)

    def body(g0_ref, g1_ref, g2_ref, b_ref, ya_ref, yb_ref, yc_ref, dm_ref,
             dya_ref, dyb_ref, dyc_ref, dg0_ref, dg1_ref, dg2_ref, db_ref):
        dmv = dm_ref[...].astype(F32)
        first = pl.program_id(1) == 0
        for b, (g_ref, y_ref, dy_ref, dg_ref) in enumerate(((g0_ref, ya_ref, dya_ref, dg0_ref),
                                                           (g1_ref, yb_ref, dyb_ref, dg1_ref),
                                                           (g2_ref, yc_ref, dyc_ref, dg2_ref))):
            gate = _sigmoid(g_ref[...].astype(F32) + b_ref[b:b + 1, :])
            dy_ref[...] = (dmv * gate).astype(BF16)
            dpre = dmv * y_ref[...].astype(F32) * gate * (1.0 - gate)
            dg_ref[...] = dpre.astype(BF16)
            part = jnp.sum(dpre, axis=0, keepdims=True)

            @pl.when(first)
            def _():
                db_ref[b:b + 1, :] = part

            @pl.when(jnp.logical_not(first))
            def _():
                db_ref[b:b + 1, :] += part

    act = jax.ShapeDtypeStruct((T, D), BF16)
    return pl.pallas_call(
        body, name="merge_bwd", grid=(D // tc, T // tr), in_specs=[*gspecs, bias, tile, tile, tile, tile],
        out_specs=[tile] * 6 + [bias], out_shape=[act] * 6 + [jax.ShapeDtypeStruct((N_BRANCH, D), F32)],
        compiler_params=_params(("parallel", "arbitrary")))(u, u, u, gate_b, ya, yb, yc, dm)


def _concat_columns(parts):
    T = parts[0].shape[0]
    widths = [p.shape[1] for p in parts]
    tr = _pick(T, ROW_TILE, 16)

    def body(*refs):
        o_ref, off = refs[-1], 0
        for ref, width in zip(refs[:-1], widths):
            o_ref[:, off:off + width] = ref[...]
            off += width

    return pl.pallas_call(
        body, name="concat_columns", grid=(T // tr,),
        in_specs=[pl.BlockSpec((tr, width), lambda i: (i, 0)) for width in widths],
        out_specs=pl.BlockSpec((tr, sum(widths)), lambda i: (i, 0)),
        out_shape=jax.ShapeDtypeStruct((T, sum(widths)), parts[0].dtype),
        compiler_params=_params(("parallel",)))(*parts)


ELEMS_PER_TILE = 512 * 1024


def _row_tile(r, c):
    return _pick(r, max(16, ELEMS_PER_TILE // c), 16) if r % 16 == 0 else r


def _cast_place(w, layer, my_block):
    L, r, c = w.shape
    tr = _row_tile(r, c)

    def body(blk_ref, w_ref, o_ref):
        o_ref[...] = w_ref[...].astype(BF16)

    return pl.pallas_call(
        body, name="cast_place",
        grid_spec=pltpu.PrefetchScalarGridSpec(
            num_scalar_prefetch=1, grid=(r // tr,),
            in_specs=[pl.BlockSpec((None, tr, c), lambda i, blk: (layer, i, 0))],
            out_specs=pl.BlockSpec((None, tr, c), lambda i, blk: (blk[0], i, 0))),
        out_shape=jax.ShapeDtypeStruct((N_DEV, r, c), BF16), compiler_params=_params(("parallel",)))(my_block, w)


def _adamw_math(w, g, m, v):
    m = ADAM_B1 * m + (1.0 - ADAM_B1) * g
    v = ADAM_B2 * v + (1.0 - ADAM_B2) * (g * g)
    m_hat = m / (1.0 - ADAM_B1 ** ADAM_STEP)
    v_hat = v / (1.0 - ADAM_B2 ** ADAM_STEP)
    delta = -ADAM_LR * (m_hat / (jnp.sqrt(v_hat) + ADAM_EPS) + ADAM_WD * w)
    return delta, m, v


def _sum_parts(part_ref):
    acc = part_ref[0].astype(F32)
    for s in range(1, N_DEV):
        acc = acc + part_ref[s].astype(F32)
    return acc


def _sum8(parts):
    L, _, r, c = parts.shape
    tr = _row_tile(r, c)

    def body(p_ref, o_ref):
        o_ref[...] = _sum_parts(p_ref)

    return pl.pallas_call(
        body, name="sum8", grid=(L, r // tr),
        in_specs=[pl.BlockSpec((None, N_DEV, tr, c), lambda l, i: (l, 0, i, 0))],
        out_specs=pl.BlockSpec((None, tr, c), lambda l, i: (l, i, 0)),
        out_shape=jax.ShapeDtypeStruct((L, r, c), F32), compiler_params=_params(("parallel", "parallel")))(parts)


def _adamw(w, g, m, v):
    L, r, c = w.shape
    tr = _row_tile(r, c)
    spec = pl.BlockSpec((None, tr, c), lambda l, i: (l, i, 0))

    def body(w_ref, g_ref, m_ref, v_ref, d_ref, mo_ref, vo_ref):
        d, mn, vn = _adamw_math(w_ref[...], g_ref[...], m_ref[...], v_ref[...])
        d_ref[...] = d
        mo_ref[...] = mn
        vo_ref[...] = vn

    shp = jax.ShapeDtypeStruct(w.shape, F32)
    return pl.pallas_call(body, name="adamw", grid=(L, r // tr), in_specs=[spec] * 4, out_specs=[spec] * 3,
                          out_shape=[shp] * 3, compiler_params=_params(("parallel", "parallel")))(w, g, m, v)


N_CHIPS = 4
CHIP_XOR = (0, 2, 1, 3)


def _chip_sum(part4, sib4, own_all, layer, pos):
    _, _, r, c = part4.shape
    tr = _row_tile(r, c)

    def chip(p, s):
        return jnp.bitwise_xor(2 * p[0] + p[1], CHIP_XOR[s])

    mine = [pl.BlockSpec((None, None, tr, c), functools.partial(lambda i, p, s: (chip(p, s), p[2], i, 0), s=s))
            for s in range(N_CHIPS)]
    theirs = [pl.BlockSpec((None, None, tr, c), functools.partial(lambda i, p, s: (chip(p, s), 0, i, 0), s=s))
              for s in range(N_CHIPS)]

    def body(pos_ref, *refs):
        a, b = refs[:N_CHIPS], refs[N_CHIPS:2 * N_CHIPS]
        out_ref, own_ref = refs[2 * N_CHIPS + 1], refs[2 * N_CHIPS + 2]
        own_ref[...] = a[0][...].astype(F32) + b[0][...].astype(F32)
        for s in range(1, N_CHIPS):
            out_ref[s - 1] = (a[s][...].astype(F32) + b[s][...].astype(F32)).astype(BF16)

    return pl.pallas_call(
        body, name="chip_sum",
        grid_spec=pltpu.PrefetchScalarGridSpec(
            num_scalar_prefetch=1, grid=(r // tr,),
            in_specs=[*mine, *theirs, pl.BlockSpec(memory_space=pl.ANY)],
            out_specs=[pl.BlockSpec((N_CHIPS - 1, tr, c), lambda i, p: (0, i, 0)),
                       pl.BlockSpec((None, tr, c), lambda i, p: (layer, i, 0))]),
        out_shape=[jax.ShapeDtypeStruct((N_CHIPS - 1, r, c), BF16), jax.ShapeDtypeStruct(own_all.shape, F32)],
        input_output_aliases={1 + 2 * N_CHIPS: 1},
        compiler_params=_params(("parallel",)))(pos, *([part4] * N_CHIPS), *([sib4] * N_CHIPS), own_all)


def _sum_chips(own_ref, got_ref):
    acc = own_ref[...]
    for s in range(N_CHIPS - 1):
        acc = acc + got_ref[s].astype(F32)
    return acc


def _sum4_adamw(own, got, w, m, v, first, count, earlier=None):
    L, r, c = w.shape
    tr = _row_tile(r, c)
    spec = pl.BlockSpec((None, tr, c), lambda l, i: (first + l, i, 0))
    earlier = [] if earlier is None else list(earlier)

    def body(own_ref, got_ref, w_ref, m_ref, v_ref, *rest):
        g_ref, d_ref, mo_ref, vo_ref = rest[len(earlier):]
        g = _sum_chips(own_ref, got_ref)
        d, mn, vn = _adamw_math(w_ref[...], g, m_ref[...], v_ref[...])
        g_ref[...] = g
        d_ref[...] = d
        mo_ref[...] = mn
        vo_ref[...] = vn

    shp = jax.ShapeDtypeStruct(w.shape, F32)
    return pl.pallas_call(
        body, name="sum4_adamw", grid=(count, r // tr),
        in_specs=[spec, pl.BlockSpec((None, N_CHIPS - 1, tr, c), lambda l, i: (first + l, 0, i, 0)), spec, spec, spec,
                  *([ANY] * len(earlier))],
        out_specs=[spec] * 4, out_shape=[shp] * 4, input_output_aliases={5 + a: a for a in range(len(earlier))},
        compiler_params=_params(("parallel", "parallel")))(own, got, w, m, v, *earlier)


def _me():
    return lax.axis_index("x"), lax.axis_index("y"), lax.axis_index("c")


def _flip(pos, k):
    x, y, c = pos
    return (1 - x if k & 4 else x, 1 - y if k & 2 else y, 1 - c if k & 1 else c)


def _block_of(pos):
    return 4 * pos[0] + 2 * pos[1] + pos[2]


ANY = pl.BlockSpec(memory_space=pl.ANY)
SIBLING = 1
OTHER_CHIPS = (4, 2, 6)


def _all_gather(name, arrays, after=None):
    n = len(arrays)
    shapes = [a.shape[-2:] for a in arrays]
    deps = [] if after is None else [after]

    def body(*refs):
        srcs, outs = refs[:n], refs[n + len(deps):2 * n + len(deps)]
        send_sems, recv_sems, local_sems = refs[2 * n + len(deps):]
        me = _me()
        sib = _flip(me, SIBLING)

        def src_of(a):
            return srcs[a]

        def copy(a, k, block_pos, to, src=None):
            dst = outs[a].at[_block_of(block_pos)]
            return pltpu.make_async_remote_copy(
                src_ref=dst if src is None else src, dst_ref=dst, send_sem=send_sems.at[a, k],
                recv_sem=recv_sems.at[a, k], device_id=to, device_id_type=MESH)

        mine = [pltpu.make_async_copy(src_of(a), outs[a].at[_block_of(me)], local_sems.at[a]) for a in range(n)]
        for cp in mine:
            cp.start()
        first = []
        for a in range(n):
            first.append(copy(a, 0, me, sib, src=src_of(a)))
            for j, k in enumerate(OTHER_CHIPS):
                first.append(copy(a, 1 + j, me, _flip(me, k), src=src_of(a)))
        for cp in first:
            cp.start()
        passed = []
        for j, k in enumerate(OTHER_CHIPS):
            for a in range(n):
                copy(a, 1 + j, _flip(me, k), me).wait_recv()
                fw = copy(a, 4 + j, _flip(me, k), sib)
                fw.start()
                passed.append(fw)
        for a in range(n):
            copy(a, 0, sib, me).wait_recv()
            for j, k in enumerate(OTHER_CHIPS):
                copy(a, 4 + j, _flip(sib, k), me).wait_recv()
        for cp in first + passed:
            cp.wait_send()
        for cp in mine:
            cp.wait()

    return pl.pallas_call(
        body, name=name, in_specs=[ANY] * (n + len(deps)), out_specs=[ANY] * n,
        out_shape=[jax.ShapeDtypeStruct((N_DEV, *s), a.dtype) for s, a in zip(shapes, arrays)],
        scratch_shapes=[pltpu.SemaphoreType.DMA((n, 7)), pltpu.SemaphoreType.DMA((n, 7)), pltpu.SemaphoreType.DMA((n,))],
    )(*arrays, *deps)


HBM = pl.BlockSpec(memory_space=pltpu.HBM)
SEM = pl.BlockSpec(memory_space=pltpu.SEMAPHORE)
DATAFLOW = pltpu.SideEffectType.DATAFLOW_SIDE_EFFECTING
X_NEIGHBOUR, Y_NEIGHBOUR = 4, 2
NEAR = (SIBLING, X_NEIGHBOUR, Y_NEIGHBOUR)


def _in_hbm(a):
    return pltpu.with_memory_space_constraint(a, pltpu.HBM)


def _hbm_like(a):
    return pltpu.HBM(a.shape, a.dtype)


def _copies_start(name, srcs, lands, plan, n_copies, after=None):
    ns, n = len(srcs), len(lands)
    deps = [] if after is None else [after]

    def body(*refs):
        k0 = ns + n + len(deps)
        send_sems, recv_sems, token = refs[k0], refs[k0 + 1], refs[k0 + 2 + n]
        for s, (src, dst, peer, _) in enumerate(plan(_me(), refs[:ns], refs[ns:ns + n])):
            pltpu.make_async_remote_copy(src_ref=src, dst_ref=dst, send_sem=send_sems.at[s], recv_sem=recv_sems.at[s],
                                         device_id=peer, device_id_type=MESH).start()
        token[...] = jnp.zeros(TOKEN, F32)

    out = pl.pallas_call(
        body, name=name, in_specs=[*([HBM] * (ns + n)), *([ANY] * len(deps))],
        out_specs=[SEM, SEM, *([HBM] * n), pl.BlockSpec(memory_space=pltpu.VMEM)],
        out_shape=[pltpu.SemaphoreType.DMA((n_copies,)), pltpu.SemaphoreType.DMA((n_copies,)),
                   *[_hbm_like(a) for a in lands], jax.ShapeDtypeStruct(TOKEN, F32)],
        input_output_aliases={ns + a: 2 + a for a in range(n)},
        compiler_params=pltpu.CompilerParams(has_side_effects=DATAFLOW),
    )(*[_in_hbm(a) for a in srcs], *[_in_hbm(a) for a in lands], *deps)
    return dict(send=out[0], recv=out[1], srcs=list(srcs), plan=plan), list(out[2:2 + n]), out[2 + n]


def _copies_wait(name, flight, lands, after):
    srcs, plan = flight["srcs"], flight["plan"]
    ns, n = len(srcs), len(lands)
    after = list(after) if isinstance(after, (list, tuple)) else [after]

    def body(*refs):
        send_sems, recv_sems = refs[ns + n], refs[ns + n + 1]
        for s, (src, dst, peer, landing) in enumerate(plan(_me(), refs[:ns], refs[ns:ns + n])):
            pltpu.make_async_remote_copy(src_ref=src, dst_ref=dst, send_sem=send_sems.at[s], recv_sem=recv_sems.at[s],
                                         device_id=peer, device_id_type=MESH).wait_send()
            pltpu.make_async_remote_copy(src_ref=landing, dst_ref=landing, send_sem=send_sems.at[s],
                                         recv_sem=recv_sems.at[s], device_id=peer, device_id_type=MESH).wait_recv()

    out = pl.pallas_call(
        body, name=name, in_specs=[*([HBM] * (ns + n)), SEM, SEM, *([ANY] * len(after))], out_specs=[HBM] * n,
        out_shape=[_hbm_like(a) for a in lands], input_output_aliases={ns + a: a for a in range(n)},
        compiler_params=pltpu.CompilerParams(has_side_effects=DATAFLOW),
    )(*srcs, *lands, flight["send"], flight["recv"], *after)
    return list(out)


def _gather_plan_near(me, srcs, lands):
    plan = []
    for land in lands:
        own = land.at[_block_of(me)]
        for k in NEAR:
            peer = _flip(me, k)
            plan.append((own, own, peer, land.at[_block_of(peer)]))
    return plan


def _gather_plan_far(me, srcs, lands):
    x_nbr, y_nbr, far = _flip(me, X_NEIGHBOUR), _flip(me, Y_NEIGHBOUR), _flip(me, X_NEIGHBOUR | Y_NEIGHBOUR)
    plan = []
    for land in lands:
        half = land.shape[1] // 2
        first, second = pl.ds(0, half), pl.ds(half, half)
        passed = land.at[_block_of(y_nbr), first]
        plan.append((passed, passed, x_nbr, land.at[_block_of(far), first]))
        passed = land.at[_block_of(x_nbr), second]
        plan.append((passed, passed, y_nbr, land.at[_block_of(far), second]))
    return plan


def _broadcast_plan(me, srcs, lands):
    plan = []
    for land in lands:
        own = land.at[_block_of(me)]
        for k in range(1, N_DEV):
            peer = _flip(me, k)
            plan.append((own, own, peer, land.at[_block_of(peer)]))
    return plan


def _sibling_plan(me, srcs, lands):
    sib = _flip(me, SIBLING)
    return [(src.at[:, pl.ds(1 - me[2], 1)], land, sib, land) for src, land in zip(srcs, lands)]


def _scatter_plan(layer):
    def plan(me, srcs, lands):
        out = []
        for src, land in zip(srcs, lands):
            for j, k in enumerate(OTHER_CHIPS):
                out.append((src.at[j], land.at[layer, j], _flip(me, k), land.at[layer, j]))
        return out
    return plan


def _pass_on_plan(relations):
    def plan(me, srcs, lands):
        sib = _flip(me, SIBLING)
        out = []
        for land in lands:
            for k in relations:
                blk = land.at[_block_of(_flip(me, k))]
                out.append((blk, blk, sib, land.at[_block_of(_flip(sib, k))]))
        return out
    return plan


def _after(small, tokens):
    for t in tokens:
        small = small + t[0:1, 0:1]
    return small


def _layer_fwd(xc, l, W, P, dims, deps=(), mid_layer=None):
    T, D, C, I, F = dims
    h = _rms_fwd(xc, _after(P["norm_mix_g"][l:l + 1], deps))
    u = _mm_plain("mm_u", h, W["w_in"][l].reshape(I, D), "NT", [BF16], tn=1280)[0]
    a1, s2 = _convs_fwd(u, P["conf_dw"][l], P["sconv_w"][l], C)
    a3 = _ln_silu(a1, P["conf_ln_g"][l:l + 1], P["conf_ln_b"][l:l + 1])
    o, lse = _attn_fwd(u, P["sinks"][l], C)
    ya = _mm_branch("mm_branch_out", a3, W["w_conf_out"][l], "NN", BF16)
    yb = _mm_branch("mm_branch_out", s2, W["w_sconv_out"][l], "NN", BF16)
    yc = _mm_branch("mm_branch_out", o, W["w_attn_out"][l], "NN", BF16)
    merged = _merge_fwd(u, P["gate_b"][l], ya, yb, yc)
    x1 = _mm_plain("mm_mix", merged, W["w_mix_out"][l].reshape(D, D), "NN", [F32], _epi_resid, [xc], tk=2048)[0]
    norm_ffn_g = P["norm_ffn_g"][l:l + 1]
    h2 = _rms_fwd(x1, norm_ffn_g if mid_layer is None else _after(norm_ffn_g, [mid_layer(x1)]))
    up, act = _mm_nn_colblocked("mm_up", h2, W["w_up"][l], [BF16, BF16], _epi_relu2)
    x2 = _mm_plain("mm_down", act, W["w_down"][l].reshape(F, D), "NN", [F32], _epi_resid, [x1], tk=2048)[0]
    saved = dict(xc=xc, h=h, u=u, a1=a1, a3=a3, s2=s2, o=o, lse=lse, ya=ya, yb=yb, yc=yc, merged=merged, x1=x1, h2=h2,
                 up=up, act=act)
    return x2, saved


def _bwd_mlp(dx2, dx2_b, l, W, P, S, dims, dep=None):
    T, D, C, I, F = dims
    d_up = _mm_plain("mm_d_up", dx2_b, W["w_down"][l].reshape(F, D), "NT", [BF16], _epi_drelu2, [S["up"]], dep=dep)[0]
    g_down = _mm_plain("mm_g_down", S["act"], dx2_b, "TN", [BF16])[0]
    dh2 = _mm_nt_colblocked("mm_d_h2", d_up, W["w_up"][l], F32)
    g_up = _mm_tn_colblocked_out("mm_g_up", S["h2"], d_up, BF16)
    dx1, dx1_b, dg_ffn = _rms_bwd(dh2, S["x1"], P["norm_ffn_g"][l:l + 1], dx2)
    return dx1, dx1_b, dict(w_up=g_up, w_down=g_down.reshape(N_DEV, F // N_DEV, D)), dg_ffn


def _bwd_mix_out(dx1_b, l, W, P, S, dims, dep=None):
    T, D, C, I, F = dims
    dm = _mm_plain("mm_d_merged", dx1_b, W["w_mix_out"][l].reshape(D, D), "NT", [BF16], dep=dep)[0]
    g_mix = _mm_plain("mm_g_mix", S["merged"], dx1_b, "TN", [BF16])[0]
    merge = _merge_bwd(S["u"], P["gate_b"][l], S["ya"], S["yb"], S["yc"], dm)
    return g_mix.reshape(N_DEV, D // N_DEV, D), merge


def _bwd_mixers(dx1, merge, dg_ffn, g_mix, l, W, P, S, dims, dep, send_mid, send_in):
    T, D, C, I, F = dims
    d_ya, d_yb, d_yc, dg0, dg1, dg2, d_gate_b = merge
    d_a3 = _mm_branch("mm_d_branch", d_ya, W["w_conf_out"][l], "NT", BF16, dep=dep)
    d_s2 = _mm_branch("mm_d_branch", d_yb, W["w_sconv_out"][l], "NT", BF16)
    d_o = _mm_branch("mm_d_branch", d_yc, W["w_attn_out"][l], "NT", BF16)
    g_conf = _mm_branch_grad("mm_g_branch", S["a3"], d_ya, BF16)
    g_sconv = _mm_branch_grad("mm_g_branch", S["s2"], d_yb, BF16)
    g_attn = _mm_branch_grad("mm_g_branch", S["o"], d_yc, BF16)
    tok = send_mid(dict(w_mix_out=g_mix, w_conf_out=g_conf, w_sconv_out=g_sconv, w_attn_out=g_attn), g_attn)
    d_a1, d_ln_g, d_ln_b = _ln_silu_bwd(S["a1"], _after(P["conf_ln_g"][l:l + 1], [tok]), P["conf_ln_b"][l:l + 1], d_a3)
    d_av, d_ag, d_bg, d_cg, d_bh, d_conf_dw, d_sconv_w = _convs_bwd(S["u"], P["conf_dw"][l], P["sconv_w"][l], d_a1, d_s2, C)
    dq, dk, dv, d_sinks = _attn_bwd(S["u"], S["o"], S["lse"], d_o, P["sinks"][l], C)
    du = _concat_columns([d_av, d_ag, d_bg, d_cg, d_bh, dq, dk, dv, dg0, dg1, dg2])
    g_in = _mm_plain("mm_g_in", du, S["h"], "TN", [BF16], tm=1280)[0]
    tok = send_in(dict(w_in=g_in.reshape(N_DEV, I // N_DEV, D)), g_in)
    dh = _mm_plain("mm_d_h", du, W["w_in"][l].reshape(I, D), "NN", [F32], tk=2560, dep=tok)[0]
    dx, dx_b, dg_mix = _rms_bwd(dh, S["xc"], P["norm_mix_g"][l:l + 1], dx1)
    small = dict(norm_mix_g=dg_mix, gate_b=d_gate_b, conf_dw=d_conf_dw, conf_ln_g=d_ln_g, conf_ln_b=d_ln_b,
                 sconv_w=d_sconv_w, sinks=d_sinks[0:1], norm_ffn_g=dg_ffn)
    return dx, dx_b, small


BIG = ("w_in", "w_conf_out", "w_sconv_out", "w_attn_out", "w_mix_out", "w_up", "w_down")
MLP_WEIGHTS = ("w_down", "w_up")
MID_WEIGHTS = ("w_mix_out", "w_conf_out", "w_sconv_out", "w_attn_out")
IN_WEIGHTS = ("w_in",)
SMALL_PER_LAYER = ("norm_mix_g", "gate_b", "conf_dw", "conf_ln_g", "conf_ln_b", "sconv_w", "sinks", "norm_ffn_g")
WEIGHTS = ("norm_mix_g", "w_in", "gate_b", "conf_dw", "conf_ln_g", "conf_ln_b", "w_conf_out", "sconv_w", "w_sconv_out",
           "sinks", "w_attn_out", "w_mix_out", "norm_ffn_g", "w_up", "w_down", "final_g")


SUBLANES = 8


def _nrows(n_el, width):
    per_tile = SUBLANES * width
    return SUBLANES * (-(-n_el // per_tile))


def _rows(a, width):
    flat = a.reshape(-1)
    nrow = _nrows(flat.shape[0], width)
    return jnp.pad(flat, (0, nrow * width - flat.shape[0])).reshape(nrow, width)


def _as3d(a):
    if a.ndim == 1:
        return a.reshape(1, 1, -1)
    if a.ndim == 2:
        return a.reshape(1, *a.shape)
    return a


def kernel(x, norm_mix_g, w_in, gate_b, conf_dw, conf_ln_g, conf_ln_b, w_conf_out, sconv_w, w_sconv_out, sinks, w_attn_out, w_mix_out, norm_ffn_g, w_up, w_down, final_g, loss_target, m_norm_mix_g, m_w_in, m_gate_b, m_conf_dw, m_conf_ln_g, m_conf_ln_b, m_w_conf_out, m_sconv_w, m_w_sconv_out, m_sinks, m_w_attn_out, m_w_mix_out, m_norm_ffn_g, m_w_up, m_w_down, m_final_g, v_norm_mix_g, v_w_in, v_gate_b, v_conf_dw, v_conf_ln_g, v_conf_ln_b, v_w_conf_out, v_sconv_w, v_w_sconv_out, v_sinks, v_w_attn_out, v_w_mix_out, v_norm_ffn_g, v_w_up, v_w_down, v_final_g):
    w = dict(norm_mix_g=norm_mix_g, w_in=w_in, gate_b=gate_b, conf_dw=conf_dw, conf_ln_g=conf_ln_g, conf_ln_b=conf_ln_b,
             w_conf_out=w_conf_out, sconv_w=sconv_w, w_sconv_out=w_sconv_out, sinks=sinks, w_attn_out=w_attn_out,
             w_mix_out=w_mix_out, norm_ffn_g=norm_ffn_g, w_up=w_up, w_down=w_down, final_g=final_g)
    mom = dict(norm_mix_g=m_norm_mix_g, w_in=m_w_in, gate_b=m_gate_b, conf_dw=m_conf_dw, conf_ln_g=m_conf_ln_g,
               conf_ln_b=m_conf_ln_b, w_conf_out=m_w_conf_out, sconv_w=m_sconv_w, w_sconv_out=m_w_sconv_out,
               sinks=m_sinks, w_attn_out=m_w_attn_out, w_mix_out=m_w_mix_out, norm_ffn_g=m_norm_ffn_g, w_up=m_w_up,
               w_down=m_w_down, final_g=m_final_g)
    var = dict(norm_mix_g=v_norm_mix_g, w_in=v_w_in, gate_b=v_gate_b, conf_dw=v_conf_dw, conf_ln_g=v_conf_ln_g,
               conf_ln_b=v_conf_ln_b, w_conf_out=v_w_conf_out, sconv_w=v_sconv_w, w_sconv_out=v_w_sconv_out,
               sinks=v_sinks, w_attn_out=v_w_attn_out, w_mix_out=v_w_mix_out, norm_ffn_g=v_norm_ffn_g, w_up=v_w_up,
               w_down=v_w_down, final_g=v_final_g)

    _, T, D = x.shape
    L = w_in.shape[0]
    C = D // 2
    I = w_in.shape[2] * N_DEV
    F = w_up.shape[2] * N_DEV
    dims = (T, D, C, I, F)
    my_block = _block_of(_me())

    w["w_in"], mom["w_in"], var["w_in"] = (jnp.swapaxes(a, 1, 2) for a in (w_in, m_w_in, v_w_in))

    shard_names = ("gate_b", "conf_dw", "sconv_w")
    packed = jnp.concatenate([_rows(w[k], LANES) for k in shard_names], axis=0)
    gathered = _all_gather("gather_small", [packed])[0]

    pos = jnp.stack(_me()).astype(jnp.int32)
    blk = my_block.reshape(1).astype(jnp.int32)
    W = {k: [_cast_place(w[k], l, blk) for l in range(L)] for k in BIG}
    n_near, n_far = len(BIG) * len(NEAR), len(BIG) * 2

    def gather_near(l, after):
        return _copies_start(f"gather_near_start_{l}", [], [W[k][l] for k in BIG], _gather_plan_near, n_near, after)

    def gather_far(l, g, after):
        lands = _copies_wait(f"gather_near_wait_{l}", g["near"], g["lands"], after)
        g["far"], lands, tok_far = _copies_start(f"gather_far_start_{l}", [], lands, _gather_plan_far, n_far)
        g["pass_near"], g["lands"], tok_pass = _copies_start(
            f"pass_near_start_{l}", [], lands, _pass_on_plan((X_NEIGHBOUR, Y_NEIGHBOUR)), 2 * len(BIG))
        return tok_far + tok_pass

    def gather_finish(l, g, after):
        lands = _copies_wait(f"gather_far_wait_{l}", g["far"], g["lands"], after)
        pass_far, lands, _ = _copies_start(f"pass_far_start_{l}", [], lands,
                                           _pass_on_plan((X_NEIGHBOUR | Y_NEIGHBOUR,)), len(BIG))
        lands = _copies_wait(f"pass_near_wait_{l}", g["pass_near"], lands, after)
        return _copies_wait(f"pass_far_wait_{l}", pass_far, lands, after)

    gathering = {}
    gathering["near"], gathering["lands"], tok = gather_near(0, gathered)
    later_layers = [W[k][l] for l in range(1, L) for k in BIG]
    tok = gather_far(0, gathering, [tok, *later_layers])

    P = dict(norm_mix_g=norm_mix_g, conf_ln_g=conf_ln_g, conf_ln_b=conf_ln_b, sinks=sinks, norm_ffn_g=norm_ffn_g)
    row0 = 0
    for k in shard_names:
        n_el = w[k].size
        nrow = _nrows(n_el, LANES)
        part = gathered[:, row0:row0 + nrow].reshape(N_DEV, -1)[:, :n_el].reshape(N_DEV, *w[k].shape)
        P[k] = jnp.moveaxis(part, 0, 2).reshape(*w[k].shape[:2], N_DEV * w[k].shape[2])
        row0 += nrow

    xc = x.reshape(T, D)
    saved = []
    for l in range(L):
        lands = gather_finish(l, gathering, xc if l else tok)
        for k, g in zip(BIG, lands):
            W[k][l] = g
        deps, mid_layer = (), None
        if l + 1 < L:
            gathering = {}
            gathering["near"], gathering["lands"], tok = gather_near(l + 1, lands[0])
            deps = (tok,)
            mid_layer = functools.partial(gather_far, l + 1, gathering)
        xc, S = _layer_fwd(xc, l, W, P, dims, deps, mid_layer)
        saved.append(S)
    dx, dx_b, d_final_g, loss_tile = _loss_head(xc, final_g.reshape(1, D), loss_target.reshape(T, D))

    own_all = {k: lax.empty((L, *W[k][0].shape[1:]), F32) for k in BIG}
    recv = {k: lax.empty((L, N_CHIPS - 1, *W[k][0].shape[1:]), BF16) for k in BIG}
    scatters = []

    def to_sibling(names, grads, l, after):
        part4 = [grads[k].reshape(N_CHIPS, 2, *grads[k].shape[1:]) for k in names]
        zone = [lax.empty((N_CHIPS, 1, *p.shape[2:]), BF16) for p in part4]
        fl, zone, token = _copies_start(f"sibling_start_{l}_{names[0]}", part4, zone, _sibling_plan, len(names), after)
        return dict(names=names, l=l, part4=part4, flight=fl, zone=zone), token

    def to_owners(group, after):
        names, l = group["names"], group["l"]
        sib4 = _copies_wait(f"sibling_wait_{l}_{names[0]}", group["flight"], group["zone"], after)
        chip_parts = []
        for k, p4, s4 in zip(names, group["part4"], sib4):
            cp, own_all[k] = _chip_sum(p4, s4, own_all[k], l, pos)
            chip_parts.append(cp)
        fl, zone, token = _copies_start(f"scatter_start_{l}_{names[0]}", chip_parts, [recv[k] for k in names],
                                        _scatter_plan(l), len(names) * len(OTHER_CHIPS))
        for k, g in zip(names, zone):
            recv[k] = g
        scatters.append((f"scatter_wait_{l}_{names[0]}", fl, names, l))
        return token

    small_grads = [None] * L
    dep, groups = None, {}
    for l in reversed(range(L)):
        S = saved[l]
        dx1, dx1_b, g_mlp, dg_ffn = _bwd_mlp(dx, dx_b, l, W, P, S, dims, dep)
        groups["mlp"], dep = to_sibling(MLP_WEIGHTS, g_mlp, l, dx1)
        if "in" in groups:
            dep = dep + to_owners(groups["in"], dx1)
        g_mix, merge = _bwd_mix_out(dx1_b, l, W, P, S, dims, dep)
        dep = to_owners(groups["mlp"], merge[0])

        def send_mid(grads, after, l=l):
            groups["mid"], token = to_sibling(MID_WEIGHTS, grads, l, after)
            return token

        def send_in(grads, after, l=l):
            token = to_owners(groups["mid"], after)
            groups["in"], token2 = to_sibling(IN_WEIGHTS, grads, l, after)
            return token + token2

        dx, dx_b, small_grads[l] = _bwd_mixers(dx1, merge, dg_ffn, g_mix, l, W, P, S, dims, dep, send_mid, send_in)
    last_start = to_owners(groups["in"], dx)

    width = LANES
    pieces = [_rows(small_grads[l][k], width) for l in range(L) for k in SMALL_PER_LAYER]
    pieces += [_rows(d_final_g, width), _rows(loss_tile[0:1, 0:1], width)]
    partial = jnp.concatenate(pieces, axis=0)
    everyone = lax.dynamic_update_slice(lax.empty((N_DEV, *partial.shape), F32), partial[None], (my_block, 0, 0))
    small_flight, (everyone,), token = _copies_start("small_grads_start", [], [everyone], _broadcast_plan, N_DEV - 1,
                                                     after=last_start)
    last_start = last_start + token

    def await_scatters(layers):
        for name, fl, names, l in scatters:
            if l in layers:
                for k, g in zip(names, _copies_wait(name, fl, [recv[k] for k in names], [dx, last_start])):
                    recv[k] = g

    await_scatters(range(1, L))
    done = {k: _sum4_adamw(own_all[k], recv[k], w[k], mom[k], var[k], 1, L - 1) for k in BIG} if L > 1 else {}
    await_scatters([0])

    grads, delta, new_m, new_v = {}, {}, {}, {}
    for k in BIG:
        grads[k], delta[k], new_m[k], new_v[k] = _sum4_adamw(own_all[k], recv[k], w[k], mom[k], var[k], 0, 1,
                                                             done.get(k))

    everyone = _copies_wait("small_grads_wait", small_flight, [everyone], [grads[k] for k in BIG])[0]
    total = _sum8(everyone.reshape(1, *everyone.shape))[0]
    row0 = 0
    per_layer = {k: [] for k in SMALL_PER_LAYER}
    for l in range(L):
        for k in SMALL_PER_LAYER:
            shape = small_grads[l][k].shape
            n_el = small_grads[l][k].size
            nrow = _nrows(n_el, width)
            per_layer[k].append(total[row0:row0 + nrow].reshape(-1)[:n_el].reshape(shape))
            row0 += nrow
    nrow = _nrows(D, width)
    grads["final_g"] = total[row0:row0 + nrow].reshape(-1)[:D]
    row0 += nrow
    loss = total[row0, 0]
    for k in SMALL_PER_LAYER:
        full = jnp.stack(per_layer[k], axis=0)
        if k in shard_names:
            shard = w[k].shape[2]
            full = lax.dynamic_slice_in_dim(full, my_block * shard, shard, axis=2)
        grads[k] = full.reshape(w[k].shape)

    for out in (grads, delta, new_m, new_v):
        out["w_in"] = jnp.swapaxes(out["w_in"], 1, 2)
    for k in WEIGHTS:
        if k in BIG:
            continue
        d, mn, vn = _adamw(_as3d(w[k]), _as3d(grads[k]), _as3d(mom[k]), _as3d(var[k]))
        delta[k], new_m[k], new_v[k] = d.reshape(w[k].shape), mn.reshape(w[k].shape), vn.reshape(w[k].shape)

    return (loss, dx.reshape(1, T, D), *[grads[k] for k in WEIGHTS], *[delta[k] for k in WEIGHTS],
            *[new_m[k] for k in WEIGHTS], *[new_v[k] for k in WEIGHTS])
```

```python
import functools

import jax
import jax.numpy as jnp
from jax import lax
from jax.experimental import pallas as pl
from jax.experimental.pallas import tpu as pltpu

F32 = jnp.float32
BF16 = jnp.bfloat16

N_DEV = 8
HEAD_DIM = 64
N_KV_HEADS = 4
ATTN_BLOCK = 128
CONF_KERNEL = 31
SCONV_KERNEL = 3
N_BRANCH = 3
RMS_EPS = 1e-6
LN_EPS = 1e-5
ADAM_LR = 0.001
ADAM_B1 = 0.9
ADAM_B2 = 0.999
ADAM_EPS = 1e-08
ADAM_WD = 0.01
ADAM_STEP = 10
LANES = 128
NEG_BIG = -1e30
VMEM_LIMIT_BYTES = 56 * 1024 * 1024
MESH = pl.DeviceIdType.MESH

NN = (((1,), (0,)), ((), ()))
NT = (((1,), (1,)), ((), ()))
TN = (((0,), (0,)), ((), ()))


def _pick(n, cap, mult=LANES):
    best = None
    for d in range(mult, min(n, cap) + 1, mult):
        if n % d == 0:
            best = d
    assert best is not None, (n, cap, mult)
    return best


def _sigmoid(x):
    return 0.5 * jnp.tanh(0.5 * x) + 0.5


def _params(sem):
    return pltpu.CompilerParams(dimension_semantics=sem, vmem_limit_bytes=VMEM_LIMIT_BYTES)


def _epi_cast(p, ex, outs):
    outs[0][...] = p.astype(outs[0].dtype)


def _epi_resid(p, ex, outs):
    outs[0][...] = ex[0][...] + p


def _epi_relu2(p, ex, outs):
    outs[0][...] = p.astype(outs[0].dtype)
    r = jnp.maximum(p, 0.0)
    outs[1][...] = (r * r).astype(outs[1].dtype)


def _epi_drelu2(p, ex, outs):
    up = ex[0][...].astype(F32)
    outs[0][...] = (p * (2.0 * jnp.maximum(up, 0.0))).astype(outs[0].dtype)


TOKEN = (8, LANES)


def _matmul(name, a, b, dnums, grid, a_spec, b_spec, out_shape, out_specs, epi, acc_shape, extra=(), extra_specs=(),
            dep=None):
    if dep is not None:
        extra = [*extra, dep]
        extra_specs = [*extra_specs, pl.BlockSpec(TOKEN, lambda j, i, k: (0, 0))]
    nk = grid[2]
    n_extra, n_out = len(extra), len(out_shape)

    def body(*refs):
        a_ref, b_ref = refs[0], refs[1]
        ex = refs[2:2 + n_extra]
        outs = refs[2 + n_extra:2 + n_extra + n_out]
        p = lax.dot_general(a_ref[...], b_ref[...], dnums, preferred_element_type=F32)
        if nk == 1:
            epi(p, ex, outs)
        else:
            acc = refs[-1]
            k = pl.program_id(2)

            @pl.when(k == 0)
            def _():
                acc[...] = p

            @pl.when(k > 0)
            def _():
                acc[...] += p

            @pl.when(k == nk - 1)
            def _():
                epi(acc[...], ex, outs)

    scratch = [pltpu.VMEM(acc_shape, F32)] if nk > 1 else []
    return pl.pallas_call(
        body, name=name, grid=grid, in_specs=[a_spec, b_spec, *extra_specs], out_specs=list(out_specs),
        out_shape=list(out_shape), scratch_shapes=scratch,
        compiler_params=_params(("parallel", "parallel", "arbitrary")))(a, b, *extra)


def _mm_plain(name, a, b, form, out_dtypes, epi=_epi_cast, extra=(), tm=1024, tn=1024, tk=2048, dep=None):
    if form == "NN":
        (M, K), N = a.shape, b.shape[1]
    elif form == "NT":
        (M, K), N = a.shape, b.shape[0]
    else:
        (K, M), N = a.shape, b.shape[1]
    tm, tn, tk = _pick(M, tm, 8), _pick(N, tn), _pick(K, tk)
    grid = (N // tn, M // tm, K // tk)
    if form == "TN":
        a_spec = pl.BlockSpec((tk, tm), lambda j, i, k: (k, i))
    else:
        a_spec = pl.BlockSpec((tm, tk), lambda j, i, k: (i, k))
    if form == "NT":
        b_spec = pl.BlockSpec((tn, tk), lambda j, i, k: (j, k))
    else:
        b_spec = pl.BlockSpec((tk, tn), lambda j, i, k: (k, j))
    o_spec = pl.BlockSpec((tm, tn), lambda j, i, k: (i, j))
    dn = {"NN": NN, "NT": NT, "TN": TN}[form]
    return _matmul(name, a, b, dn, grid, a_spec, b_spec,
                   [jax.ShapeDtypeStruct((M, N), dt) for dt in out_dtypes], [o_spec] * len(out_dtypes), epi,
                   (tm, tn), extra, [o_spec] * len(extra), dep)


def _mm_nn_colblocked(name, a, bb, out_dtypes, epi=_epi_cast, tm=1024, tn=1024, tk=2048):
    M, K = a.shape
    ns = bb.shape[2]
    N = N_DEV * ns
    tm, tn, tk = _pick(M, tm, 8), _pick(ns, tn), _pick(K, tk)
    q = ns // tn
    grid = (N // tn, M // tm, K // tk)
    a_spec = pl.BlockSpec((tm, tk), lambda j, i, k: (i, k))
    b_spec = pl.BlockSpec((None, tk, tn), lambda j, i, k: (j // q, k, j % q))
    o_spec = pl.BlockSpec((tm, tn), lambda j, i, k: (i, j))
    return _matmul(name, a, bb, NN, grid, a_spec, b_spec,
                   [jax.ShapeDtypeStruct((M, N), dt) for dt in out_dtypes], [o_spec] * len(out_dtypes), epi, (tm, tn))


def _mm_nt_colblocked(name, a, bb, out_dtype, tm=1024, tn=1024, tk=1024):
    M, N = a.shape
    K, ns = bb.shape[1], bb.shape[2]
    tm, tn, tk = _pick(M, tm, 8), _pick(K, tn), _pick(ns, tk)
    q = ns // tk
    grid = (K // tn, M // tm, N // tk)
    a_spec = pl.BlockSpec((tm, tk), lambda j, i, k: (i, k))
    b_spec = pl.BlockSpec((None, tn, tk), lambda j, i, k: (k // q, j, k % q))
    o_spec = pl.BlockSpec((tm, tn), lambda j, i, k: (i, j))
    return _matmul(name, a, bb, NT, grid, a_spec, b_spec, [jax.ShapeDtypeStruct((M, K), out_dtype)], [o_spec],
                   _epi_cast, (tm, tn))[0]


def _mm_tn_colblocked_out(name, a, b, out_dtype, tm=1024, tn=1024, tk=2048):
    T, M = a.shape
    N = b.shape[1]
    ns = N // N_DEV
    tm, tn, tk = _pick(M, tm, 8), _pick(ns, tn), _pick(T, tk)
    q = ns // tn
    grid = (N // tn, M // tm, T // tk)
    a_spec = pl.BlockSpec((tk, tm), lambda j, i, k: (k, i))
    b_spec = pl.BlockSpec((tk, tn), lambda j, i, k: (k, j))
    o_spec = pl.BlockSpec((None, tm, tn), lambda j, i, k: (j // q, i, j % q))
    return _matmul(name, a, b, TN, grid, a_spec, b_spec, [jax.ShapeDtypeStruct((N_DEV, M, ns), out_dtype)], [o_spec],
                   _epi_cast, (tm, tn))[0]


def _unblock(bb_ref, full_ref):
    ns = bb_ref.shape[2]
    for j in range(N_DEV):
        full_ref[:, j * ns:(j + 1) * ns] = bb_ref[j]


def _mm_branch(name, a, bb, form, out_dtype, tm=512, dep=None):
    M = a.shape[0]
    K, ns = bb.shape[1], bb.shape[2]
    N = N_DEV * ns
    tm = _pick(M, tm, 8)
    out_cols = N if form == "NN" else K
    deps = [] if dep is None else [dep]

    def body(a_ref, b_ref, *rest):
        o_ref, w_full = rest[len(deps):]

        @pl.when(pl.program_id(0) == 0)
        def _():
            _unblock(b_ref, w_full)

        o_ref[...] = lax.dot_general(a_ref[...], w_full[...], NN if form == "NN" else NT,
                                     preferred_element_type=F32).astype(out_dtype)

    return pl.pallas_call(
        body, name=name, grid=(M // tm,),
        in_specs=[pl.BlockSpec((tm, a.shape[1]), lambda i: (i, 0)), pl.BlockSpec(bb.shape, lambda i: (0, 0, 0)),
                  *[pl.BlockSpec(TOKEN, lambda i: (0, 0)) for _ in deps]],
        out_specs=pl.BlockSpec((tm, out_cols), lambda i: (i, 0)),
        out_shape=jax.ShapeDtypeStruct((M, out_cols), out_dtype),
        scratch_shapes=[pltpu.VMEM((K, N), BF16)],
        compiler_params=_params(("arbitrary",)))(a, bb, *deps)


def _mm_branch_grad(name, a, b, out_dtype, tm=256):
    T, M = a.shape
    N = b.shape[1]
    ns = N // N_DEV
    tm = _pick(M, tm, 8)

    def body(a_ref, b_ref, o_ref):
        p = lax.dot_general(a_ref[...], b_ref[...], TN, preferred_element_type=F32)
        for j in range(N_DEV):
            o_ref[j] = p[:, j * ns:(j + 1) * ns].astype(out_dtype)

    return pl.pallas_call(
        body, name=name, grid=(M // tm,),
        in_specs=[pl.BlockSpec((T, tm), lambda i: (0, i)), pl.BlockSpec((T, N), lambda i: (0, 0))],
        out_specs=pl.BlockSpec((N_DEV, tm, ns), lambda i: (0, i, 0)),
        out_shape=jax.ShapeDtypeStruct((N_DEV, M, ns), out_dtype),
        compiler_params=_params(("parallel",)))(a, b)


ROW_TILE = 256


def _rms_fwd(x, g):
    T, D = x.shape
    tr = _pick(T, ROW_TILE, 8)

    def body(x_ref, g_ref, h_ref):
        xv = x_ref[...]
        r = lax.rsqrt(jnp.mean(xv * xv, axis=-1, keepdims=True) + RMS_EPS)
        h_ref[...] = (xv * r * g_ref[...]).astype(BF16)

    return pl.pallas_call(
        body, name="rms_fwd", grid=(T // tr,),
        in_specs=[pl.BlockSpec((tr, D), lambda i: (i, 0)), pl.BlockSpec((1, D), lambda i: (0, 0))],
        out_specs=pl.BlockSpec((tr, D), lambda i: (i, 0)),
        out_shape=jax.ShapeDtypeStruct((T, D), BF16), compiler_params=_params(("parallel",)))(x, g)


def _rms_bwd_math(dh, xv, g):
    r = lax.rsqrt(jnp.mean(xv * xv, axis=-1, keepdims=True) + RMS_EPS)
    gdh = dh * g
    dot = jnp.mean(gdh * xv, axis=-1, keepdims=True)
    dx = r * gdh - xv * (r * r * r * dot)
    return dx, dh * xv * r


def _rms_bwd(dh, x, g, dres):
    T, D = x.shape
    tr = _pick(T, ROW_TILE, 8)

    def body(dh_ref, x_ref, g_ref, dres_ref, dx_ref, dxb_ref, dg_ref):
        dx, dgrow = _rms_bwd_math(dh_ref[...], x_ref[...], g_ref[...])
        dx = dx + dres_ref[...]
        dx_ref[...] = dx
        dxb_ref[...] = dx.astype(BF16)
        part = jnp.sum(dgrow, axis=0, keepdims=True)

        @pl.when(pl.program_id(0) == 0)
        def _():
            dg_ref[...] = part

        @pl.when(pl.program_id(0) > 0)
        def _():
            dg_ref[...] += part

    row = pl.BlockSpec((tr, D), lambda i: (i, 0))
    vec = pl.BlockSpec((1, D), lambda i: (0, 0))
    return pl.pallas_call(
        body, name="rms_bwd", grid=(T // tr,), in_specs=[row, row, vec, row], out_specs=[row, row, vec],
        out_shape=[jax.ShapeDtypeStruct((T, D), F32), jax.ShapeDtypeStruct((T, D), BF16),
                   jax.ShapeDtypeStruct((1, D), F32)],
        compiler_params=_params(("arbitrary",)))(dh, x, g, dres)


def _loss_head(x, g, target):
    T, D = x.shape
    tr = _pick(T, ROW_TILE, 8)

    def body(x_ref, g_ref, t_ref, dx_ref, dxb_ref, dg_ref, loss_ref):
        xv, gv = x_ref[...], g_ref[...]
        r = lax.rsqrt(jnp.mean(xv * xv, axis=-1, keepdims=True) + RMS_EPS)
        err = xv * r * gv - t_ref[...]
        part_loss = 0.5 * jnp.sum(jnp.mean(err * err, axis=-1, keepdims=True), axis=0, keepdims=True)
        dx, dgrow = _rms_bwd_math(err * (1.0 / D), xv, gv)
        dx_ref[...] = dx
        dxb_ref[...] = dx.astype(BF16)
        part = jnp.sum(dgrow, axis=0, keepdims=True)
        lpart = jnp.broadcast_to(part_loss, (8, LANES))

        @pl.when(pl.program_id(0) == 0)
        def _():
            dg_ref[...] = part
            loss_ref[...] = lpart

        @pl.when(pl.program_id(0) > 0)
        def _():
            dg_ref[...] += part
            loss_ref[...] += lpart

    row = pl.BlockSpec((tr, D), lambda i: (i, 0))
    vec = pl.BlockSpec((1, D), lambda i: (0, 0))
    lsp = pl.BlockSpec((8, LANES), lambda i: (0, 0))
    return pl.pallas_call(
        body, name="loss_head", grid=(T // tr,), in_specs=[row, vec, row], out_specs=[row, row, vec, lsp],
        out_shape=[jax.ShapeDtypeStruct((T, D), F32), jax.ShapeDtypeStruct((T, D), BF16),
                   jax.ShapeDtypeStruct((1, D), F32), jax.ShapeDtypeStruct((8, LANES), F32)],
        compiler_params=_params(("arbitrary",)))(x, g, target)


def _ln_math(a1, g, b):
    mu = jnp.mean(a1, axis=-1, keepdims=True)
    xc = a1 - mu
    rstd = lax.rsqrt(jnp.mean(xc * xc, axis=-1, keepdims=True) + LN_EPS)
    xhat = xc * rstd
    return xhat, rstd, xhat * g + b


def _ln_silu(a1, g, b):
    T, C = a1.shape
    tr = _pick(T, ROW_TILE, 8)

    def body(a_ref, g_ref, b_ref, o_ref):
        _, _, y = _ln_math(a_ref[...], g_ref[...], b_ref[...])
        o_ref[...] = (y * _sigmoid(y)).astype(BF16)

    row = pl.BlockSpec((tr, C), lambda i: (i, 0))
    vec = pl.BlockSpec((1, C), lambda i: (0, 0))
    return pl.pallas_call(body, name="ln_silu", grid=(T // tr,), in_specs=[row, vec, vec], out_specs=row,
                          out_shape=jax.ShapeDtypeStruct((T, C), BF16), compiler_params=_params(("parallel",)))(a1, g, b)


def _ln_silu_bwd(a1, g, b, d_a3):
    T, C = a1.shape
    tr = _pick(T, ROW_TILE, 8)

    def body(a_ref, g_ref, b_ref, d_ref, da_ref, dg_ref, db_ref):
        gv = g_ref[...]
        xhat, rstd, y = _ln_math(a_ref[...], gv, b_ref[...])
        s = _sigmoid(y)
        dy = d_ref[...].astype(F32) * (s * (1.0 + y * (1.0 - s)))
        dxh = dy * gv
        m1 = jnp.mean(dxh, axis=-1, keepdims=True)
        m2 = jnp.mean(dxh * xhat, axis=-1, keepdims=True)
        da_ref[...] = rstd * (dxh - m1 - xhat * m2)
        pg = jnp.sum(dy * xhat, axis=0, keepdims=True)
        pb = jnp.sum(dy, axis=0, keepdims=True)

        @pl.when(pl.program_id(0) == 0)
        def _():
            dg_ref[...] = pg
            db_ref[...] = pb

        @pl.when(pl.program_id(0) > 0)
        def _():
            dg_ref[...] += pg
            db_ref[...] += pb

    row = pl.BlockSpec((tr, C), lambda i: (i, 0))
    vec = pl.BlockSpec((1, C), lambda i: (0, 0))
    return pl.pallas_call(
        body, name="ln_silu_bwd", grid=(T // tr,), in_specs=[row, vec, vec, row], out_specs=[row, vec, vec],
        out_shape=[jax.ShapeDtypeStruct((T, C), F32), jax.ShapeDtypeStruct((1, C), F32),
                   jax.ShapeDtypeStruct((1, C), F32)],
        compiler_params=_params(("arbitrary",)))(a1, g, b, d_a3)


CONV_ROWS = 128
PAD_A = 32
PAD_B = 8


def _u_block(T, first):
    return pl.BlockSpec((T, LANES), lambda i: (0, first + i))


def _causal_conv(xpad_ref, w_ref, ksize, pad, T, emit):
    for r0 in range(0, T, CONV_ROWS):
        acc = None
        for j in range(ksize):
            off = pad - (ksize - 1) + j + r0
            term = w_ref[j:j + 1, :] * xpad_ref[off:off + CONV_ROWS, :]
            acc = term if acc is None else acc + term
        emit(r0, acc)


def _anticausal_conv(gpad_ref, w_ref, ksize, T, emit):
    for r0 in range(0, T, CONV_ROWS):
        acc = None
        for j in range(ksize):
            off = (ksize - 1) - j + r0
            term = w_ref[j:j + 1, :] * gpad_ref[off:off + CONV_ROWS, :]
            acc = term if acc is None else acc + term
        emit(r0, acc)


def _conv_wgrad(xpad_ref, g_ref, dw_ref, ksize, pad, T):
    for j in range(ksize):
        acc = None
        for r0 in range(0, T, CONV_ROWS):
            off = pad - (ksize - 1) + j + r0
            term = g_ref[r0:r0 + CONV_ROWS, :] * xpad_ref[off:off + CONV_ROWS, :]
            term = jnp.sum(term.reshape(CONV_ROWS // 8, 8, LANES), axis=0)
            acc = term if acc is None else acc + term
        dw_ref[j:j + 1, :] = jnp.sum(acc, axis=0, keepdims=True)


def _convs_fwd(u, conf_dw, sconv_w, C):
    T = u.shape[0]
    nb = C // LANES

    def body(av_ref, ag_ref, bg_ref, cg_ref, bh_ref, dw_ref, sw_ref, a1_ref, s2_ref, xa, xs, s1):
        xa[0:PAD_A, :] = jnp.zeros((PAD_A, LANES), F32)
        xa[PAD_A:PAD_A + T, :] = av_ref[...].astype(F32) * _sigmoid(ag_ref[...].astype(F32))

        def emit_a(r0, acc):
            a1_ref[r0:r0 + CONV_ROWS, :] = acc

        _causal_conv(xa, dw_ref, CONF_KERNEL, PAD_A, T, emit_a)

        xs[0:PAD_B, :] = jnp.zeros((PAD_B, LANES), F32)
        xs[PAD_B:PAD_B + T, :] = cg_ref[...].astype(F32) * bh_ref[...].astype(F32)

        def emit_b(r0, acc):
            s1[r0:r0 + CONV_ROWS, :] = acc

        _causal_conv(xs, sw_ref, SCONV_KERNEL, PAD_B, T, emit_b)
        s2_ref[...] = (bg_ref[...].astype(F32) * s1[...]).astype(BF16)

    col = pl.BlockSpec((T, LANES), lambda i: (0, i))
    return pl.pallas_call(
        body, name="convs_fwd", grid=(nb,),
        in_specs=[_u_block(T, 0), _u_block(T, nb), _u_block(T, 2 * nb), _u_block(T, 3 * nb), _u_block(T, 4 * nb),
                  pl.BlockSpec((CONF_KERNEL, LANES), lambda i: (0, i)),
                  pl.BlockSpec((SCONV_KERNEL, LANES), lambda i: (0, i))],
        out_specs=[col, col],
        out_shape=[jax.ShapeDtypeStruct((T, C), F32), jax.ShapeDtypeStruct((T, C), BF16)],
        scratch_shapes=[pltpu.VMEM((T + PAD_A, LANES), F32), pltpu.VMEM((T + PAD_B, LANES), F32),
                        pltpu.VMEM((T, LANES), F32)],
        compiler_params=_params(("parallel",)))(u, u, u, u, u, conf_dw, sconv_w)


def _convs_bwd(u, conf_dw, sconv_w, d_a1, d_s2, C):
    T = u.shape[0]
    nb = C // LANES

    def body(*refs):
        _convs_bwd_block(*refs, T)

    col = pl.BlockSpec((T, LANES), lambda i: (0, i))
    wa = pl.BlockSpec((CONF_KERNEL, LANES), lambda i: (0, i))
    wb = pl.BlockSpec((SCONV_KERNEL, LANES), lambda i: (0, i))
    act = jax.ShapeDtypeStruct((T, C), BF16)
    return pl.pallas_call(
        body, name="convs_bwd", grid=(nb,),
        in_specs=[_u_block(T, 0), _u_block(T, nb), _u_block(T, 2 * nb), _u_block(T, 3 * nb), _u_block(T, 4 * nb),
                  wa, wb, col, col],
        out_specs=[col, col, col, col, col, wa, wb],
        out_shape=[act, act, act, act, act, jax.ShapeDtypeStruct((CONF_KERNEL, C), F32),
                   jax.ShapeDtypeStruct((SCONV_KERNEL, C), F32)],
        scratch_shapes=[pltpu.VMEM((T + PAD_A, LANES), F32), pltpu.VMEM((T + PAD_A, LANES), F32),
                        pltpu.VMEM((T + PAD_B, LANES), F32), pltpu.VMEM((T + PAD_B, LANES), F32),
                        pltpu.VMEM((T, LANES), F32)],
        compiler_params=_params(("parallel",)))(u, u, u, u, u, conf_dw, sconv_w, d_a1, d_s2)


def _convs_bwd_block(av_ref, ag_ref, bg_ref, cg_ref, bh_ref, dw_ref, sw_ref, da1_ref, ds2_ref,
                     dav_ref, dag_ref, dbg_ref, dcg_ref, dbh_ref, ddw_ref, dsw_ref, xa, ga, xs, gs, tmp, T):
    def to_tmp(r0, acc):
        tmp[r0:r0 + CONV_ROWS, :] = acc

    av = av_ref[...].astype(F32)
    sg = _sigmoid(ag_ref[...].astype(F32))
    xa[0:PAD_A, :] = jnp.zeros((PAD_A, LANES), F32)
    xa[PAD_A:PAD_A + T, :] = av * sg
    ga[0:T, :] = da1_ref[...]
    ga[T:T + PAD_A, :] = jnp.zeros((PAD_A, LANES), F32)
    _conv_wgrad(xa, ga, ddw_ref, CONF_KERNEL, PAD_A, T)
    _anticausal_conv(ga, dw_ref, CONF_KERNEL, T, to_tmp)
    da0 = tmp[...]
    dav_ref[...] = (da0 * sg).astype(BF16)
    dag_ref[...] = (da0 * av * sg * (1.0 - sg)).astype(BF16)

    cg = cg_ref[...].astype(F32)
    bh = bh_ref[...].astype(F32)
    ds2 = ds2_ref[...].astype(F32)
    xs[0:PAD_B, :] = jnp.zeros((PAD_B, LANES), F32)
    xs[PAD_B:PAD_B + T, :] = cg * bh
    _causal_conv(xs, sw_ref, SCONV_KERNEL, PAD_B, T, to_tmp)
    dbg_ref[...] = (ds2 * tmp[...]).astype(BF16)
    gs[0:T, :] = ds2 * bg_ref[...].astype(F32)
    gs[T:T + PAD_B, :] = jnp.zeros((PAD_B, LANES), F32)
    _conv_wgrad(xs, gs, dsw_ref, SCONV_KERNEL, PAD_B, T)
    _anticausal_conv(gs, sw_ref, SCONV_KERNEL, T, to_tmp)
    ds0 = tmp[...]
    dcg_ref[...] = (ds0 * bh).astype(BF16)
    dbh_ref[...] = (ds0 * cg).astype(BF16)


def _attn_specs(C):
    kvw = N_KV_HEADS * HEAD_DIM
    kb = (5 * C + C) // kvw
    qs = pl.BlockSpec((ATTN_BLOCK, C), lambda n: (n, 5))
    kc = pl.BlockSpec((ATTN_BLOCK, kvw), lambda n: (n, kb))
    kp = pl.BlockSpec((ATTN_BLOCK, kvw), lambda n: (jnp.maximum(n - 1, 0), kb))
    vc = pl.BlockSpec((ATTN_BLOCK, kvw), lambda n: (n, kb + 1))
    vp = pl.BlockSpec((ATTN_BLOCK, kvw), lambda n: (jnp.maximum(n - 1, 0), kb + 1))
    return qs, kc, kp, vc, vp


def _attn_masks(n, rows):
    row = lax.broadcasted_iota(jnp.int32, (rows, ATTN_BLOCK), 0) % ATTN_BLOCK
    col = lax.broadcasted_iota(jnp.int32, (rows, ATTN_BLOCK), 1)
    from_cur = col <= row
    return from_cur, jnp.logical_or(from_cur, n > 0)


def _stack_heads(ref, heads, width=HEAD_DIM):
    return jnp.concatenate([ref[:, h * width:(h + 1) * width] for h in heads], axis=0)


def _unstack_heads(t, count):
    return [t[j * ATTN_BLOCK:(j + 1) * ATTN_BLOCK] for j in range(count)]


def _attn_scores(qh, kc, kp, from_cur, valid):
    qs = qh * (HEAD_DIM ** -0.5)
    s_c = lax.dot_general(qs, kc, NT, preferred_element_type=F32)
    s_p = lax.dot_general(qs, kp, NT, preferred_element_type=F32)
    return jnp.where(valid, jnp.where(from_cur, s_c, s_p), NEG_BIG)


def _attn_split(t, from_cur):
    t = t.astype(BF16)
    zero = jnp.zeros_like(t)
    return jnp.where(from_cur, t, zero), jnp.where(from_cur, zero, t)


def _attn_fwd(u, sinks, C):
    T = u.shape[0]
    H = C // HEAD_DIM
    grp = H // N_KV_HEADS

    def body(sink_ref, q_ref, kc_ref, kp_ref, vc_ref, vp_ref, o_ref, lse_ref):
        n = pl.program_id(0)
        from_cur, valid = _attn_masks(n, grp * ATTN_BLOCK)
        outs, lses = [], []
        for g in range(N_KV_HEADS):
            kv = slice(g * HEAD_DIM, (g + 1) * HEAD_DIM)
            heads = range(g * grp, (g + 1) * grp)
            sink = jnp.concatenate([jnp.full((ATTN_BLOCK, 1), sink_ref[h], F32) for h in heads], axis=0)
            s = _attn_scores(_stack_heads(q_ref, heads), kc_ref[:, kv], kp_ref[:, kv], from_cur, valid)
            m = jnp.maximum(jnp.max(s, axis=-1, keepdims=True), sink)
            p = jnp.exp(s - m)
            den = jnp.sum(p, axis=-1, keepdims=True) + jnp.exp(sink - m)
            p_c, p_p = _attn_split(p, from_cur)
            acc = jnp.dot(p_c, vc_ref[:, kv], preferred_element_type=F32)
            acc = acc + jnp.dot(p_p, vp_ref[:, kv], preferred_element_type=F32)
            outs += _unstack_heads((acc / den).astype(BF16), grp)
            lses += _unstack_heads(m + jnp.log(den), grp)
        o_ref[...] = jnp.concatenate(outs, axis=1)
        lse_ref[...] = jnp.concatenate(lses, axis=1)

    qs, kc, kp, vc, vp = _attn_specs(C)
    return pl.pallas_call(
        body, name="attn_fwd", grid=(T // ATTN_BLOCK,),
        in_specs=[pl.BlockSpec(memory_space=pltpu.SMEM), qs, kc, kp, vc, vp],
        out_specs=[pl.BlockSpec((ATTN_BLOCK, C), lambda n: (n, 0)), pl.BlockSpec((ATTN_BLOCK, H), lambda n: (n, 0))],
        out_shape=[jax.ShapeDtypeStruct((T, C), BF16), jax.ShapeDtypeStruct((T, H), F32)],
        compiler_params=_params(("parallel",)))(sinks, u, u, u, u, u)


def _attn_bwd(u, o, lse, d_o, sinks, C):
    T = u.shape[0]
    H = C // HEAD_DIM
    grp = H // N_KV_HEADS
    kvw = N_KV_HEADS * HEAD_DIM
    nblk = T // ATTN_BLOCK
    scale = HEAD_DIM ** -0.5

    def body(sink_ref, q_ref, kc_ref, kp_ref, vc_ref, vp_ref, o_ref, lse_ref, do_ref,
             dq_ref, dk_ref, dv_ref, ds_ref, dk_acc, dv_acc):
        n = pl.program_id(0)

        @pl.when(n == 0)
        def _():
            dk_acc[...] = jnp.zeros_like(dk_acc)
            dv_acc[...] = jnp.zeros_like(dv_acc)
            ds_ref[...] = jnp.zeros_like(ds_ref)

        from_cur, valid = _attn_masks(n, grp * ATTN_BLOCK)
        cur = pl.ds(pl.multiple_of(n * ATTN_BLOCK, ATTN_BLOCK), ATTN_BLOCK)
        prev = pl.ds(pl.multiple_of(jnp.maximum(n - 1, 0) * ATTN_BLOCK, ATTN_BLOCK), ATTN_BLOCK)
        dqs, dsinks, dk_cs, dk_ps, dv_cs, dv_ps = [], [], [], [], [], []
        for g in range(N_KV_HEADS):
            kv = slice(g * HEAD_DIM, (g + 1) * HEAD_DIM)
            kc, kp, vc, vp = kc_ref[:, kv], kp_ref[:, kv], vc_ref[:, kv], vp_ref[:, kv]
            heads = range(g * grp, (g + 1) * grp)
            qg, dog, og = _stack_heads(q_ref, heads), _stack_heads(do_ref, heads), _stack_heads(o_ref, heads)
            lse_g = _stack_heads(lse_ref, heads, 1)
            sink = jnp.concatenate([jnp.full((ATTN_BLOCK, 1), sink_ref[h], F32) for h in heads], axis=0)
            p = jnp.exp(_attn_scores(qg, kc, kp, from_cur, valid) - lse_g)
            delta = jnp.sum(dog.astype(F32) * og.astype(F32), axis=-1, keepdims=True)
            dp = jnp.where(from_cur, lax.dot_general(dog, vc, NT, preferred_element_type=F32),
                           lax.dot_general(dog, vp, NT, preferred_element_type=F32))
            ds_c, ds_p = _attn_split(p * (dp - delta) * scale, from_cur)
            p_c, p_p = _attn_split(p, from_cur)
            dq = jnp.dot(ds_c, kc, preferred_element_type=F32) + jnp.dot(ds_p, kp, preferred_element_type=F32)
            dqs += _unstack_heads(dq.astype(BF16), grp)
            dk_cs.append(lax.dot_general(ds_c, qg, TN, preferred_element_type=F32))
            dk_ps.append(lax.dot_general(ds_p, qg, TN, preferred_element_type=F32))
            dv_cs.append(lax.dot_general(p_c, dog, TN, preferred_element_type=F32))
            dv_ps.append(lax.dot_general(p_p, dog, TN, preferred_element_type=F32))
            for t in _unstack_heads(jnp.exp(sink - lse_g) * delta, grp):
                dsinks.append(jnp.broadcast_to(-jnp.sum(t, axis=0, keepdims=True), (8, 1)))
        dq_ref[...] = jnp.concatenate(dqs, axis=1)
        ds_ref[...] += jnp.concatenate(dsinks, axis=1)
        dk_acc[cur, :] += jnp.concatenate(dk_cs, axis=1)
        dk_acc[prev, :] += jnp.concatenate(dk_ps, axis=1)
        dv_acc[cur, :] += jnp.concatenate(dv_cs, axis=1)
        dv_acc[prev, :] += jnp.concatenate(dv_ps, axis=1)

        @pl.when(n == nblk - 1)
        def _():
            dk_ref[...] = dk_acc[...].astype(BF16)
            dv_ref[...] = dv_acc[...].astype(BF16)

    qs, kc, kp, vc, vp = _attn_specs(C)
    blk = pl.BlockSpec((ATTN_BLOCK, C), lambda n: (n, 0))
    full = pl.BlockSpec((T, kvw), lambda n: (0, 0))
    return pl.pallas_call(
        body, name="attn_bwd", grid=(nblk,),
        in_specs=[pl.BlockSpec(memory_space=pltpu.SMEM), qs, kc, kp, vc, vp, blk,
                  pl.BlockSpec((ATTN_BLOCK, H), lambda n: (n, 0)), blk],
        out_specs=[blk, full, full, pl.BlockSpec((8, H), lambda n: (0, 0))],
        out_shape=[jax.ShapeDtypeStruct((T, C), BF16), jax.ShapeDtypeStruct((T, kvw), BF16),
                   jax.ShapeDtypeStruct((T, kvw), BF16), jax.ShapeDtypeStruct((8, H), F32)],
        scratch_shapes=[pltpu.VMEM((T, kvw), F32), pltpu.VMEM((T, kvw), F32)],
        compiler_params=_params(("arbitrary",)))(sinks, u, u, u, u, u, o, lse, d_o)


MERGE_COLS = 512


def _merge_specs(T, D, I):
    tr = _pick(T, ROW_TILE, 8)
    tc = _pick(D, MERGE_COLS)
    g0 = (I - N_BRANCH * D) // tc
    per = D // tc
    gspecs = [pl.BlockSpec((tr, tc), functools.partial(lambda j, i, b: (i, g0 + b * per + j), b=b)) for b in range(N_BRANCH)]
    tile = pl.BlockSpec((tr, tc), lambda j, i: (i, j))
    bias = pl.BlockSpec((N_BRANCH, tc), lambda j, i: (0, j))
    return tr, tc, gspecs, tile, bias


def _merge_fwd(u, gate_b, ya, yb, yc):
    T, I = u.shape
    D = ya.shape[1]
    tr, tc, gspecs, tile, bias = _merge_specs(T, D, I)

    def body(g0_ref, g1_ref, g2_ref, b_ref, ya_ref, yb_ref, yc_ref, o_ref):
        acc = None
        for b, (g_ref, y_ref) in enumerate(((g0_ref, ya_ref), (g1_ref, yb_ref), (g2_ref, yc_ref))):
            gate = _sigmoid(g_ref[...].astype(F32) + b_ref[b:b + 1, :])
            term = gate * y_ref[...].astype(F32)
            acc = term if acc is None else acc + term
        o_ref[...] = acc.astype(BF16)

    return pl.pallas_call(
        body, name="merge_fwd", grid=(D // tc, T // tr), in_specs=[*gspecs, bias, tile, tile, tile], out_specs=tile,
        out_shape=jax.ShapeDtypeStruct((T, D), BF16),
        compiler_params=_params(("parallel", "parallel")))(u, u, u, gate_b, ya, yb, yc)


def _merge_bwd(u, gate_b, ya, yb, yc, dm):
    T, I = u.shape
    D = ya.shape[1]
    tr, tc, gspecs, tile, bias = _merge_specs(T, D, I)

    def body(g0_ref, g1_ref, g2_ref, b_ref, ya_ref, yb_ref, yc_ref, dm_ref,
             dya_ref, dyb_ref, dyc_ref, dg0_ref, dg1_ref, dg2_ref, db_ref):
        dmv = dm_ref[...].astype(F32)
        first = pl.program_id(1) == 0
        for b, (g_ref, y_ref, dy_ref, dg_ref) in enumerate(((g0_ref, ya_ref, dya_ref, dg0_ref),
                                                           (g1_ref, yb_ref, dyb_ref, dg1_ref),
                                                           (g2_ref, yc_ref, dyc_ref, dg2_ref))):
            gate = _sigmoid(g_ref[...].astype(F32) + b_ref[b:b + 1, :])
            dy_ref[...] = (dmv * gate).astype(BF16)
            dpre = dmv * y_ref[...].astype(F32) * gate * (1.0 - gate)
            dg_ref[...] = dpre.astype(BF16)
            part = jnp.sum(dpre, axis=0, keepdims=True)

            @pl.when(first)
            def _():
                db_ref[b:b + 1, :] = part

            @pl.when(jnp.logical_not(first))
            def _():
                db_ref[b:b + 1, :] += part

    act = jax.ShapeDtypeStruct((T, D), BF16)
    return pl.pallas_call(
        body, name="merge_bwd", grid=(D // tc, T // tr), in_specs=[*gspecs, bias, tile, tile, tile, tile],
        out_specs=[tile] * 6 + [bias], out_shape=[act] * 6 + [jax.ShapeDtypeStruct((N_BRANCH, D), F32)],
        compiler_params=_params(("parallel", "arbitrary")))(u, u, u, gate_b, ya, yb, yc, dm)


def _concat_columns(parts):
    T = parts[0].shape[0]
    widths = [p.shape[1] for p in parts]
    tr = _pick(T, ROW_TILE, 16)

    def body(*refs):
        o_ref, off = refs[-1], 0
        for ref, width in zip(refs[:-1], widths):
            o_ref[:, off:off + width] = ref[...]
            off += width

    return pl.pallas_call(
        body, name="concat_columns", grid=(T // tr,),
        in_specs=[pl.BlockSpec((tr, width), lambda i: (i, 0)) for width in widths],
        out_specs=pl.BlockSpec((tr, sum(widths)), lambda i: (i, 0)),
        out_shape=jax.ShapeDtypeStruct((T, sum(widths)), parts[0].dtype),
        compiler_params=_params(("parallel",)))(*parts)


ELEMS_PER_TILE = 512 * 1024


def _row_tile(r, c):
    return _pick(r, max(16, ELEMS_PER_TILE // c), 16) if r % 16 == 0 else r


def _cast_place(w, layer, my_block):
    L, r, c = w.shape
    tr = _row_tile(r, c)

    def body(blk_ref, w_ref, o_ref):
        o_ref[...] = w_ref[...].astype(BF16)

    return pl.pallas_call(
        body, name="cast_place",
        grid_spec=pltpu.PrefetchScalarGridSpec(
            num_scalar_prefetch=1, grid=(r // tr,),
            in_specs=[pl.BlockSpec((None, tr, c), lambda i, blk: (layer, i, 0))],
            out_specs=pl.BlockSpec((None, tr, c), lambda i, blk: (blk[0], i, 0))),
        out_shape=jax.ShapeDtypeStruct((N_DEV, r, c), BF16), compiler_params=_params(("parallel",)))(my_block, w)


def _adamw_math(w, g, m, v):
    m = ADAM_B1 * m + (1.0 - ADAM_B1) * g
    v = ADAM_B2 * v + (1.0 - ADAM_B2) * (g * g)
    m_hat = m / (1.0 - ADAM_B1 ** ADAM_STEP)
    v_hat = v / (1.0 - ADAM_B2 ** ADAM_STEP)
    delta = -ADAM_LR * (m_hat / (jnp.sqrt(v_hat) + ADAM_EPS) + ADAM_WD * w)
    return delta, m, v


def _sum_parts(part_ref):
    acc = part_ref[0].astype(F32)
    for s in range(1, N_DEV):
        acc = acc + part_ref[s].astype(F32)
    return acc


def _sum8(parts):
    L, _, r, c = parts.shape
    tr = _row_tile(r, c)

    def body(p_ref, o_ref):
        o_ref[...] = _sum_parts(p_ref)

    return pl.pallas_call(
        body, name="sum8", grid=(L, r // tr),
        in_specs=[pl.BlockSpec((None, N_DEV, tr, c), lambda l, i: (l, 0, i, 0))],
        out_specs=pl.BlockSpec((None, tr, c), lambda l, i: (l, i, 0)),
        out_shape=jax.ShapeDtypeStruct((L, r, c), F32), compiler_params=_params(("parallel", "parallel")))(parts)


def _adamw(w, g, m, v):
    L, r, c = w.shape
    tr = _row_tile(r, c)
    spec = pl.BlockSpec((None, tr, c), lambda l, i: (l, i, 0))

    def body(w_ref, g_ref, m_ref, v_ref, d_ref, mo_ref, vo_ref):
        d, mn, vn = _adamw_math(w_ref[...], g_ref[...], m_ref[...], v_ref[...])
        d_ref[...] = d
        mo_ref[...] = mn
        vo_ref[...] = vn

    shp = jax.ShapeDtypeStruct(w.shape, F32)
    return pl.pallas_call(body, name="adamw", grid=(L, r // tr), in_specs=[spec] * 4, out_specs=[spec] * 3,
                          out_shape=[shp] * 3, compiler_params=_params(("parallel", "parallel")))(w, g, m, v)


N_CHIPS = 4
CHIP_XOR = (0, 2, 1, 3)


def _chip_sum(part4, sib4, own_all, layer, pos):
    _, _, r, c = part4.shape
    tr = _row_tile(r, c)

    def chip(p, s):
        return jnp.bitwise_xor(2 * p[0] + p[1], CHIP_XOR[s])

    mine = [pl.BlockSpec((None, None, tr, c), functools.partial(lambda i, p, s: (chip(p, s), p[2], i, 0), s=s))
            for s in range(N_CHIPS)]
    theirs = [pl.BlockSpec((None, None, tr, c), functools.partial(lambda i, p, s: (chip(p, s), 0, i, 0), s=s))
              for s in range(N_CHIPS)]

    def body(pos_ref, *refs):
        a, b = refs[:N_CHIPS], refs[N_CHIPS:2 * N_CHIPS]
        out_ref, own_ref = refs[2 * N_CHIPS + 1], refs[2 * N_CHIPS + 2]
        own_ref[...] = a[0][...].astype(F32) + b[0][...].astype(F32)
        for s in range(1, N_CHIPS):
            out_ref[s - 1] = (a[s][...].astype(F32) + b[s][...].astype(F32)).astype(BF16)

    return pl.pallas_call(
        body, name="chip_sum",
        grid_spec=pltpu.PrefetchScalarGridSpec(
            num_scalar_prefetch=1, grid=(r // tr,),
            in_specs=[*mine, *theirs, pl.BlockSpec(memory_space=pl.ANY)],
            out_specs=[pl.BlockSpec((N_CHIPS - 1, tr, c), lambda i, p: (0, i, 0)),
                       pl.BlockSpec((None, tr, c), lambda i, p: (layer, i, 0))]),
        out_shape=[jax.ShapeDtypeStruct((N_CHIPS - 1, r, c), BF16), jax.ShapeDtypeStruct(own_all.shape, F32)],
        input_output_aliases={1 + 2 * N_CHIPS: 1},
        compiler_params=_params(("parallel",)))(pos, *([part4] * N_CHIPS), *([sib4] * N_CHIPS), own_all)


def _sum_chips(own_ref, got_ref):
    acc = own_ref[...]
    for s in range(N_CHIPS - 1):
        acc = acc + got_ref[s].astype(F32)
    return acc


def _sum4_adamw(own, got, w, m, v, first, count, earlier=None):
    L, r, c = w.shape
    tr = _row_tile(r, c)
    spec = pl.BlockSpec((None, tr, c), lambda l, i: (first + l, i, 0))
    earlier = [] if earlier is None else list(earlier)

    def body(own_ref, got_ref, w_ref, m_ref, v_ref, *rest):
        g_ref, d_ref, mo_ref, vo_ref = rest[len(earlier):]
        g = _sum_chips(own_ref, got_ref)
        d, mn, vn = _adamw_math(w_ref[...], g, m_ref[...], v_ref[...])
        g_ref[...] = g
        d_ref[...] = d
        mo_ref[...] = mn
        vo_ref[...] = vn

    shp = jax.ShapeDtypeStruct(w.shape, F32)
    return pl.pallas_call(
        body, name="sum4_adamw", grid=(count, r // tr),
        in_specs=[spec, pl.BlockSpec((None, N_CHIPS - 1, tr, c), lambda l, i: (first + l, 0, i, 0)), spec, spec, spec,
                  *([ANY] * len(earlier))],
        out_specs=[spec] * 4, out_shape=[shp] * 4, input_output_aliases={5 + a: a for a in range(len(earlier))},
        compiler_params=_params(("parallel", "parallel")))(own, got, w, m, v, *earlier)


def _me():
    return lax.axis_index("x"), lax.axis_index("y"), lax.axis_index("c")


def _flip(pos, k):
    x, y, c = pos
    return (1 - x if k & 4 else x, 1 - y if k & 2 else y, 1 - c if k & 1 else c)


def _block_of(pos):
    return 4 * pos[0] + 2 * pos[1] + pos[2]


ANY = pl.BlockSpec(memory_space=pl.ANY)
SIBLING = 1
OTHER_CHIPS = (4, 2, 6)


def _all_gather(name, arrays):
    n = len(arrays)
    shapes = [a.shape[-2:] for a in arrays]

    def body(*refs):
        srcs, outs = refs[:n], refs[n:2 * n]
        send_sems, recv_sems, local_sems = refs[2 * n:]
        me = _me()
        sib = _flip(me, SIBLING)

        def copy(a, k, block_pos, to, src=None):
            dst = outs[a].at[_block_of(block_pos)]
            return pltpu.make_async_remote_copy(
                src_ref=dst if src is None else src, dst_ref=dst, send_sem=send_sems.at[a, k],
                recv_sem=recv_sems.at[a, k], device_id=to, device_id_type=MESH)

        mine = [pltpu.make_async_copy(srcs[a], outs[a].at[_block_of(me)], local_sems.at[a]) for a in range(n)]
        for cp in mine:
            cp.start()
        first = []
        for a in range(n):
            first.append(copy(a, 0, me, sib, src=srcs[a]))
            for j, k in enumerate(OTHER_CHIPS):
                first.append(copy(a, 1 + j, me, _flip(me, k), src=srcs[a]))
        for cp in first:
            cp.start()
        passed = []
        for j, k in enumerate(OTHER_CHIPS):
            for a in range(n):
                copy(a, 1 + j, _flip(me, k), me).wait_recv()
                fw = copy(a, 4 + j, _flip(me, k), sib)
                fw.start()
                passed.append(fw)
        for a in range(n):
            copy(a, 0, sib, me).wait_recv()
            for j, k in enumerate(OTHER_CHIPS):
                copy(a, 4 + j, _flip(sib, k), me).wait_recv()
        for cp in first + passed:
            cp.wait_send()
        for cp in mine:
            cp.wait()

    return pl.pallas_call(
        body, name=name, in_specs=[ANY] * n, out_specs=[ANY] * n,
        out_shape=[jax.ShapeDtypeStruct((N_DEV, *s), a.dtype) for s, a in zip(shapes, arrays)],
        scratch_shapes=[pltpu.SemaphoreType.DMA((n, 7)), pltpu.SemaphoreType.DMA((n, 7)), pltpu.SemaphoreType.DMA((n,))],
    )(*arrays)


HBM = pl.BlockSpec(memory_space=pltpu.HBM)
SEM = pl.BlockSpec(memory_space=pltpu.SEMAPHORE)
DATAFLOW = pltpu.SideEffectType.DATAFLOW_SIDE_EFFECTING
X_NEIGHBOUR, Y_NEIGHBOUR = 4, 2
NEAR = (SIBLING, X_NEIGHBOUR, Y_NEIGHBOUR)


def _in_hbm(a):
    return pltpu.with_memory_space_constraint(a, pltpu.HBM)


def _hbm_like(a):
    return pltpu.HBM(a.shape, a.dtype)


def _copies_start(name, srcs, lands, plan, n_copies, after=None):
    ns, n = len(srcs), len(lands)
    deps = [] if after is None else [after]

    def body(*refs):
        k0 = ns + n + len(deps)
        send_sems, recv_sems, token = refs[k0], refs[k0 + 1], refs[k0 + 2 + n]
        for s, (src, dst, peer, _) in enumerate(plan(_me(), refs[:ns], refs[ns:ns + n])):
            pltpu.make_async_remote_copy(src_ref=src, dst_ref=dst, send_sem=send_sems.at[s], recv_sem=recv_sems.at[s],
                                         device_id=peer, device_id_type=MESH).start()
        token[...] = jnp.zeros(TOKEN, F32)

    out = pl.pallas_call(
        body, name=name, in_specs=[*([HBM] * (ns + n)), *([ANY] * len(deps))],
        out_specs=[SEM, SEM, *([HBM] * n), pl.BlockSpec(memory_space=pltpu.VMEM)],
        out_shape=[pltpu.SemaphoreType.DMA((n_copies,)), pltpu.SemaphoreType.DMA((n_copies,)),
                   *[_hbm_like(a) for a in lands], jax.ShapeDtypeStruct(TOKEN, F32)],
        input_output_aliases={ns + a: 2 + a for a in range(n)},
        compiler_params=pltpu.CompilerParams(has_side_effects=DATAFLOW),
    )(*[_in_hbm(a) for a in srcs], *[_in_hbm(a) for a in lands], *deps)
    return dict(send=out[0], recv=out[1], srcs=list(srcs), plan=plan), list(out[2:2 + n]), out[2 + n]


def _copies_wait(name, flight, lands, after):
    srcs, plan = flight["srcs"], flight["plan"]
    ns, n = len(srcs), len(lands)
    after = list(after) if isinstance(after, (list, tuple)) else [after]

    def body(*refs):
        send_sems, recv_sems = refs[ns + n], refs[ns + n + 1]
        for s, (src, dst, peer, landing) in enumerate(plan(_me(), refs[:ns], refs[ns:ns + n])):
            pltpu.make_async_remote_copy(src_ref=src, dst_ref=dst, send_sem=send_sems.at[s], recv_sem=recv_sems.at[s],
                                         device_id=peer, device_id_type=MESH).wait_send()
            pltpu.make_async_remote_copy(src_ref=landing, dst_ref=landing, send_sem=send_sems.at[s],
                                         recv_sem=recv_sems.at[s], device_id=peer, device_id_type=MESH).wait_recv()

    out = pl.pallas_call(
        body, name=name, in_specs=[*([HBM] * (ns + n)), SEM, SEM, *([ANY] * len(after))], out_specs=[HBM] * n,
        out_shape=[_hbm_like(a) for a in lands], input_output_aliases={ns + a: a for a in range(n)},
        compiler_params=pltpu.CompilerParams(has_side_effects=DATAFLOW),
    )(*srcs, *lands, flight["send"], flight["recv"], *after)
    return list(out)


def _gather_plan_near(me, srcs, lands):
    plan = []
    for land in lands:
        own = land.at[_block_of(me)]
        for k in NEAR:
            peer = _flip(me, k)
            plan.append((own, own, peer, land.at[_block_of(peer)]))
    return plan


def _gather_plan_far(me, srcs, lands):
    x_nbr, y_nbr, far = _flip(me, X_NEIGHBOUR), _flip(me, Y_NEIGHBOUR), _flip(me, X_NEIGHBOUR | Y_NEIGHBOUR)
    plan = []
    for land in lands:
        half = land.shape[1] // 2
        first, second = pl.ds(0, half), pl.ds(half, half)
        passed = land.at[_block_of(y_nbr), first]
        plan.append((passed, passed, x_nbr, land.at[_block_of(far), first]))
        passed = land.at[_block_of(x_nbr), second]
        plan.append((passed, passed, y_nbr, land.at[_block_of(far), second]))
    return plan


def _broadcast_plan(me, srcs, lands):
    plan = []
    for land in lands:
        own = land.at[_block_of(me)]
        for k in range(1, N_DEV):
            peer = _flip(me, k)
            plan.append((own, own, peer, land.at[_block_of(peer)]))
    return plan


def _sibling_plan(me, srcs, lands):
    sib = _flip(me, SIBLING)
    return [(src.at[:, pl.ds(1 - me[2], 1)], land, sib, land) for src, land in zip(srcs, lands)]


def _scatter_plan(layer):
    def plan(me, srcs, lands):
        out = []
        for src, land in zip(srcs, lands):
            for j, k in enumerate(OTHER_CHIPS):
                out.append((src.at[j], land.at[layer, j], _flip(me, k), land.at[layer, j]))
        return out
    return plan


def _pass_on_plan(relations):
    def plan(me, srcs, lands):
        sib = _flip(me, SIBLING)
        out = []
        for land in lands:
            for k in relations:
                blk = land.at[_block_of(_flip(me, k))]
                out.append((blk, blk, sib, land.at[_block_of(_flip(sib, k))]))
        return out
    return plan


def _after(small, tokens):
    for t in tokens:
        small = small + t[0:1, 0:1]
    return small


def _layer_fwd(xc, l, W, P, dims, deps=(), mid_layer=None):
    T, D, C, I, F = dims
    h = _rms_fwd(xc, _after(P["norm_mix_g"][l:l + 1], deps))
    u = _mm_plain("mm_u", h, W["w_in"][l].reshape(I, D), "NT", [BF16], tn=1280)[0]
    a1, s2 = _convs_fwd(u, P["conf_dw"][l], P["sconv_w"][l], C)
    a3 = _ln_silu(a1, P["conf_ln_g"][l:l + 1], P["conf_ln_b"][l:l + 1])
    o, lse = _attn_fwd(u, P["sinks"][l], C)
    ya = _mm_branch("mm_branch_out", a3, W["w_conf_out"][l], "NN", BF16)
    yb = _mm_branch("mm_branch_out", s2, W["w_sconv_out"][l], "NN", BF16)
    yc = _mm_branch("mm_branch_out", o, W["w_attn_out"][l], "NN", BF16)
    merged = _merge_fwd(u, P["gate_b"][l], ya, yb, yc)
    x1 = _mm_plain("mm_mix", merged, W["w_mix_out"][l].reshape(D, D), "NN", [F32], _epi_resid, [xc], tk=2048)[0]
    norm_ffn_g = P["norm_ffn_g"][l:l + 1]
    h2 = _rms_fwd(x1, norm_ffn_g if mid_layer is None else _after(norm_ffn_g, [mid_layer(x1)]))
    up, act = _mm_nn_colblocked("mm_up", h2, W["w_up"][l], [BF16, BF16], _epi_relu2)
    x2 = _mm_plain("mm_down", act, W["w_down"][l].reshape(F, D), "NN", [F32], _epi_resid, [x1], tk=2048)[0]
    saved = dict(xc=xc, h=h, u=u, a1=a1, a3=a3, s2=s2, o=o, lse=lse, ya=ya, yb=yb, yc=yc, merged=merged, x1=x1, h2=h2,
                 up=up, act=act)
    return x2, saved


def _bwd_mlp(dx2, dx2_b, l, W, P, S, dims, dep=None):
    T, D, C, I, F = dims
    d_up = _mm_plain("mm_d_up", dx2_b, W["w_down"][l].reshape(F, D), "NT", [BF16], _epi_drelu2, [S["up"]], dep=dep)[0]
    g_down = _mm_plain("mm_g_down", S["act"], dx2_b, "TN", [BF16])[0]
    dh2 = _mm_nt_colblocked("mm_d_h2", d_up, W["w_up"][l], F32)
    g_up = _mm_tn_colblocked_out("mm_g_up", S["h2"], d_up, BF16)
    dx1, dx1_b, dg_ffn = _rms_bwd(dh2, S["x1"], P["norm_ffn_g"][l:l + 1], dx2)
    return dx1, dx1_b, dict(w_up=g_up, w_down=g_down.reshape(N_DEV, F // N_DEV, D)), dg_ffn


def _bwd_mix_out(dx1_b, l, W, P, S, dims, dep=None):
    T, D, C, I, F = dims
    dm = _mm_plain("mm_d_merged", dx1_b, W["w_mix_out"][l].reshape(D, D), "NT", [BF16], dep=dep)[0]
    g_mix = _mm_plain("mm_g_mix", S["merged"], dx1_b, "TN", [BF16])[0]
    merge = _merge_bwd(S["u"], P["gate_b"][l], S["ya"], S["yb"], S["yc"], dm)
    return g_mix.reshape(N_DEV, D // N_DEV, D), merge


def _bwd_mixers(dx1, merge, dg_ffn, g_mix, l, W, P, S, dims, dep, send_mid, send_in):
    T, D, C, I, F = dims
    d_ya, d_yb, d_yc, dg0, dg1, dg2, d_gate_b = merge
    d_a3 = _mm_branch("mm_d_branch", d_ya, W["w_conf_out"][l], "NT", BF16, dep=dep)
    d_s2 = _mm_branch("mm_d_branch", d_yb, W["w_sconv_out"][l], "NT", BF16)
    d_o = _mm_branch("mm_d_branch", d_yc, W["w_attn_out"][l], "NT", BF16)
    g_conf = _mm_branch_grad("mm_g_branch", S["a3"], d_ya, BF16)
    g_sconv = _mm_branch_grad("mm_g_branch", S["s2"], d_yb, BF16)
    g_attn = _mm_branch_grad("mm_g_branch", S["o"], d_yc, BF16)
    tok = send_mid(dict(w_mix_out=g_mix, w_conf_out=g_conf, w_sconv_out=g_sconv, w_attn_out=g_attn), g_attn)
    d_a1, d_ln_g, d_ln_b = _ln_silu_bwd(S["a1"], _after(P["conf_ln_g"][l:l + 1], [tok]), P["conf_ln_b"][l:l + 1], d_a3)
    d_av, d_ag, d_bg, d_cg, d_bh, d_conf_dw, d_sconv_w = _convs_bwd(S["u"], P["conf_dw"][l], P["sconv_w"][l], d_a1, d_s2, C)
    dq, dk, dv, d_sinks = _attn_bwd(S["u"], S["o"], S["lse"], d_o, P["sinks"][l], C)
    du = _concat_columns([d_av, d_ag, d_bg, d_cg, d_bh, dq, dk, dv, dg0, dg1, dg2])
    g_in = _mm_plain("mm_g_in", du, S["h"], "TN", [BF16], tm=1280)[0]
    tok = send_in(dict(w_in=g_in.reshape(N_DEV, I // N_DEV, D)), g_in)
    dh = _mm_plain("mm_d_h", du, W["w_in"][l].reshape(I, D), "NN", [F32], tk=2560, dep=tok)[0]
    dx, dx_b, dg_mix = _rms_bwd(dh, S["xc"], P["norm_mix_g"][l:l + 1], dx1)
    small = dict(norm_mix_g=dg_mix, gate_b=d_gate_b, conf_dw=d_conf_dw, conf_ln_g=d_ln_g, conf_ln_b=d_ln_b,
                 sconv_w=d_sconv_w, sinks=d_sinks[0:1], norm_ffn_g=dg_ffn)
    return dx, dx_b, small


BIG = ("w_in", "w_conf_out", "w_sconv_out", "w_attn_out", "w_mix_out", "w_up", "w_down")
MLP_WEIGHTS = ("w_down", "w_up")
MID_WEIGHTS = ("w_mix_out", "w_conf_out", "w_sconv_out", "w_attn_out")
IN_WEIGHTS = ("w_in",)
SMALL_PER_LAYER = ("norm_mix_g", "gate_b", "conf_dw", "conf_ln_g", "conf_ln_b", "sconv_w", "sinks", "norm_ffn_g")
WEIGHTS = ("norm_mix_g", "w_in", "gate_b", "conf_dw", "conf_ln_g", "conf_ln_b", "w_conf_out", "sconv_w", "w_sconv_out",
           "sinks", "w_attn_out", "w_mix_out", "norm_ffn_g", "w_up", "w_down", "final_g")


SUBLANES = 8


def _nrows(n_el, width):
    per_tile = SUBLANES * width
    return SUBLANES * (-(-n_el // per_tile))


def _rows(a, width):
    flat = a.reshape(-1)
    nrow = _nrows(flat.shape[0], width)
    return jnp.pad(flat, (0, nrow * width - flat.shape[0])).reshape(nrow, width)


def _as3d(a):
    if a.ndim == 1:
        return a.reshape(1, 1, -1)
    if a.ndim == 2:
        return a.reshape(1, *a.shape)
    return a


def kernel(x, norm_mix_g, w_in, gate_b, conf_dw, conf_ln_g, conf_ln_b, w_conf_out, sconv_w, w_sconv_out, sinks, w_attn_out, w_mix_out, norm_ffn_g, w_up, w_down, final_g, loss_target, m_norm_mix_g, m_w_in, m_gate_b, m_conf_dw, m_conf_ln_g, m_conf_ln_b, m_w_conf_out, m_sconv_w, m_w_sconv_out, m_sinks, m_w_attn_out, m_w_mix_out, m_norm_ffn_g, m_w_up, m_w_down, m_final_g, v_norm_mix_g, v_w_in, v_gate_b, v_conf_dw, v_conf_ln_g, v_conf_ln_b, v_w_conf_out, v_sconv_w, v_w_sconv_out, v_sinks, v_w_attn_out, v_w_mix_out, v_norm_ffn_g, v_w_up, v_w_down, v_final_g):
    w = dict(norm_mix_g=norm_mix_g, w_in=w_in, gate_b=gate_b, conf_dw=conf_dw, conf_ln_g=conf_ln_g, conf_ln_b=conf_ln_b,
             w_conf_out=w_conf_out, sconv_w=sconv_w, w_sconv_out=w_sconv_out, sinks=sinks, w_attn_out=w_attn_out,
             w_mix_out=w_mix_out, norm_ffn_g=norm_ffn_g, w_up=w_up, w_down=w_down, final_g=final_g)
    mom = dict(norm_mix_g=m_norm_mix_g, w_in=m_w_in, gate_b=m_gate_b, conf_dw=m_conf_dw, conf_ln_g=m_conf_ln_g,
               conf_ln_b=m_conf_ln_b, w_conf_out=m_w_conf_out, sconv_w=m_sconv_w, w_sconv_out=m_w_sconv_out,
               sinks=m_sinks, w_attn_out=m_w_attn_out, w_mix_out=m_w_mix_out, norm_ffn_g=m_norm_ffn_g, w_up=m_w_up,
               w_down=m_w_down, final_g=m_final_g)
    var = dict(norm_mix_g=v_norm_mix_g, w_in=v_w_in, gate_b=v_gate_b, conf_dw=v_conf_dw, conf_ln_g=v_conf_ln_g,
               conf_ln_b=v_conf_ln_b, w_conf_out=v_w_conf_out, sconv_w=v_sconv_w, w_sconv_out=v_w_sconv_out,
               sinks=v_sinks, w_attn_out=v_w_attn_out, w_mix_out=v_w_mix_out, norm_ffn_g=v_norm_ffn_g, w_up=v_w_up,
               w_down=v_w_down, final_g=v_final_g)

    _, T, D = x.shape
    L = w_in.shape[0]
    C = D // 2
    I = w_in.shape[2] * N_DEV
    F = w_up.shape[2] * N_DEV
    dims = (T, D, C, I, F)
    my_block = _block_of(_me())

    w["w_in"], mom["w_in"], var["w_in"] = (jnp.swapaxes(a, 1, 2) for a in (w_in, m_w_in, v_w_in))

    shard_names = ("gate_b", "conf_dw", "sconv_w")
    packed = jnp.concatenate([_rows(w[k], LANES) for k in shard_names], axis=0)
    gathered = _all_gather("gather_small", [packed])[0]

    pos = jnp.stack(_me()).astype(jnp.int32)
    blk = my_block.reshape(1).astype(jnp.int32)
    W = {k: [_cast_place(w[k], l, blk) for l in range(L)] for k in BIG}
    n_near, n_far = len(BIG) * len(NEAR), len(BIG) * 2

    def gather_near(l, after):
        return _copies_start(f"gather_near_start_{l}", [], [W[k][l] for k in BIG], _gather_plan_near, n_near, after)

    def gather_far(l, g, after):
        lands = _copies_wait(f"gather_near_wait_{l}", g["near"], g["lands"], after)
        g["far"], lands, tok_far = _copies_start(f"gather_far_start_{l}", [], lands, _gather_plan_far, n_far)
        g["pass_near"], g["lands"], tok_pass = _copies_start(
            f"pass_near_start_{l}", [], lands, _pass_on_plan((X_NEIGHBOUR, Y_NEIGHBOUR)), 2 * len(BIG))
        return tok_far + tok_pass

    def gather_finish(l, g, after):
        lands = _copies_wait(f"gather_far_wait_{l}", g["far"], g["lands"], after)
        pass_far, lands, token = _copies_start(f"pass_far_start_{l}", [], lands,
                                               _pass_on_plan((X_NEIGHBOUR | Y_NEIGHBOUR,)), len(BIG))
        nxt = {}
        if l + 1 < L:
            nxt["near"], nxt["lands"], token = gather_near(l + 1, token)
        lands = _copies_wait(f"pass_near_wait_{l}", g["pass_near"], lands, [after, token])
        return _copies_wait(f"pass_far_wait_{l}", pass_far, lands, [after, token]), nxt, token

    gathering = {}
    gathering["near"], gathering["lands"], tok = gather_near(0, gathered)
    later_layers = [W[k][l] for l in range(1, L) for k in BIG]
    tok = gather_far(0, gathering, [tok, *later_layers])

    P = dict(norm_mix_g=norm_mix_g, conf_ln_g=conf_ln_g, conf_ln_b=conf_ln_b, sinks=sinks, norm_ffn_g=norm_ffn_g)
    row0 = 0
    for k in shard_names:
        n_el = w[k].size
        nrow = _nrows(n_el, LANES)
        part = gathered[:, row0:row0 + nrow].reshape(N_DEV, -1)[:, :n_el].reshape(N_DEV, *w[k].shape)
        P[k] = jnp.moveaxis(part, 0, 2).reshape(*w[k].shape[:2], N_DEV * w[k].shape[2])
        row0 += nrow

    xc = x.reshape(T, D)
    saved = []
    for l in range(L):
        lands, gathering, tok = gather_finish(l, gathering, xc if l else tok)
        for k, g in zip(BIG, lands):
            W[k][l] = g
        mid_layer = functools.partial(gather_far, l + 1, gathering) if l + 1 < L else None
        xc, S = _layer_fwd(xc, l, W, P, dims, (), mid_layer)
        saved.append(S)
    dx, dx_b, d_final_g, loss_tile = _loss_head(xc, final_g.reshape(1, D), loss_target.reshape(T, D))

    own_all = {k: lax.empty((L, *W[k][0].shape[1:]), F32) for k in BIG}
    recv = {k: lax.empty((L, N_CHIPS - 1, *W[k][0].shape[1:]), BF16) for k in BIG}
    scatters = []

    def to_sibling(names, grads, l, after):
        part4 = [grads[k].reshape(N_CHIPS, 2, *grads[k].shape[1:]) for k in names]
        zone = [lax.empty((N_CHIPS, 1, *p.shape[2:]), BF16) for p in part4]
        fl, zone, token = _copies_start(f"sibling_start_{l}_{names[0]}", part4, zone, _sibling_plan, len(names), after)
        return dict(names=names, l=l, part4=part4, flight=fl, zone=zone), token

    def to_owners(group, after):
        names, l = group["names"], group["l"]
        sib4 = _copies_wait(f"sibling_wait_{l}_{names[0]}", group["flight"], group["zone"], after)
        chip_parts = []
        for k, p4, s4 in zip(names, group["part4"], sib4):
            cp, own_all[k] = _chip_sum(p4, s4, own_all[k], l, pos)
            chip_parts.append(cp)
        fl, zone, token = _copies_start(f"scatter_start_{l}_{names[0]}", chip_parts, [recv[k] for k in names],
                                        _scatter_plan(l), len(names) * len(OTHER_CHIPS))
        for k, g in zip(names, zone):
            recv[k] = g
        scatters.append((f"scatter_wait_{l}_{names[0]}", fl, names, l))
        return token

    small_grads = [None] * L
    dep, groups = None, {}
    for l in reversed(range(L)):
        S = saved[l]
        dx1, dx1_b, g_mlp, dg_ffn = _bwd_mlp(dx, dx_b, l, W, P, S, dims, dep)
        groups["mlp"], dep = to_sibling(MLP_WEIGHTS, g_mlp, l, dx1)
        if "in" in groups:
            dep = dep + to_owners(groups["in"], dx1)
        g_mix, merge = _bwd_mix_out(dx1_b, l, W, P, S, dims, dep)
        dep = to_owners(groups["mlp"], merge[0])

        def send_mid(grads, after, l=l):
            groups["mid"], token = to_sibling(MID_WEIGHTS, grads, l, after)
            return token

        def send_in(grads, after, l=l):
            token = to_owners(groups["mid"], after)
            groups["in"], token2 = to_sibling(IN_WEIGHTS, grads, l, after)
            return token + token2

        dx, dx_b, small_grads[l] = _bwd_mixers(dx1, merge, dg_ffn, g_mix, l, W, P, S, dims, dep, send_mid, send_in)
    last_start = to_owners(groups["in"], dx)

    width = LANES
    pieces = [_rows(small_grads[l][k], width) for l in range(L) for k in SMALL_PER_LAYER]
    pieces += [_rows(d_final_g, width), _rows(loss_tile[0:1, 0:1], width)]
    partial = jnp.concatenate(pieces, axis=0)
    everyone = lax.dynamic_update_slice(lax.empty((N_DEV, *partial.shape), F32), partial[None], (my_block, 0, 0))
    small_flight, (everyone,), token = _copies_start("small_grads_start", [], [everyone], _broadcast_plan, N_DEV - 1,
                                                     after=last_start)
    last_start = last_start + token

    def await_scatters(layers):
        for name, fl, names, l in scatters:
            if l in layers:
                for k, g in zip(names, _copies_wait(name, fl, [recv[k] for k in names], [dx, last_start])):
                    recv[k] = g

    await_scatters(range(1, L))
    done = {k: _sum4_adamw(own_all[k], recv[k], w[k], mom[k], var[k], 1, L - 1) for k in BIG} if L > 1 else {}
    await_scatters([0])

    grads, delta, new_m, new_v = {}, {}, {}, {}
    for k in BIG:
        grads[k], delta[k], new_m[k], new_v[k] = _sum4_adamw(own_all[k], recv[k], w[k], mom[k], var[k], 0, 1,
                                                             done.get(k))

    everyone = _copies_wait("small_grads_wait", small_flight, [everyone], [grads[k] for k in BIG])[0]
    total = _sum8(everyone.reshape(1, *everyone.shape))[0]
    row0 = 0
    per_layer = {k: [] for k in SMALL_PER_LAYER}
    for l in range(L):
        for k in SMALL_PER_LAYER:
            shape = small_grads[l][k].shape
            n_el = small_grads[l][k].size
            nrow = _nrows(n_el, width)
            per_layer[k].append(total[row0:row0 + nrow].reshape(-1)[:n_el].reshape(shape))
            row0 += nrow
    nrow = _nrows(D, width)
    grads["final_g"] = total[row0:row0 + nrow].reshape(-1)[:D]
    row0 += nrow
    loss = total[row0, 0]
    for k in SMALL_PER_LAYER:
        full = jnp.stack(per_layer[k], axis=0)
        if k in shard_names:
            shard = w[k].shape[2]
            full = lax.dynamic_slice_in_dim(full, my_block * shard, shard, axis=2)
        grads[k] = full.reshape(w[k].shape)

    for out in (grads, delta, new_m, new_v):
        out["w_in"] = jnp.swapaxes(out["w_in"], 1, 2)
    for k in WEIGHTS:
        if k in BIG:
            continue
        d, mn, vn = _adamw(_as3d(w[k]), _as3d(grads[k]), _as3d(mom[k]), _as3d(var[k]))
        delta[k], new_m[k], new_v[k] = d.reshape(w[k].shape), mn.reshape(w[k].shape), vn.reshape(w[k].shape)

    return (loss, dx.reshape(1, T, D), *[grads[k] for k in WEIGHTS], *[delta[k] for k in WEIGHTS],
            *[new_m[k] for k in WEIGHTS], *[new_v[k] for k in WEIGHTS])
```

```python
import functools

import jax
import jax.numpy as jnp
from jax import lax
from jax.experimental import pallas as pl
from jax.experimental.pallas import tpu as pltpu

F32 = jnp.float32
BF16 = jnp.bfloat16

N_DEV = 8
HEAD_DIM = 64
N_KV_HEADS = 4
ATTN_BLOCK = 128
CONF_KERNEL = 31
SCONV_KERNEL = 3
N_BRANCH = 3
RMS_EPS = 1e-6
LN_EPS = 1e-5
ADAM_LR = 0.001
ADAM_B1 = 0.9
ADAM_B2 = 0.999
ADAM_EPS = 1e-08
ADAM_WD = 0.01
ADAM_STEP = 10
LANES = 128
NEG_BIG = -1e30
VMEM_LIMIT_BYTES = 56 * 1024 * 1024
MESH = pl.DeviceIdType.MESH

NN = (((1,), (0,)), ((), ()))
NT = (((1,), (1,)), ((), ()))
TN = (((0,), (0,)), ((), ()))


def _pick(n, cap, mult=LANES):
    best = None
    for d in range(mult, min(n, cap) + 1, mult):
        if n % d == 0:
            best = d
    assert best is not None, (n, cap, mult)
    return best


def _sigmoid(x):
    return 0.5 * jnp.tanh(0.5 * x) + 0.5


def _params(sem):
    return pltpu.CompilerParams(dimension_semantics=sem, vmem_limit_bytes=VMEM_LIMIT_BYTES)


def _epi_cast(p, ex, outs):
    outs[0][...] = p.astype(outs[0].dtype)


def _epi_resid(p, ex, outs):
    outs[0][...] = ex[0][...] + p


def _epi_relu2(p, ex, outs):
    outs[0][...] = p.astype(outs[0].dtype)
    r = jnp.maximum(p, 0.0)
    outs[1][...] = (r * r).astype(outs[1].dtype)


def _epi_drelu2(p, ex, outs):
    up = ex[0][...].astype(F32)
    outs[0][...] = (p * (2.0 * jnp.maximum(up, 0.0))).astype(outs[0].dtype)


TOKEN = (8, LANES)


def _matmul(name, a, b, dnums, grid, a_spec, b_spec, out_shape, out_specs, epi, acc_shape, extra=(), extra_specs=(),
            dep=None):
    if dep is not None:
        extra = [*extra, dep]
        extra_specs = [*extra_specs, pl.BlockSpec(TOKEN, lambda j, i, k: (0, 0))]
    nk = grid[2]
    n_extra, n_out = len(extra), len(out_shape)

    def body(*refs):
        a_ref, b_ref = refs[0], refs[1]
        ex = refs[2:2 + n_extra]
        outs = refs[2 + n_extra:2 + n_extra + n_out]
        p = lax.dot_general(a_ref[...], b_ref[...], dnums, preferred_element_type=F32)
        if nk == 1:
            epi(p, ex, outs)
        else:
            acc = refs[-1]
            k = pl.program_id(2)

            @pl.when(k == 0)
            def _():
                acc[...] = p

            @pl.when(k > 0)
            def _():
                acc[...] += p

            @pl.when(k == nk - 1)
            def _():
                epi(acc[...], ex, outs)

    scratch = [pltpu.VMEM(acc_shape, F32)] if nk > 1 else []
    return pl.pallas_call(
        body, name=name, grid=grid, in_specs=[a_spec, b_spec, *extra_specs], out_specs=list(out_specs),
        out_shape=list(out_shape), scratch_shapes=scratch,
        compiler_params=_params(("parallel", "parallel", "arbitrary")))(a, b, *extra)


def _mm_plain(name, a, b, form, out_dtypes, epi=_epi_cast, extra=(), tm=1024, tn=1024, tk=2048, dep=None):
    if form == "NN":
        (M, K), N = a.shape, b.shape[1]
    elif form == "NT":
        (M, K), N = a.shape, b.shape[0]
    else:
        (K, M), N = a.shape, b.shape[1]
    tm, tn, tk = _pick(M, tm, 8), _pick(N, tn), _pick(K, tk)
    grid = (N // tn, M // tm, K // tk)
    if form == "TN":
        a_spec = pl.BlockSpec((tk, tm), lambda j, i, k: (k, i))
    else:
        a_spec = pl.BlockSpec((tm, tk), lambda j, i, k: (i, k))
    if form == "NT":
        b_spec = pl.BlockSpec((tn, tk), lambda j, i, k: (j, k))
    else:
        b_spec = pl.BlockSpec((tk, tn), lambda j, i, k: (k, j))
    o_spec = pl.BlockSpec((tm, tn), lambda j, i, k: (i, j))
    dn = {"NN": NN, "NT": NT, "TN": TN}[form]
    return _matmul(name, a, b, dn, grid, a_spec, b_spec,
                   [jax.ShapeDtypeStruct((M, N), dt) for dt in out_dtypes], [o_spec] * len(out_dtypes), epi,
                   (tm, tn), extra, [o_spec] * len(extra), dep)


def _mm_nn_colblocked(name, a, bb, out_dtypes, epi=_epi_cast, tm=1024, tn=1024, tk=2048):
    M, K = a.shape
    ns = bb.shape[2]
    N = N_DEV * ns
    tm, tn, tk = _pick(M, tm, 8), _pick(ns, tn), _pick(K, tk)
    q = ns // tn
    grid = (N // tn, M // tm, K // tk)
    a_spec = pl.BlockSpec((tm, tk), lambda j, i, k: (i, k))
    b_spec = pl.BlockSpec((None, tk, tn), lambda j, i, k: (j // q, k, j % q))
    o_spec = pl.BlockSpec((tm, tn), lambda j, i, k: (i, j))
    return _matmul(name, a, bb, NN, grid, a_spec, b_spec,
                   [jax.ShapeDtypeStruct((M, N), dt) for dt in out_dtypes], [o_spec] * len(out_dtypes), epi, (tm, tn))


def _mm_nt_colblocked(name, a, bb, out_dtype, tm=1024, tn=1024, tk=1024):
    M, N = a.shape
    K, ns = bb.shape[1], bb.shape[2]
    tm, tn, tk = _pick(M, tm, 8), _pick(K, tn), _pick(ns, tk)
    q = ns // tk
    grid = (K // tn, M // tm, N // tk)
    a_spec = pl.BlockSpec((tm, tk), lambda j, i, k: (i, k))
    b_spec = pl.BlockSpec((None, tn, tk), lambda j, i, k: (k // q, j, k % q))
    o_spec = pl.BlockSpec((tm, tn), lambda j, i, k: (i, j))
    return _matmul(name, a, bb, NT, grid, a_spec, b_spec, [jax.ShapeDtypeStruct((M, K), out_dtype)], [o_spec],
                   _epi_cast, (tm, tn))[0]


def _mm_tn_colblocked_out(name, a, b, out_dtype, tm=1024, tn=1024, tk=2048):
    T, M = a.shape
    N = b.shape[1]
    ns = N // N_DEV
    tm, tn, tk = _pick(M, tm, 8), _pick(ns, tn), _pick(T, tk)
    q = ns // tn
    grid = (N // tn, M // tm, T // tk)
    a_spec = pl.BlockSpec((tk, tm), lambda j, i, k: (k, i))
    b_spec = pl.BlockSpec((tk, tn), lambda j, i, k: (k, j))
    o_spec = pl.BlockSpec((None, tm, tn), lambda j, i, k: (j // q, i, j % q))
    return _matmul(name, a, b, TN, grid, a_spec, b_spec, [jax.ShapeDtypeStruct((N_DEV, M, ns), out_dtype)], [o_spec],
                   _epi_cast, (tm, tn))[0]


def _unblock(bb_ref, full_ref):
    ns = bb_ref.shape[2]
    for j in range(N_DEV):
        full_ref[:, j * ns:(j + 1) * ns] = bb_ref[j]


def _mm_branch(name, a, bb, form, out_dtype, tm=512, dep=None):
    M = a.shape[0]
    K, ns = bb.shape[1], bb.shape[2]
    N = N_DEV * ns
    tm = _pick(M, tm, 8)
    out_cols = N if form == "NN" else K
    deps = [] if dep is None else [dep]

    def body(a_ref, b_ref, *rest):
        o_ref, w_full = rest[len(deps):]

        @pl.when(pl.program_id(0) == 0)
        def _():
            _unblock(b_ref, w_full)

        o_ref[...] = lax.dot_general(a_ref[...], w_full[...], NN if form == "NN" else NT,
                                     preferred_element_type=F32).astype(out_dtype)

    return pl.pallas_call(
        body, name=name, grid=(M // tm,),
        in_specs=[pl.BlockSpec((tm, a.shape[1]), lambda i: (i, 0)), pl.BlockSpec(bb.shape, lambda i: (0, 0, 0)),
                  *[pl.BlockSpec(TOKEN, lambda i: (0, 0)) for _ in deps]],
        out_specs=pl.BlockSpec((tm, out_cols), lambda i: (i, 0)),
        out_shape=jax.ShapeDtypeStruct((M, out_cols), out_dtype),
        scratch_shapes=[pltpu.VMEM((K, N), BF16)],
        compiler_params=_params(("arbitrary",)))(a, bb, *deps)


def _mm_branch_grad(name, a, b, out_dtype, tm=256):
    T, M = a.shape
    N = b.shape[1]
    ns = N // N_DEV
    tm = _pick(M, tm, 8)

    def body(a_ref, b_ref, o_ref):
        p = lax.dot_general(a_ref[...], b_ref[...], TN, preferred_element_type=F32)
        for j in range(N_DEV):
            o_ref[j] = p[:, j * ns:(j + 1) * ns].astype(out_dtype)

    return pl.pallas_call(
        body, name=name, grid=(M // tm,),
        in_specs=[pl.BlockSpec((T, tm), lambda i: (0, i)), pl.BlockSpec((T, N), lambda i: (0, 0))],
        out_specs=pl.BlockSpec((N_DEV, tm, ns), lambda i: (0, i, 0)),
        out_shape=jax.ShapeDtypeStruct((N_DEV, M, ns), out_dtype),
        compiler_params=_params(("parallel",)))(a, b)


ROW_TILE = 256


def _rms_fwd(x, g):
    T, D = x.shape
    tr = _pick(T, ROW_TILE, 8)

    def body(x_ref, g_ref, h_ref):
        xv = x_ref[...]
        r = lax.rsqrt(jnp.mean(xv * xv, axis=-1, keepdims=True) + RMS_EPS)
        h_ref[...] = (xv * r * g_ref[...]).astype(BF16)

    return pl.pallas_call(
        body, name="rms_fwd", grid=(T // tr,),
        in_specs=[pl.BlockSpec((tr, D), lambda i: (i, 0)), pl.BlockSpec((1, D), lambda i: (0, 0))],
        out_specs=pl.BlockSpec((tr, D), lambda i: (i, 0)),
        out_shape=jax.ShapeDtypeStruct((T, D), BF16), compiler_params=_params(("parallel",)))(x, g)


def _rms_bwd_math(dh, xv, g):
    r = lax.rsqrt(jnp.mean(xv * xv, axis=-1, keepdims=True) + RMS_EPS)
    gdh = dh * g
    dot = jnp.mean(gdh * xv, axis=-1, keepdims=True)
    dx = r * gdh - xv * (r * r * r * dot)
    return dx, dh * xv * r


def _rms_bwd(dh, x, g, dres):
    T, D = x.shape
    tr = _pick(T, ROW_TILE, 8)

    def body(dh_ref, x_ref, g_ref, dres_ref, dx_ref, dxb_ref, dg_ref):
        dx, dgrow = _rms_bwd_math(dh_ref[...], x_ref[...], g_ref[...])
        dx = dx + dres_ref[...]
        dx_ref[...] = dx
        dxb_ref[...] = dx.astype(BF16)
        part = jnp.sum(dgrow, axis=0, keepdims=True)

        @pl.when(pl.program_id(0) == 0)
        def _():
            dg_ref[...] = part

        @pl.when(pl.program_id(0) > 0)
        def _():
            dg_ref[...] += part

    row = pl.BlockSpec((tr, D), lambda i: (i, 0))
    vec = pl.BlockSpec((1, D), lambda i: (0, 0))
    return pl.pallas_call(
        body, name="rms_bwd", grid=(T // tr,), in_specs=[row, row, vec, row], out_specs=[row, row, vec],
        out_shape=[jax.ShapeDtypeStruct((T, D), F32), jax.ShapeDtypeStruct((T, D), BF16),
                   jax.ShapeDtypeStruct((1, D), F32)],
        compiler_params=_params(("arbitrary",)))(dh, x, g, dres)


def _loss_head(x, g, target):
    T, D = x.shape
    tr = _pick(T, ROW_TILE, 8)

    def body(x_ref, g_ref, t_ref, dx_ref, dxb_ref, dg_ref, loss_ref):
        xv, gv = x_ref[...], g_ref[...]
        r = lax.rsqrt(jnp.mean(xv * xv, axis=-1, keepdims=True) + RMS_EPS)
        err = xv * r * gv - t_ref[...]
        part_loss = 0.5 * jnp.sum(jnp.mean(err * err, axis=-1, keepdims=True), axis=0, keepdims=True)
        dx, dgrow = _rms_bwd_math(err * (1.0 / D), xv, gv)
        dx_ref[...] = dx
        dxb_ref[...] = dx.astype(BF16)
        part = jnp.sum(dgrow, axis=0, keepdims=True)
        lpart = jnp.broadcast_to(part_loss, (8, LANES))

        @pl.when(pl.program_id(0) == 0)
        def _():
            dg_ref[...] = part
            loss_ref[...] = lpart

        @pl.when(pl.program_id(0) > 0)
        def _():
            dg_ref[...] += part
            loss_ref[...] += lpart

    row = pl.BlockSpec((tr, D), lambda i: (i, 0))
    vec = pl.BlockSpec((1, D), lambda i: (0, 0))
    lsp = pl.BlockSpec((8, LANES), lambda i: (0, 0))
    return pl.pallas_call(
        body, name="loss_head", grid=(T // tr,), in_specs=[row, vec, row], out_specs=[row, row, vec, lsp],
        out_shape=[jax.ShapeDtypeStruct((T, D), F32), jax.ShapeDtypeStruct((T, D), BF16),
                   jax.ShapeDtypeStruct((1, D), F32), jax.ShapeDtypeStruct((8, LANES), F32)],
        compiler_params=_params(("arbitrary",)))(x, g, target)


def _ln_math(a1, g, b):
    mu = jnp.mean(a1, axis=-1, keepdims=True)
    xc = a1 - mu
    rstd = lax.rsqrt(jnp.mean(xc * xc, axis=-1, keepdims=True) + LN_EPS)
    xhat = xc * rstd
    return xhat, rstd, xhat * g + b


def _ln_silu(a1, g, b):
    T, C = a1.shape
    tr = _pick(T, ROW_TILE, 8)

    def body(a_ref, g_ref, b_ref, o_ref):
        _, _, y = _ln_math(a_ref[...], g_ref[...], b_ref[...])
        o_ref[...] = (y * _sigmoid(y)).astype(BF16)

    row = pl.BlockSpec((tr, C), lambda i: (i, 0))
    vec = pl.BlockSpec((1, C), lambda i: (0, 0))
    return pl.pallas_call(body, name="ln_silu", grid=(T // tr,), in_specs=[row, vec, vec], out_specs=row,
                          out_shape=jax.ShapeDtypeStruct((T, C), BF16), compiler_params=_params(("parallel",)))(a1, g, b)


def _ln_silu_bwd(a1, g, b, d_a3):
    T, C = a1.shape
    tr = _pick(T, ROW_TILE, 8)

    def body(a_ref, g_ref, b_ref, d_ref, da_ref, dg_ref, db_ref):
        gv = g_ref[...]
        xhat, rstd, y = _ln_math(a_ref[...], gv, b_ref[...])
        s = _sigmoid(y)
        dy = d_ref[...].astype(F32) * (s * (1.0 + y * (1.0 - s)))
        dxh = dy * gv
        m1 = jnp.mean(dxh, axis=-1, keepdims=True)
        m2 = jnp.mean(dxh * xhat, axis=-1, keepdims=True)
        da_ref[...] = rstd * (dxh - m1 - xhat * m2)
        pg = jnp.sum(dy * xhat, axis=0, keepdims=True)
        pb = jnp.sum(dy, axis=0, keepdims=True)

        @pl.when(pl.program_id(0) == 0)
        def _():
            dg_ref[...] = pg
            db_ref[...] = pb

        @pl.when(pl.program_id(0) > 0)
        def _():
            dg_ref[...] += pg
            db_ref[...] += pb

    row = pl.BlockSpec((tr, C), lambda i: (i, 0))
    vec = pl.BlockSpec((1, C), lambda i: (0, 0))
    return pl.pallas_call(
        body, name="ln_silu_bwd", grid=(T // tr,), in_specs=[row, vec, vec, row], out_specs=[row, vec, vec],
        out_shape=[jax.ShapeDtypeStruct((T, C), F32), jax.ShapeDtypeStruct((1, C), F32),
                   jax.ShapeDtypeStruct((1, C), F32)],
        compiler_params=_params(("arbitrary",)))(a1, g, b, d_a3)


CONV_ROWS = 128
PAD_A = 32
PAD_B = 8


def _u_block(T, first):
    return pl.BlockSpec((T, LANES), lambda i: (0, first + i))


def _causal_conv(xpad_ref, w_ref, ksize, pad, T, emit):
    for r0 in range(0, T, CONV_ROWS):
        acc = None
        for j in range(ksize):
            off = pad - (ksize - 1) + j + r0
            term = w_ref[j:j + 1, :] * xpad_ref[off:off + CONV_ROWS, :]
            acc = term if acc is None else acc + term
        emit(r0, acc)


def _anticausal_conv(gpad_ref, w_ref, ksize, T, emit):
    for r0 in range(0, T, CONV_ROWS):
        acc = None
        for j in range(ksize):
            off = (ksize - 1) - j + r0
            term = w_ref[j:j + 1, :] * gpad_ref[off:off + CONV_ROWS, :]
            acc = term if acc is None else acc + term
        emit(r0, acc)


def _conv_wgrad(xpad_ref, g_ref, dw_ref, ksize, pad, T):
    for j in range(ksize):
        acc = None
        for r0 in range(0, T, CONV_ROWS):
            off = pad - (ksize - 1) + j + r0
            term = g_ref[r0:r0 + CONV_ROWS, :] * xpad_ref[off:off + CONV_ROWS, :]
            term = jnp.sum(term.reshape(CONV_ROWS // 8, 8, LANES), axis=0)
            acc = term if acc is None else acc + term
        dw_ref[j:j + 1, :] = jnp.sum(acc, axis=0, keepdims=True)


def _convs_fwd(u, conf_dw, sconv_w, C):
    T = u.shape[0]
    nb = C // LANES

    def body(av_ref, ag_ref, bg_ref, cg_ref, bh_ref, dw_ref, sw_ref, a1_ref, s2_ref, xa, xs, s1):
        xa[0:PAD_A, :] = jnp.zeros((PAD_A, LANES), F32)
        xa[PAD_A:PAD_A + T, :] = av_ref[...].astype(F32) * _sigmoid(ag_ref[...].astype(F32))

        def emit_a(r0, acc):
            a1_ref[r0:r0 + CONV_ROWS, :] = acc

        _causal_conv(xa, dw_ref, CONF_KERNEL, PAD_A, T, emit_a)

        xs[0:PAD_B, :] = jnp.zeros((PAD_B, LANES), F32)
        xs[PAD_B:PAD_B + T, :] = cg_ref[...].astype(F32) * bh_ref[...].astype(F32)

        def emit_b(r0, acc):
            s1[r0:r0 + CONV_ROWS, :] = acc

        _causal_conv(xs, sw_ref, SCONV_KERNEL, PAD_B, T, emit_b)
        s2_ref[...] = (bg_ref[...].astype(F32) * s1[...]).astype(BF16)

    col = pl.BlockSpec((T, LANES), lambda i: (0, i))
    return pl.pallas_call(
        body, name="convs_fwd", grid=(nb,),
        in_specs=[_u_block(T, 0), _u_block(T, nb), _u_block(T, 2 * nb), _u_block(T, 3 * nb), _u_block(T, 4 * nb),
                  pl.BlockSpec((CONF_KERNEL, LANES), lambda i: (0, i)),
                  pl.BlockSpec((SCONV_KERNEL, LANES), lambda i: (0, i))],
        out_specs=[col, col],
        out_shape=[jax.ShapeDtypeStruct((T, C), F32), jax.ShapeDtypeStruct((T, C), BF16)],
        scratch_shapes=[pltpu.VMEM((T + PAD_A, LANES), F32), pltpu.VMEM((T + PAD_B, LANES), F32),
                        pltpu.VMEM((T, LANES), F32)],
        compiler_params=_params(("parallel",)))(u, u, u, u, u, conf_dw, sconv_w)


def _convs_bwd(u, conf_dw, sconv_w, d_a1, d_s2, C):
    T = u.shape[0]
    nb = C // LANES

    def body(*refs):
        _convs_bwd_block(*refs, T)

    col = pl.BlockSpec((T, LANES), lambda i: (0, i))
    wa = pl.BlockSpec((CONF_KERNEL, LANES), lambda i: (0, i))
    wb = pl.BlockSpec((SCONV_KERNEL, LANES), lambda i: (0, i))
    act = jax.ShapeDtypeStruct((T, C), BF16)
    return pl.pallas_call(
        body, name="convs_bwd", grid=(nb,),
        in_specs=[_u_block(T, 0), _u_block(T, nb), _u_block(T, 2 * nb), _u_block(T, 3 * nb), _u_block(T, 4 * nb),
                  wa, wb, col, col],
        out_specs=[col, col, col, col, col, wa, wb],
        out_shape=[act, act, act, act, act, jax.ShapeDtypeStruct((CONF_KERNEL, C), F32),
                   jax.ShapeDtypeStruct((SCONV_KERNEL, C), F32)],
        scratch_shapes=[pltpu.VMEM((T + PAD_A, LANES), F32), pltpu.VMEM((T + PAD_A, LANES), F32),
                        pltpu.VMEM((T + PAD_B, LANES), F32), pltpu.VMEM((T + PAD_B, LANES), F32),
                        pltpu.VMEM((T, LANES), F32)],
        compiler_params=_params(("parallel",)))(u, u, u, u, u, conf_dw, sconv_w, d_a1, d_s2)


def _convs_bwd_block(av_ref, ag_ref, bg_ref, cg_ref, bh_ref, dw_ref, sw_ref, da1_ref, ds2_ref,
                     dav_ref, dag_ref, dbg_ref, dcg_ref, dbh_ref, ddw_ref, dsw_ref, xa, ga, xs, gs, tmp, T):
    def to_tmp(r0, acc):
        tmp[r0:r0 + CONV_ROWS, :] = acc

    av = av_ref[...].astype(F32)
    sg = _sigmoid(ag_ref[...].astype(F32))
    xa[0:PAD_A, :] = jnp.zeros((PAD_A, LANES), F32)
    xa[PAD_A:PAD_A + T, :] = av * sg
    ga[0:T, :] = da1_ref[...]
    ga[T:T + PAD_A, :] = jnp.zeros((PAD_A, LANES), F32)
    _conv_wgrad(xa, ga, ddw_ref, CONF_KERNEL, PAD_A, T)
    _anticausal_conv(ga, dw_ref, CONF_KERNEL, T, to_tmp)
    da0 = tmp[...]
    dav_ref[...] = (da0 * sg).astype(BF16)
    dag_ref[...] = (da0 * av * sg * (1.0 - sg)).astype(BF16)

    cg = cg_ref[...].astype(F32)
    bh = bh_ref[...].astype(F32)
    ds2 = ds2_ref[...].astype(F32)
    xs[0:PAD_B, :] = jnp.zeros((PAD_B, LANES), F32)
    xs[PAD_B:PAD_B + T, :] = cg * bh
    _causal_conv(xs, sw_ref, SCONV_KERNEL, PAD_B, T, to_tmp)
    dbg_ref[...] = (ds2 * tmp[...]).astype(BF16)
    gs[0:T, :] = ds2 * bg_ref[...].astype(F32)
    gs[T:T + PAD_B, :] = jnp.zeros((PAD_B, LANES), F32)
    _conv_wgrad(xs, gs, dsw_ref, SCONV_KERNEL, PAD_B, T)
    _anticausal_conv(gs, sw_ref, SCONV_KERNEL, T, to_tmp)
    ds0 = tmp[...]
    dcg_ref[...] = (ds0 * bh).astype(BF16)
    dbh_ref[...] = (ds0 * cg).astype(BF16)


def _attn_specs(C):
    kvw = N_KV_HEADS * HEAD_DIM
    kb = (5 * C + C) // kvw
    qs = pl.BlockSpec((ATTN_BLOCK, C), lambda n: (n, 5))
    kc = pl.BlockSpec((ATTN_BLOCK, kvw), lambda n: (n, kb))
    kp = pl.BlockSpec((ATTN_BLOCK, kvw), lambda n: (jnp.maximum(n - 1, 0), kb))
    vc = pl.BlockSpec((ATTN_BLOCK, kvw), lambda n: (n, kb + 1))
    vp = pl.BlockSpec((ATTN_BLOCK, kvw), lambda n: (jnp.maximum(n - 1, 0), kb + 1))
    return qs, kc, kp, vc, vp


def _attn_masks(n, rows):
    row = lax.broadcasted_iota(jnp.int32, (rows, ATTN_BLOCK), 0) % ATTN_BLOCK
    col = lax.broadcasted_iota(jnp.int32, (rows, ATTN_BLOCK), 1)
    from_cur = col <= row
    return from_cur, jnp.logical_or(from_cur, n > 0)


def _stack_heads(ref, heads, width=HEAD_DIM):
    return jnp.concatenate([ref[:, h * width:(h + 1) * width] for h in heads], axis=0)


def _unstack_heads(t, count):
    return [t[j * ATTN_BLOCK:(j + 1) * ATTN_BLOCK] for j in range(count)]


def _attn_scores(qh, kc, kp, from_cur, valid):
    qs = qh * (HEAD_DIM ** -0.5)
    s_c = lax.dot_general(qs, kc, NT, preferred_element_type=F32)
    s_p = lax.dot_general(qs, kp, NT, preferred_element_type=F32)
    return jnp.where(valid, jnp.where(from_cur, s_c, s_p), NEG_BIG)


def _attn_split(t, from_cur):
    t = t.astype(BF16)
    zero = jnp.zeros_like(t)
    return jnp.where(from_cur, t, zero), jnp.where(from_cur, zero, t)


def _attn_fwd(u, sinks, C):
    T = u.shape[0]
    H = C // HEAD_DIM
    grp = H // N_KV_HEADS

    def body(sink_ref, q_ref, kc_ref, kp_ref, vc_ref, vp_ref, o_ref, lse_ref):
        n = pl.program_id(0)
        from_cur, valid = _attn_masks(n, grp * ATTN_BLOCK)
        outs, lses = [], []
        for g in range(N_KV_HEADS):
            kv = slice(g * HEAD_DIM, (g + 1) * HEAD_DIM)
            heads = range(g * grp, (g + 1) * grp)
            sink = jnp.concatenate([jnp.full((ATTN_BLOCK, 1), sink_ref[h], F32) for h in heads], axis=0)
            s = _attn_scores(_stack_heads(q_ref, heads), kc_ref[:, kv], kp_ref[:, kv], from_cur, valid)
            m = jnp.maximum(jnp.max(s, axis=-1, keepdims=True), sink)
            p = jnp.exp(s - m)
            den = jnp.sum(p, axis=-1, keepdims=True) + jnp.exp(sink - m)
            p_c, p_p = _attn_split(p, from_cur)
            acc = jnp.dot(p_c, vc_ref[:, kv], preferred_element_type=F32)
            acc = acc + jnp.dot(p_p, vp_ref[:, kv], preferred_element_type=F32)
            outs += _unstack_heads((acc / den).astype(BF16), grp)
            lses += _unstack_heads(m + jnp.log(den), grp)
        o_ref[...] = jnp.concatenate(outs, axis=1)
        lse_ref[...] = jnp.concatenate(lses, axis=1)

    qs, kc, kp, vc, vp = _attn_specs(C)
    return pl.pallas_call(
        body, name="attn_fwd", grid=(T // ATTN_BLOCK,),
        in_specs=[pl.BlockSpec(memory_space=pltpu.SMEM), qs, kc, kp, vc, vp],
        out_specs=[pl.BlockSpec((ATTN_BLOCK, C), lambda n: (n, 0)), pl.BlockSpec((ATTN_BLOCK, H), lambda n: (n, 0))],
        out_shape=[jax.ShapeDtypeStruct((T, C), BF16), jax.ShapeDtypeStruct((T, H), F32)],
        compiler_params=_params(("parallel",)))(sinks, u, u, u, u, u)


def _attn_bwd(u, o, lse, d_o, sinks, C):
    T = u.shape[0]
    H = C // HEAD_DIM
    grp = H // N_KV_HEADS
    kvw = N_KV_HEADS * HEAD_DIM
    nblk = T // ATTN_BLOCK
    scale = HEAD_DIM ** -0.5

    def body(sink_ref, q_ref, kc_ref, kp_ref, vc_ref, vp_ref, o_ref, lse_ref, do_ref,
             dq_ref, dk_ref, dv_ref, ds_ref, dk_acc, dv_acc):
        n = pl.program_id(0)

        @pl.when(n == 0)
        def _():
            dk_acc[...] = jnp.zeros_like(dk_acc)
            dv_acc[...] = jnp.zeros_like(dv_acc)
            ds_ref[...] = jnp.zeros_like(ds_ref)

        from_cur, valid = _attn_masks(n, grp * ATTN_BLOCK)
        cur = pl.ds(pl.multiple_of(n * ATTN_BLOCK, ATTN_BLOCK), ATTN_BLOCK)
        prev = pl.ds(pl.multiple_of(jnp.maximum(n - 1, 0) * ATTN_BLOCK, ATTN_BLOCK), ATTN_BLOCK)
        dqs, dsinks, dk_cs, dk_ps, dv_cs, dv_ps = [], [], [], [], [], []
        for g in range(N_KV_HEADS):
            kv = slice(g * HEAD_DIM, (g + 1) * HEAD_DIM)
            kc, kp, vc, vp = kc_ref[:, kv], kp_ref[:, kv], vc_ref[:, kv], vp_ref[:, kv]
            heads = range(g * grp, (g + 1) * grp)
            qg, dog, og = _stack_heads(q_ref, heads), _stack_heads(do_ref, heads), _stack_heads(o_ref, heads)
            lse_g = _stack_heads(lse_ref, heads, 1)
            sink = jnp.concatenate([jnp.full((ATTN_BLOCK, 1), sink_ref[h], F32) for h in heads], axis=0)
            p = jnp.exp(_attn_scores(qg, kc, kp, from_cur, valid) - lse_g)
            delta = jnp.sum(dog.astype(F32) * og.astype(F32), axis=-1, keepdims=True)
            dp = jnp.where(from_cur, lax.dot_general(dog, vc, NT, preferred_element_type=F32),
                           lax.dot_general(dog, vp, NT, preferred_element_type=F32))
            ds_c, ds_p = _attn_split(p * (dp - delta) * scale, from_cur)
            p_c, p_p = _attn_split(p, from_cur)
            dq = jnp.dot(ds_c, kc, preferred_element_type=F32) + jnp.dot(ds_p, kp, preferred_element_type=F32)
            dqs += _unstack_heads(dq.astype(BF16), grp)
            dk_cs.append(lax.dot_general(ds_c, qg, TN, preferred_element_type=F32))
            dk_ps.append(lax.dot_general(ds_p, qg, TN, preferred_element_type=F32))
            dv_cs.append(lax.dot_general(p_c, dog, TN, preferred_element_type=F32))
            dv_ps.append(lax.dot_general(p_p, dog, TN, preferred_element_type=F32))
            for t in _unstack_heads(jnp.exp(sink - lse_g) * delta, grp):
                dsinks.append(jnp.broadcast_to(-jnp.sum(t, axis=0, keepdims=True), (8, 1)))
        dq_ref[...] = jnp.concatenate(dqs, axis=1)
        ds_ref[...] += jnp.concatenate(dsinks, axis=1)
        dk_acc[cur, :] += jnp.concatenate(dk_cs, axis=1)
        dk_acc[prev, :] += jnp.concatenate(dk_ps, axis=1)
        dv_acc[cur, :] += jnp.concatenate(dv_cs, axis=1)
        dv_acc[prev, :] += jnp.concatenate(dv_ps, axis=1)

        @pl.when(n == nblk - 1)
        def _():
            dk_ref[...] = dk_acc[...].astype(BF16)
            dv_ref[...] = dv_acc[...].astype(BF16)

    qs, kc, kp, vc, vp = _attn_specs(C)
    blk = pl.BlockSpec((ATTN_BLOCK, C), lambda n: (n, 0))
    full = pl.BlockSpec((T, kvw), lambda n: (0, 0))
    return pl.pallas_call(
        body, name="attn_bwd", grid=(nblk,),
        in_specs=[pl.BlockSpec(memory_space=pltpu.SMEM), qs, kc, kp, vc, vp, blk,
                  pl.BlockSpec((ATTN_BLOCK, H), lambda n: (n, 0)), blk],
        out_specs=[blk, full, full, pl.BlockSpec((8, H), lambda n: (0, 0))],
        out_shape=[jax.ShapeDtypeStruct((T, C), BF16), jax.ShapeDtypeStruct((T, kvw), BF16),
                   jax.ShapeDtypeStruct((T, kvw), BF16), jax.ShapeDtypeStruct((8, H), F32)],
        scratch_shapes=[pltpu.VMEM((T, kvw), F32), pltpu.VMEM((T, kvw), F32)],
        compiler_params=_params(("arbitrary",)))(sinks, u, u, u, u, u, o, lse, d_o)


MERGE_COLS = 512


def _merge_specs(T, D, I):
    tr = _pick(T, ROW_TILE, 8)
    tc = _pick(D, MERGE_COLS)
    g0 = (I - N_BRANCH * D) // tc
    per = D // tc
    gspecs = [pl.BlockSpec((tr, tc), functools.partial(lambda j, i, b: (i, g0 + b * per + j), b=b)) for b in range(N_BRANCH)]
    tile = pl.BlockSpec((tr, tc), lambda j, i: (i, j))
    bias = pl.BlockSpec((N_BRANCH, tc), lambda j, i: (0, j))
    return tr, tc, gspecs, tile, bias


def _merge_fwd(u, gate_b, ya, yb, yc):
    T, I = u.shape
    D = ya.shape[1]
    tr, tc, gspecs, tile, bias = _merge_specs(T, D, I)

    def body(g0_ref, g1_ref, g2_ref, b_ref, ya_ref, yb_ref, yc_ref, o_ref):
        acc = None
        for b, (g_ref, y_ref) in enumerate(((g0_ref, ya_ref), (g1_ref, yb_ref), (g2_ref, yc_ref))):
            gate = _sigmoid(g_ref[...].astype(F32) + b_ref[b:b + 1, :])
            term = gate * y_ref[...].astype(F32)
            acc = term if acc is None else acc + term
        o_ref[...] = acc.astype(BF16)

    return pl.pallas_call(
        body, name="merge_fwd", grid=(D // tc, T // tr), in_specs=[*gspecs, bias, tile, tile, tile], out_specs=tile,
        out_shape=jax.ShapeDtypeStruct((T, D), BF16),
        compiler_params=_params(("parallel", "parallel")))(u, u, u, gate_b, ya, yb, yc)


def _merge_bwd(u, gate_b, ya, yb, yc, dm):
    T, I = u.shape
    D = ya.shape[1]
    tr, tc, gspecs, tile, bias = _merge_specs(T, D, I)

    def body(g0_ref, g1_ref, g2_ref, b_ref, ya_ref, yb_ref, yc_ref, dm_ref,
             dya_ref, dyb_ref, dyc_ref, dg0_ref, dg1_ref, dg2_ref, db_ref):
        dmv = dm_ref[...].astype(F32)
        first = pl.program_id(1) == 0
        for b, (g_ref, y_ref, dy_ref, dg_ref) in enumerate(((g0_ref, ya_ref, dya_ref, dg0_ref),
                                                           (g1_ref, yb_ref, dyb_ref, dg1_ref),
                                                           (g2_ref, yc_ref, dyc_ref, dg2_ref))):
            gate = _sigmoid(g_ref[...].astype(F32) + b_ref[b:b + 1, :])
            dy_ref[...] = (dmv * gate).astype(BF16)
            dpre = dmv * y_ref[...].astype(F32) * gate * (1.0 - gate)
            dg_ref[...] = dpre.astype(BF16)
            part = jnp.sum(dpre, axis=0, keepdims=True)

            @pl.when(first)
            def _():
                db_ref[b:b + 1, :] = part

            @pl.when(jnp.logical_not(first))
            def _():
                db_ref[b:b + 1, :] += part

    act = jax.ShapeDtypeStruct((T, D), BF16)
    return pl.pallas_call(
        body, name="merge_bwd", grid=(D // tc, T // tr), in_specs=[*gspecs, bias, tile, tile, tile, tile],
        out_specs=[tile] * 6 + [bias], out_shape=[act] * 6 + [jax.ShapeDtypeStruct((N_BRANCH, D), F32)],
        compiler_params=_params(("parallel", "arbitrary")))(u, u, u, gate_b, ya, yb, yc, dm)


def _concat_columns(parts):
    T = parts[0].shape[0]
    widths = [p.shape[1] for p in parts]
    tr = _pick(T, ROW_TILE, 16)

    def body(*refs):
        o_ref, off = refs[-1], 0
        for ref, width in zip(refs[:-1], widths):
            o_ref[:, off:off + width] = ref[...]
            off += width

    return pl.pallas_call(
        body, name="concat_columns", grid=(T // tr,),
        in_specs=[pl.BlockSpec((tr, width), lambda i: (i, 0)) for width in widths],
        out_specs=pl.BlockSpec((tr, sum(widths)), lambda i: (i, 0)),
        out_shape=jax.ShapeDtypeStruct((T, sum(widths)), parts[0].dtype),
        compiler_params=_params(("parallel",)))(*parts)


ELEMS_PER_TILE = 512 * 1024


def _row_tile(r, c):
    return _pick(r, max(16, ELEMS_PER_TILE // c), 16) if r % 16 == 0 else r


def _cast_place(w, layer, my_block):
    L, r, c = w.shape
    tr = _row_tile(r, c)

    def body(blk_ref, w_ref, o_ref):
        o_ref[...] = w_ref[...].astype(BF16)

    return pl.pallas_call(
        body, name="cast_place",
        grid_spec=pltpu.PrefetchScalarGridSpec(
            num_scalar_prefetch=1, grid=(r // tr,),
            in_specs=[pl.BlockSpec((None, tr, c), lambda i, blk: (layer, i, 0))],
            out_specs=pl.BlockSpec((None, tr, c), lambda i, blk: (blk[0], i, 0))),
        out_shape=jax.ShapeDtypeStruct((N_DEV, r, c), BF16), compiler_params=_params(("parallel",)))(my_block, w)


def _adamw_math(w, g, m, v):
    m = ADAM_B1 * m + (1.0 - ADAM_B1) * g
    v = ADAM_B2 * v + (1.0 - ADAM_B2) * (g * g)
    m_hat = m / (1.0 - ADAM_B1 ** ADAM_STEP)
    v_hat = v / (1.0 - ADAM_B2 ** ADAM_STEP)
    delta = -ADAM_LR * (m_hat / (jnp.sqrt(v_hat) + ADAM_EPS) + ADAM_WD * w)
    return delta, m, v


def _sum_parts(part_ref):
    acc = part_ref[0].astype(F32)
    for s in range(1, N_DEV):
        acc = acc + part_ref[s].astype(F32)
    return acc


def _sum8(parts):
    L, _, r, c = parts.shape
    tr = _row_tile(r, c)

    def body(p_ref, o_ref):
        o_ref[...] = _sum_parts(p_ref)

    return pl.pallas_call(
        body, name="sum8", grid=(L, r // tr),
        in_specs=[pl.BlockSpec((None, N_DEV, tr, c), lambda l, i: (l, 0, i, 0))],
        out_specs=pl.BlockSpec((None, tr, c), lambda l, i: (l, i, 0)),
        out_shape=jax.ShapeDtypeStruct((L, r, c), F32), compiler_params=_params(("parallel", "parallel")))(parts)


def _adamw(w, g, m, v):
    L, r, c = w.shape
    tr = _row_tile(r, c)
    spec = pl.BlockSpec((None, tr, c), lambda l, i: (l, i, 0))

    def body(w_ref, g_ref, m_ref, v_ref, d_ref, mo_ref, vo_ref):
        d, mn, vn = _adamw_math(w_ref[...], g_ref[...], m_ref[...], v_ref[...])
        d_ref[...] = d
        mo_ref[...] = mn
        vo_ref[...] = vn

    shp = jax.ShapeDtypeStruct(w.shape, F32)
    return pl.pallas_call(body, name="adamw", grid=(L, r // tr), in_specs=[spec] * 4, out_specs=[spec] * 3,
                          out_shape=[shp] * 3, compiler_params=_params(("parallel", "parallel")))(w, g, m, v)


N_CHIPS = 4
CHIP_XOR = (0, 2, 1, 3)


def _chip_sum(part4, sib4, own_all, layer, pos):
    _, _, r, c = part4.shape
    tr = _row_tile(r, c)

    def chip(p, s):
        return jnp.bitwise_xor(2 * p[0] + p[1], CHIP_XOR[s])

    mine = [pl.BlockSpec((None, None, tr, c), functools.partial(lambda i, p, s: (chip(p, s), p[2], i, 0), s=s))
            for s in range(N_CHIPS)]
    theirs = [pl.BlockSpec((None, None, tr, c), functools.partial(lambda i, p, s: (chip(p, s), 0, i, 0), s=s))
              for s in range(N_CHIPS)]

    def body(pos_ref, *refs):
        a, b = refs[:N_CHIPS], refs[N_CHIPS:2 * N_CHIPS]
        out_ref, own_ref = refs[2 * N_CHIPS + 1], refs[2 * N_CHIPS + 2]
        own_ref[...] = (a[0][...].astype(F32) + b[0][...].astype(F32)).astype(BF16)
        for s in range(1, N_CHIPS):
            out_ref[s - 1] = (a[s][...].astype(F32) + b[s][...].astype(F32)).astype(BF16)

    return pl.pallas_call(
        body, name="chip_sum",
        grid_spec=pltpu.PrefetchScalarGridSpec(
            num_scalar_prefetch=1, grid=(r // tr,),
            in_specs=[*mine, *theirs, pl.BlockSpec(memory_space=pl.ANY)],
            out_specs=[pl.BlockSpec((N_CHIPS - 1, tr, c), lambda i, p: (0, i, 0)),
                       pl.BlockSpec((None, tr, c), lambda i, p: (layer, i, 0))]),
        out_shape=[jax.ShapeDtypeStruct((N_CHIPS - 1, r, c), BF16), jax.ShapeDtypeStruct(own_all.shape, BF16)],
        input_output_aliases={1 + 2 * N_CHIPS: 1},
        compiler_params=_params(("parallel",)))(pos, *([part4] * N_CHIPS), *([sib4] * N_CHIPS), own_all)


def _sum_chips(own_ref, got_ref):
    acc = own_ref[...].astype(F32)
    for s in range(N_CHIPS - 1):
        acc = acc + got_ref[s].astype(F32)
    return acc


def _sum4_adamw(own, got, w, m, v, first, count, earlier=None):
    L, r, c = w.shape
    tr = _row_tile(r, c)
    spec = pl.BlockSpec((None, tr, c), lambda l, i: (first + l, i, 0))
    earlier = [] if earlier is None else list(earlier)

    def body(own_ref, got_ref, w_ref, m_ref, v_ref, *rest):
        g_ref, d_ref, mo_ref, vo_ref = rest[len(earlier):]
        g = _sum_chips(own_ref, got_ref)
        d, mn, vn = _adamw_math(w_ref[...], g, m_ref[...], v_ref[...])
        g_ref[...] = g
        d_ref[...] = d
        mo_ref[...] = mn
        vo_ref[...] = vn

    shp = jax.ShapeDtypeStruct(w.shape, F32)
    return pl.pallas_call(
        body, name="sum4_adamw", grid=(count, r // tr),
        in_specs=[spec, pl.BlockSpec((None, N_CHIPS - 1, tr, c), lambda l, i: (first + l, 0, i, 0)), spec, spec, spec,
                  *([ANY] * len(earlier))],
        out_specs=[spec] * 4, out_shape=[shp] * 4, input_output_aliases={5 + a: a for a in range(len(earlier))},
        compiler_params=_params(("parallel", "parallel")))(own, got, w, m, v, *earlier)


def _me():
    return lax.axis_index("x"), lax.axis_index("y"), lax.axis_index("c")


def _flip(pos, k):
    x, y, c = pos
    return (1 - x if k & 4 else x, 1 - y if k & 2 else y, 1 - c if k & 1 else c)


def _block_of(pos):
    return 4 * pos[0] + 2 * pos[1] + pos[2]


ANY = pl.BlockSpec(memory_space=pl.ANY)
SIBLING = 1
OTHER_CHIPS = (4, 2, 6)


def _all_gather(name, arrays):
    n = len(arrays)
    shapes = [a.shape[-2:] for a in arrays]

    def body(*refs):
        srcs, outs = refs[:n], refs[n:2 * n]
        send_sems, recv_sems, local_sems = refs[2 * n:]
        me = _me()
        sib = _flip(me, SIBLING)

        def copy(a, k, block_pos, to, src=None):
            dst = outs[a].at[_block_of(block_pos)]
            return pltpu.make_async_remote_copy(
                src_ref=dst if src is None else src, dst_ref=dst, send_sem=send_sems.at[a, k],
                recv_sem=recv_sems.at[a, k], device_id=to, device_id_type=MESH)

        mine = [pltpu.make_async_copy(srcs[a], outs[a].at[_block_of(me)], local_sems.at[a]) for a in range(n)]
        for cp in mine:
            cp.start()
        first = []
        for a in range(n):
            first.append(copy(a, 0, me, sib, src=srcs[a]))
            for j, k in enumerate(OTHER_CHIPS):
                first.append(copy(a, 1 + j, me, _flip(me, k), src=srcs[a]))
        for cp in first:
            cp.start()
        passed = []
        for j, k in enumerate(OTHER_CHIPS):
            for a in range(n):
                copy(a, 1 + j, _flip(me, k), me).wait_recv()
                fw = copy(a, 4 + j, _flip(me, k), sib)
                fw.start()
                passed.append(fw)
        for a in range(n):
            copy(a, 0, sib, me).wait_recv()
            for j, k in enumerate(OTHER_CHIPS):
                copy(a, 4 + j, _flip(sib, k), me).wait_recv()
        for cp in first + passed:
            cp.wait_send()
        for cp in mine:
            cp.wait()

    return pl.pallas_call(
        body, name=name, in_specs=[ANY] * n, out_specs=[ANY] * n,
        out_shape=[jax.ShapeDtypeStruct((N_DEV, *s), a.dtype) for s, a in zip(shapes, arrays)],
        scratch_shapes=[pltpu.SemaphoreType.DMA((n, 7)), pltpu.SemaphoreType.DMA((n, 7)), pltpu.SemaphoreType.DMA((n,))],
    )(*arrays)


HBM = pl.BlockSpec(memory_space=pltpu.HBM)
SEM = pl.BlockSpec(memory_space=pltpu.SEMAPHORE)
DATAFLOW = pltpu.SideEffectType.DATAFLOW_SIDE_EFFECTING
X_NEIGHBOUR, Y_NEIGHBOUR = 4, 2
NEAR = (SIBLING, X_NEIGHBOUR, Y_NEIGHBOUR)


def _in_hbm(a):
    return pltpu.with_memory_space_constraint(a, pltpu.HBM)


def _hbm_like(a):
    return pltpu.HBM(a.shape, a.dtype)


def _copies_start(name, srcs, lands, plan, n_copies, after=None):
    ns, n = len(srcs), len(lands)
    deps = [] if after is None else [after]

    def body(*refs):
        k0 = ns + n + len(deps)
        send_sems, recv_sems, token = refs[k0], refs[k0 + 1], refs[k0 + 2 + n]
        for s, (src, dst, peer, _) in enumerate(plan(_me(), refs[:ns], refs[ns:ns + n])):
            pltpu.make_async_remote_copy(src_ref=src, dst_ref=dst, send_sem=send_sems.at[s], recv_sem=recv_sems.at[s],
                                         device_id=peer, device_id_type=MESH).start()
        token[...] = jnp.zeros(TOKEN, F32)

    out = pl.pallas_call(
        body, name=name, in_specs=[*([HBM] * (ns + n)), *([ANY] * len(deps))],
        out_specs=[SEM, SEM, *([HBM] * n), pl.BlockSpec(memory_space=pltpu.VMEM)],
        out_shape=[pltpu.SemaphoreType.DMA((n_copies,)), pltpu.SemaphoreType.DMA((n_copies,)),
                   *[_hbm_like(a) for a in lands], jax.ShapeDtypeStruct(TOKEN, F32)],
        input_output_aliases={ns + a: 2 + a for a in range(n)},
        compiler_params=pltpu.CompilerParams(has_side_effects=DATAFLOW),
    )(*[_in_hbm(a) for a in srcs], *[_in_hbm(a) for a in lands], *deps)
    return dict(send=out[0], recv=out[1], srcs=list(srcs), plan=plan), list(out[2:2 + n]), out[2 + n]


def _copies_wait(name, flight, lands, after):
    srcs, plan = flight["srcs"], flight["plan"]
    ns, n = len(srcs), len(lands)
    after = list(after) if isinstance(after, (list, tuple)) else [after]

    def body(*refs):
        send_sems, recv_sems = refs[ns + n], refs[ns + n + 1]
        for s, (src, dst, peer, landing) in enumerate(plan(_me(), refs[:ns], refs[ns:ns + n])):
            pltpu.make_async_remote_copy(src_ref=src, dst_ref=dst, send_sem=send_sems.at[s], recv_sem=recv_sems.at[s],
                                         device_id=peer, device_id_type=MESH).wait_send()
            pltpu.make_async_remote_copy(src_ref=landing, dst_ref=landing, send_sem=send_sems.at[s],
                                         recv_sem=recv_sems.at[s], device_id=peer, device_id_type=MESH).wait_recv()

    out = pl.pallas_call(
        body, name=name, in_specs=[*([HBM] * (ns + n)), SEM, SEM, *([ANY] * len(after))], out_specs=[HBM] * n,
        out_shape=[_hbm_like(a) for a in lands], input_output_aliases={ns + a: a for a in range(n)},
        compiler_params=pltpu.CompilerParams(has_side_effects=DATAFLOW),
    )(*srcs, *lands, flight["send"], flight["recv"], *after)
    return list(out)


def _gather_plan_near(me, srcs, lands):
    plan = []
    for land in lands:
        own = land.at[_block_of(me)]
        for k in NEAR:
            peer = _flip(me, k)
            plan.append((own, own, peer, land.at[_block_of(peer)]))
    return plan


def _gather_plan_far(me, srcs, lands):
    x_nbr, y_nbr, far = _flip(me, X_NEIGHBOUR), _flip(me, Y_NEIGHBOUR), _flip(me, X_NEIGHBOUR | Y_NEIGHBOUR)
    plan = []
    for land in lands:
        half = land.shape[1] // 2
        first, second = pl.ds(0, half), pl.ds(half, half)
        passed = land.at[_block_of(y_nbr), first]
        plan.append((passed, passed, x_nbr, land.at[_block_of(far), first]))
        passed = land.at[_block_of(x_nbr), second]
        plan.append((passed, passed, y_nbr, land.at[_block_of(far), second]))
    return plan


def _broadcast_plan(me, srcs, lands):
    plan = []
    for land in lands:
        own = land.at[_block_of(me)]
        for k in range(1, N_DEV):
            peer = _flip(me, k)
            plan.append((own, own, peer, land.at[_block_of(peer)]))
    return plan


def _sibling_plan(me, srcs, lands):
    sib = _flip(me, SIBLING)
    return [(src.at[:, pl.ds(1 - me[2], 1)], land, sib, land) for src, land in zip(srcs, lands)]


def _scatter_plan(layer):
    def plan(me, srcs, lands):
        out = []
        for src, land in zip(srcs, lands):
            for j, k in enumerate(OTHER_CHIPS):
                out.append((src.at[j], land.at[layer, j], _flip(me, k), land.at[layer, j]))
        return out
    return plan


def _pass_on_plan(relations):
    def plan(me, srcs, lands):
        sib = _flip(me, SIBLING)
        out = []
        for land in lands:
            for k in relations:
                blk = land.at[_block_of(_flip(me, k))]
                out.append((blk, blk, sib, land.at[_block_of(_flip(sib, k))]))
        return out
    return plan


def _after(small, tokens):
    for t in tokens:
        small = small + t[0:1, 0:1]
    return small


def _layer_fwd(xc, l, W, P, dims, deps=(), mid_layer=None):
    T, D, C, I, F = dims
    h = _rms_fwd(xc, _after(P["norm_mix_g"][l:l + 1], deps))
    u = _mm_plain("mm_u", h, W["w_in"][l].reshape(I, D), "NT", [BF16], tn=1280)[0]
    a1, s2 = _convs_fwd(u, P["conf_dw"][l], P["sconv_w"][l], C)
    a3 = _ln_silu(a1, P["conf_ln_g"][l:l + 1], P["conf_ln_b"][l:l + 1])
    o, lse = _attn_fwd(u, P["sinks"][l], C)
    ya = _mm_branch("mm_branch_out", a3, W["w_conf_out"][l], "NN", BF16)
    yb = _mm_branch("mm_branch_out", s2, W["w_sconv_out"][l], "NN", BF16)
    yc = _mm_branch("mm_branch_out", o, W["w_attn_out"][l], "NN", BF16)
    merged = _merge_fwd(u, P["gate_b"][l], ya, yb, yc)
    x1 = _mm_plain("mm_mix", merged, W["w_mix_out"][l].reshape(D, D), "NN", [F32], _epi_resid, [xc], tk=2048)[0]
    norm_ffn_g = P["norm_ffn_g"][l:l + 1]
    h2 = _rms_fwd(x1, norm_ffn_g if mid_layer is None else _after(norm_ffn_g, [mid_layer(x1)]))
    up, act = _mm_nn_colblocked("mm_up", h2, W["w_up"][l], [BF16, BF16], _epi_relu2)
    x2 = _mm_plain("mm_down", act, W["w_down"][l].reshape(F, D), "NN", [F32], _epi_resid, [x1], tk=2048)[0]
    saved = dict(xc=xc, h=h, u=u, a1=a1, a3=a3, s2=s2, o=o, lse=lse, ya=ya, yb=yb, yc=yc, merged=merged, x1=x1, h2=h2,
                 up=up, act=act)
    return x2, saved


def _bwd_mlp(dx2, dx2_b, l, W, P, S, dims, dep=None):
    T, D, C, I, F = dims
    d_up = _mm_plain("mm_d_up", dx2_b, W["w_down"][l].reshape(F, D), "NT", [BF16], _epi_drelu2, [S["up"]], dep=dep)[0]
    g_down = _mm_plain("mm_g_down", S["act"], dx2_b, "TN", [BF16])[0]
    dh2 = _mm_nt_colblocked("mm_d_h2", d_up, W["w_up"][l], F32)
    g_up = _mm_tn_colblocked_out("mm_g_up", S["h2"], d_up, BF16)
    dx1, dx1_b, dg_ffn = _rms_bwd(dh2, S["x1"], P["norm_ffn_g"][l:l + 1], dx2)
    return dx1, dx1_b, dict(w_up=g_up, w_down=g_down.reshape(N_DEV, F // N_DEV, D)), dg_ffn


def _bwd_mix_out(dx1_b, l, W, P, S, dims, dep=None):
    T, D, C, I, F = dims
    dm = _mm_plain("mm_d_merged", dx1_b, W["w_mix_out"][l].reshape(D, D), "NT", [BF16], dep=dep)[0]
    g_mix = _mm_plain("mm_g_mix", S["merged"], dx1_b, "TN", [BF16])[0]
    merge = _merge_bwd(S["u"], P["gate_b"][l], S["ya"], S["yb"], S["yc"], dm)
    return g_mix.reshape(N_DEV, D // N_DEV, D), merge


def _bwd_mixers(dx1, merge, dg_ffn, g_mix, l, W, P, S, dims, dep, send_mid, send_in):
    T, D, C, I, F = dims
    d_ya, d_yb, d_yc, dg0, dg1, dg2, d_gate_b = merge
    d_a3 = _mm_branch("mm_d_branch", d_ya, W["w_conf_out"][l], "NT", BF16, dep=dep)
    d_s2 = _mm_branch("mm_d_branch", d_yb, W["w_sconv_out"][l], "NT", BF16)
    d_o = _mm_branch("mm_d_branch", d_yc, W["w_attn_out"][l], "NT", BF16)
    g_conf = _mm_branch_grad("mm_g_branch", S["a3"], d_ya, BF16)
    g_sconv = _mm_branch_grad("mm_g_branch", S["s2"], d_yb, BF16)
    g_attn = _mm_branch_grad("mm_g_branch", S["o"], d_yc, BF16)
    tok = send_mid(dict(w_mix_out=g_mix, w_conf_out=g_conf, w_sconv_out=g_sconv, w_attn_out=g_attn), g_attn)
    d_a1, d_ln_g, d_ln_b = _ln_silu_bwd(S["a1"], _after(P["conf_ln_g"][l:l + 1], [tok]), P["conf_ln_b"][l:l + 1], d_a3)
    d_av, d_ag, d_bg, d_cg, d_bh, d_conf_dw, d_sconv_w = _convs_bwd(S["u"], P["conf_dw"][l], P["sconv_w"][l], d_a1, d_s2, C)
    dq, dk, dv, d_sinks = _attn_bwd(S["u"], S["o"], S["lse"], d_o, P["sinks"][l], C)
    du = _concat_columns([d_av, d_ag, d_bg, d_cg, d_bh, dq, dk, dv, dg0, dg1, dg2])
    g_in = _mm_plain("mm_g_in", du, S["h"], "TN", [BF16], tm=1280)[0]
    tok = send_in(dict(w_in=g_in.reshape(N_DEV, I // N_DEV, D)), g_in)
    dh = _mm_plain("mm_d_h", du, W["w_in"][l].reshape(I, D), "NN", [F32], tk=2560, dep=tok)[0]
    dx, dx_b, dg_mix = _rms_bwd(dh, S["xc"], P["norm_mix_g"][l:l + 1], dx1)
    small = dict(norm_mix_g=dg_mix, gate_b=d_gate_b, conf_dw=d_conf_dw, conf_ln_g=d_ln_g, conf_ln_b=d_ln_b,
                 sconv_w=d_sconv_w, sinks=d_sinks[0:1], norm_ffn_g=dg_ffn)
    return dx, dx_b, small


BIG = ("w_in", "w_conf_out", "w_sconv_out", "w_attn_out", "w_mix_out", "w_up", "w_down")
MLP_WEIGHTS = ("w_down", "w_up")
MID_WEIGHTS = ("w_mix_out", "w_conf_out", "w_sconv_out", "w_attn_out")
IN_WEIGHTS = ("w_in",)
SMALL_PER_LAYER = ("norm_mix_g", "gate_b", "conf_dw", "conf_ln_g", "conf_ln_b", "sconv_w", "sinks", "norm_ffn_g")
WEIGHTS = ("norm_mix_g", "w_in", "gate_b", "conf_dw", "conf_ln_g", "conf_ln_b", "w_conf_out", "sconv_w", "w_sconv_out",
           "sinks", "w_attn_out", "w_mix_out", "norm_ffn_g", "w_up", "w_down", "final_g")


SUBLANES = 8


def _nrows(n_el, width):
    per_tile = SUBLANES * width
    return SUBLANES * (-(-n_el // per_tile))


def _rows(a, width):
    flat = a.reshape(-1)
    nrow = _nrows(flat.shape[0], width)
    return jnp.pad(flat, (0, nrow * width - flat.shape[0])).reshape(nrow, width)


def _as3d(a):
    if a.ndim == 1:
        return a.reshape(1, 1, -1)
    if a.ndim == 2:
        return a.reshape(1, *a.shape)
    return a


def kernel(x, norm_mix_g, w_in, gate_b, conf_dw, conf_ln_g, conf_ln_b, w_conf_out, sconv_w, w_sconv_out, sinks, w_attn_out, w_mix_out, norm_ffn_g, w_up, w_down, final_g, loss_target, m_norm_mix_g, m_w_in, m_gate_b, m_conf_dw, m_conf_ln_g, m_conf_ln_b, m_w_conf_out, m_sconv_w, m_w_sconv_out, m_sinks, m_w_attn_out, m_w_mix_out, m_norm_ffn_g, m_w_up, m_w_down, m_final_g, v_norm_mix_g, v_w_in, v_gate_b, v_conf_dw, v_conf_ln_g, v_conf_ln_b, v_w_conf_out, v_sconv_w, v_w_sconv_out, v_sinks, v_w_attn_out, v_w_mix_out, v_norm_ffn_g, v_w_up, v_w_down, v_final_g):
    w = dict(norm_mix_g=norm_mix_g, w_in=w_in, gate_b=gate_b, conf_dw=conf_dw, conf_ln_g=conf_ln_g, conf_ln_b=conf_ln_b,
             w_conf_out=w_conf_out, sconv_w=sconv_w, w_sconv_out=w_sconv_out, sinks=sinks, w_attn_out=w_attn_out,
             w_mix_out=w_mix_out, norm_ffn_g=norm_ffn_g, w_up=w_up, w_down=w_down, final_g=final_g)
    mom = dict(norm_mix_g=m_norm_mix_g, w_in=m_w_in, gate_b=m_gate_b, conf_dw=m_conf_dw, conf_ln_g=m_conf_ln_g,
               conf_ln_b=m_conf_ln_b, w_conf_out=m_w_conf_out, sconv_w=m_sconv_w, w_sconv_out=m_w_sconv_out,
               sinks=m_sinks, w_attn_out=m_w_attn_out, w_mix_out=m_w_mix_out, norm_ffn_g=m_norm_ffn_g, w_up=m_w_up,
               w_down=m_w_down, final_g=m_final_g)
    var = dict(norm_mix_g=v_norm_mix_g, w_in=v_w_in, gate_b=v_gate_b, conf_dw=v_conf_dw, conf_ln_g=v_conf_ln_g,
               conf_ln_b=v_conf_ln_b, w_conf_out=v_w_conf_out, sconv_w=v_sconv_w, w_sconv_out=v_w_sconv_out,
               sinks=v_sinks, w_attn_out=v_w_attn_out, w_mix_out=v_w_mix_out, norm_ffn_g=v_norm_ffn_g, w_up=v_w_up,
               w_down=v_w_down, final_g=v_final_g)

    _, T, D = x.shape
    L = w_in.shape[0]
    C = D // 2
    I = w_in.shape[2] * N_DEV
    F = w_up.shape[2] * N_DEV
    dims = (T, D, C, I, F)
    my_block = _block_of(_me())

    w["w_in"], mom["w_in"], var["w_in"] = (jnp.swapaxes(a, 1, 2) for a in (w_in, m_w_in, v_w_in))

    shard_names = ("gate_b", "conf_dw", "sconv_w")
    packed = jnp.concatenate([_rows(w[k], LANES) for k in shard_names], axis=0)
    gathered = _all_gather("gather_small", [packed])[0]

    pos = jnp.stack(_me()).astype(jnp.int32)
    blk = my_block.reshape(1).astype(jnp.int32)
    W = {k: [_cast_place(w[k], l, blk) for l in range(L)] for k in BIG}
    n_near, n_far = len(BIG) * len(NEAR), len(BIG) * 2

    def gather_near(l, after):
        return _copies_start(f"gather_near_start_{l}", [], [W[k][l] for k in BIG], _gather_plan_near, n_near, after)

    def gather_far(l, g, after):
        lands = _copies_wait(f"gather_near_wait_{l}", g["near"], g["lands"], after)
        g["far"], lands, tok_far = _copies_start(f"gather_far_start_{l}", [], lands, _gather_plan_far, n_far)
        g["pass_near"], g["lands"], tok_pass = _copies_start(
            f"pass_near_start_{l}", [], lands, _pass_on_plan((X_NEIGHBOUR, Y_NEIGHBOUR)), 2 * len(BIG))
        return tok_far + tok_pass

    def gather_finish(l, g, after):
        lands = _copies_wait(f"gather_far_wait_{l}", g["far"], g["lands"], after)
        pass_far, lands, token = _copies_start(f"pass_far_start_{l}", [], lands,
                                               _pass_on_plan((X_NEIGHBOUR | Y_NEIGHBOUR,)), len(BIG))
        nxt = {}
        if l + 1 < L:
            nxt["near"], nxt["lands"], token = gather_near(l + 1, token)
        lands = _copies_wait(f"pass_near_wait_{l}", g["pass_near"], lands, [after, token])
        return _copies_wait(f"pass_far_wait_{l}", pass_far, lands, [after, token]), nxt, token

    gathering = {}
    gathering["near"], gathering["lands"], tok = gather_near(0, gathered)
    later_layers = [W[k][l] for l in range(1, L) for k in BIG]
    tok = gather_far(0, gathering, [tok, *later_layers])

    P = dict(norm_mix_g=norm_mix_g, conf_ln_g=conf_ln_g, conf_ln_b=conf_ln_b, sinks=sinks, norm_ffn_g=norm_ffn_g)
    row0 = 0
    for k in shard_names:
        n_el = w[k].size
        nrow = _nrows(n_el, LANES)
        part = gathered[:, row0:row0 + nrow].reshape(N_DEV, -1)[:, :n_el].reshape(N_DEV, *w[k].shape)
        P[k] = jnp.moveaxis(part, 0, 2).reshape(*w[k].shape[:2], N_DEV * w[k].shape[2])
        row0 += nrow

    xc = x.reshape(T, D)
    saved = []
    for l in range(L):
        lands, gathering, tok = gather_finish(l, gathering, xc if l else tok)
        for k, g in zip(BIG, lands):
            W[k][l] = g
        mid_layer = functools.partial(gather_far, l + 1, gathering) if l + 1 < L else None
        xc, S = _layer_fwd(xc, l, W, P, dims, (), mid_layer)
        saved.append(S)
    dx, dx_b, d_final_g, loss_tile = _loss_head(xc, final_g.reshape(1, D), loss_target.reshape(T, D))

    own_all = {k: lax.empty((L, *W[k][0].shape[1:]), BF16) for k in BIG}
    recv = {k: lax.empty((L, N_CHIPS - 1, *W[k][0].shape[1:]), BF16) for k in BIG}
    scatters = []

    def to_sibling(names, grads, l, after):
        part4 = [grads[k].reshape(N_CHIPS, 2, *grads[k].shape[1:]) for k in names]
        zone = [lax.empty((N_CHIPS, 1, *p.shape[2:]), BF16) for p in part4]
        fl, zone, token = _copies_start(f"sibling_start_{l}_{names[0]}", part4, zone, _sibling_plan, len(names), after)
        return dict(names=names, l=l, part4=part4, flight=fl, zone=zone), token

    def to_owners(group, after):
        names, l = group["names"], group["l"]
        sib4 = _copies_wait(f"sibling_wait_{l}_{names[0]}", group["flight"], group["zone"], after)
        chip_parts = []
        for k, p4, s4 in zip(names, group["part4"], sib4):
            cp, own_all[k] = _chip_sum(p4, s4, own_all[k], l, pos)
            chip_parts.append(cp)
        fl, zone, token = _copies_start(f"scatter_start_{l}_{names[0]}", chip_parts, [recv[k] for k in names],
                                        _scatter_plan(l), len(names) * len(OTHER_CHIPS))
        for k, g in zip(names, zone):
            recv[k] = g
        scatters.append((f"scatter_wait_{l}_{names[0]}", fl, names, l))
        return token

    small_grads = [None] * L
    dep, groups = None, {}
    for l in reversed(range(L)):
        S = saved[l]
        dx1, dx1_b, g_mlp, dg_ffn = _bwd_mlp(dx, dx_b, l, W, P, S, dims, dep)
        groups["mlp"], dep = to_sibling(MLP_WEIGHTS, g_mlp, l, dx1)
        if "in" in groups:
            dep = dep + to_owners(groups["in"], dx1)
        g_mix, merge = _bwd_mix_out(dx1_b, l, W, P, S, dims, dep)
        dep = to_owners(groups["mlp"], merge[0])

        def send_mid(grads, after, l=l):
            groups["mid"], token = to_sibling(MID_WEIGHTS, grads, l, after)
            return token

        def send_in(grads, after, l=l):
            token = to_owners(groups["mid"], after)
            groups["in"], token2 = to_sibling(IN_WEIGHTS, grads, l, after)
            return token + token2

        dx, dx_b, small_grads[l] = _bwd_mixers(dx1, merge, dg_ffn, g_mix, l, W, P, S, dims, dep, send_mid, send_in)
    last_start = to_owners(groups["in"], dx)

    width = LANES
    pieces = [_rows(small_grads[l][k], width) for l in range(L) for k in SMALL_PER_LAYER]
    pieces += [_rows(d_final_g, width), _rows(loss_tile[0:1, 0:1], width)]
    partial = jnp.concatenate(pieces, axis=0)
    everyone = lax.dynamic_update_slice(lax.empty((N_DEV, *partial.shape), F32), partial[None], (my_block, 0, 0))
    small_flight, (everyone,), token = _copies_start("small_grads_start", [], [everyone], _broadcast_plan, N_DEV - 1,
                                                     after=last_start)
    last_start = last_start + token

    def await_scatters(layers):
        for name, fl, names, l in scatters:
            if l in layers:
                for k, g in zip(names, _copies_wait(name, fl, [recv[k] for k in names], [dx, last_start])):
                    recv[k] = g

    await_scatters(range(1, L))
    done = {k: _sum4_adamw(own_all[k], recv[k], w[k], mom[k], var[k], 1, L - 1) for k in BIG} if L > 1 else {}
    await_scatters([0])

    grads, delta, new_m, new_v = {}, {}, {}, {}
    for k in BIG:
        grads[k], delta[k], new_m[k], new_v[k] = _sum4_adamw(own_all[k], recv[k], w[k], mom[k], var[k], 0, 1,
                                                             done.get(k))

    everyone = _copies_wait("small_grads_wait", small_flight, [everyone], [grads[k] for k in BIG])[0]
    total = _sum8(everyone.reshape(1, *everyone.shape))[0]
    row0 = 0
    per_layer = {k: [] for k in SMALL_PER_LAYER}
    for l in range(L):
        for k in SMALL_PER_LAYER:
            shape = small_grads[l][k].shape
            n_el = small_grads[l][k].size
            nrow = _nrows(n_el, width)
            per_layer[k].append(total[row0:row0 + nrow].reshape(-1)[:n_el].reshape(shape))
            row0 += nrow
    nrow = _nrows(D, width)
    grads["final_g"] = total[row0:row0 + nrow].reshape(-1)[:D]
    row0 += nrow
    loss = total[row0, 0]
    for k in SMALL_PER_LAYER:
        full = jnp.stack(per_layer[k], axis=0)
        if k in shard_names:
            shard = w[k].shape[2]
            full = lax.dynamic_slice_in_dim(full, my_block * shard, shard, axis=2)
        grads[k] = full.reshape(w[k].shape)

    for out in (grads, delta, new_m, new_v):
        out["w_in"] = jnp.swapaxes(out["w_in"], 1, 2)
    for k in WEIGHTS:
        if k in BIG:
            continue
        d, mn, vn = _adamw(_as3d(w[k]), _as3d(grads[k]), _as3d(mom[k]), _as3d(var[k]))
        delta[k], new_m[k], new_v[k] = d.reshape(w[k].shape), mn.reshape(w[k].shape), vn.reshape(w[k].shape)

    return (loss, dx.reshape(1, T, D), *[grads[k] for k in WEIGHTS], *[delta[k] for k in WEIGHTS],
            *[new_m[k] for k in WEIGHTS], *[new_v[k] for k in WEIGHTS])
```

```python
import functools

import jax
import jax.numpy as jnp
from jax import lax
from jax.experimental import pallas as pl
from jax.experimental.pallas import tpu as pltpu

F32 = jnp.float32
BF16 = jnp.bfloat16

N_DEV = 8
HEAD_DIM = 64
N_KV_HEADS = 4
ATTN_BLOCK = 128
CONF_KERNEL = 31
SCONV_KERNEL = 3
N_BRANCH = 3
RMS_EPS = 1e-6
LN_EPS = 1e-5
ADAM_LR = 0.001
ADAM_B1 = 0.9
ADAM_B2 = 0.999
ADAM_EPS = 1e-08
ADAM_WD = 0.01
ADAM_STEP = 10
LANES = 128
NEG_BIG = -1e30
VMEM_LIMIT_BYTES = 56 * 1024 * 1024
MESH = pl.DeviceIdType.MESH

NN = (((1,), (0,)), ((), ()))
NT = (((1,), (1,)), ((), ()))
TN = (((0,), (0,)), ((), ()))


def _pick(n, cap, mult=LANES):
    best = None
    for d in range(mult, min(n, cap) + 1, mult):
        if n % d == 0:
            best = d
    assert best is not None, (n, cap, mult)
    return best


def _sigmoid(x):
    return 0.5 * jnp.tanh(0.5 * x) + 0.5


def _params(sem):
    return pltpu.CompilerParams(dimension_semantics=sem, vmem_limit_bytes=VMEM_LIMIT_BYTES)


def _epi_cast(p, ex, outs):
    outs[0][...] = p.astype(outs[0].dtype)


def _epi_resid(p, ex, outs):
    outs[0][...] = ex[0][...] + p


def _epi_relu2(p, ex, outs):
    outs[0][...] = p.astype(outs[0].dtype)
    r = jnp.maximum(p, 0.0)
    outs[1][...] = (r * r).astype(outs[1].dtype)


def _epi_drelu2(p, ex, outs):
    up = ex[0][...].astype(F32)
    outs[0][...] = (p * (2.0 * jnp.maximum(up, 0.0))).astype(outs[0].dtype)


TOKEN = (8, LANES)


def _matmul(name, a, b, dnums, grid, a_spec, b_spec, out_shape, out_specs, epi, acc_shape, extra=(), extra_specs=(),
            dep=None):
    if dep is not None:
        extra = [*extra, dep]
        extra_specs = [*extra_specs, pl.BlockSpec(TOKEN, lambda j, i, k: (0, 0))]
    nk = grid[2]
    n_extra, n_out = len(extra), len(out_shape)

    def body(*refs):
        a_ref, b_ref = refs[0], refs[1]
        ex = refs[2:2 + n_extra]
        outs = refs[2 + n_extra:2 + n_extra + n_out]
        p = lax.dot_general(a_ref[...], b_ref[...], dnums, preferred_element_type=F32)
        if nk == 1:
            epi(p, ex, outs)
        else:
            acc = refs[-1]
            k = pl.program_id(2)

            @pl.when(k == 0)
            def _():
                acc[...] = p

            @pl.when(k > 0)
            def _():
                acc[...] += p

            @pl.when(k == nk - 1)
            def _():
                epi(acc[...], ex, outs)

    scratch = [pltpu.VMEM(acc_shape, F32)] if nk > 1 else []
    return pl.pallas_call(
        body, name=name, grid=grid, in_specs=[a_spec, b_spec, *extra_specs], out_specs=list(out_specs),
        out_shape=list(out_shape), scratch_shapes=scratch,
        compiler_params=_params(("parallel", "parallel", "arbitrary")))(a, b, *extra)


def _mm_plain(name, a, b, form, out_dtypes, epi=_epi_cast, extra=(), tm=1024, tn=1024, tk=2048, dep=None):
    if form == "NN":
        (M, K), N = a.shape, b.shape[1]
    elif form == "NT":
        (M, K), N = a.shape, b.shape[0]
    else:
        (K, M), N = a.shape, b.shape[1]
    tm, tn, tk = _pick(M, tm, 8), _pick(N, tn), _pick(K, tk)
    grid = (N // tn, M // tm, K // tk)
    if form == "TN":
        a_spec = pl.BlockSpec((tk, tm), lambda j, i, k: (k, i))
    else:
        a_spec = pl.BlockSpec((tm, tk), lambda j, i, k: (i, k))
    if form == "NT":
        b_spec = pl.BlockSpec((tn, tk), lambda j, i, k: (j, k))
    else:
        b_spec = pl.BlockSpec((tk, tn), lambda j, i, k: (k, j))
    o_spec = pl.BlockSpec((tm, tn), lambda j, i, k: (i, j))
    dn = {"NN": NN, "NT": NT, "TN": TN}[form]
    return _matmul(name, a, b, dn, grid, a_spec, b_spec,
                   [jax.ShapeDtypeStruct((M, N), dt) for dt in out_dtypes], [o_spec] * len(out_dtypes), epi,
                   (tm, tn), extra, [o_spec] * len(extra), dep)


def _mm_nn_colblocked(name, a, bb, out_dtypes, epi=_epi_cast, tm=1024, tn=1024, tk=2048):
    M, K = a.shape
    ns = bb.shape[2]
    N = N_DEV * ns
    tm, tn, tk = _pick(M, tm, 8), _pick(ns, tn), _pick(K, tk)
    q = ns // tn
    grid = (N // tn, M // tm, K // tk)
    a_spec = pl.BlockSpec((tm, tk), lambda j, i, k: (i, k))
    b_spec = pl.BlockSpec((None, tk, tn), lambda j, i, k: (j // q, k, j % q))
    o_spec = pl.BlockSpec((tm, tn), lambda j, i, k: (i, j))
    return _matmul(name, a, bb, NN, grid, a_spec, b_spec,
                   [jax.ShapeDtypeStruct((M, N), dt) for dt in out_dtypes], [o_spec] * len(out_dtypes), epi, (tm, tn))


def _mm_nt_colblocked(name, a, bb, out_dtype, tm=1024, tn=1024, tk=1024):
    M, N = a.shape
    K, ns = bb.shape[1], bb.shape[2]
    tm, tn, tk = _pick(M, tm, 8), _pick(K, tn), _pick(ns, tk)
    q = ns // tk
    grid = (K // tn, M // tm, N // tk)
    a_spec = pl.BlockSpec((tm, tk), lambda j, i, k: (i, k))
    b_spec = pl.BlockSpec((None, tn, tk), lambda j, i, k: (k // q, j, k % q))
    o_spec = pl.BlockSpec((tm, tn), lambda j, i, k: (i, j))
    return _matmul(name, a, bb, NT, grid, a_spec, b_spec, [jax.ShapeDtypeStruct((M, K), out_dtype)], [o_spec],
                   _epi_cast, (tm, tn))[0]


def _mm_tn_colblocked_out(name, a, b, out_dtype, tm=1024, tn=1024, tk=2048):
    T, M = a.shape
    N = b.shape[1]
    ns = N // N_DEV
    tm, tn, tk = _pick(M, tm, 8), _pick(ns, tn), _pick(T, tk)
    q = ns // tn
    grid = (N // tn, M // tm, T // tk)
    a_spec = pl.BlockSpec((tk, tm), lambda j, i, k: (k, i))
    b_spec = pl.BlockSpec((tk, tn), lambda j, i, k: (k, j))
    o_spec = pl.BlockSpec((None, tm, tn), lambda j, i, k: (j // q, i, j % q))
    return _matmul(name, a, b, TN, grid, a_spec, b_spec, [jax.ShapeDtypeStruct((N_DEV, M, ns), out_dtype)], [o_spec],
                   _epi_cast, (tm, tn))[0]


def _unblock(bb_ref, full_ref):
    ns = bb_ref.shape[2]
    for j in range(N_DEV):
        full_ref[:, j * ns:(j + 1) * ns] = bb_ref[j]


def _mm_branch(name, a, bb, form, out_dtype, tm=512, dep=None):
    M = a.shape[0]
    K, ns = bb.shape[1], bb.shape[2]
    N = N_DEV * ns
    tm = _pick(M, tm, 8)
    out_cols = N if form == "NN" else K
    deps = [] if dep is None else [dep]

    def body(a_ref, b_ref, *rest):
        o_ref, w_full = rest[len(deps):]

        @pl.when(pl.program_id(0) == 0)
        def _():
            _unblock(b_ref, w_full)

        o_ref[...] = lax.dot_general(a_ref[...], w_full[...], NN if form == "NN" else NT,
                                     preferred_element_type=F32).astype(out_dtype)

    return pl.pallas_call(
        body, name=name, grid=(M // tm,),
        in_specs=[pl.BlockSpec((tm, a.shape[1]), lambda i: (i, 0)), pl.BlockSpec(bb.shape, lambda i: (0, 0, 0)),
                  *[pl.BlockSpec(TOKEN, lambda i: (0, 0)) for _ in deps]],
        out_specs=pl.BlockSpec((tm, out_cols), lambda i: (i, 0)),
        out_shape=jax.ShapeDtypeStruct((M, out_cols), out_dtype),
        scratch_shapes=[pltpu.VMEM((K, N), BF16)],
        compiler_params=_params(("arbitrary",)))(a, bb, *deps)


def _mm_branch_grad(name, a, b, out_dtype, tm=256):
    T, M = a.shape
    N = b.shape[1]
    ns = N // N_DEV
    tm = _pick(M, tm, 8)

    def body(a_ref, b_ref, o_ref):
        p = lax.dot_general(a_ref[...], b_ref[...], TN, preferred_element_type=F32)
        for j in range(N_DEV):
            o_ref[j] = p[:, j * ns:(j + 1) * ns].astype(out_dtype)

    return pl.pallas_call(
        body, name=name, grid=(M // tm,),
        in_specs=[pl.BlockSpec((T, tm), lambda i: (0, i)), pl.BlockSpec((T, N), lambda i: (0, 0))],
        out_specs=pl.BlockSpec((N_DEV, tm, ns), lambda i: (0, i, 0)),
        out_shape=jax.ShapeDtypeStruct((N_DEV, M, ns), out_dtype),
        compiler_params=_params(("parallel",)))(a, b)


ROW_TILE = 256


def _rms_fwd(x, g):
    T, D = x.shape
    tr = _pick(T, ROW_TILE, 8)

    def body(x_ref, g_ref, h_ref):
        xv = x_ref[...]
        r = lax.rsqrt(jnp.mean(xv * xv, axis=-1, keepdims=True) + RMS_EPS)
        h_ref[...] = (xv * r * g_ref[...]).astype(BF16)

    return pl.pallas_call(
        body, name="rms_fwd", grid=(T // tr,),
        in_specs=[pl.BlockSpec((tr, D), lambda i: (i, 0)), pl.BlockSpec((1, D), lambda i: (0, 0))],
        out_specs=pl.BlockSpec((tr, D), lambda i: (i, 0)),
        out_shape=jax.ShapeDtypeStruct((T, D), BF16), compiler_params=_params(("parallel",)))(x, g)


def _rms_bwd_math(dh, xv, g):
    r = lax.rsqrt(jnp.mean(xv * xv, axis=-1, keepdims=True) + RMS_EPS)
    gdh = dh * g
    dot = jnp.mean(gdh * xv, axis=-1, keepdims=True)
    dx = r * gdh - xv * (r * r * r * dot)
    return dx, dh * xv * r


def _rms_bwd(dh, x, g, dres):
    T, D = x.shape
    tr = _pick(T, ROW_TILE, 8)

    def body(dh_ref, x_ref, g_ref, dres_ref, dx_ref, dxb_ref, dg_ref):
        dx, dgrow = _rms_bwd_math(dh_ref[...], x_ref[...], g_ref[...])
        dx = dx + dres_ref[...]
        dx_ref[...] = dx
        dxb_ref[...] = dx.astype(BF16)
        part = jnp.sum(dgrow, axis=0, keepdims=True)

        @pl.when(pl.program_id(0) == 0)
        def _():
            dg_ref[...] = part

        @pl.when(pl.program_id(0) > 0)
        def _():
            dg_ref[...] += part

    row = pl.BlockSpec((tr, D), lambda i: (i, 0))
    vec = pl.BlockSpec((1, D), lambda i: (0, 0))
    return pl.pallas_call(
        body, name="rms_bwd", grid=(T // tr,), in_specs=[row, row, vec, row], out_specs=[row, row, vec],
        out_shape=[jax.ShapeDtypeStruct((T, D), F32), jax.ShapeDtypeStruct((T, D), BF16),
                   jax.ShapeDtypeStruct((1, D), F32)],
        compiler_params=_params(("arbitrary",)))(dh, x, g, dres)


def _loss_head(x, g, target):
    T, D = x.shape
    tr = _pick(T, ROW_TILE, 8)

    def body(x_ref, g_ref, t_ref, dx_ref, dxb_ref, dg_ref, loss_ref):
        xv, gv = x_ref[...], g_ref[...]
        r = lax.rsqrt(jnp.mean(xv * xv, axis=-1, keepdims=True) + RMS_EPS)
        err = xv * r * gv - t_ref[...]
        part_loss = 0.5 * jnp.sum(jnp.mean(err * err, axis=-1, keepdims=True), axis=0, keepdims=True)
        dx, dgrow = _rms_bwd_math(err * (1.0 / D), xv, gv)
        dx_ref[...] = dx
        dxb_ref[...] = dx.astype(BF16)
        part = jnp.sum(dgrow, axis=0, keepdims=True)
        lpart = jnp.broadcast_to(part_loss, (8, LANES))

        @pl.when(pl.program_id(0) == 0)
        def _():
            dg_ref[...] = part
            loss_ref[...] = lpart

        @pl.when(pl.program_id(0) > 0)
        def _():
            dg_ref[...] += part
            loss_ref[...] += lpart

    row = pl.BlockSpec((tr, D), lambda i: (i, 0))
    vec = pl.BlockSpec((1, D), lambda i: (0, 0))
    lsp = pl.BlockSpec((8, LANES), lambda i: (0, 0))
    return pl.pallas_call(
        body, name="loss_head", grid=(T // tr,), in_specs=[row, vec, row], out_specs=[row, row, vec, lsp],
        out_shape=[jax.ShapeDtypeStruct((T, D), F32), jax.ShapeDtypeStruct((T, D), BF16),
                   jax.ShapeDtypeStruct((1, D), F32), jax.ShapeDtypeStruct((8, LANES), F32)],
        compiler_params=_params(("arbitrary",)))(x, g, target)


def _ln_math(a1, g, b):
    mu = jnp.mean(a1, axis=-1, keepdims=True)
    xc = a1 - mu
    rstd = lax.rsqrt(jnp.mean(xc * xc, axis=-1, keepdims=True) + LN_EPS)
    xhat = xc * rstd
    return xhat, rstd, xhat * g + b


def _ln_silu(a1, g, b):
    T, C = a1.shape
    tr = _pick(T, ROW_TILE, 8)

    def body(a_ref, g_ref, b_ref, o_ref):
        _, _, y = _ln_math(a_ref[...], g_ref[...], b_ref[...])
        o_ref[...] = (y * _sigmoid(y)).astype(BF16)

    row = pl.BlockSpec((tr, C), lambda i: (i, 0))
    vec = pl.BlockSpec((1, C), lambda i: (0, 0))
    return pl.pallas_call(body, name="ln_silu", grid=(T // tr,), in_specs=[row, vec, vec], out_specs=row,
                          out_shape=jax.ShapeDtypeStruct((T, C), BF16), compiler_params=_params(("parallel",)))(a1, g, b)


def _ln_silu_bwd(a1, g, b, d_a3):
    T, C = a1.shape
    tr = _pick(T, ROW_TILE, 8)

    def body(a_ref, g_ref, b_ref, d_ref, da_ref, dg_ref, db_ref):
        gv = g_ref[...]
        xhat, rstd, y = _ln_math(a_ref[...], gv, b_ref[...])
        s = _sigmoid(y)
        dy = d_ref[...].astype(F32) * (s * (1.0 + y * (1.0 - s)))
        dxh = dy * gv
        m1 = jnp.mean(dxh, axis=-1, keepdims=True)
        m2 = jnp.mean(dxh * xhat, axis=-1, keepdims=True)
        da_ref[...] = rstd * (dxh - m1 - xhat * m2)
        pg = jnp.sum(dy * xhat, axis=0, keepdims=True)
        pb = jnp.sum(dy, axis=0, keepdims=True)

        @pl.when(pl.program_id(0) == 0)
        def _():
            dg_ref[...] = pg
            db_ref[...] = pb

        @pl.when(pl.program_id(0) > 0)
        def _():
            dg_ref[...] += pg
            db_ref[...] += pb

    row = pl.BlockSpec((tr, C), lambda i: (i, 0))
    vec = pl.BlockSpec((1, C), lambda i: (0, 0))
    return pl.pallas_call(
        body, name="ln_silu_bwd", grid=(T // tr,), in_specs=[row, vec, vec, row], out_specs=[row, vec, vec],
        out_shape=[jax.ShapeDtypeStruct((T, C), F32), jax.ShapeDtypeStruct((1, C), F32),
                   jax.ShapeDtypeStruct((1, C), F32)],
        compiler_params=_params(("arbitrary",)))(a1, g, b, d_a3)


CONV_ROWS = 128
PAD_A = 32
PAD_B = 8


def _u_block(T, first):
    return pl.BlockSpec((T, LANES), lambda i: (0, first + i))


def _causal_conv(xpad_ref, w_ref, ksize, pad, T, emit):
    for r0 in range(0, T, CONV_ROWS):
        acc = None
        for j in range(ksize):
            off = pad - (ksize - 1) + j + r0
            term = w_ref[j:j + 1, :] * xpad_ref[off:off + CONV_ROWS, :]
            acc = term if acc is None else acc + term
        emit(r0, acc)


def _anticausal_conv(gpad_ref, w_ref, ksize, T, emit):
    for r0 in range(0, T, CONV_ROWS):
        acc = None
        for j in range(ksize):
            off = (ksize - 1) - j + r0
            term = w_ref[j:j + 1, :] * gpad_ref[off:off + CONV_ROWS, :]
            acc = term if acc is None else acc + term
        emit(r0, acc)


def _conv_wgrad(xpad_ref, g_ref, dw_ref, ksize, pad, T):
    for j in range(ksize):
        acc = None
        for r0 in range(0, T, CONV_ROWS):
            off = pad - (ksize - 1) + j + r0
            term = g_ref[r0:r0 + CONV_ROWS, :] * xpad_ref[off:off + CONV_ROWS, :]
            term = jnp.sum(term.reshape(CONV_ROWS // 8, 8, LANES), axis=0)
            acc = term if acc is None else acc + term
        dw_ref[j:j + 1, :] = jnp.sum(acc, axis=0, keepdims=True)


def _convs_fwd(u, conf_dw, sconv_w, C):
    T = u.shape[0]
    nb = C // LANES

    def body(av_ref, ag_ref, bg_ref, cg_ref, bh_ref, dw_ref, sw_ref, a1_ref, s2_ref, xa, xs, s1):
        xa[0:PAD_A, :] = jnp.zeros((PAD_A, LANES), F32)
        xa[PAD_A:PAD_A + T, :] = av_ref[...].astype(F32) * _sigmoid(ag_ref[...].astype(F32))

        def emit_a(r0, acc):
            a1_ref[r0:r0 + CONV_ROWS, :] = acc

        _causal_conv(xa, dw_ref, CONF_KERNEL, PAD_A, T, emit_a)

        xs[0:PAD_B, :] = jnp.zeros((PAD_B, LANES), F32)
        xs[PAD_B:PAD_B + T, :] = cg_ref[...].astype(F32) * bh_ref[...].astype(F32)

        def emit_b(r0, acc):
            s1[r0:r0 + CONV_ROWS, :] = acc

        _causal_conv(xs, sw_ref, SCONV_KERNEL, PAD_B, T, emit_b)
        s2_ref[...] = (bg_ref[...].astype(F32) * s1[...]).astype(BF16)

    col = pl.BlockSpec((T, LANES), lambda i: (0, i))
    return pl.pallas_call(
        body, name="convs_fwd", grid=(nb,),
        in_specs=[_u_block(T, 0), _u_block(T, nb), _u_block(T, 2 * nb), _u_block(T, 3 * nb), _u_block(T, 4 * nb),
                  pl.BlockSpec((CONF_KERNEL, LANES), lambda i: (0, i)),
                  pl.BlockSpec((SCONV_KERNEL, LANES), lambda i: (0, i))],
        out_specs=[col, col],
        out_shape=[jax.ShapeDtypeStruct((T, C), F32), jax.ShapeDtypeStruct((T, C), BF16)],
        scratch_shapes=[pltpu.VMEM((T + PAD_A, LANES), F32), pltpu.VMEM((T + PAD_B, LANES), F32),
                        pltpu.VMEM((T, LANES), F32)],
        compiler_params=_params(("parallel",)))(u, u, u, u, u, conf_dw, sconv_w)


def _convs_bwd(u, conf_dw, sconv_w, d_a1, d_s2, C):
    T = u.shape[0]
    nb = C // LANES

    def body(*refs):
        _convs_bwd_block(*refs, T)

    col = pl.BlockSpec((T, LANES), lambda i: (0, i))
    wa = pl.BlockSpec((CONF_KERNEL, LANES), lambda i: (0, i))
    wb = pl.BlockSpec((SCONV_KERNEL, LANES), lambda i: (0, i))
    act = jax.ShapeDtypeStruct((T, C), BF16)
    return pl.pallas_call(
        body, name="convs_bwd", grid=(nb,),
        in_specs=[_u_block(T, 0), _u_block(T, nb), _u_block(T, 2 * nb), _u_block(T, 3 * nb), _u_block(T, 4 * nb),
                  wa, wb, col, col],
        out_specs=[col, col, col, col, col, wa, wb],
        out_shape=[act, act, act, act, act, jax.ShapeDtypeStruct((CONF_KERNEL, C), F32),
                   jax.ShapeDtypeStruct((SCONV_KERNEL, C), F32)],
        scratch_shapes=[pltpu.VMEM((T + PAD_A, LANES), F32), pltpu.VMEM((T + PAD_A, LANES), F32),
                        pltpu.VMEM((T + PAD_B, LANES), F32), pltpu.VMEM((T + PAD_B, LANES), F32),
                        pltpu.VMEM((T, LANES), F32)],
        compiler_params=_params(("parallel",)))(u, u, u, u, u, conf_dw, sconv_w, d_a1, d_s2)


def _convs_bwd_block(av_ref, ag_ref, bg_ref, cg_ref, bh_ref, dw_ref, sw_ref, da1_ref, ds2_ref,
                     dav_ref, dag_ref, dbg_ref, dcg_ref, dbh_ref, ddw_ref, dsw_ref, xa, ga, xs, gs, tmp, T):
    def to_tmp(r0, acc):
        tmp[r0:r0 + CONV_ROWS, :] = acc

    av = av_ref[...].astype(F32)
    sg = _sigmoid(ag_ref[...].astype(F32))
    xa[0:PAD_A, :] = jnp.zeros((PAD_A, LANES), F32)
    xa[PAD_A:PAD_A + T, :] = av * sg
    ga[0:T, :] = da1_ref[...]
    ga[T:T + PAD_A, :] = jnp.zeros((PAD_A, LANES), F32)
    _conv_wgrad(xa, ga, ddw_ref, CONF_KERNEL, PAD_A, T)
    _anticausal_conv(ga, dw_ref, CONF_KERNEL, T, to_tmp)
    da0 = tmp[...]
    dav_ref[...] = (da0 * sg).astype(BF16)
    dag_ref[...] = (da0 * av * sg * (1.0 - sg)).astype(BF16)

    cg = cg_ref[...].astype(F32)
    bh = bh_ref[...].astype(F32)
    ds2 = ds2_ref[...].astype(F32)
    xs[0:PAD_B, :] = jnp.zeros((PAD_B, LANES), F32)
    xs[PAD_B:PAD_B + T, :] = cg * bh
    _causal_conv(xs, sw_ref, SCONV_KERNEL, PAD_B, T, to_tmp)
    dbg_ref[...] = (ds2 * tmp[...]).astype(BF16)
    gs[0:T, :] = ds2 * bg_ref[...].astype(F32)
    gs[T:T + PAD_B, :] = jnp.zeros((PAD_B, LANES), F32)
    _conv_wgrad(xs, gs, dsw_ref, SCONV_KERNEL, PAD_B, T)
    _anticausal_conv(gs, sw_ref, SCONV_KERNEL, T, to_tmp)
    ds0 = tmp[...]
    dcg_ref[...] = (ds0 * bh).astype(BF16)
    dbh_ref[...] = (ds0 * cg).astype(BF16)


def _attn_specs(C):
    kvw = N_KV_HEADS * HEAD_DIM
    kb = (5 * C + C) // kvw
    qs = pl.BlockSpec((ATTN_BLOCK, C), lambda n: (n, 5))
    kc = pl.BlockSpec((ATTN_BLOCK, kvw), lambda n: (n, kb))
    kp = pl.BlockSpec((ATTN_BLOCK, kvw), lambda n: (jnp.maximum(n - 1, 0), kb))
    vc = pl.BlockSpec((ATTN_BLOCK, kvw), lambda n: (n, kb + 1))
    vp = pl.BlockSpec((ATTN_BLOCK, kvw), lambda n: (jnp.maximum(n - 1, 0), kb + 1))
    return qs, kc, kp, vc, vp


def _attn_masks(n, rows):
    row = lax.broadcasted_iota(jnp.int32, (rows, ATTN_BLOCK), 0) % ATTN_BLOCK
    col = lax.broadcasted_iota(jnp.int32, (rows, ATTN_BLOCK), 1)
    from_cur = col <= row
    return from_cur, jnp.logical_or(from_cur, n > 0)


def _stack_heads(ref, heads, width=HEAD_DIM):
    return jnp.concatenate([ref[:, h * width:(h + 1) * width] for h in heads], axis=0)


def _unstack_heads(t, count):
    return [t[j * ATTN_BLOCK:(j + 1) * ATTN_BLOCK] for j in range(count)]


def _attn_scores(qh, kc, kp, from_cur, valid):
    qs = qh * (HEAD_DIM ** -0.5)
    s_c = lax.dot_general(qs, kc, NT, preferred_element_type=F32)
    s_p = lax.dot_general(qs, kp, NT, preferred_element_type=F32)
    return jnp.where(valid, jnp.where(from_cur, s_c, s_p), NEG_BIG)


def _attn_split(t, from_cur):
    t = t.astype(BF16)
    zero = jnp.zeros_like(t)
    return jnp.where(from_cur, t, zero), jnp.where(from_cur, zero, t)


def _attn_fwd(u, sinks, C):
    T = u.shape[0]
    H = C // HEAD_DIM
    grp = H // N_KV_HEADS

    def body(sink_ref, q_ref, kc_ref, kp_ref, vc_ref, vp_ref, o_ref, lse_ref):
        n = pl.program_id(0)
        from_cur, valid = _attn_masks(n, grp * ATTN_BLOCK)
        outs, lses = [], []
        for g in range(N_KV_HEADS):
            kv = slice(g * HEAD_DIM, (g + 1) * HEAD_DIM)
            heads = range(g * grp, (g + 1) * grp)
            sink = jnp.concatenate([jnp.full((ATTN_BLOCK, 1), sink_ref[h], F32) for h in heads], axis=0)
            s = _attn_scores(_stack_heads(q_ref, heads), kc_ref[:, kv], kp_ref[:, kv], from_cur, valid)
            m = jnp.maximum(jnp.max(s, axis=-1, keepdims=True), sink)
            p = jnp.exp(s - m)
            den = jnp.sum(p, axis=-1, keepdims=True) + jnp.exp(sink - m)
            p_c, p_p = _attn_split(p, from_cur)
            acc = jnp.dot(p_c, vc_ref[:, kv], preferred_element_type=F32)
            acc = acc + jnp.dot(p_p, vp_ref[:, kv], preferred_element_type=F32)
            outs += _unstack_heads((acc / den).astype(BF16), grp)
            lses += _unstack_heads(m + jnp.log(den), grp)
        o_ref[...] = jnp.concatenate(outs, axis=1)
        lse_ref[...] = jnp.concatenate(lses, axis=1)

    qs, kc, kp, vc, vp = _attn_specs(C)
    return pl.pallas_call(
        body, name="attn_fwd", grid=(T // ATTN_BLOCK,),
        in_specs=[pl.BlockSpec(memory_space=pltpu.SMEM), qs, kc, kp, vc, vp],
        out_specs=[pl.BlockSpec((ATTN_BLOCK, C), lambda n: (n, 0)), pl.BlockSpec((ATTN_BLOCK, H), lambda n: (n, 0))],
        out_shape=[jax.ShapeDtypeStruct((T, C), BF16), jax.ShapeDtypeStruct((T, H), F32)],
        compiler_params=_params(("parallel",)))(sinks, u, u, u, u, u)


def _attn_bwd(u, o, lse, d_o, sinks, C):
    T = u.shape[0]
    H = C // HEAD_DIM
    grp = H // N_KV_HEADS
    kvw = N_KV_HEADS * HEAD_DIM
    nblk = T // ATTN_BLOCK
    scale = HEAD_DIM ** -0.5

    def body(sink_ref, q_ref, kc_ref, kp_ref, vc_ref, vp_ref, o_ref, lse_ref, do_ref,
             dq_ref, dk_ref, dv_ref, ds_ref, dk_acc, dv_acc):
        n = pl.program_id(0)

        @pl.when(n == 0)
        def _():
            dk_acc[...] = jnp.zeros_like(dk_acc)
            dv_acc[...] = jnp.zeros_like(dv_acc)
            ds_ref[...] = jnp.zeros_like(ds_ref)

        from_cur, valid = _attn_masks(n, grp * ATTN_BLOCK)
        cur = pl.ds(pl.multiple_of(n * ATTN_BLOCK, ATTN_BLOCK), ATTN_BLOCK)
        prev = pl.ds(pl.multiple_of(jnp.maximum(n - 1, 0) * ATTN_BLOCK, ATTN_BLOCK), ATTN_BLOCK)
        dqs, dsinks, dk_cs, dk_ps, dv_cs, dv_ps = [], [], [], [], [], []
        for g in range(N_KV_HEADS):
            kv = slice(g * HEAD_DIM, (g + 1) * HEAD_DIM)
            kc, kp, vc, vp = kc_ref[:, kv], kp_ref[:, kv], vc_ref[:, kv], vp_ref[:, kv]
            heads = range(g * grp, (g + 1) * grp)
            qg, dog, og = _stack_heads(q_ref, heads), _stack_heads(do_ref, heads), _stack_heads(o_ref, heads)
            lse_g = _stack_heads(lse_ref, heads, 1)
            sink = jnp.concatenate([jnp.full((ATTN_BLOCK, 1), sink_ref[h], F32) for h in heads], axis=0)
            p = jnp.exp(_attn_scores(qg, kc, kp, from_cur, valid) - lse_g)
            delta = jnp.sum(dog.astype(F32) * og.astype(F32), axis=-1, keepdims=True)
            dp = jnp.where(from_cur, lax.dot_general(dog, vc, NT, preferred_element_type=F32),
                           lax.dot_general(dog, vp, NT, preferred_element_type=F32))
            ds_c, ds_p = _attn_split(p * (dp - delta) * scale, from_cur)
            p_c, p_p = _attn_split(p, from_cur)
            dq = jnp.dot(ds_c, kc, preferred_element_type=F32) + jnp.dot(ds_p, kp, preferred_element_type=F32)
            dqs += _unstack_heads(dq.astype(BF16), grp)
            dk_cs.append(lax.dot_general(ds_c, qg, TN, preferred_element_type=F32))
            dk_ps.append(lax.dot_general(ds_p, qg, TN, preferred_element_type=F32))
            dv_cs.append(lax.dot_general(p_c, dog, TN, preferred_element_type=F32))
            dv_ps.append(lax.dot_general(p_p, dog, TN, preferred_element_type=F32))
            for t in _unstack_heads(jnp.exp(sink - lse_g) * delta, grp):
                dsinks.append(jnp.broadcast_to(-jnp.sum(t, axis=0, keepdims=True), (8, 1)))
        dq_ref[...] = jnp.concatenate(dqs, axis=1)
        ds_ref[...] += jnp.concatenate(dsinks, axis=1)
        dk_acc[cur, :] += jnp.concatenate(dk_cs, axis=1)
        dk_acc[prev, :] += jnp.concatenate(dk_ps, axis=1)
        dv_acc[cur, :] += jnp.concatenate(dv_cs, axis=1)
        dv_acc[prev, :] += jnp.concatenate(dv_ps, axis=1)

        @pl.when(n == nblk - 1)
        def _():
            dk_ref[...] = dk_acc[...].astype(BF16)
            dv_ref[...] = dv_acc[...].astype(BF16)

    qs, kc, kp, vc, vp = _attn_specs(C)
    blk = pl.BlockSpec((ATTN_BLOCK, C), lambda n: (n, 0))
    full = pl.BlockSpec((T, kvw), lambda n: (0, 0))
    return pl.pallas_call(
        body, name="attn_bwd", grid=(nblk,),
        in_specs=[pl.BlockSpec(memory_space=pltpu.SMEM), qs, kc, kp, vc, vp, blk,
                  pl.BlockSpec((ATTN_BLOCK, H), lambda n: (n, 0)), blk],
        out_specs=[blk, full, full, pl.BlockSpec((8, H), lambda n: (0, 0))],
        out_shape=[jax.ShapeDtypeStruct((T, C), BF16), jax.ShapeDtypeStruct((T, kvw), BF16),
                   jax.ShapeDtypeStruct((T, kvw), BF16), jax.ShapeDtypeStruct((8, H), F32)],
        scratch_shapes=[pltpu.VMEM((T, kvw), F32), pltpu.VMEM((T, kvw), F32)],
        compiler_params=_params(("arbitrary",)))(sinks, u, u, u, u, u, o, lse, d_o)


MERGE_COLS = 512


def _merge_specs(T, D, I):
    tr = _pick(T, ROW_TILE, 8)
    tc = _pick(D, MERGE_COLS)
    g0 = (I - N_BRANCH * D) // tc
    per = D // tc
    gspecs = [pl.BlockSpec((tr, tc), functools.partial(lambda j, i, b: (i, g0 + b * per + j), b=b)) for b in range(N_BRANCH)]
    tile = pl.BlockSpec((tr, tc), lambda j, i: (i, j))
    bias = pl.BlockSpec((N_BRANCH, tc), lambda j, i: (0, j))
    return tr, tc, gspecs, tile, bias


def _merge_fwd(u, gate_b, ya, yb, yc):
    T, I = u.shape
    D = ya.shape[1]
    tr, tc, gspecs, tile, bias = _merge_specs(T, D, I)

    def body(g0_ref, g1_ref, g2_ref, b_ref, ya_ref, yb_ref, yc_ref, o_ref):
        acc = None
        for b, (g_ref, y_ref) in enumerate(((g0_ref, ya_ref), (g1_ref, yb_ref), (g2_ref, yc_ref))):
            gate = _sigmoid(g_ref[...].astype(F32) + b_ref[b:b + 1, :])
            term = gate * y_ref[...].astype(F32)
            acc = term if acc is None else acc + term
        o_ref[...] = acc.astype(BF16)

    return pl.pallas_call(
        body, name="merge_fwd", grid=(D // tc, T // tr), in_specs=[*gspecs, bias, tile, tile, tile], out_specs=tile,
        out_shape=jax.ShapeDtypeStruct((T, D), BF16),
        compiler_params=_params(("parallel", "parallel")))(u, u, u, gate_b, ya, yb, yc)


def _merge_bwd(u, gate_b, ya, yb, yc, dm):
    T, I = u.shape
    D = ya.shape[1]
    tr, tc, gspecs, tile, bias = _merge_specs(T, D, I)

    def body(g0_ref, g1_ref, g2_ref, b_ref, ya_ref, yb_ref, yc_ref, dm_ref,
             dya_ref, dyb_ref, dyc_ref, dg0_ref, dg1_ref, dg2_ref, db_ref):
        dmv = dm_ref[...].astype(F32)
        first = pl.program_id(1) == 0
        for b, (g_ref, y_ref, dy_ref, dg_ref) in enumerate(((g0_ref, ya_ref, dya_ref, dg0_ref),
                                                           (g1_ref, yb_ref, dyb_ref, dg1_ref),
                                                           (g2_ref, yc_ref, dyc_ref, dg2_ref))):
            gate = _sigmoid(g_ref[...].astype(F32) + b_ref[b:b + 1, :])
            dy_ref[...] = (dmv * gate).astype(BF16)
            dpre = dmv * y_ref[...].astype(F32) * gate * (1.0 - gate)
            dg_ref[...] = dpre.astype(BF16)
            part = jnp.sum(dpre, axis=0, keepdims=True)

            @pl.when(first)
            def _():
                db_ref[b:b + 1, :] = part

            @pl.when(jnp.logical_not(first))
            def _():
                db_ref[b:b + 1, :] += part

    act = jax.ShapeDtypeStruct((T, D), BF16)
    return pl.pallas_call(
        body, name="merge_bwd", grid=(D // tc, T // tr), in_specs=[*gspecs, bias, tile, tile, tile, tile],
        out_specs=[tile] * 6 + [bias], out_shape=[act] * 6 + [jax.ShapeDtypeStruct((N_BRANCH, D), F32)],
        compiler_params=_params(("parallel", "arbitrary")))(u, u, u, gate_b, ya, yb, yc, dm)


def _concat_columns(parts):
    T = parts[0].shape[0]
    widths = [p.shape[1] for p in parts]
    tr = _pick(T, ROW_TILE, 16)

    def body(*refs):
        o_ref, off = refs[-1], 0
        for ref, width in zip(refs[:-1], widths):
            o_ref[:, off:off + width] = ref[...]
            off += width

    return pl.pallas_call(
        body, name="concat_columns", grid=(T // tr,),
        in_specs=[pl.BlockSpec((tr, width), lambda i: (i, 0)) for width in widths],
        out_specs=pl.BlockSpec((tr, sum(widths)), lambda i: (i, 0)),
        out_shape=jax.ShapeDtypeStruct((T, sum(widths)), parts[0].dtype),
        compiler_params=_params(("parallel",)))(*parts)


ELEMS_PER_TILE = 512 * 1024


def _row_tile(r, c):
    return _pick(r, max(16, ELEMS_PER_TILE // c), 16) if r % 16 == 0 else r


def _cast_place(w, layer, my_block):
    L, r, c = w.shape
    tr = _row_tile(r, c)

    def body(blk_ref, w_ref, o_ref):
        o_ref[...] = w_ref[...].astype(BF16)

    return pl.pallas_call(
        body, name="cast_place",
        grid_spec=pltpu.PrefetchScalarGridSpec(
            num_scalar_prefetch=1, grid=(r // tr,),
            in_specs=[pl.BlockSpec((None, tr, c), lambda i, blk: (layer, i, 0))],
            out_specs=pl.BlockSpec((None, tr, c), lambda i, blk: (blk[0], i, 0))),
        out_shape=jax.ShapeDtypeStruct((N_DEV, r, c), BF16), compiler_params=_params(("parallel",)))(my_block, w)


def _adamw_math(w, g, m, v):
    m = ADAM_B1 * m + (1.0 - ADAM_B1) * g
    v = ADAM_B2 * v + (1.0 - ADAM_B2) * (g * g)
    m_hat = m / (1.0 - ADAM_B1 ** ADAM_STEP)
    v_hat = v / (1.0 - ADAM_B2 ** ADAM_STEP)
    delta = -ADAM_LR * (m_hat / (jnp.sqrt(v_hat) + ADAM_EPS) + ADAM_WD * w)
    return delta, m, v


def _sum_parts(part_ref):
    acc = part_ref[0].astype(F32)
    for s in range(1, N_DEV):
        acc = acc + part_ref[s].astype(F32)
    return acc


def _sum8(parts):
    L, _, r, c = parts.shape
    tr = _row_tile(r, c)

    def body(p_ref, o_ref):
        o_ref[...] = _sum_parts(p_ref)

    return pl.pallas_call(
        body, name="sum8", grid=(L, r // tr),
        in_specs=[pl.BlockSpec((None, N_DEV, tr, c), lambda l, i: (l, 0, i, 0))],
        out_specs=pl.BlockSpec((None, tr, c), lambda l, i: (l, i, 0)),
        out_shape=jax.ShapeDtypeStruct((L, r, c), F32), compiler_params=_params(("parallel", "parallel")))(parts)


def _adamw(w, g, m, v):
    L, r, c = w.shape
    tr = _row_tile(r, c)
    spec = pl.BlockSpec((None, tr, c), lambda l, i: (l, i, 0))

    def body(w_ref, g_ref, m_ref, v_ref, d_ref, mo_ref, vo_ref):
        d, mn, vn = _adamw_math(w_ref[...], g_ref[...], m_ref[...], v_ref[...])
        d_ref[...] = d
        mo_ref[...] = mn
        vo_ref[...] = vn

    shp = jax.ShapeDtypeStruct(w.shape, F32)
    return pl.pallas_call(body, name="adamw", grid=(L, r // tr), in_specs=[spec] * 4, out_specs=[spec] * 3,
                          out_shape=[shp] * 3, compiler_params=_params(("parallel", "parallel")))(w, g, m, v)


N_CHIPS = 4
CHIP_XOR = (0, 2, 1, 3)


def _chip_sum(part4, sib4, own_all, layer, pos):
    _, _, r, c = part4.shape
    tr = _row_tile(r, c)

    def chip(p, s):
        return jnp.bitwise_xor(2 * p[0] + p[1], CHIP_XOR[s])

    mine = [pl.BlockSpec((None, None, tr, c), functools.partial(lambda i, p, s: (chip(p, s), p[2], i, 0), s=s))
            for s in range(N_CHIPS)]
    theirs = [pl.BlockSpec((None, None, tr, c), functools.partial(lambda i, p, s: (chip(p, s), 0, i, 0), s=s))
              for s in range(N_CHIPS)]

    def body(pos_ref, *refs):
        a, b = refs[:N_CHIPS], refs[N_CHIPS:2 * N_CHIPS]
        out_ref, own_ref = refs[2 * N_CHIPS + 1], refs[2 * N_CHIPS + 2]
        own_ref[...] = (a[0][...].astype(F32) + b[0][...].astype(F32)).astype(BF16)
        for s in range(1, N_CHIPS):
            out_ref[s - 1] = (a[s][...].astype(F32) + b[s][...].astype(F32)).astype(BF16)

    return pl.pallas_call(
        body, name="chip_sum",
        grid_spec=pltpu.PrefetchScalarGridSpec(
            num_scalar_prefetch=1, grid=(r // tr,),
            in_specs=[*mine, *theirs, pl.BlockSpec(memory_space=pl.ANY)],
            out_specs=[pl.BlockSpec((N_CHIPS - 1, tr, c), lambda i, p: (0, i, 0)),
                       pl.BlockSpec((None, tr, c), lambda i, p: (layer, i, 0))]),
        out_shape=[jax.ShapeDtypeStruct((N_CHIPS - 1, r, c), BF16), jax.ShapeDtypeStruct(own_all.shape, BF16)],
        input_output_aliases={1 + 2 * N_CHIPS: 1},
        compiler_params=_params(("parallel",)))(pos, *([part4] * N_CHIPS), *([sib4] * N_CHIPS), own_all)


def _sum_chips(own_ref, got_ref):
    acc = own_ref[...].astype(F32)
    for s in range(N_CHIPS - 1):
        acc = acc + got_ref[s].astype(F32)
    return acc


def _sum4_adamw(own, got, w, m, v, first, count, earlier=None):
    L, r, c = w.shape
    tr = _row_tile(r, c)
    spec = pl.BlockSpec((None, tr, c), lambda l, i: (first + l, i, 0))
    earlier = [] if earlier is None else list(earlier)

    def body(own_ref, got_ref, w_ref, m_ref, v_ref, *rest):
        g_ref, d_ref, mo_ref, vo_ref = rest[len(earlier):]
        g = _sum_chips(own_ref, got_ref)
        d, mn, vn = _adamw_math(w_ref[...], g, m_ref[...], v_ref[...])
        g_ref[...] = g
        d_ref[...] = d
        mo_ref[...] = mn
        vo_ref[...] = vn

    shp = jax.ShapeDtypeStruct(w.shape, F32)
    return pl.pallas_call(
        body, name="sum4_adamw", grid=(count, r // tr),
        in_specs=[spec, pl.BlockSpec((None, N_CHIPS - 1, tr, c), lambda l, i: (first + l, 0, i, 0)), spec, spec, spec,
                  *([ANY] * len(earlier))],
        out_specs=[spec] * 4, out_shape=[shp] * 4, input_output_aliases={5 + a: a for a in range(len(earlier))},
        compiler_params=_params(("parallel", "parallel")))(own, got, w, m, v, *earlier)


def _me():
    return lax.axis_index("x"), lax.axis_index("y"), lax.axis_index("c")


def _flip(pos, k):
    x, y, c = pos
    return (1 - x if k & 4 else x, 1 - y if k & 2 else y, 1 - c if k & 1 else c)


def _block_of(pos):
    return 4 * pos[0] + 2 * pos[1] + pos[2]


ANY = pl.BlockSpec(memory_space=pl.ANY)
SIBLING = 1
OTHER_CHIPS = (4, 2, 6)


def _all_gather(name, arrays):
    n = len(arrays)
    shapes = [a.shape[-2:] for a in arrays]

    def body(*refs):
        srcs, outs = refs[:n], refs[n:2 * n]
        send_sems, recv_sems, local_sems = refs[2 * n:]
        me = _me()
        sib = _flip(me, SIBLING)

        def copy(a, k, block_pos, to, src=None):
            dst = outs[a].at[_block_of(block_pos)]
            return pltpu.make_async_remote_copy(
                src_ref=dst if src is None else src, dst_ref=dst, send_sem=send_sems.at[a, k],
                recv_sem=recv_sems.at[a, k], device_id=to, device_id_type=MESH)

        mine = [pltpu.make_async_copy(srcs[a], outs[a].at[_block_of(me)], local_sems.at[a]) for a in range(n)]
        for cp in mine:
            cp.start()
        first = []
        for a in range(n):
            first.append(copy(a, 0, me, sib, src=srcs[a]))
            for j, k in enumerate(OTHER_CHIPS):
                first.append(copy(a, 1 + j, me, _flip(me, k), src=srcs[a]))
        for cp in first:
            cp.start()
        passed = []
        for j, k in enumerate(OTHER_CHIPS):
            for a in range(n):
                copy(a, 1 + j, _flip(me, k), me).wait_recv()
                fw = copy(a, 4 + j, _flip(me, k), sib)
                fw.start()
                passed.append(fw)
        for a in range(n):
            copy(a, 0, sib, me).wait_recv()
            for j, k in enumerate(OTHER_CHIPS):
                copy(a, 4 + j, _flip(sib, k), me).wait_recv()
        for cp in first + passed:
            cp.wait_send()
        for cp in mine:
            cp.wait()

    return pl.pallas_call(
        body, name=name, in_specs=[ANY] * n, out_specs=[ANY] * n,
        out_shape=[jax.ShapeDtypeStruct((N_DEV, *s), a.dtype) for s, a in zip(shapes, arrays)],
        scratch_shapes=[pltpu.SemaphoreType.DMA((n, 7)), pltpu.SemaphoreType.DMA((n, 7)), pltpu.SemaphoreType.DMA((n,))],
    )(*arrays)


HBM = pl.BlockSpec(memory_space=pltpu.HBM)
SEM = pl.BlockSpec(memory_space=pltpu.SEMAPHORE)
DATAFLOW = pltpu.SideEffectType.DATAFLOW_SIDE_EFFECTING
X_NEIGHBOUR, Y_NEIGHBOUR = 4, 2
NEAR = (SIBLING, X_NEIGHBOUR, Y_NEIGHBOUR)


def _in_hbm(a):
    return pltpu.with_memory_space_constraint(a, pltpu.HBM)


def _hbm_like(a):
    return pltpu.HBM(a.shape, a.dtype)


def _copies_start(name, srcs, lands, plan, n_copies, after=None):
    ns, n = len(srcs), len(lands)
    deps = [] if after is None else [after]

    def body(*refs):
        k0 = ns + n + len(deps)
        send_sems, recv_sems, token = refs[k0], refs[k0 + 1], refs[k0 + 2 + n]
        for s, (src, dst, peer, _) in enumerate(plan(_me(), refs[:ns], refs[ns:ns + n])):
            pltpu.make_async_remote_copy(src_ref=src, dst_ref=dst, send_sem=send_sems.at[s], recv_sem=recv_sems.at[s],
                                         device_id=peer, device_id_type=MESH).start()
        token[...] = jnp.zeros(TOKEN, F32)

    out = pl.pallas_call(
        body, name=name, in_specs=[*([HBM] * (ns + n)), *([ANY] * len(deps))],
        out_specs=[SEM, SEM, *([HBM] * n), pl.BlockSpec(memory_space=pltpu.VMEM)],
        out_shape=[pltpu.SemaphoreType.DMA((n_copies,)), pltpu.SemaphoreType.DMA((n_copies,)),
                   *[_hbm_like(a) for a in lands], jax.ShapeDtypeStruct(TOKEN, F32)],
        input_output_aliases={ns + a: 2 + a for a in range(n)},
        compiler_params=pltpu.CompilerParams(has_side_effects=DATAFLOW),
    )(*[_in_hbm(a) for a in srcs], *[_in_hbm(a) for a in lands], *deps)
    return dict(send=out[0], recv=out[1], srcs=list(srcs), plan=plan), list(out[2:2 + n]), out[2 + n]


def _copies_wait(name, flight, lands, after):
    srcs, plan = flight["srcs"], flight["plan"]
    ns, n = len(srcs), len(lands)
    after = list(after) if isinstance(after, (list, tuple)) else [after]

    def body(*refs):
        send_sems, recv_sems = refs[ns + n], refs[ns + n + 1]
        for s, (src, dst, peer, landing) in enumerate(plan(_me(), refs[:ns], refs[ns:ns + n])):
            pltpu.make_async_remote_copy(src_ref=src, dst_ref=dst, send_sem=send_sems.at[s], recv_sem=recv_sems.at[s],
                                         device_id=peer, device_id_type=MESH).wait_send()
            pltpu.make_async_remote_copy(src_ref=landing, dst_ref=landing, send_sem=send_sems.at[s],
                                         recv_sem=recv_sems.at[s], device_id=peer, device_id_type=MESH).wait_recv()

    out = pl.pallas_call(
        body, name=name, in_specs=[*([HBM] * (ns + n)), SEM, SEM, *([ANY] * len(after))], out_specs=[HBM] * n,
        out_shape=[_hbm_like(a) for a in lands], input_output_aliases={ns + a: a for a in range(n)},
        compiler_params=pltpu.CompilerParams(has_side_effects=DATAFLOW),
    )(*srcs, *lands, flight["send"], flight["recv"], *after)
    return list(out)


def _gather_plan_near(me, srcs, lands):
    plan = []
    for land in lands:
        own = land.at[_block_of(me)]
        for k in NEAR:
            peer = _flip(me, k)
            plan.append((own, own, peer, land.at[_block_of(peer)]))
    return plan


def _gather_plan_far(me, srcs, lands):
    x_nbr, y_nbr, far = _flip(me, X_NEIGHBOUR), _flip(me, Y_NEIGHBOUR), _flip(me, X_NEIGHBOUR | Y_NEIGHBOUR)
    plan = []
    for land in lands:
        half = land.shape[1] // 2
        first, second = pl.ds(0, half), pl.ds(half, half)
        passed = land.at[_block_of(y_nbr), first]
        plan.append((passed, passed, x_nbr, land.at[_block_of(far), first]))
        passed = land.at[_block_of(x_nbr), second]
        plan.append((passed, passed, y_nbr, land.at[_block_of(far), second]))
    return plan


def _broadcast_plan(me, srcs, lands):
    plan = []
    for land in lands:
        own = land.at[_block_of(me)]
        for k in range(1, N_DEV):
            peer = _flip(me, k)
            plan.append((own, own, peer, land.at[_block_of(peer)]))
    return plan


def _sibling_plan(me, srcs, lands):
    sib = _flip(me, SIBLING)
    return [(src.at[:, pl.ds(1 - me[2], 1)], land, sib, land) for src, land in zip(srcs, lands)]


def _scatter_plan(layer):
    def plan(me, srcs, lands):
        out = []
        for src, land in zip(srcs, lands):
            for j, k in enumerate(OTHER_CHIPS):
                out.append((src.at[j], land.at[layer, j], _flip(me, k), land.at[layer, j]))
        return out
    return plan


def _pass_on_plan(relations):
    def plan(me, srcs, lands):
        sib = _flip(me, SIBLING)
        out = []
        for land in lands:
            for k in relations:
                blk = land.at[_block_of(_flip(me, k))]
                out.append((blk, blk, sib, land.at[_block_of(_flip(sib, k))]))
        return out
    return plan


def _after(small, tokens):
    for t in tokens:
        small = small + t[0:1, 0:1]
    return small


def _layer_fwd(xc, l, W, P, dims, deps=(), mid_layer=None):
    T, D, C, I, F = dims
    h = _rms_fwd(xc, _after(P["norm_mix_g"][l:l + 1], deps))
    u = _mm_plain("mm_u", h, W["w_in"][l].reshape(I, D), "NT", [BF16], tn=1280)[0]
    a1, s2 = _convs_fwd(u, P["conf_dw"][l], P["sconv_w"][l], C)
    a3 = _ln_silu(a1, P["conf_ln_g"][l:l + 1], P["conf_ln_b"][l:l + 1])
    o, lse = _attn_fwd(u, P["sinks"][l], C)
    ya = _mm_branch("mm_branch_out", a3, W["w_conf_out"][l], "NN", BF16)
    yb = _mm_branch("mm_branch_out", s2, W["w_sconv_out"][l], "NN", BF16)
    yc = _mm_branch("mm_branch_out", o, W["w_attn_out"][l], "NN", BF16)
    merged = _merge_fwd(u, P["gate_b"][l], ya, yb, yc)
    x1 = _mm_plain("mm_mix", merged, W["w_mix_out"][l].reshape(D, D), "NN", [F32], _epi_resid, [xc], tk=2048)[0]
    norm_ffn_g = P["norm_ffn_g"][l:l + 1]
    h2 = _rms_fwd(x1, norm_ffn_g if mid_layer is None else _after(norm_ffn_g, [mid_layer(x1)]))
    up, act = _mm_nn_colblocked("mm_up", h2, W["w_up"][l], [BF16, BF16], _epi_relu2)
    x2 = _mm_plain("mm_down", act, W["w_down"][l].reshape(F, D), "NN", [F32], _epi_resid, [x1], tk=2048)[0]
    saved = dict(xc=xc, h=h, u=u, a1=a1, a3=a3, s2=s2, o=o, lse=lse, ya=ya, yb=yb, yc=yc, merged=merged, x1=x1, h2=h2,
                 up=up, act=act)
    return x2, saved


def _bwd_mlp(dx2, dx2_b, l, W, P, S, dims, dep=None):
    T, D, C, I, F = dims
    d_up = _mm_plain("mm_d_up", dx2_b, W["w_down"][l].reshape(F, D), "NT", [BF16], _epi_drelu2, [S["up"]], dep=dep)[0]
    g_down = _mm_plain("mm_g_down", S["act"], dx2_b, "TN", [BF16])[0]
    dh2 = _mm_nt_colblocked("mm_d_h2", d_up, W["w_up"][l], F32, tm=2048)
    g_up = _mm_tn_colblocked_out("mm_g_up", S["h2"], d_up, BF16)
    dx1, dx1_b, dg_ffn = _rms_bwd(dh2, S["x1"], P["norm_ffn_g"][l:l + 1], dx2)
    return dx1, dx1_b, dict(w_up=g_up, w_down=g_down.reshape(N_DEV, F // N_DEV, D)), dg_ffn


def _bwd_mix_out(dx1_b, l, W, P, S, dims, dep=None):
    T, D, C, I, F = dims
    dm = _mm_plain("mm_d_merged", dx1_b, W["w_mix_out"][l].reshape(D, D), "NT", [BF16], dep=dep)[0]
    g_mix = _mm_plain("mm_g_mix", S["merged"], dx1_b, "TN", [BF16])[0]
    merge = _merge_bwd(S["u"], P["gate_b"][l], S["ya"], S["yb"], S["yc"], dm)
    return g_mix.reshape(N_DEV, D // N_DEV, D), merge


def _bwd_mixers(dx1, merge, dg_ffn, g_mix, l, W, P, S, dims, dep, send_mid, send_in):
    T, D, C, I, F = dims
    d_ya, d_yb, d_yc, dg0, dg1, dg2, d_gate_b = merge
    d_a3 = _mm_branch("mm_d_branch", d_ya, W["w_conf_out"][l], "NT", BF16, dep=dep)
    d_s2 = _mm_branch("mm_d_branch", d_yb, W["w_sconv_out"][l], "NT", BF16)
    d_o = _mm_branch("mm_d_branch", d_yc, W["w_attn_out"][l], "NT", BF16)
    g_conf = _mm_branch_grad("mm_g_branch", S["a3"], d_ya, BF16)
    g_sconv = _mm_branch_grad("mm_g_branch", S["s2"], d_yb, BF16)
    g_attn = _mm_branch_grad("mm_g_branch", S["o"], d_yc, BF16)
    tok = send_mid(dict(w_mix_out=g_mix, w_conf_out=g_conf, w_sconv_out=g_sconv, w_attn_out=g_attn), g_attn)
    d_a1, d_ln_g, d_ln_b = _ln_silu_bwd(S["a1"], _after(P["conf_ln_g"][l:l + 1], [tok]), P["conf_ln_b"][l:l + 1], d_a3)
    d_av, d_ag, d_bg, d_cg, d_bh, d_conf_dw, d_sconv_w = _convs_bwd(S["u"], P["conf_dw"][l], P["sconv_w"][l], d_a1, d_s2, C)
    dq, dk, dv, d_sinks = _attn_bwd(S["u"], S["o"], S["lse"], d_o, P["sinks"][l], C)
    du = _concat_columns([d_av, d_ag, d_bg, d_cg, d_bh, dq, dk, dv, dg0, dg1, dg2])
    g_in = _mm_plain("mm_g_in", du, S["h"], "TN", [BF16], tm=1280)[0]
    tok = send_in(dict(w_in=g_in.reshape(N_DEV, I // N_DEV, D)), g_in)
    dh = _mm_plain("mm_d_h", du, W["w_in"][l].reshape(I, D), "NN", [F32], tk=2560, dep=tok)[0]
    dx, dx_b, dg_mix = _rms_bwd(dh, S["xc"], P["norm_mix_g"][l:l + 1], dx1)
    small = dict(norm_mix_g=dg_mix, gate_b=d_gate_b, conf_dw=d_conf_dw, conf_ln_g=d_ln_g, conf_ln_b=d_ln_b,
                 sconv_w=d_sconv_w, sinks=d_sinks[0:1], norm_ffn_g=dg_ffn)
    return dx, dx_b, small


BIG = ("w_in", "w_conf_out", "w_sconv_out", "w_attn_out", "w_mix_out", "w_up", "w_down")
MLP_WEIGHTS = ("w_down", "w_up")
MID_WEIGHTS = ("w_mix_out", "w_conf_out", "w_sconv_out", "w_attn_out")
IN_WEIGHTS = ("w_in",)
SMALL_PER_LAYER = ("norm_mix_g", "gate_b", "conf_dw", "conf_ln_g", "conf_ln_b", "sconv_w", "sinks", "norm_ffn_g")
WEIGHTS = ("norm_mix_g", "w_in", "gate_b", "conf_dw", "conf_ln_g", "conf_ln_b", "w_conf_out", "sconv_w", "w_sconv_out",
           "sinks", "w_attn_out", "w_mix_out", "norm_ffn_g", "w_up", "w_down", "final_g")


SUBLANES = 8


def _nrows(n_el, width):
    per_tile = SUBLANES * width
    return SUBLANES * (-(-n_el // per_tile))


def _rows(a, width):
    flat = a.reshape(-1)
    nrow = _nrows(flat.shape[0], width)
    return jnp.pad(flat, (0, nrow * width - flat.shape[0])).reshape(nrow, width)


def _as3d(a):
    if a.ndim == 1:
        return a.reshape(1, 1, -1)
    if a.ndim == 2:
        return a.reshape(1, *a.shape)
    return a


def kernel(x, norm_mix_g, w_in, gate_b, conf_dw, conf_ln_g, conf_ln_b, w_conf_out, sconv_w, w_sconv_out, sinks, w_attn_out, w_mix_out, norm_ffn_g, w_up, w_down, final_g, loss_target, m_norm_mix_g, m_w_in, m_gate_b, m_conf_dw, m_conf_ln_g, m_conf_ln_b, m_w_conf_out, m_sconv_w, m_w_sconv_out, m_sinks, m_w_attn_out, m_w_mix_out, m_norm_ffn_g, m_w_up, m_w_down, m_final_g, v_norm_mix_g, v_w_in, v_gate_b, v_conf_dw, v_conf_ln_g, v_conf_ln_b, v_w_conf_out, v_sconv_w, v_w_sconv_out, v_sinks, v_w_attn_out, v_w_mix_out, v_norm_ffn_g, v_w_up, v_w_down, v_final_g):
    w = dict(norm_mix_g=norm_mix_g, w_in=w_in, gate_b=gate_b, conf_dw=conf_dw, conf_ln_g=conf_ln_g, conf_ln_b=conf_ln_b,
             w_conf_out=w_conf_out, sconv_w=sconv_w, w_sconv_out=w_sconv_out, sinks=sinks, w_attn_out=w_attn_out,
             w_mix_out=w_mix_out, norm_ffn_g=norm_ffn_g, w_up=w_up, w_down=w_down, final_g=final_g)
    mom = dict(norm_mix_g=m_norm_mix_g, w_in=m_w_in, gate_b=m_gate_b, conf_dw=m_conf_dw, conf_ln_g=m_conf_ln_g,
               conf_ln_b=m_conf_ln_b, w_conf_out=m_w_conf_out, sconv_w=m_sconv_w, w_sconv_out=m_w_sconv_out,
               sinks=m_sinks, w_attn_out=m_w_attn_out, w_mix_out=m_w_mix_out, norm_ffn_g=m_norm_ffn_g, w_up=m_w_up,
               w_down=m_w_down, final_g=m_final_g)
    var = dict(norm_mix_g=v_norm_mix_g, w_in=v_w_in, gate_b=v_gate_b, conf_dw=v_conf_dw, conf_ln_g=v_conf_ln_g,
               conf_ln_b=v_conf_ln_b, w_conf_out=v_w_conf_out, sconv_w=v_sconv_w, w_sconv_out=v_w_sconv_out,
               sinks=v_sinks, w_attn_out=v_w_attn_out, w_mix_out=v_w_mix_out, norm_ffn_g=v_norm_ffn_g, w_up=v_w_up,
               w_down=v_w_down, final_g=v_final_g)

    _, T, D = x.shape
    L = w_in.shape[0]
    C = D // 2
    I = w_in.shape[2] * N_DEV
    F = w_up.shape[2] * N_DEV
    dims = (T, D, C, I, F)
    my_block = _block_of(_me())

    w["w_in"], mom["w_in"], var["w_in"] = (jnp.swapaxes(a, 1, 2) for a in (w_in, m_w_in, v_w_in))

    shard_names = ("gate_b", "conf_dw", "sconv_w")
    packed = jnp.concatenate([_rows(w[k], LANES) for k in shard_names], axis=0)
    gathered = _all_gather("gather_small", [packed])[0]

    pos = jnp.stack(_me()).astype(jnp.int32)
    blk = my_block.reshape(1).astype(jnp.int32)
    W = {k: [_cast_place(w[k], l, blk) for l in range(L)] for k in BIG}
    n_near, n_far = len(BIG) * len(NEAR), len(BIG) * 2

    def gather_near(l, after):
        return _copies_start(f"gather_near_start_{l}", [], [W[k][l] for k in BIG], _gather_plan_near, n_near, after)

    def gather_far(l, g, after):
        lands = _copies_wait(f"gather_near_wait_{l}", g["near"], g["lands"], after)
        g["far"], lands, tok_far = _copies_start(f"gather_far_start_{l}", [], lands, _gather_plan_far, n_far)
        g["pass_near"], g["lands"], tok_pass = _copies_start(
            f"pass_near_start_{l}", [], lands, _pass_on_plan((X_NEIGHBOUR, Y_NEIGHBOUR)), 2 * len(BIG))
        return tok_far + tok_pass

    def gather_finish(l, g, after):
        lands = _copies_wait(f"gather_far_wait_{l}", g["far"], g["lands"], after)
        pass_far, lands, token = _copies_start(f"pass_far_start_{l}", [], lands,
                                               _pass_on_plan((X_NEIGHBOUR | Y_NEIGHBOUR,)), len(BIG))
        nxt = {}
        if l + 1 < L:
            nxt["near"], nxt["lands"], token = gather_near(l + 1, token)
        lands = _copies_wait(f"pass_near_wait_{l}", g["pass_near"], lands, [after, token])
        return _copies_wait(f"pass_far_wait_{l}", pass_far, lands, [after, token]), nxt, token

    gathering = {}
    gathering["near"], gathering["lands"], tok = gather_near(0, gathered)
    later_layers = [W[k][l] for l in range(1, L) for k in BIG]
    tok = gather_far(0, gathering, [tok, *later_layers])

    P = dict(norm_mix_g=norm_mix_g, conf_ln_g=conf_ln_g, conf_ln_b=conf_ln_b, sinks=sinks, norm_ffn_g=norm_ffn_g)
    row0 = 0
    for k in shard_names:
        n_el = w[k].size
        nrow = _nrows(n_el, LANES)
        part = gathered[:, row0:row0 + nrow].reshape(N_DEV, -1)[:, :n_el].reshape(N_DEV, *w[k].shape)
        P[k] = jnp.moveaxis(part, 0, 2).reshape(*w[k].shape[:2], N_DEV * w[k].shape[2])
        row0 += nrow

    xc = x.reshape(T, D)
    saved = []
    for l in range(L):
        lands, gathering, tok = gather_finish(l, gathering, xc if l else tok)
        for k, g in zip(BIG, lands):
            W[k][l] = g
        mid_layer = functools.partial(gather_far, l + 1, gathering) if l + 1 < L else None
        xc, S = _layer_fwd(xc, l, W, P, dims, (), mid_layer)
        saved.append(S)
    dx, dx_b, d_final_g, loss_tile = _loss_head(xc, final_g.reshape(1, D), loss_target.reshape(T, D))

    own_all = {k: lax.empty((L, *W[k][0].shape[1:]), BF16) for k in BIG}
    recv = {k: lax.empty((L, N_CHIPS - 1, *W[k][0].shape[1:]), BF16) for k in BIG}
    scatters = []

    def to_sibling(names, grads, l, after):
        part4 = [grads[k].reshape(N_CHIPS, 2, *grads[k].shape[1:]) for k in names]
        zone = [lax.empty((N_CHIPS, 1, *p.shape[2:]), BF16) for p in part4]
        fl, zone, token = _copies_start(f"sibling_start_{l}_{names[0]}", part4, zone, _sibling_plan, len(names), after)
        return dict(names=names, l=l, part4=part4, flight=fl, zone=zone), token

    def to_owners(group, after):
        names, l = group["names"], group["l"]
        sib4 = _copies_wait(f"sibling_wait_{l}_{names[0]}", group["flight"], group["zone"], after)
        chip_parts = []
        for k, p4, s4 in zip(names, group["part4"], sib4):
            cp, own_all[k] = _chip_sum(p4, s4, own_all[k], l, pos)
            chip_parts.append(cp)
        fl, zone, token = _copies_start(f"scatter_start_{l}_{names[0]}", chip_parts, [recv[k] for k in names],
                                        _scatter_plan(l), len(names) * len(OTHER_CHIPS))
        for k, g in zip(names, zone):
            recv[k] = g
        scatters.append((f"scatter_wait_{l}_{names[0]}", fl, names, l))
        return token

    small_grads = [None] * L
    dep, groups = None, {}
    for l in reversed(range(L)):
        S = saved[l]
        dx1, dx1_b, g_mlp, dg_ffn = _bwd_mlp(dx, dx_b, l, W, P, S, dims, dep)
        groups["mlp"], dep = to_sibling(MLP_WEIGHTS, g_mlp, l, dx1)
        if "in" in groups:
            dep = dep + to_owners(groups["in"], dx1)
        g_mix, merge = _bwd_mix_out(dx1_b, l, W, P, S, dims, dep)
        dep = to_owners(groups["mlp"], merge[0])

        def send_mid(grads, after, l=l):
            groups["mid"], token = to_sibling(MID_WEIGHTS, grads, l, after)
            return token

        def send_in(grads, after, l=l):
            token = to_owners(groups["mid"], after)
            groups["in"], token2 = to_sibling(IN_WEIGHTS, grads, l, after)
            return token + token2

        dx, dx_b, small_grads[l] = _bwd_mixers(dx1, merge, dg_ffn, g_mix, l, W, P, S, dims, dep, send_mid, send_in)
    last_start = to_owners(groups["in"], dx)

    width = LANES
    pieces = [_rows(small_grads[l][k], width) for l in range(L) for k in SMALL_PER_LAYER]
    pieces += [_rows(d_final_g, width), _rows(loss_tile[0:1, 0:1], width)]
    partial = jnp.concatenate(pieces, axis=0)
    everyone = lax.dynamic_update_slice(lax.empty((N_DEV, *partial.shape), F32), partial[None], (my_block, 0, 0))
    small_flight, (everyone,), token = _copies_start("small_grads_start", [], [everyone], _broadcast_plan, N_DEV - 1,
                                                     after=last_start)
    last_start = last_start + token

    def await_scatters(layers):
        for name, fl, names, l in scatters:
            if l in layers:
                for k, g in zip(names, _copies_wait(name, fl, [recv[k] for k in names], [dx, last_start])):
                    recv[k] = g

    await_scatters(range(1, L))
    done = {k: _sum4_adamw(own_all[k], recv[k], w[k], mom[k], var[k], 1, L - 1) for k in BIG} if L > 1 else {}
    await_scatters([0])

    grads, delta, new_m, new_v = {}, {}, {}, {}
    for k in BIG:
        grads[k], delta[k], new_m[k], new_v[k] = _sum4_adamw(own_all[k], recv[k], w[k], mom[k], var[k], 0, 1,
                                                             done.get(k))

    everyone = _copies_wait("small_grads_wait", small_flight, [everyone], [grads[k] for k in BIG])[0]
    total = _sum8(everyone.reshape(1, *everyone.shape))[0]
    row0 = 0
    per_layer = {k: [] for k in SMALL_PER_LAYER}
    for l in range(L):
        for k in SMALL_PER_LAYER:
            shape = small_grads[l][k].shape
            n_el = small_grads[l][k].size
            nrow = _nrows(n_el, width)
            per_layer[k].append(total[row0:row0 + nrow].reshape(-1)[:n_el].reshape(shape))
            row0 += nrow
    nrow = _nrows(D, width)
    grads["final_g"] = total[row0:row0 + nrow].reshape(-1)[:D]
    row0 += nrow
    loss = total[row0, 0]
    for k in SMALL_PER_LAYER:
        full = jnp.stack(per_layer[k], axis=0)
        if k in shard_names:
            shard = w[k].shape[2]
            full = lax.dynamic_slice_in_dim(full, my_block * shard, shard, axis=2)
        grads[k] = full.reshape(w[k].shape)

    for out in (grads, delta, new_m, new_v):
        out["w_in"] = jnp.swapaxes(out["w_in"], 1, 2)
    for k in WEIGHTS:
        if k in BIG:
            continue
        d, mn, vn = _adamw(_as3d(w[k]), _as3d(grads[k]), _as3d(mom[k]), _as3d(var[k]))
        delta[k], new_m[k], new_v[k] = d.reshape(w[k].shape), mn.reshape(w[k].shape), vn.reshape(w[k].shape)

    return (loss, dx.reshape(1, T, D), *[grads[k] for k in WEIGHTS], *[delta[k] for k in WEIGHTS],
            *[new_m[k] for k in WEIGHTS], *[new_v[k] for k in WEIGHTS])
```
